```python
import jax, jax.numpy as jnp
from jax import lax
import numpy as np

D_MODEL = 1024
BATCH = 2
SEQ = 8192
DEPTH = 2

MIX_WIDTH = 2 * D_MODEL
HEAD_DIM = 64
BLOCK = 128
SWA_HEADS = (MIX_WIDTH // 4) // HEAD_DIM
SWA_KV_HEADS = SWA_HEADS // 4
SWA_WINDOW = 128
FOX_HEADS = (MIX_WIDTH // 4) // HEAD_DIM
SSM_HEAD_DIM = 64
SSM_HEADS = (MIX_WIDTH // 2) // SSM_HEAD_DIM
SSM_GROUPS = 2
SSM_STATE = 128
SSM_CONV = 4
SSM_CHUNK = 128
MEM_TOKENS = 256
MEM_HEADS = 4
MEM_HEAD_DIM = D_MODEL // MEM_HEADS
ROPE_THETA = 10000.0
EPS = 1e-6
NEG_INF = -1e30

SWA_W = SWA_HEADS * HEAD_DIM
SWA_KV_W = SWA_KV_HEADS * HEAD_DIM
FOX_W = FOX_HEADS * HEAD_DIM
SSM_W = SSM_HEADS * SSM_HEAD_DIM
SSM_BC_W = SSM_GROUPS * SSM_STATE
SSM_CONV_W = SSM_W + 2 * SSM_BC_W
IN_SPLITS = (SWA_W, SWA_KV_W, SWA_KV_W, SWA_W,
             FOX_W, FOX_W, FOX_W, FOX_HEADS, FOX_W,
             SSM_W, SSM_CONV_W, SSM_HEADS)
IN_W = SWA_W + 2 * SWA_KV_W + SWA_W + 4 * FOX_W + FOX_HEADS + SSM_W + SSM_CONV_W + SSM_HEADS

kernel_name = "hymba_style_swa_fox_ssd_hybrid"


def _offsets(sizes):
    out, acc = [], 0
    for s in sizes[:-1]:
        acc += s
        out.append(acc)
    return out


def rms_norm(x, w):
    xf = x.astype(jnp.float32)
    y = xf * lax.rsqrt(jnp.mean(xf * xf, axis=-1, keepdims=True) + EPS)
    return (y * w.astype(jnp.float32)).astype(x.dtype)


def rope_tables(seq):
    pos = jnp.arange(seq, dtype=jnp.float32)
    inv = 1.0 / (ROPE_THETA ** (jnp.arange(0, HEAD_DIM, 2, dtype=jnp.float32) / HEAD_DIM))
    ang = pos[:, None] * inv[None, :]
    return jnp.cos(ang), jnp.sin(ang)


def apply_rope(x, cos, sin):
    xf = x.astype(jnp.float32)
    x1, x2 = jnp.split(xf, 2, axis=-1)
    c = cos[None, :, None, :]
    s = sin[None, :, None, :]
    return jnp.concatenate([x1 * c - x2 * s, x2 * c + x1 * s], axis=-1).astype(x.dtype)


def sliding_window_attention(q, k, v, sinks):
    b, s, hq, hd = q.shape
    hkv = k.shape[2]
    g = hq // hkv
    n = s // BLOCK
    qb = q.reshape(b, n, BLOCK, hkv, g, hd)
    kb = k.reshape(b, n, BLOCK, hkv, hd)
    vb = v.reshape(b, n, BLOCK, hkv, hd)
    pad = ((0, 0), (1, 0), (0, 0), (0, 0), (0, 0))
    kcat = jnp.concatenate([jnp.pad(kb, pad)[:, :-1], kb], axis=2)
    vcat = jnp.concatenate([jnp.pad(vb, pad)[:, :-1], vb], axis=2)
    scores = jnp.einsum('bnqkgd,bnskd->bnkgqs', qb, kcat).astype(jnp.float32) * (hd ** -0.5)
    qi = jnp.arange(BLOCK)[:, None]
    si = jnp.arange(2 * BLOCK)[None, :] - BLOCK
    rel = qi - si
    band = (rel >= 0) & (rel < SWA_WINDOW)
    valid = (jnp.arange(n)[:, None, None] * BLOCK + si[None]) >= 0
    mask = band[None] & valid
    scores = jnp.where(mask[None, :, None, None], scores, NEG_INF)
    sink = sinks.astype(jnp.float32).reshape(hkv, g)[None, None, :, :, None, None]
    m = jnp.maximum(jnp.max(scores, axis=-1, keepdims=True), sink)
    p = jnp.exp(scores - m)
    probs = (p / (jnp.sum(p, axis=-1, keepdims=True) + jnp.exp(sink - m))).astype(v.dtype)
    out = jnp.einsum('bnkgqs,bnskd->bnqkgd', probs, vcat)
    return out.reshape(b, s, hq, hd)


def forgetting_attention(q, k, v, log_f):
    b, s, h, hd = q.shape
    n = s // BLOCK
    c = jnp.cumsum(log_f, axis=1).transpose(0, 2, 1)
    qb = q.reshape(b, n, BLOCK, h, hd).swapaxes(0, 1)
    cq = c.reshape(b, h, n, BLOCK).transpose(2, 0, 1, 3)
    kpos = jnp.arange(s)
    scale = hd ** -0.5

    def one_block(args):
        qi, ci, i = args
        logits = jnp.einsum('bqhd,bkhd->bhqk', qi, k).astype(jnp.float32) * scale
        logits = logits + ci[..., :, None] - c[:, :, None, :]
        qpos = i * BLOCK + jnp.arange(BLOCK)
        mask = kpos[None, :] <= qpos[:, None]
        logits = jnp.where(mask, logits, NEG_INF)
        probs = jax.nn.softmax(logits, axis=-1).astype(v.dtype)
        return jnp.einsum('bhqk,bkhd->bqhd', probs, v)

    out = lax.map(one_block, (qb, cq, jnp.arange(n)))
    return out.swapaxes(0, 1).reshape(b, s, h, hd)


def causal_depthwise_conv(u, w, bias):
    kw = w.astype(u.dtype)[:, None, :]
    out = lax.conv_general_dilated(u, kw, window_strides=(1,), padding=[(SSM_CONV - 1, 0)],
                                   dimension_numbers=('NWC', 'WIO', 'NWC'),
                                   feature_group_count=u.shape[-1])
    return out + bias.astype(u.dtype)


def ssd_chunked(x, dt, a, bmat, cmat):
    b, s, h, p = x.shape
    g, nst = bmat.shape[2], bmat.shape[3]
    hg = h // g
    L = SSM_CHUNK
    nc = s // L
    xs = (x.astype(jnp.float32) * dt[..., None]).reshape(b, nc, L, g, hg, p)
    acs = jnp.cumsum((dt * a.astype(jnp.float32)).reshape(b, nc, L, g, hg), axis=2)
    bm = bmat.astype(jnp.float32).reshape(b, nc, L, g, nst)
    cm = cmat.astype(jnp.float32).reshape(b, nc, L, g, nst)
    diff = acs[:, :, :, None] - acs[:, :, None, :]
    causal = jnp.tril(jnp.ones((L, L), dtype=bool))
    decay = jnp.exp(jnp.where(causal[:, :, None, None], diff, NEG_INF))
    cb = jnp.einsum('bclgn,bcsgn->bclsg', cm, bm)
    y_diag = jnp.einsum('bclsg,bclsgh,bcsghp->bclghp', cb, decay, xs)
    decay_st = jnp.exp(acs[:, :, -1:] - acs)
    states = jnp.einsum('bclgn,bclgh,bclghp->bcghpn', bm, decay_st, xs)
    chunk_decay = jnp.exp(acs[:, :, -1])

    def step(carry, inp):
        st, dec = inp
        return carry * dec[..., None, None] + st, carry

    init = jnp.zeros((b, g, hg, p, nst), jnp.float32)
    _, prev = lax.scan(step, init, (states.swapaxes(0, 1), chunk_decay.swapaxes(0, 1)))
    prev = prev.swapaxes(0, 1)
    y_off = jnp.einsum('bclgn,bcghpn,bclgh->bclghp', cm, prev, jnp.exp(acs))
    return (y_diag + y_off).reshape(b, s, h, p)


def hybrid_mixer(h, w_in, b_forget, sinks, conv_w, conv_b, dt_bias, a_log, d_skip,
                 ssm_norm_w, w_out, cos, sin):
    b, s, _ = h.shape
    proj = h @ w_in.astype(h.dtype)
    (q_a, k_a, v_a, g_a, q_b, k_b, v_b, f_b, g_b, z_c, xbc_c, dt_c) = jnp.split(
        proj, _offsets(IN_SPLITS), axis=-1)
    qa = apply_rope(q_a.reshape(b, s, SWA_HEADS, HEAD_DIM), cos, sin)
    ka = apply_rope(k_a.reshape(b, s, SWA_KV_HEADS, HEAD_DIM), cos, sin)
    va = v_a.reshape(b, s, SWA_KV_HEADS, HEAD_DIM)
    ya = sliding_window_attention(qa, ka, va, sinks).reshape(b, s, SWA_W)
    ya = ya * jax.nn.silu(g_a)
    log_f = jax.nn.log_sigmoid(f_b.astype(jnp.float32) + b_forget.astype(jnp.float32))
    yb = forgetting_attention(q_b.reshape(b, s, FOX_HEADS, HEAD_DIM),
                              k_b.reshape(b, s, FOX_HEADS, HEAD_DIM),
                              v_b.reshape(b, s, FOX_HEADS, HEAD_DIM), log_f).reshape(b, s, FOX_W)
    yb = yb * jax.nn.silu(g_b)
    xbc = jax.nn.silu(causal_depthwise_conv(xbc_c, conv_w, conv_b))
    xs, bm, cm = jnp.split(xbc, [SSM_W, SSM_W + SSM_BC_W], axis=-1)
    xs = xs.reshape(b, s, SSM_HEADS, SSM_HEAD_DIM)
    bm = bm.reshape(b, s, SSM_GROUPS, SSM_STATE)
    cm = cm.reshape(b, s, SSM_GROUPS, SSM_STATE)
    dt = jax.nn.softplus(dt_c.astype(jnp.float32) + dt_bias.astype(jnp.float32))
    a = -jnp.exp(a_log.astype(jnp.float32))
    yc = ssd_chunked(xs, dt, a, bm, cm) + d_skip.astype(jnp.float32)[:, None] * xs.astype(jnp.float32)
    yc = yc.reshape(b, s, SSM_W).astype(h.dtype)
    yc = rms_norm(yc * jax.nn.silu(z_c), ssm_norm_w)
    y = jnp.concatenate([ya, yb, yc], axis=-1)
    return y @ w_out.astype(h.dtype)


def memory_cross_attention(h, mem_n, wq, wk, wv, wo):
    b, s, _ = h.shape
    m = mem_n.shape[1]
    q = (h @ wq.astype(h.dtype)).reshape(b, s, MEM_HEADS, MEM_HEAD_DIM)
    k = (mem_n @ wk.astype(h.dtype)).reshape(b, m, MEM_HEADS, MEM_HEAD_DIM)
    v = (mem_n @ wv.astype(h.dtype)).reshape(b, m, MEM_HEADS, MEM_HEAD_DIM)
    logits = jnp.einsum('bqhd,bkhd->bhqk', q, k).astype(jnp.float32) * (MEM_HEAD_DIM ** -0.5)
    probs = jax.nn.softmax(logits, axis=-1).astype(v.dtype)
    out = jnp.einsum('bhqk,bkhd->bqhd', probs, v).reshape(b, s, D_MODEL)
    return out @ wo.astype(h.dtype)


def setup_inputs(seed: int = 0) -> dict:
    key = jax.random.key(seed)
    ks = jax.random.split(key, 20)
    f32 = jnp.float32

    def nrm(k, shape, scale):
        return jax.random.normal(k, shape, f32) * scale

    dt0 = jnp.exp(jax.random.uniform(ks[7], (DEPTH, SSM_HEADS), f32, np.log(1e-3), np.log(1e-1)))
    return {
        "x": nrm(ks[0], (BATCH, SEQ, D_MODEL), 1.0),
        "mem": nrm(ks[1], (BATCH, MEM_TOKENS, D_MODEL), 1.0),
        "norm_mix_w": 1.0 + nrm(ks[2], (DEPTH, D_MODEL), 0.02),
        "w_in": nrm(ks[3], (DEPTH, D_MODEL, IN_W), D_MODEL ** -0.5),
        "b_forget": jax.random.uniform(ks[4], (DEPTH, FOX_HEADS), f32, 1.0, 6.0),
        "swa_sinks": nrm(ks[5], (DEPTH, SWA_HEADS), 0.5),
        "conv_w": nrm(ks[6], (DEPTH, SSM_CONV, SSM_CONV_W), SSM_CONV ** -0.5),
        "conv_b": nrm(ks[8], (DEPTH, SSM_CONV_W), 0.02),
        "dt_bias": dt0 + jnp.log(-jnp.expm1(-dt0)),
        "a_log": jnp.log(jax.random.uniform(ks[9], (DEPTH, SSM_HEADS), f32, 1.0, 16.0)),
        "d_skip": 1.0 + nrm(ks[10], (DEPTH, SSM_HEADS), 0.1),
        "ssm_norm_w": 1.0 + nrm(ks[11], (DEPTH, SSM_W), 0.02),
        "w_out": nrm(ks[12], (DEPTH, MIX_WIDTH, D_MODEL), MIX_WIDTH ** -0.5),
        "norm_xq_w": 1.0 + nrm(ks[13], (DEPTH, D_MODEL), 0.02),
        "norm_mem_w": 1.0 + nrm(ks[14], (DEPTH, D_MODEL), 0.02),
        "w_mq": nrm(ks[15], (DEPTH, D_MODEL, D_MODEL), D_MODEL ** -0.5),
        "w_mk": nrm(ks[16], (DEPTH, D_MODEL, D_MODEL), D_MODEL ** -0.5),
        "w_mv": nrm(ks[17], (DEPTH, D_MODEL, D_MODEL), D_MODEL ** -0.5),
        "w_mo": nrm(ks[18], (DEPTH, D_MODEL, D_MODEL), D_MODEL ** -0.5),
        "final_norm_w": 1.0 + nrm(ks[19], (D_MODEL,), 0.02),
    }


def reference(x, mem, norm_mix_w, w_in, b_forget, swa_sinks, conv_w, conv_b, dt_bias, a_log,
              d_skip, ssm_norm_w, w_out, norm_xq_w, norm_mem_w, w_mq, w_mk, w_mv, w_mo,
              final_norm_w):
    cos, sin = rope_tables(x.shape[1])
    for l in range(DEPTH):
        h = rms_norm(x, norm_mix_w[l])
        x = x + hybrid_mixer(h, w_in[l], b_forget[l], swa_sinks[l], conv_w[l], conv_b[l],
                             dt_bias[l], a_log[l], d_skip[l], ssm_norm_w[l], w_out[l], cos, sin)
        hq = rms_norm(x, norm_xq_w[l])
        mem_n = rms_norm(mem, norm_mem_w[l])
        x = x + memory_cross_attention(hq, mem_n, w_mq[l], w_mk[l], w_mv[l], w_mo[l])
    return rms_norm(x, final_norm_w)
```

```python
import functools

import jax
import jax.numpy as jnp
from jax import lax
from jax.experimental import pallas as pl
from jax.experimental.pallas import tpu as pltpu

D_MODEL = 1024
HEAD_DIM = 64
BLOCK = 128
SWA_HEADS = 8
SWA_KV_HEADS = 2
FOX_HEADS = 8
SSM_HEADS = 16
SSM_STATE = 128
SSM_CONV = 4
MEM_HEADS = 4
MEM_HEAD_DIM = 256
ROPE_THETA = 10000.0
EPS = 1e-6
NEG_INF = -1e30

SWA_W = SWA_HEADS * HEAD_DIM
SWA_KV_W = SWA_KV_HEADS * HEAD_DIM
FOX_W = FOX_HEADS * HEAD_DIM
SSM_W = SSM_HEADS * HEAD_DIM
SSM_BC_W = 2 * SSM_STATE
SSM_CONV_W = SSM_W + 2 * SSM_BC_W
LANES = 128
F_LANE0 = 0
DT_LANE0 = FOX_HEADS

_O_QA = 0
_O_KA = _O_QA + SWA_W
_O_VA = _O_KA + SWA_KV_W
_O_GA = _O_VA + SWA_KV_W
_O_QB = _O_GA + SWA_W
_O_KB = _O_QB + FOX_W
_O_VB = _O_KB + FOX_W
_O_FB = _O_VB + FOX_W
_O_GB = _O_FB + FOX_HEADS
_O_ZC = _O_GB + FOX_W
_O_XBC = _O_ZC + SSM_W
_O_DT = _O_XBC + SSM_CONV_W
_IN_W = _O_DT + SSM_HEADS

_SEG_W = (SWA_W, 2 * SWA_KV_W, 2 * SWA_KV_W, 2 * SWA_W, FOX_W, FOX_W, FOX_W, SSM_W, SSM_CONV_W, LANES)
_SEG_OFF = tuple(sum(_SEG_W[:i]) for i in range(len(_SEG_W)))
_PROJ_W = sum(_SEG_W)

_VMEM_LIMIT = 56 * 1024 * 1024


def _bdot(a, b):
    return jnp.dot(a, b, preferred_element_type=jnp.float32)


def _bdot_nt(a, b):
    return lax.dot_general(a, b, (((1,), (1,)), ((), ())), preferred_element_type=jnp.float32)


def _silu(x):
    return x * (1.0 / (1.0 + jnp.exp(-x)))


def _softplus(x):
    return jnp.maximum(x, 0.0) + jnp.log1p(jnp.exp(-jnp.abs(x)))


def _rms(x, w):
    return x * lax.rsqrt(jnp.mean(x * x, axis=-1, keepdims=True) + EPS) * w


def _rope(x, cos, sin_signed):
    width = x.shape[1]
    reps = width // LANES
    lane = lax.broadcasted_iota(jnp.int32, x.shape, 1)
    first_half = (lane % HEAD_DIM) < (HEAD_DIM // 2)
    swapped = jnp.where(first_half,
                        pltpu.roll(x, width - HEAD_DIM // 2, 1),
                        pltpu.roll(x, HEAD_DIM // 2, 1))
    cos_t = jnp.concatenate([cos] * reps, axis=1)
    sin_t = jnp.concatenate([sin_signed] * reps, axis=1)
    return x * cos_t + swapped * sin_t


def _inproj_kernel(x_ref, nw_ref, w_ref, cos_ref, sin_ref,
                   qa_ref, ka_ref, va_ref, g_ref, qb_ref, kb_ref, vb_ref, z_ref, xbc_ref, fdt_ref):
    h = _rms(x_ref[...], nw_ref[...]).astype(jnp.bfloat16)

    def seg(i):
        return _bdot(h, w_ref[:, _SEG_OFF[i]:_SEG_OFF[i] + _SEG_W[i]])

    cos = cos_ref[...]
    sin = sin_ref[...]
    scale = HEAD_DIM ** -0.5
    qa_ref[...] = (_rope(seg(0), cos, sin) * scale).astype(qa_ref.dtype)
    ka_ref[...] = _rope(seg(1), cos, sin).astype(ka_ref.dtype)
    va_ref[...] = seg(2).astype(va_ref.dtype)
    g_ref[...] = seg(3)
    qb_ref[...] = (seg(4) * scale).astype(qb_ref.dtype)
    kb_ref[...] = seg(5).astype(kb_ref.dtype)
    vb_ref[...] = seg(6).astype(vb_ref.dtype)
    z_ref[...] = seg(7)
    xbc_ref[...] = seg(8)
    fdt_ref[...] = seg(9)


def _arrange_w_in(w_in):
    def cols(o, n):
        return w_in[:, o:o + n]
    k0, k1 = cols(_O_KA, HEAD_DIM), cols(_O_KA + HEAD_DIM, HEAD_DIM)
    v0, v1 = cols(_O_VA, HEAD_DIM), cols(_O_VA + HEAD_DIM, HEAD_DIM)
    pad = jnp.zeros((w_in.shape[0], LANES - FOX_HEADS - SSM_HEADS), w_in.dtype)
    parts = [cols(_O_QA, SWA_W),
             k0, k1, k1, k0,
             v0, v1, v1, v0,
             cols(_O_GA, SWA_W), cols(_O_GB, FOX_W),
             cols(_O_QB, FOX_W), cols(_O_KB, FOX_W), cols(_O_VB, FOX_W),
             cols(_O_ZC, SSM_W), cols(_O_XBC, SSM_CONV_W),
             cols(_O_FB, FOX_HEADS), cols(_O_DT, SSM_HEADS), pad]
    return jnp.concatenate(parts, axis=1).astype(jnp.bfloat16)


def _inproj(x2, norm_w, w_r, cos, sin, seq, tm):
    t = x2.shape[0]
    nblk_s = seq // tm
    row = lambda i: (i, 0)
    const = lambda i: (0, 0)
    pos = lambda i: (i % nblk_s, 0)
    bf, f32 = jnp.bfloat16, jnp.float32
    outs = [(SWA_W, bf), (2 * SWA_KV_W, bf), (2 * SWA_KV_W, bf), (2 * SWA_W, f32), (FOX_W, bf), (FOX_W, bf),
            (FOX_W, bf), (SSM_W, f32), (SSM_CONV_W, f32), (LANES, f32)]
    return pl.pallas_call(
        _inproj_kernel,
        grid=(t // tm,),
        in_specs=[pl.BlockSpec((tm, D_MODEL), row),
                  pl.BlockSpec((1, D_MODEL), const),
                  pl.BlockSpec((D_MODEL, _PROJ_W), const, pipeline_mode=pl.Buffered(1)),
                  pl.BlockSpec((tm, LANES), pos),
                  pl.BlockSpec((tm, LANES), pos)],
        out_specs=[pl.BlockSpec((tm, w), row) for w, _ in outs],
        out_shape=[jax.ShapeDtypeStruct((t, w), d) for w, d in outs],
        compiler_params=pltpu.CompilerParams(dimension_semantics=("arbitrary",),
                                             vmem_limit_bytes=_VMEM_LIMIT),
        name="inproj",
    )(x2, norm_w.reshape(1, D_MODEL), w_r, cos, sin)


def _rope_tables(seq):
    pos = jnp.arange(seq, dtype=jnp.float32)
    inv = 1.0 / (ROPE_THETA ** (jnp.arange(0, HEAD_DIM, 2, dtype=jnp.float32) / HEAD_DIM))
    ang = pos[:, None] * inv[None, :]
    cos, sin = jnp.cos(ang), jnp.sin(ang)
    cos_t = jnp.concatenate([cos, cos, cos, cos], axis=1)
    sin_t = jnp.concatenate([-sin, sin, -sin, sin], axis=1)
    return cos_t, sin_t


def _swa_kernel(sink_ref, q_ref, kc_ref, kp_ref, vc_ref, vp_ref, g_ref, o_ref):
    n = pl.program_id(1)
    q = q_ref[...]
    kcat = jnp.concatenate([kp_ref[...], kc_ref[...]], axis=0)
    vcat = jnp.concatenate([vp_ref[...], vc_ref[...]], axis=0)
    qi = lax.broadcasted_iota(jnp.int32, (BLOCK, 2 * BLOCK), 0)
    sj = lax.broadcasted_iota(jnp.int32, (BLOCK, 2 * BLOCK), 1)
    rel = qi - (sj - BLOCK)
    band = (rel >= 0) & (rel < BLOCK)
    prev_ok = jnp.where(n > 0, 0.0, NEG_INF)
    first_block_pen = jnp.where(sj < BLOCK, prev_ok, 0.0)
    lane = lax.broadcasted_iota(jnp.int32, (BLOCK, LANES), 1)
    low = lane < HEAD_DIM
    cols = []
    for c in range(SWA_HEADS // 2):
        qc = q[:, c * LANES:(c + 1) * LANES]
        halves = []
        for half in range(2):
            head = 2 * c + half
            kv = head // (SWA_HEADS // SWA_KV_HEADS)
            sel = 0 if kv == half else 1
            kk = kcat[:, sel * LANES:(sel + 1) * LANES]
            vv = vcat[:, sel * LANES:(sel + 1) * LANES]
            qh = jnp.where(low if half == 0 else ~low, qc, jnp.zeros_like(qc))
            s = _bdot_nt(qh, kk)
            s = jnp.where(band, s, NEG_INF) + first_block_pen
            sink = sink_ref[0, head]
            m = jnp.maximum(jnp.max(s, axis=-1, keepdims=True), sink)
            p = jnp.exp(s - m)
            denom = jnp.sum(p, axis=-1, keepdims=True) + jnp.exp(sink - m)
            probs = (p / denom).astype(jnp.bfloat16)
            halves.append(_bdot(probs, vv))
        cols.append(jnp.where(low, halves[0], halves[1]))
    y = jnp.concatenate(cols, axis=1)
    o_ref[...] = (y * _silu(g_ref[...])).astype(o_ref.dtype)


def _swa(qa, ka, va, g, sinks, batch, seq):
    n = seq // BLOCK
    cur = lambda b, i: (b * n + i, 0)
    prev = lambda b, i: (b * n + jnp.maximum(i - 1, 0), 0)
    return pl.pallas_call(
        _swa_kernel,
        grid=(batch, n),
        in_specs=[pl.BlockSpec(memory_space=pltpu.SMEM),
                  pl.BlockSpec((BLOCK, SWA_W), cur),
                  pl.BlockSpec((BLOCK, 2 * SWA_KV_W), cur),
                  pl.BlockSpec((BLOCK, 2 * SWA_KV_W), prev),
                  pl.BlockSpec((BLOCK, 2 * SWA_KV_W), cur),
                  pl.BlockSpec((BLOCK, 2 * SWA_KV_W), prev),
                  pl.BlockSpec((BLOCK, SWA_W), cur)],
        out_specs=pl.BlockSpec((BLOCK, SWA_W), cur),
        out_shape=jax.ShapeDtypeStruct((batch * seq, SWA_W), jnp.bfloat16),
        compiler_params=pltpu.CompilerParams(dimension_semantics=("arbitrary", "arbitrary")),
        name="swa",
    )(sinks.reshape(1, SWA_HEADS).astype(jnp.float32), qa, ka, ka, va, va, g)


def _fox_kernel(q_ref, k_ref, v_ref, c_ref, g_ref, o_ref, *, tq):
    i = pl.program_id(2)
    q = q_ref[...]
    lane = lax.broadcasted_iota(jnp.int32, (tq, LANES), 1)
    low = lane < HEAD_DIM
    zero = jnp.zeros_like(q)
    qs = (jnp.where(low, q, zero), jnp.where(low, zero, q))
    q0 = pl.multiple_of(i * tq, tq)
    c0 = [c_ref[0, 0, hh:hh + 1, pl.ds(q0, LANES)][:, 0:1] for hh in range(2)]

    def tile(j, carry, masked):
        k0 = pl.multiple_of(j * tq, tq)
        kt = k_ref[pl.ds(k0, tq), :]
        vt = v_ref[pl.ds(k0, tq), :]
        new = []
        for hh in range(2):
            m_prev, l_prev, acc_prev = carry[hh]
            bias = c0[hh] - c_ref[0, 0, hh:hh + 1, pl.ds(k0, tq)]
            s = _bdot_nt(qs[hh], kt) + bias
            if masked:
                r = lax.broadcasted_iota(jnp.int32, (tq, tq), 0)
                cidx = lax.broadcasted_iota(jnp.int32, (tq, tq), 1)
                s = jnp.where(cidx <= r, s, NEG_INF)
            m_new = jnp.maximum(m_prev, jnp.max(s, axis=-1, keepdims=True))
            alpha = jnp.exp(m_prev - m_new)
            p = jnp.exp(s - m_new)
            l_new = alpha * l_prev + jnp.sum(p, axis=-1, keepdims=True)
            acc_new = alpha * acc_prev + _bdot(p.astype(jnp.bfloat16), vt)
            new.append((m_new, l_new, acc_new))
        return tuple(new)

    init = tuple((jnp.full((tq, 1), NEG_INF, jnp.float32),
                  jnp.zeros((tq, 1), jnp.float32),
                  jnp.zeros((tq, LANES), jnp.float32)) for _ in range(2))
    carry = lax.fori_loop(0, i, lambda j, c: tile(j, c, False), init)
    carry = tile(i, carry, True)
    outs = [carry[hh][2] / carry[hh][1] for hh in range(2)]
    y = jnp.where(low, outs[0], outs[1])
    o_ref[...] = (y * _silu(g_ref[...])).astype(o_ref.dtype)


def _fox(qb, kb, vb, c_t, g, batch, seq, tq):
    nq = seq // tq
    npair = FOX_HEADS // 2
    gcol0 = SWA_W // LANES
    return pl.pallas_call(
        functools.partial(_fox_kernel, tq=tq),
        grid=(batch, npair, nq),
        in_specs=[pl.BlockSpec((tq, LANES), lambda b, h, i: (b * nq + i, h)),
                  pl.BlockSpec((seq, LANES), lambda b, h, i: (b, h)),
                  pl.BlockSpec((seq, LANES), lambda b, h, i: (b, h)),
                  pl.BlockSpec((1, 1, 2, seq), lambda b, h, i: (b, h, 0, 0)),
                  pl.BlockSpec((tq, LANES), lambda b, h, i: (b * nq + i, gcol0 + h))],
        out_specs=pl.BlockSpec((tq, LANES), lambda b, h, i: (b * nq + i, h)),
        out_shape=jax.ShapeDtypeStruct((batch * seq, FOX_W), jnp.bfloat16),
        compiler_params=pltpu.CompilerParams(dimension_semantics=("arbitrary", "arbitrary", "arbitrary"),
                                             vmem_limit_bytes=_VMEM_LIMIT),
        name="fox",
    )(qb, kb, vb, c_t.reshape(batch, npair, 2, seq), g)


def _split3(x):
    hi = x.astype(jnp.bfloat16)
    r1 = x - hi.astype(jnp.float32)
    mid = r1.astype(jnp.bfloat16)
    lo = (r1 - mid.astype(jnp.float32)).astype(jnp.bfloat16)
    return hi, mid, lo


def _ssd_kernel(xbc_ref, z_ref, fdt_ref, cw_ref, cb_ref, bias_ref, alog_ref, dskip_ref, nw_ref,
                y_ref, ct_ref, state_ref, tail_ref, ccarry_ref):
    ci = pl.program_id(1)
    L = BLOCK

    @pl.when(ci == 0)
    def _():
        state_ref[...] = jnp.zeros_like(state_ref)
        tail_ref[...] = jnp.zeros_like(tail_ref)
        ccarry_ref[...] = jnp.zeros_like(ccarry_ref)

    u = xbc_ref[...]
    tail = tail_ref[...]
    row8 = lax.broadcasted_iota(jnp.int32, (8, SSM_CONV_W), 0)
    conv = cb_ref[...] + cw_ref[SSM_CONV - 1:SSM_CONV, :] * u
    for k in range(1, SSM_CONV):
        r = pltpu.roll(u, k, 0)
        first = jnp.where(row8 < k, pltpu.roll(tail, k, 0), r[0:8])
        uk = jnp.concatenate([first, r[8:]], axis=0)
        conv = conv + cw_ref[SSM_CONV - 1 - k:SSM_CONV - k, :] * uk
    tail_ref[...] = u[L - 8:L]
    act = _silu(conv)
    xs = act[:, :SSM_W]
    bm = act[:, SSM_W:SSM_W + SSM_BC_W].astype(jnp.bfloat16)
    cm = act[:, SSM_W + SSM_BC_W:].astype(jnp.bfloat16)

    lane = lax.broadcasted_iota(jnp.int32, (L, LANES), 1)
    vals = fdt_ref[...] + bias_ref[...]
    is_f = lane < DT_LANE0
    is_dt = (lane >= DT_LANE0) & (lane < DT_LANE0 + SSM_HEADS)
    sp = _softplus(jnp.where(is_f, -vals, vals))
    a_row = jnp.where(is_dt[0:1], -jnp.exp(alog_ref[...]), 0.0)
    dt = jnp.where(is_dt, sp, 0.0)
    scan_in = jnp.where(is_f, -sp, dt * a_row)

    ri = lax.broadcasted_iota(jnp.int32, (L, L), 0)
    cj = lax.broadcasted_iota(jnp.int32, (L, L), 1)
    causal = cj <= ri
    tri = jnp.where(causal, 1.0, 0.0).astype(jnp.bfloat16)
    hi, mid, lo = _split3(scan_in)
    cs = _bdot(tri, hi) + _bdot(tri, mid) + _bdot(tri, lo)
    cs_t = cs.T

    c_t = cs_t[0:FOX_HEADS, :] + ccarry_ref[...]
    ct_ref[0] = c_t
    ccarry_ref[...] = jnp.broadcast_to(c_t[:, L - 1:L], (FOX_HEADS, L))

    rowp = lax.broadcasted_iota(jnp.int32, (L, LANES), 0)
    low_l = lane < HEAD_DIM
    low_r = rowp < HEAD_DIM
    hpg = SSM_HEADS // 2
    ys = []
    cb = [None, None]
    for pair in range(SSM_HEADS // 2):
        grp = (2 * pair) // hpg
        bm_g = bm[:, grp * SSM_STATE:(grp + 1) * SSM_STATE]
        cm_g = cm[:, grp * SSM_STATE:(grp + 1) * SSM_STATE]
        if cb[grp] is None:
            cb[grp] = _bdot_nt(cm_g, bm_g)
        x_pair = xs[:, pair * LANES:(pair + 1) * LANES]
        ha, hb = DT_LANE0 + 2 * pair, DT_LANE0 + 2 * pair + 1
        acs_col = (cs[:, ha:ha + 1], cs[:, hb:hb + 1])
        acs_row = (cs_t[ha:ha + 1, :], cs_t[hb:hb + 1, :])
        dt_pair = jnp.where(low_l, dt[:, ha:ha + 1], dt[:, hb:hb + 1])
        xdt = x_pair * dt_pair
        xdt_b = xdt.astype(jnp.bfloat16)
        y_diag = []
        for hh in range(2):
            diff = acs_col[hh] - acs_row[hh]
            decay = jnp.exp(jnp.where(causal, diff, NEG_INF))
            gmat = (cb[grp] * decay).astype(jnp.bfloat16)
            y_diag.append(_bdot(gmat, xdt_b))
        y_pair = jnp.where(low_l, y_diag[0], y_diag[1])
        prev = state_ref[pair]
        y_off = _bdot_nt(cm_g, prev.astype(jnp.bfloat16))
        y_pair = y_pair + y_off * jnp.exp(jnp.where(low_l, acs_col[0], acs_col[1]))
        last = (acs_col[0][L - 1:L, :], acs_col[1][L - 1:L, :])
        dst = jnp.exp(jnp.where(low_l, last[0] - acs_col[0], last[1] - acs_col[1]))
        st_new = _bdot((xdt * dst).T.astype(jnp.bfloat16), bm_g)
        cdec = jnp.exp(jnp.where(low_r, last[0], last[1]))
        state_ref[pair] = prev * cdec + st_new
        ys.append(y_pair)
    y = jnp.concatenate(ys, axis=1) + dskip_ref[...] * xs
    y_ref[...] = _rms(y * _silu(z_ref[...]), nw_ref[...]).astype(y_ref.dtype)


def _ssd(xbc, z, fdt, conv_w, conv_b, b_forget, dt_bias, a_log, d_skip, norm_w, batch, seq):
    nc = seq // BLOCK
    row = lambda b, c: (b * nc + c, 0)
    const = lambda b, c: (0, 0)
    zpad = jnp.zeros((LANES - FOX_HEADS - SSM_HEADS,), jnp.float32)
    bias_row = jnp.concatenate([b_forget, dt_bias, zpad]).reshape(1, LANES)
    alog_row = jnp.concatenate([jnp.zeros((FOX_HEADS,), jnp.float32), a_log, zpad]).reshape(1, LANES)
    dskip_row = jnp.repeat(d_skip, HEAD_DIM).reshape(1, SSM_W)
    return pl.pallas_call(
        _ssd_kernel,
        grid=(batch, nc),
        in_specs=[pl.BlockSpec((BLOCK, SSM_CONV_W), row),
                  pl.BlockSpec((BLOCK, SSM_W), row),
                  pl.BlockSpec((BLOCK, LANES), row),
                  pl.BlockSpec((SSM_CONV, SSM_CONV_W), const),
                  pl.BlockSpec((1, SSM_CONV_W), const),
                  pl.BlockSpec((1, LANES), const),
                  pl.BlockSpec((1, LANES), const),
                  pl.BlockSpec((1, SSM_W), const),
                  pl.BlockSpec((1, SSM_W), const)],
        out_specs=[pl.BlockSpec((BLOCK, SSM_W), row),
                   pl.BlockSpec((1, FOX_HEADS, BLOCK), lambda b, c: (b, 0, c))],
        out_shape=[jax.ShapeDtypeStruct((batch * seq, SSM_W), jnp.bfloat16),
                   jax.ShapeDtypeStruct((batch, FOX_HEADS, seq), jnp.float32)],
        scratch_shapes=[pltpu.VMEM((SSM_HEADS // 2, 2 * HEAD_DIM, SSM_STATE), jnp.float32),
                        pltpu.VMEM((8, SSM_CONV_W), jnp.float32),
                        pltpu.VMEM((FOX_HEADS, BLOCK), jnp.float32)],
        compiler_params=pltpu.CompilerParams(dimension_semantics=("arbitrary", "arbitrary")),
        name="ssd",
    )(xbc, z, fdt, conv_w, conv_b.reshape(1, SSM_CONV_W), bias_row, alog_row, dskip_row,
      norm_w.reshape(1, SSM_W))


def _memkv_kernel(mem_ref, nw_ref, wk_ref, wv_ref, k_ref, v_ref):
    mn = _rms(mem_ref[...], nw_ref[...]).astype(jnp.bfloat16)
    k_ref[...] = _bdot(mn, wk_ref[...]).astype(k_ref.dtype)
    v_ref[...] = _bdot(mn, wv_ref[...]).astype(v_ref.dtype)


def _memkv(mem2, norm_w, wk, wv, mem_tokens):
    t = mem2.shape[0]
    row = lambda b: (b, 0)
    const = lambda b: (0, 0)
    return pl.pallas_call(
        _memkv_kernel,
        grid=(t // mem_tokens,),
        in_specs=[pl.BlockSpec((mem_tokens, D_MODEL), row),
                  pl.BlockSpec((1, D_MODEL), const),
                  pl.BlockSpec((D_MODEL, D_MODEL), const),
                  pl.BlockSpec((D_MODEL, D_MODEL), const)],
        out_specs=[pl.BlockSpec((mem_tokens, D_MODEL), row)] * 2,
        out_shape=[jax.ShapeDtypeStruct((t, D_MODEL), jnp.bfloat16)] * 2,
        compiler_params=pltpu.CompilerParams(dimension_semantics=("arbitrary",)),
        name="memkv",
    )(mem2, norm_w.reshape(1, D_MODEL), wk.astype(jnp.bfloat16), wv.astype(jnp.bfloat16))


def _out_kernel(x_ref, ya_ref, yb_ref, yc_ref, wo_ref, nq_ref, wq_ref, k_ref, v_ref, wmo_ref, fn_ref,
                o_ref, *, final_norm):
    x1 = (x_ref[...]
          + _bdot(ya_ref[...], wo_ref[0:SWA_W, :])
          + _bdot(yb_ref[...], wo_ref[SWA_W:SWA_W + FOX_W, :])
          + _bdot(yc_ref[...], wo_ref[SWA_W + FOX_W:, :]))
    hq = _rms(x1, nq_ref[...]).astype(jnp.bfloat16)
    q = (_bdot(hq, wq_ref[...]) * (MEM_HEAD_DIM ** -0.5)).astype(jnp.bfloat16)
    heads = []
    for h in range(MEM_HEADS):
        sl = slice(h * MEM_HEAD_DIM, (h + 1) * MEM_HEAD_DIM)
        s = _bdot_nt(q[:, sl], k_ref[:, sl])
        m = jnp.max(s, axis=-1, keepdims=True)
        p = jnp.exp(s - m)
        probs = (p / jnp.sum(p, axis=-1, keepdims=True)).astype(jnp.bfloat16)
        heads.append(_bdot(probs, v_ref[:, sl]).astype(jnp.bfloat16))
    att = jnp.concatenate(heads, axis=1)
    x2 = x1 + _bdot(att, wmo_ref[...])
    if final_norm:
        x2 = _rms(x2, fn_ref[...])
    o_ref[...] = x2


def _out_block(x2, ya, yb, yc, w_out, norm_xq_w, w_mq, kmem, vmem, w_mo, final_w, seq, mem_tokens, tm,
               final_norm):
    t = x2.shape[0]
    nblk_s = seq // tm
    row = lambda i: (i, 0)
    const = lambda i: (0, 0)
    memrow = lambda i: (i // nblk_s, 0)
    bf = jnp.bfloat16
    return pl.pallas_call(
        functools.partial(_out_kernel, final_norm=final_norm),
        grid=(t // tm,),
        in_specs=[pl.BlockSpec((tm, D_MODEL), row),
                  pl.BlockSpec((tm, SWA_W), row),
                  pl.BlockSpec((tm, FOX_W), row),
                  pl.BlockSpec((tm, SSM_W), row),
                  pl.BlockSpec((SWA_W + FOX_W + SSM_W, D_MODEL), const),
                  pl.BlockSpec((1, D_MODEL), const),
                  pl.BlockSpec((D_MODEL, D_MODEL), const),
                  pl.BlockSpec((mem_tokens, D_MODEL), memrow),
                  pl.BlockSpec((mem_tokens, D_MODEL), memrow),
                  pl.BlockSpec((D_MODEL, D_MODEL), const),
                  pl.BlockSpec((1, D_MODEL), const)],
        out_specs=pl.BlockSpec((tm, D_MODEL), row),
        out_shape=jax.ShapeDtypeStruct((t, D_MODEL), jnp.float32),
        compiler_params=pltpu.CompilerParams(dimension_semantics=("arbitrary",),
                                             vmem_limit_bytes=_VMEM_LIMIT),
        name="outproj_mem",
    )(x2, ya, yb, yc, w_out.astype(bf), norm_xq_w.reshape(1, D_MODEL), w_mq.astype(bf), kmem, vmem,
      w_mo.astype(bf), final_w.reshape(1, D_MODEL))


def _tiles(seq):
    tm = min(512, seq)
    tq = min(512, seq)
    return tm, tq


def kernel(x, mem, norm_mix_w, w_in, b_forget, swa_sinks, conv_w, conv_b, dt_bias, a_log, d_skip, ssm_norm_w, w_out, norm_xq_w, norm_mem_w, w_mq, w_mk, w_mv, w_mo, final_norm_w):
    batch, seq, _ = x.shape
    mem_tokens = mem.shape[1]
    depth = w_in.shape[0]
    assert seq % BLOCK == 0
    tm, tq = _tiles(seq)
    assert seq % tm == 0 and seq % tq == 0
    cos, sin = _rope_tables(seq)
    x2 = x.reshape(batch * seq, D_MODEL)
    mem2 = mem.reshape(batch * mem_tokens, D_MODEL)
    for l in range(depth):
        w_r = _arrange_w_in(w_in[l])
        qa, ka, va, g, qb, kb, vb, z, xbc, fdt = _inproj(x2, norm_mix_w[l], w_r, cos, sin, seq, tm)
        yc, c_t = _ssd(xbc, z, fdt, conv_w[l], conv_b[l], b_forget[l], dt_bias[l], a_log[l], d_skip[l],
                       ssm_norm_w[l], batch, seq)
        ya = _swa(qa, ka, va, g, swa_sinks[l], batch, seq)
        yb = _fox(qb, kb, vb, c_t, g, batch, seq, tq)
        kmem, vmem = _memkv(mem2, norm_mem_w[l], w_mk[l], w_mv[l], mem_tokens)
        x2 = _out_block(x2, ya, yb, yc, w_out[l], norm_xq_w[l], w_mq[l], kmem, vmem, w_mo[l],
                        final_norm_w, seq, mem_tokens, tm, final_norm=(l == depth - 1))
    return x2.reshape(batch, seq, D_MODEL)
```

```python
import functools

import jax
import jax.numpy as jnp
from jax import lax
from jax.experimental import pallas as pl
from jax.experimental.pallas import tpu as pltpu

D_MODEL = 1024
HEAD_DIM = 64
BLOCK = 128
SWA_HEADS = 8
SWA_KV_HEADS = 2
FOX_HEADS = 8
SSM_HEADS = 16
SSM_STATE = 128
SSM_CONV = 4
MEM_HEADS = 4
MEM_HEAD_DIM = 256
ROPE_THETA = 10000.0
EPS = 1e-6
NEG_INF = -1e30

SWA_W = SWA_HEADS * HEAD_DIM
SWA_KV_W = SWA_KV_HEADS * HEAD_DIM
FOX_W = FOX_HEADS * HEAD_DIM
SSM_W = SSM_HEADS * HEAD_DIM
SSM_BC_W = 2 * SSM_STATE
SSM_CONV_W = SSM_W + 2 * SSM_BC_W
LANES = 128
LOG2E = 1.4426950408889634
FOX_TQ = 512
FOX_TK = 512
DT_LANE0 = FOX_HEADS

_O_QA = 0
_O_KA = _O_QA + SWA_W
_O_VA = _O_KA + SWA_KV_W
_O_GA = _O_VA + SWA_KV_W
_O_QB = _O_GA + SWA_W
_O_KB = _O_QB + FOX_W
_O_VB = _O_KB + FOX_W
_O_FB = _O_VB + FOX_W
_O_GB = _O_FB + FOX_HEADS
_O_ZC = _O_GB + FOX_W
_O_XBC = _O_ZC + SSM_W
_O_DT = _O_XBC + SSM_CONV_W
_IN_W = _O_DT + SSM_HEADS

_SEG_W = (SWA_W, 2 * SWA_KV_W, 2 * SWA_KV_W, 2 * SWA_W, FOX_W, SSM_W, SSM_CONV_W, LANES)
_SEG_OFF = tuple(sum(_SEG_W[:i]) for i in range(len(_SEG_W)))
_PROJ_W = sum(_SEG_W)

_VMEM_LIMIT = 56 * 1024 * 1024


def _bdot(a, b):
    return jnp.dot(a, b, preferred_element_type=jnp.float32)


def _bdot_nt(a, b):
    return lax.dot_general(a, b, (((1,), (1,)), ((), ())), preferred_element_type=jnp.float32)


def _silu(x):
    return x * (1.0 / (1.0 + jnp.exp(-x)))


def _softplus(x):
    return jnp.maximum(x, 0.0) + jnp.log1p(jnp.exp(-jnp.abs(x)))


def _rms(x, w):
    return x * lax.rsqrt(jnp.mean(x * x, axis=-1, keepdims=True) + EPS) * w


def _rope(x, cos, sin_signed):
    width = x.shape[1]
    reps = width // LANES
    lane = lax.broadcasted_iota(jnp.int32, x.shape, 1)
    first_half = (lane % HEAD_DIM) < (HEAD_DIM // 2)
    swapped = jnp.where(first_half,
                        pltpu.roll(x, width - HEAD_DIM // 2, 1),
                        pltpu.roll(x, HEAD_DIM // 2, 1))
    cos_t = jnp.concatenate([cos] * reps, axis=1)
    sin_t = jnp.concatenate([sin_signed] * reps, axis=1)
    return x * cos_t + swapped * sin_t


def _inproj_kernel(x_ref, nw_ref, w_ref, wt_ref, cos_ref, sin_ref,
                   qa_ref, ka_ref, va_ref, g_ref, kb_ref, z_ref, xbc_ref, fdt_ref, qbt_ref, vbt_ref):
    h = _rms(x_ref[...], nw_ref[...]).astype(jnp.bfloat16)

    def seg(i):
        return _bdot(h, w_ref[:, _SEG_OFF[i]:_SEG_OFF[i] + _SEG_W[i]])

    cos = cos_ref[...]
    sin = sin_ref[...]
    scale = HEAD_DIM ** -0.5
    qa_ref[...] = (_rope(seg(0), cos, sin) * scale).astype(qa_ref.dtype)
    ka_ref[...] = _rope(seg(1), cos, sin).astype(ka_ref.dtype)
    va_ref[...] = seg(2).astype(va_ref.dtype)
    g_ref[...] = seg(3)
    kb_ref[...] = seg(4).astype(kb_ref.dtype)
    z_ref[...] = seg(5)
    xbc_ref[...] = seg(6)
    fdt_ref[...] = seg(7)
    qbt_ref[...] = (_bdot_nt(wt_ref[0:FOX_W, :], h) * (scale * LOG2E)).astype(qbt_ref.dtype)
    vbt = _bdot_nt(wt_ref[FOX_W:, :], h).astype(vbt_ref.dtype)
    for hp in range(FOX_W // LANES):
        for c in range(vbt.shape[1] // FOX_TK):
            vbt_ref[hp, c] = vbt[hp * LANES:(hp + 1) * LANES, c * FOX_TK:(c + 1) * FOX_TK]


def _arrange_w_in(w_in):
    def cols(o, n):
        return w_in[:, o:o + n]
    k0, k1 = cols(_O_KA, HEAD_DIM), cols(_O_KA + HEAD_DIM, HEAD_DIM)
    v0, v1 = cols(_O_VA, HEAD_DIM), cols(_O_VA + HEAD_DIM, HEAD_DIM)
    pad = jnp.zeros((w_in.shape[0], LANES - FOX_HEADS - SSM_HEADS), w_in.dtype)
    parts = [cols(_O_QA, SWA_W),
             k0, k1, k1, k0,
             v0, v1, v1, v0,
             cols(_O_GA, SWA_W), cols(_O_GB, FOX_W),
             cols(_O_KB, FOX_W),
             cols(_O_ZC, SSM_W), cols(_O_XBC, SSM_CONV_W),
             cols(_O_FB, FOX_HEADS), cols(_O_DT, SSM_HEADS), pad]
    w_r = jnp.concatenate(parts, axis=1).astype(jnp.bfloat16)
    w_t = jnp.concatenate([cols(_O_QB, FOX_W), cols(_O_VB, FOX_W)], axis=1).T.astype(jnp.bfloat16)
    return w_r, w_t


def _inproj(x2, norm_w, w_r, w_t, cos, sin, seq, tm):
    t = x2.shape[0]
    nblk_s = seq // tm
    row = lambda i: (i, 0)
    const = lambda i: (0, 0)
    pos = lambda i: (i % nblk_s, 0)
    bf, f32 = jnp.bfloat16, jnp.float32
    outs = [(SWA_W, bf), (2 * SWA_KV_W, bf), (2 * SWA_KV_W, bf), (2 * SWA_W, f32), (FOX_W, bf),
            (SSM_W, f32), (SSM_CONV_W, f32), (LANES, f32)]
    npair = FOX_W // LANES
    return pl.pallas_call(
        _inproj_kernel,
        grid=(t // tm,),
        in_specs=[pl.BlockSpec((tm, D_MODEL), row),
                  pl.BlockSpec((1, D_MODEL), const),
                  pl.BlockSpec((D_MODEL, _PROJ_W), const, pipeline_mode=pl.Buffered(1)),
                  pl.BlockSpec((2 * FOX_W, D_MODEL), const, pipeline_mode=pl.Buffered(1)),
                  pl.BlockSpec((tm, LANES), pos),
                  pl.BlockSpec((tm, LANES), pos)],
        out_specs=[pl.BlockSpec((tm, w), row) for w, _ in outs]
        + [pl.BlockSpec((FOX_W, tm), lambda i: (0, i)),
           pl.BlockSpec((npair, tm // FOX_TK, LANES, FOX_TK), lambda i: (0, i, 0, 0))],
        out_shape=[jax.ShapeDtypeStruct((t, w), d) for w, d in outs]
        + [jax.ShapeDtypeStruct((FOX_W, t), bf),
           jax.ShapeDtypeStruct((npair, t // FOX_TK, LANES, FOX_TK), bf)],
        compiler_params=pltpu.CompilerParams(dimension_semantics=("arbitrary",),
                                             vmem_limit_bytes=_VMEM_LIMIT),
        name="inproj",
    )(x2, norm_w.reshape(1, D_MODEL), w_r, w_t, cos, sin)


def _rope_tables(seq):
    pos = jnp.arange(seq, dtype=jnp.float32)
    inv = 1.0 / (ROPE_THETA ** (jnp.arange(0, HEAD_DIM, 2, dtype=jnp.float32) / HEAD_DIM))
    ang = pos[:, None] * inv[None, :]
    cos, sin = jnp.cos(ang), jnp.sin(ang)
    cos_t = jnp.concatenate([cos, cos, cos, cos], axis=1)
    sin_t = jnp.concatenate([-sin, sin, -sin, sin], axis=1)
    return cos_t, sin_t


def _swa_kernel(sink_ref, q_ref, kc_ref, kp_ref, vc_ref, vp_ref, g_ref, o_ref):
    n = pl.program_id(1)
    q = q_ref[...]
    kcat = jnp.concatenate([kp_ref[...], kc_ref[...]], axis=0)
    vcat = jnp.concatenate([vp_ref[...], vc_ref[...]], axis=0)
    qi = lax.broadcasted_iota(jnp.int32, (BLOCK, 2 * BLOCK), 0)
    sj = lax.broadcasted_iota(jnp.int32, (BLOCK, 2 * BLOCK), 1)
    rel = qi - (sj - BLOCK)
    band = (rel >= 0) & (rel < BLOCK)
    prev_ok = jnp.where(n > 0, 0.0, NEG_INF)
    first_block_pen = jnp.where(sj < BLOCK, prev_ok, 0.0)
    lane = lax.broadcasted_iota(jnp.int32, (BLOCK, LANES), 1)
    low = lane < HEAD_DIM
    cols = []
    for c in range(SWA_HEADS // 2):
        qc = q[:, c * LANES:(c + 1) * LANES]
        halves = []
        for half in range(2):
            head = 2 * c + half
            kv = head // (SWA_HEADS // SWA_KV_HEADS)
            sel = 0 if kv == half else 1
            kk = kcat[:, sel * LANES:(sel + 1) * LANES]
            vv = vcat[:, sel * LANES:(sel + 1) * LANES]
            qh = jnp.where(low if half == 0 else ~low, qc, jnp.zeros_like(qc))
            s = _bdot_nt(qh, kk)
            s = jnp.where(band, s, NEG_INF) + first_block_pen
            sink = sink_ref[0, head]
            m = jnp.maximum(jnp.max(s, axis=-1, keepdims=True), sink)
            p = jnp.exp(s - m)
            denom = jnp.sum(p, axis=-1, keepdims=True) + jnp.exp(sink - m)
            probs = (p / denom).astype(jnp.bfloat16)
            halves.append(_bdot(probs, vv))
        cols.append(jnp.where(low, halves[0], halves[1]))
    y = jnp.concatenate(cols, axis=1)
    o_ref[...] = (y * _silu(g_ref[...])).astype(o_ref.dtype)


def _swa(qa, ka, va, g, sinks, batch, seq):
    n = seq // BLOCK
    cur = lambda b, i: (b * n + i, 0)
    prev = lambda b, i: (b * n + jnp.maximum(i - 1, 0), 0)
    return pl.pallas_call(
        _swa_kernel,
        grid=(batch, n),
        in_specs=[pl.BlockSpec(memory_space=pltpu.SMEM),
                  pl.BlockSpec((BLOCK, SWA_W), cur),
                  pl.BlockSpec((BLOCK, 2 * SWA_KV_W), cur),
                  pl.BlockSpec((BLOCK, 2 * SWA_KV_W), prev),
                  pl.BlockSpec((BLOCK, 2 * SWA_KV_W), cur),
                  pl.BlockSpec((BLOCK, 2 * SWA_KV_W), prev),
                  pl.BlockSpec((BLOCK, SWA_W), cur)],
        out_specs=pl.BlockSpec((BLOCK, SWA_W), cur),
        out_shape=jax.ShapeDtypeStruct((batch * seq, SWA_W), jnp.bfloat16),
        compiler_params=pltpu.CompilerParams(dimension_semantics=("arbitrary", "arbitrary")),
        name="swa",
    )(sinks.reshape(1, SWA_HEADS).astype(jnp.float32), qa, ka, ka, va, va, g)


def _fox_kernel(qt_ref, k_ref, ca_ref, vt_ref, g_ref, o_ref, w_buf, s_buf0, s_buf1, p_buf0, p_buf1, acc_buf, bias_buf):
    tq, tk = FOX_TQ, FOX_TK
    per_q = tq // tk
    nq = qt_ref.shape[1] // tq
    n_tiles = per_q * nq * (nq + 1) // 2
    all_masked = per_q
    acc_rows = acc_buf.shape[1]

    krow = lax.broadcasted_iota(jnp.int32, (tk, tq), 0)
    qcol = lax.broadcasted_iota(jnp.int32, (tk, tq), 1)
    for d in range(per_q):
        bias_buf[d] = jnp.where(krow + d * tk <= qcol, 0.0, NEG_INF)
    bias_buf[all_masked] = jnp.full((tk, tq), NEG_INF, jnp.float32)
    for buf in (s_buf0, s_buf1, p_buf0, p_buf1, acc_buf):
        buf[...] = jnp.zeros_like(buf)

    row = lax.broadcasted_iota(jnp.int32, (LANES, tq), 0)
    nsplit = 3
    pick = (jnp.where(row < nsplit, 1.0, 0.0).astype(jnp.bfloat16),
            jnp.where((row >= HEAD_DIM) & (row < HEAD_DIM + nsplit), 1.0, 0.0).astype(jnp.bfloat16))
    ones_rows = jnp.ones((acc_rows - HEAD_DIM, tk), jnp.bfloat16)

    def step(carry, s_cur, s_prv, p_cur, p_prv, masked):
        (i_a, j_a), (i_b, j_b), (i_c, j_c, v_c), smax, m, alpha_c = carry

        @pl.when(j_a == 0)
        def _():
            q0 = pl.multiple_of(jnp.minimum(i_a, nq - 1) * tq, tq)
            qt = qt_ref[:, pl.ds(q0, tq)].astype(jnp.float32)
            q_lo = jnp.where(row < HEAD_DIM, qt, 0.0).astype(jnp.bfloat16)
            q_hi = jnp.where(row < HEAD_DIM, 0.0, qt).astype(jnp.bfloat16)
            w_buf[:, 0:tq] = jnp.concatenate([q_lo, pick[0]], axis=0)
            w_buf[:, tq:2 * tq] = jnp.concatenate([q_hi, pick[1]], axis=0)

        k0 = pl.multiple_of(j_a * tk, tk)
        ka = jnp.concatenate([k_ref[pl.ds(k0, tk), :], ca_ref[pl.ds(k0, tk), :]], axis=1)
        s_new = _bdot(ka, w_buf[...])
        if masked:
            bias = bias_buf[jnp.where(i_a < nq, j_a - per_q * i_a, all_masked)]
            s_new = s_new + jnp.concatenate([bias, bias], axis=1)
        s_cur[...] = s_new
        smax_new = tuple(jnp.max(s_new[:, hh * tq:(hh + 1) * tq], axis=0, keepdims=True) for hh in range(2))

        first = j_b == 0
        s_prev = s_prv[...]
        m_new, alpha = [], []
        for hh in range(2):
            m_prev = jnp.where(first, NEG_INF, m[hh])
            mn = jnp.maximum(m_prev, smax[hh])
            p = jnp.exp2(s_prev[:, hh * tq:(hh + 1) * tq] - mn)
            p_cur[:, hh * tq:(hh + 1) * tq] = p.astype(jnp.bfloat16)
            m_new.append(mn)
            alpha.append(jnp.exp2(m_prev - mn))

        vt = vt_ref[0, j_c]
        p_prev = p_prv[...]
        for hh in range(2):
            lhs = jnp.concatenate([vt[hh * HEAD_DIM:(hh + 1) * HEAD_DIM, :], ones_rows], axis=0)
            acc_buf[hh] = alpha_c[hh] * acc_buf[hh] + _bdot(lhs, p_prev[:, hh * tq:(hh + 1) * tq])

        @pl.when(v_c & (j_c == per_q * (i_c + 1) - 1))
        def _():
            q0 = pl.multiple_of(i_c * tq, tq)
            out_t = jnp.concatenate([acc_buf[hh, 0:HEAD_DIM, :] / acc_buf[hh, HEAD_DIM:HEAD_DIM + 1, :]
                                     for hh in range(2)], axis=0)
            y = out_t.T * _silu(g_ref[pl.ds(q0, tq), :])
            o_ref[pl.ds(q0, tq), :] = y.astype(o_ref.dtype)

        last_a = j_a == per_q * (i_a + 1) - 1
        nxt_a = (jnp.where(last_a, i_a + 1, i_a), jnp.where(last_a, 0, j_a + 1))
        return (nxt_a, (i_a, j_a), (i_b, j_b, i_b < nq), smax_new, tuple(m_new), tuple(alpha))

    def either(carry, bufs):
        i_a, j_a = carry[0]
        needs_mask = (j_a >= per_q * i_a) | (i_a >= nq)
        return lax.cond(needs_mask, lambda c: step(c, *bufs, True), lambda c: step(c, *bufs, False), carry)

    def body(_, carry):
        carry = either(carry, (s_buf0, s_buf1, p_buf0, p_buf1))
        return either(carry, (s_buf1, s_buf0, p_buf1, p_buf0))

    zi = jnp.int32(0)
    row_vec = lambda v: tuple(jnp.full((1, tq), v, jnp.float32) for _ in range(2))
    init = ((zi, zi), (jnp.int32(nq), zi), (jnp.int32(nq), zi, False), row_vec(0.0), row_vec(0.0), row_vec(1.0))
    lax.fori_loop(0, (n_tiles + 3) // 2, body, init)


def _fox(qbt, kb, caug, vbt, g, batch, seq):
    npair = FOX_HEADS // 2
    nkt = seq // FOX_TK
    gcol0 = SWA_W // LANES
    return pl.pallas_call(
        _fox_kernel,
        grid=(batch, npair),
        in_specs=[pl.BlockSpec((LANES, seq), lambda b, h: (h, b)),
                  pl.BlockSpec((seq, LANES), lambda b, h: (b, h)),
                  pl.BlockSpec((seq, LANES), lambda b, h: (b, h)),
                  pl.BlockSpec((1, nkt, LANES, FOX_TK), lambda b, h: (h, b, 0, 0)),
                  pl.BlockSpec((seq, LANES), lambda b, h: (b, gcol0 + h))],
        out_specs=pl.BlockSpec((seq, LANES), lambda b, h: (b, h)),
        out_shape=jax.ShapeDtypeStruct((batch * seq, FOX_W), jnp.bfloat16),
        scratch_shapes=[pltpu.VMEM((2 * LANES, 2 * FOX_TQ), jnp.bfloat16),
                        pltpu.VMEM((FOX_TK, 2 * FOX_TQ), jnp.float32),
                        pltpu.VMEM((FOX_TK, 2 * FOX_TQ), jnp.float32),
                        pltpu.VMEM((FOX_TK, 2 * FOX_TQ), jnp.bfloat16),
                        pltpu.VMEM((FOX_TK, 2 * FOX_TQ), jnp.bfloat16),
                        pltpu.VMEM((2, HEAD_DIM + 16, FOX_TQ), jnp.float32),
                        pltpu.VMEM((FOX_TQ // FOX_TK + 1, FOX_TK, FOX_TQ), jnp.float32)],
        compiler_params=pltpu.CompilerParams(dimension_semantics=("arbitrary", "arbitrary"),
                                             vmem_limit_bytes=_VMEM_LIMIT),
        name="fox",
    )(qbt, kb, caug, vbt, g)


def _split3(x):
    hi = x.astype(jnp.bfloat16)
    r1 = x - hi.astype(jnp.float32)
    mid = r1.astype(jnp.bfloat16)
    lo = (r1 - mid.astype(jnp.float32)).astype(jnp.bfloat16)
    return hi, mid, lo


def _ssd_kernel(xbc_ref, z_ref, fdt_ref, cw_ref, cb_ref, bias_ref, alog_ref, dskip_ref, nw_ref,
                y_ref, ca_ref, state_ref, tail_ref, ccarry_ref):
    ci = pl.program_id(1)
    L = BLOCK

    @pl.when(ci == 0)
    def _():
        state_ref[...] = jnp.zeros_like(state_ref)
        tail_ref[...] = jnp.zeros_like(tail_ref)
        ccarry_ref[...] = jnp.zeros_like(ccarry_ref)

    u = xbc_ref[...]
    tail = tail_ref[...]
    row8 = lax.broadcasted_iota(jnp.int32, (8, SSM_CONV_W), 0)
    conv = cb_ref[...] + cw_ref[SSM_CONV - 1:SSM_CONV, :] * u
    for k in range(1, SSM_CONV):
        r = pltpu.roll(u, k, 0)
        first = jnp.where(row8 < k, pltpu.roll(tail, k, 0), r[0:8])
        uk = jnp.concatenate([first, r[8:]], axis=0)
        conv = conv + cw_ref[SSM_CONV - 1 - k:SSM_CONV - k, :] * uk
    tail_ref[...] = u[L - 8:L]
    act = _silu(conv)
    xs = act[:, :SSM_W]
    bm = act[:, SSM_W:SSM_W + SSM_BC_W].astype(jnp.bfloat16)
    cm = act[:, SSM_W + SSM_BC_W:].astype(jnp.bfloat16)

    lane = lax.broadcasted_iota(jnp.int32, (L, LANES), 1)
    vals = fdt_ref[...] + bias_ref[...]
    is_f = lane < DT_LANE0
    is_dt = (lane >= DT_LANE0) & (lane < DT_LANE0 + SSM_HEADS)
    sp = _softplus(jnp.where(is_f, -vals, vals))
    a_row = jnp.where(is_dt[0:1], -jnp.exp(alog_ref[...]), 0.0)
    dt = jnp.where(is_dt, sp, 0.0)
    scan_in = jnp.where(is_f, -sp, dt * a_row)

    ri = lax.broadcasted_iota(jnp.int32, (L, L), 0)
    cj = lax.broadcasted_iota(jnp.int32, (L, L), 1)
    causal = cj <= ri
    tri = jnp.where(causal, 1.0, 0.0).astype(jnp.bfloat16)
    hi, mid, lo = _split3(scan_in)
    cs = _bdot(tri, hi) + _bdot(tri, mid) + _bdot(tri, lo)
    cs_t = cs.T

    c_full = cs + ccarry_ref[...]
    ccarry_ref[...] = jnp.where(is_f[0:1], c_full[L - 1:L, :], 0.0)
    lane64 = lane % HEAD_DIM
    ca_cols = []
    for hp in range(FOX_HEADS // 2):
        negc = jnp.where(lane < HEAD_DIM, c_full[:, 2 * hp:2 * hp + 1], c_full[:, 2 * hp + 1:2 * hp + 2]) * (-LOG2E)
        hi, mid, lo = (term.astype(jnp.float32) for term in _split3(negc))
        terms = jnp.where(lane64 == 0, hi, jnp.where(lane64 == 1, mid, jnp.where(lane64 == 2, lo, 0.0)))
        ca_cols.append(terms.astype(jnp.bfloat16))
    ca_ref[...] = jnp.concatenate(ca_cols, axis=1)

    rowp = lax.broadcasted_iota(jnp.int32, (L, LANES), 0)
    low_l = lane < HEAD_DIM
    low_r = rowp < HEAD_DIM
    hpg = SSM_HEADS // 2
    ys = []
    cb = [None, None]
    for pair in range(SSM_HEADS // 2):
        grp = (2 * pair) // hpg
        bm_g = bm[:, grp * SSM_STATE:(grp + 1) * SSM_STATE]
        cm_g = cm[:, grp * SSM_STATE:(grp + 1) * SSM_STATE]
        if cb[grp] is None:
            cb[grp] = _bdot_nt(cm_g, bm_g)
        x_pair = xs[:, pair * LANES:(pair + 1) * LANES]
        ha, hb = DT_LANE0 + 2 * pair, DT_LANE0 + 2 * pair + 1
        acs_col = (cs[:, ha:ha + 1], cs[:, hb:hb + 1])
        acs_row = (cs_t[ha:ha + 1, :], cs_t[hb:hb + 1, :])
        dt_pair = jnp.where(low_l, dt[:, ha:ha + 1], dt[:, hb:hb + 1])
        xdt = x_pair * dt_pair
        xdt_b = xdt.astype(jnp.bfloat16)
        y_diag = []
        for hh in range(2):
            diff = acs_col[hh] - acs_row[hh]
            decay = jnp.exp(jnp.where(causal, diff, NEG_INF))
            gmat = (cb[grp] * decay).astype(jnp.bfloat16)
            y_diag.append(_bdot(gmat, xdt_b))
        y_pair = jnp.where(low_l, y_diag[0], y_diag[1])
        prev = state_ref[pair]
        y_off = _bdot_nt(cm_g, prev.astype(jnp.bfloat16))
        y_pair = y_pair + y_off * jnp.exp(jnp.where(low_l, acs_col[0], acs_col[1]))
        last = (acs_col[0][L - 1:L, :], acs_col[1][L - 1:L, :])
        dst = jnp.exp(jnp.where(low_l, last[0] - acs_col[0], last[1] - acs_col[1]))
        st_new = _bdot((xdt * dst).T.astype(jnp.bfloat16), bm_g)
        cdec = jnp.exp(jnp.where(low_r, last[0], last[1]))
        state_ref[pair] = prev * cdec + st_new
        ys.append(y_pair)
    y = jnp.concatenate(ys, axis=1) + dskip_ref[...] * xs
    y_ref[...] = _rms(y * _silu(z_ref[...]), nw_ref[...]).astype(y_ref.dtype)


def _ssd(xbc, z, fdt, conv_w, conv_b, b_forget, dt_bias, a_log, d_skip, norm_w, batch, seq):
    nc = seq // BLOCK
    row = lambda b, c: (b * nc + c, 0)
    const = lambda b, c: (0, 0)
    zpad = jnp.zeros((LANES - FOX_HEADS - SSM_HEADS,), jnp.float32)
    bias_row = jnp.concatenate([b_forget, dt_bias, zpad]).reshape(1, LANES)
    alog_row = jnp.concatenate([jnp.zeros((FOX_HEADS,), jnp.float32), a_log, zpad]).reshape(1, LANES)
    dskip_row = jnp.repeat(d_skip, HEAD_DIM).reshape(1, SSM_W)
    return pl.pallas_call(
        _ssd_kernel,
        grid=(batch, nc),
        in_specs=[pl.BlockSpec((BLOCK, SSM_CONV_W), row),
                  pl.BlockSpec((BLOCK, SSM_W), row),
                  pl.BlockSpec((BLOCK, LANES), row),
                  pl.BlockSpec((SSM_CONV, SSM_CONV_W), const),
                  pl.BlockSpec((1, SSM_CONV_W), const),
                  pl.BlockSpec((1, LANES), const),
                  pl.BlockSpec((1, LANES), const),
                  pl.BlockSpec((1, SSM_W), const),
                  pl.BlockSpec((1, SSM_W), const)],
        out_specs=[pl.BlockSpec((BLOCK, SSM_W), row),
                   pl.BlockSpec((BLOCK, FOX_W), row)],
        out_shape=[jax.ShapeDtypeStruct((batch * seq, SSM_W), jnp.bfloat16),
                   jax.ShapeDtypeStruct((batch * seq, FOX_W), jnp.bfloat16)],
        scratch_shapes=[pltpu.VMEM((SSM_HEADS // 2, 2 * HEAD_DIM, SSM_STATE), jnp.float32),
                        pltpu.VMEM((8, SSM_CONV_W), jnp.float32),
                        pltpu.VMEM((1, LANES), jnp.float32)],
        compiler_params=pltpu.CompilerParams(dimension_semantics=("arbitrary", "arbitrary")),
        name="ssd",
    )(xbc, z, fdt, conv_w, conv_b.reshape(1, SSM_CONV_W), bias_row, alog_row, dskip_row,
      norm_w.reshape(1, SSM_W))


def _memkv_kernel(mem_ref, nw_ref, wk_ref, wv_ref, k_ref, v_ref):
    mn = _rms(mem_ref[...], nw_ref[...]).astype(jnp.bfloat16)
    k_ref[...] = _bdot(mn, wk_ref[...]).astype(k_ref.dtype)
    v_ref[...] = _bdot(mn, wv_ref[...]).astype(v_ref.dtype)


def _memkv(mem2, norm_w, wk, wv, mem_tokens):
    t = mem2.shape[0]
    row = lambda b: (b, 0)
    const = lambda b: (0, 0)
    return pl.pallas_call(
        _memkv_kernel,
        grid=(t // mem_tokens,),
        in_specs=[pl.BlockSpec((mem_tokens, D_MODEL), row),
                  pl.BlockSpec((1, D_MODEL), const),
                  pl.BlockSpec((D_MODEL, D_MODEL), const),
                  pl.BlockSpec((D_MODEL, D_MODEL), const)],
        out_specs=[pl.BlockSpec((mem_tokens, D_MODEL), row)] * 2,
        out_shape=[jax.ShapeDtypeStruct((t, D_MODEL), jnp.bfloat16)] * 2,
        compiler_params=pltpu.CompilerParams(dimension_semantics=("arbitrary",)),
        name="memkv",
    )(mem2, norm_w.reshape(1, D_MODEL), wk.astype(jnp.bfloat16), wv.astype(jnp.bfloat16))


def _out_kernel(x_ref, ya_ref, yb_ref, yc_ref, wo_ref, nq_ref, wq_ref, k_ref, v_ref, wmo_ref, fn_ref,
                o_ref, *, final_norm):
    x1 = (x_ref[...]
          + _bdot(ya_ref[...], wo_ref[0:SWA_W, :])
          + _bdot(yb_ref[...], wo_ref[SWA_W:SWA_W + FOX_W, :])
          + _bdot(yc_ref[...], wo_ref[SWA_W + FOX_W:, :]))
    hq = _rms(x1, nq_ref[...]).astype(jnp.bfloat16)
    q = (_bdot(hq, wq_ref[...]) * (MEM_HEAD_DIM ** -0.5)).astype(jnp.bfloat16)
    heads = []
    for h in range(MEM_HEADS):
        sl = slice(h * MEM_HEAD_DIM, (h + 1) * MEM_HEAD_DIM)
        s = _bdot_nt(q[:, sl], k_ref[:, sl])
        m = jnp.max(s, axis=-1, keepdims=True)
        p = jnp.exp(s - m)
        probs = (p / jnp.sum(p, axis=-1, keepdims=True)).astype(jnp.bfloat16)
        heads.append(_bdot(probs, v_ref[:, sl]).astype(jnp.bfloat16))
    att = jnp.concatenate(heads, axis=1)
    x2 = x1 + _bdot(att, wmo_ref[...])
    if final_norm:
        x2 = _rms(x2, fn_ref[...])
    o_ref[...] = x2


def _out_block(x2, ya, yb, yc, w_out, norm_xq_w, w_mq, kmem, vmem, w_mo, final_w, seq, mem_tokens, tm,
               final_norm):
    t = x2.shape[0]
    nblk_s = seq // tm
    row = lambda i: (i, 0)
    const = lambda i: (0, 0)
    memrow = lambda i: (i // nblk_s, 0)
    bf = jnp.bfloat16
    return pl.pallas_call(
        functools.partial(_out_kernel, final_norm=final_norm),
        grid=(t // tm,),
        in_specs=[pl.BlockSpec((tm, D_MODEL), row),
                  pl.BlockSpec((tm, SWA_W), row),
                  pl.BlockSpec((tm, FOX_W), row),
                  pl.BlockSpec((tm, SSM_W), row),
                  pl.BlockSpec((SWA_W + FOX_W + SSM_W, D_MODEL), const),
                  pl.BlockSpec((1, D_MODEL), const),
                  pl.BlockSpec((D_MODEL, D_MODEL), const),
                  pl.BlockSpec((mem_tokens, D_MODEL), memrow),
                  pl.BlockSpec((mem_tokens, D_MODEL), memrow),
                  pl.BlockSpec((D_MODEL, D_MODEL), const),
                  pl.BlockSpec((1, D_MODEL), const)],
        out_specs=pl.BlockSpec((tm, D_MODEL), row),
        out_shape=jax.ShapeDtypeStruct((t, D_MODEL), jnp.float32),
        compiler_params=pltpu.CompilerParams(dimension_semantics=("arbitrary",),
                                             vmem_limit_bytes=_VMEM_LIMIT),
        name="outproj_mem",
    )(x2, ya, yb, yc, w_out.astype(bf), norm_xq_w.reshape(1, D_MODEL), w_mq.astype(bf), kmem, vmem,
      w_mo.astype(bf), final_w.reshape(1, D_MODEL))


def _row_tile(seq):
    return min(512, seq)


def kernel(x, mem, norm_mix_w, w_in, b_forget, swa_sinks, conv_w, conv_b, dt_bias, a_log, d_skip, ssm_norm_w, w_out, norm_xq_w, norm_mem_w, w_mq, w_mk, w_mv, w_mo, final_norm_w):
    batch, seq, _ = x.shape
    mem_tokens = mem.shape[1]
    depth = w_in.shape[0]
    assert seq % BLOCK == 0
    tm = _row_tile(seq)
    assert seq % tm == 0 and seq % FOX_TQ == 0
    cos, sin = _rope_tables(seq)
    x2 = x.reshape(batch * seq, D_MODEL)
    mem2 = mem.reshape(batch * mem_tokens, D_MODEL)
    for l in range(depth):
        w_r, w_t = _arrange_w_in(w_in[l])
        qa, ka, va, g, kb, z, xbc, fdt, qbt, vbt = _inproj(x2, norm_mix_w[l], w_r, w_t, cos, sin, seq, tm)
        yc, caug = _ssd(xbc, z, fdt, conv_w[l], conv_b[l], b_forget[l], dt_bias[l], a_log[l], d_skip[l],
                       ssm_norm_w[l], batch, seq)
        ya = _swa(qa, ka, va, g, swa_sinks[l], batch, seq)
        yb = _fox(qbt, kb, caug, vbt, g, batch, seq)
        kmem, vmem = _memkv(mem2, norm_mem_w[l], w_mk[l], w_mv[l], mem_tokens)
        x2 = _out_block(x2, ya, yb, yc, w_out[l], norm_xq_w[l], w_mq[l], kmem, vmem, w_mo[l],
                        final_norm_w, seq, mem_tokens, tm, final_norm=(l == depth - 1))
    return x2.reshape(batch, seq, D_MODEL)
```

```python
import functools

import jax
import jax.numpy as jnp
from jax import lax
from jax.experimental import pallas as pl
from jax.experimental.pallas import tpu as pltpu

D_MODEL = 1024
HEAD_DIM = 64
BLOCK = 128
SWA_HEADS = 8
SWA_KV_HEADS = 2
FOX_HEADS = 8
SSM_HEADS = 16
SSM_STATE = 128
SSM_CONV = 4
MEM_HEADS = 4
MEM_HEAD_DIM = 256
ROPE_THETA = 10000.0
EPS = 1e-6
NEG_INF = -1e30

SWA_W = SWA_HEADS * HEAD_DIM
SWA_KV_W = SWA_KV_HEADS * HEAD_DIM
FOX_W = FOX_HEADS * HEAD_DIM
SSM_W = SSM_HEADS * HEAD_DIM
SSM_BC_W = 2 * SSM_STATE
SSM_CONV_W = SSM_W + 2 * SSM_BC_W
LANES = 128
LOG2E = 1.4426950408889634
SWA_TQ = 512
FOX_TQ = 512
FOX_TK = 512
DT_LANE0 = FOX_HEADS

_O_QA = 0
_O_KA = _O_QA + SWA_W
_O_VA = _O_KA + SWA_KV_W
_O_GA = _O_VA + SWA_KV_W
_O_QB = _O_GA + SWA_W
_O_KB = _O_QB + FOX_W
_O_VB = _O_KB + FOX_W
_O_FB = _O_VB + FOX_W
_O_GB = _O_FB + FOX_HEADS
_O_ZC = _O_GB + FOX_W
_O_XBC = _O_ZC + SSM_W
_O_DT = _O_XBC + SSM_CONV_W
_IN_W = _O_DT + SSM_HEADS

_SEG_W = (2 * SWA_KV_W, 2 * SWA_W, FOX_W, SSM_W, SSM_CONV_W, LANES)
_SEG_OFF = tuple(sum(_SEG_W[:i]) for i in range(len(_SEG_W)))
_PROJ_W = sum(_SEG_W)

_VMEM_LIMIT = 56 * 1024 * 1024


def _bdot(a, b):
    return jnp.dot(a, b, preferred_element_type=jnp.float32)


def _bdot_nt(a, b):
    return lax.dot_general(a, b, (((1,), (1,)), ((), ())), preferred_element_type=jnp.float32)


def _silu(x):
    h = 0.5 * x
    return h + h * jnp.tanh(h)


def _softplus(x):
    return jnp.maximum(x, 0.0) + jnp.log(1.0 + jnp.exp(-jnp.abs(x)))


def _rms(x, w):
    return x * lax.rsqrt(jnp.mean(x * x, axis=-1, keepdims=True) + EPS) * w


def _rope(x, cos, sin_signed):
    width = x.shape[1]
    reps = width // LANES
    lane = lax.broadcasted_iota(jnp.int32, x.shape, 1)
    first_half = (lane % HEAD_DIM) < (HEAD_DIM // 2)
    swapped = jnp.where(first_half,
                        pltpu.roll(x, width - HEAD_DIM // 2, 1),
                        pltpu.roll(x, HEAD_DIM // 2, 1))
    cos_t = jnp.concatenate([cos] * reps, axis=1)
    sin_t = jnp.concatenate([sin_signed] * reps, axis=1)
    return x * cos_t + swapped * sin_t


def _rope_t(x, cos, sin_signed):
    rows = x.shape[0]
    reps = rows // LANES
    r = lax.broadcasted_iota(jnp.int32, x.shape, 0)
    first_half = (r % HEAD_DIM) < (HEAD_DIM // 2)
    swapped = jnp.where(first_half,
                        pltpu.roll(x, rows - HEAD_DIM // 2, 0),
                        pltpu.roll(x, HEAD_DIM // 2, 0))
    cos_t = jnp.concatenate([cos] * reps, axis=0)
    sin_t = jnp.concatenate([sin_signed] * reps, axis=0)
    return x * cos_t + swapped * sin_t


def _inproj_kernel(x_ref, nw_ref, w_ref, wt_ref, cos_ref, sin_ref, cost_ref, sint_ref,
                   ka_ref, g_ref, kb_ref, z_ref, xbc_ref, fdt_ref, qbt_ref, vbt_ref, qat_ref, vat_ref):
    h = _rms(x_ref[...], nw_ref[...]).astype(jnp.bfloat16)

    def seg(i):
        return _bdot(h, w_ref[:, _SEG_OFF[i]:_SEG_OFF[i] + _SEG_W[i]])

    scale = HEAD_DIM ** -0.5 * LOG2E
    ka_ref[...] = _rope(seg(0), cos_ref[...], sin_ref[...]).astype(ka_ref.dtype)
    g_ref[...] = seg(1)
    kb_ref[...] = seg(2).astype(kb_ref.dtype)
    z_ref[...] = seg(3)
    xbc_ref[...] = seg(4)
    fdt_ref[...] = seg(5)
    o_vb, o_qa, o_va = FOX_W, 2 * FOX_W, 2 * FOX_W + SWA_W
    qbt_ref[...] = (_bdot_nt(wt_ref[0:o_vb, :], h) * scale).astype(qbt_ref.dtype)
    qat = _rope_t(_bdot_nt(wt_ref[o_qa:o_va, :], h), cost_ref[...], sint_ref[...])
    qat_ref[...] = (qat * scale).astype(qat_ref.dtype)
    vat_ref[...] = _bdot_nt(wt_ref[o_va:, :], h).astype(vat_ref.dtype)
    vbt = _bdot_nt(wt_ref[o_vb:o_qa, :], h).astype(vbt_ref.dtype)
    for hp in range(FOX_W // LANES):
        for c in range(vbt.shape[1] // FOX_TK):
            vbt_ref[hp, c] = vbt[hp * LANES:(hp + 1) * LANES, c * FOX_TK:(c + 1) * FOX_TK]


def _arrange_w_in(w_in):
    def cols(o, n):
        return w_in[:, o:o + n]
    k0, k1 = cols(_O_KA, HEAD_DIM), cols(_O_KA + HEAD_DIM, HEAD_DIM)
    pad = jnp.zeros((w_in.shape[0], LANES - FOX_HEADS - SSM_HEADS), w_in.dtype)
    parts = [k0, k1, k1, k0,
             cols(_O_GA, SWA_W), cols(_O_GB, FOX_W),
             cols(_O_KB, FOX_W),
             cols(_O_ZC, SSM_W), cols(_O_XBC, SSM_CONV_W),
             cols(_O_FB, FOX_HEADS), cols(_O_DT, SSM_HEADS), pad]
    w_r = jnp.concatenate(parts, axis=1).astype(jnp.bfloat16)
    w_t = jnp.concatenate([cols(_O_QB, FOX_W), cols(_O_VB, FOX_W), cols(_O_QA, SWA_W), cols(_O_VA, SWA_KV_W)],
                          axis=1).T.astype(jnp.bfloat16)
    return w_r, w_t


def _inproj(x2, norm_w, w_r, w_t, rope, seq, tm):
    t = x2.shape[0]
    nblk_s = seq // tm
    row = lambda i: (i, 0)
    const = lambda i: (0, 0)
    pos = lambda i: (i % nblk_s, 0)
    pos_t = lambda i: (0, i % nblk_s)
    col = lambda i: (0, i)
    bf, f32 = jnp.bfloat16, jnp.float32
    outs = [(2 * SWA_KV_W, bf), (2 * SWA_W, f32), (FOX_W, bf), (SSM_W, f32), (SSM_CONV_W, f32), (LANES, f32)]
    npair = FOX_W // LANES
    return pl.pallas_call(
        _inproj_kernel,
        grid=(t // tm,),
        in_specs=[pl.BlockSpec((tm, D_MODEL), row),
                  pl.BlockSpec((1, D_MODEL), const),
                  pl.BlockSpec((D_MODEL, _PROJ_W), const, pipeline_mode=pl.Buffered(1)),
                  pl.BlockSpec(w_t.shape, const, pipeline_mode=pl.Buffered(1)),
                  pl.BlockSpec((tm, LANES), pos),
                  pl.BlockSpec((tm, LANES), pos),
                  pl.BlockSpec((LANES, tm), pos_t),
                  pl.BlockSpec((LANES, tm), pos_t)],
        out_specs=[pl.BlockSpec((tm, w), row) for w, _ in outs]
        + [pl.BlockSpec((FOX_W, tm), col),
           pl.BlockSpec((npair, tm // FOX_TK, LANES, FOX_TK), lambda i: (0, i, 0, 0)),
           pl.BlockSpec((SWA_W, tm), col),
           pl.BlockSpec((SWA_KV_W, tm), col)],
        out_shape=[jax.ShapeDtypeStruct((t, w), d) for w, d in outs]
        + [jax.ShapeDtypeStruct((FOX_W, t), bf),
           jax.ShapeDtypeStruct((npair, t // FOX_TK, LANES, FOX_TK), bf),
           jax.ShapeDtypeStruct((SWA_W, t), bf),
           jax.ShapeDtypeStruct((SWA_KV_W, t), bf)],
        compiler_params=pltpu.CompilerParams(dimension_semantics=("arbitrary",),
                                             vmem_limit_bytes=_VMEM_LIMIT),
        name="inproj",
    )(x2, norm_w.reshape(1, D_MODEL), w_r, w_t, *rope)


def _rope_tables(seq):
    pos = jnp.arange(seq, dtype=jnp.float32)
    inv = 1.0 / (ROPE_THETA ** (jnp.arange(0, HEAD_DIM, 2, dtype=jnp.float32) / HEAD_DIM))
    ang = pos[:, None] * inv[None, :]
    cos, sin = jnp.cos(ang), jnp.sin(ang)
    cos_t = jnp.concatenate([cos, cos, cos, cos], axis=1)
    sin_t = jnp.concatenate([-sin, sin, -sin, sin], axis=1)
    return cos_t, sin_t, cos_t.T, sin_t.T


def _swa_kernel(sink_ref, qt_ref, kc_ref, kp_ref, vc_ref, vp_ref, g_ref, o_ref):
    n = pl.program_id(1)
    nsub = SWA_TQ // BLOCK
    key = lax.broadcasted_iota(jnp.int32, (2 * BLOCK, BLOCK), 0)
    qry = lax.broadcasted_iota(jnp.int32, (2 * BLOCK, BLOCK), 1)
    band = jnp.where((key > qry) & (key <= qry + BLOCK), 0.0, NEG_INF)
    band_first = jnp.where(key < BLOCK, NEG_INF, band)
    row = lax.broadcasted_iota(jnp.int32, (LANES, BLOCK), 0)
    ones_rows = jnp.ones((16, 2 * BLOCK), jnp.bfloat16)

    def scores(u):
        if u == 0:
            kcat = jnp.concatenate([kp_ref[...], kc_ref[0:BLOCK, :]], axis=0)
            vcat = jnp.concatenate([vp_ref[...], vc_ref[:, 0:BLOCK]], axis=1)
            bias = jnp.where(n > 0, band, band_first)
        else:
            kcat = kc_ref[(u - 1) * BLOCK:(u + 1) * BLOCK, :]
            vcat = vc_ref[:, (u - 1) * BLOCK:(u + 1) * BLOCK]
            bias = band
        tiles = []
        for c in range(SWA_HEADS // 2):
            qt = qt_ref[c * LANES:(c + 1) * LANES, u * BLOCK:(u + 1) * BLOCK].astype(jnp.float32)
            for half in range(2):
                kv = (2 * c + half) // (SWA_HEADS // SWA_KV_HEADS)
                sel = 0 if kv == half else 1
                in_half = (row < HEAD_DIM) if half == 0 else (row >= HEAD_DIM)
                w = jnp.where(in_half, qt, 0.0).astype(jnp.bfloat16)
                tiles.append(_bdot(kcat[:, sel * LANES:(sel + 1) * LANES], w) + bias)
        return tiles, vcat

    def finish(u, tiles, vcat):
        probs, sink_terms = [], []
        for head, s in enumerate(tiles):
            sink = sink_ref[0, head] * LOG2E
            m = jnp.maximum(jnp.max(s, axis=0, keepdims=True), sink)
            probs.append(jnp.exp2(s - m).astype(jnp.bfloat16))
            sink_terms.append(jnp.exp2(sink - m))
        outs = []
        for head, p in enumerate(probs):
            kv = head // (SWA_HEADS // SWA_KV_HEADS)
            lhs = jnp.concatenate([vcat[kv * HEAD_DIM:(kv + 1) * HEAD_DIM, :], ones_rows], axis=0)
            pv = _bdot(lhs, p)
            outs.append(pv[0:HEAD_DIM, :] / (pv[HEAD_DIM:HEAD_DIM + 1, :] + sink_terms[head]))
        y = jnp.concatenate(outs, axis=0).T
        rows = slice(u * BLOCK, (u + 1) * BLOCK)
        o_ref[rows, :] = (y * _silu(g_ref[rows, :])).astype(o_ref.dtype)

    pending = scores(0)
    for u in range(nsub):
        nxt = scores(u + 1) if u + 1 < nsub else None
        finish(u, *pending)
        pending = nxt


def _swa(qat, ka, vat, g, sinks, batch, seq):
    n = seq // SWA_TQ
    nsub = SWA_TQ // BLOCK
    cur = lambda b, i: (b * n + i, 0)
    cur_t = lambda b, i: (0, b * n + i)
    prev = lambda b, i: ((b * n + i) * nsub - jnp.minimum(i, 1), 0)
    prev_t = lambda b, i: (0, (b * n + i) * nsub - jnp.minimum(i, 1))
    return pl.pallas_call(
        _swa_kernel,
        grid=(batch, n),
        in_specs=[pl.BlockSpec(memory_space=pltpu.SMEM),
                  pl.BlockSpec((SWA_W, SWA_TQ), cur_t),
                  pl.BlockSpec((SWA_TQ, 2 * SWA_KV_W), cur),
                  pl.BlockSpec((BLOCK, 2 * SWA_KV_W), prev),
                  pl.BlockSpec((SWA_KV_W, SWA_TQ), cur_t),
                  pl.BlockSpec((SWA_KV_W, BLOCK), prev_t),
                  pl.BlockSpec((SWA_TQ, SWA_W), cur)],
        out_specs=pl.BlockSpec((SWA_TQ, SWA_W), cur),
        out_shape=jax.ShapeDtypeStruct((batch * seq, SWA_W), jnp.bfloat16),
        compiler_params=pltpu.CompilerParams(dimension_semantics=("arbitrary", "arbitrary")),
        name="swa",
    )(sinks.reshape(1, SWA_HEADS).astype(jnp.float32), qat, ka, ka, vat, vat, g)


def _fox_kernel(qt_ref, k_ref, ca_ref, vt_ref, g_ref, o_ref, w_buf, s_buf0, s_buf1, p_buf0, p_buf1, acc_buf, bias_buf):
    tq, tk = FOX_TQ, FOX_TK
    per_q = tq // tk
    nq = qt_ref.shape[1] // tq
    n_tiles = per_q * nq * (nq + 1) // 2
    all_masked = per_q
    acc_rows = acc_buf.shape[1]

    krow = lax.broadcasted_iota(jnp.int32, (tk, tq), 0)
    qcol = lax.broadcasted_iota(jnp.int32, (tk, tq), 1)
    for d in range(per_q):
        bias_buf[d] = jnp.where(krow + d * tk <= qcol, 0.0, NEG_INF)
    bias_buf[all_masked] = jnp.full((tk, tq), NEG_INF, jnp.float32)
    for buf in (s_buf0, s_buf1, p_buf0, p_buf1, acc_buf):
        buf[...] = jnp.zeros_like(buf)

    row = lax.broadcasted_iota(jnp.int32, (LANES, tq), 0)
    nsplit = 3
    pick = (jnp.where(row < nsplit, 1.0, 0.0).astype(jnp.bfloat16),
            jnp.where((row >= HEAD_DIM) & (row < HEAD_DIM + nsplit), 1.0, 0.0).astype(jnp.bfloat16))
    ones_rows = jnp.ones((acc_rows - HEAD_DIM, tk), jnp.bfloat16)

    def step(carry, s_cur, s_prv, p_cur, p_prv, masked):
        (i_a, j_a), (i_b, j_b), (i_c, j_c, v_c), smax, m, alpha_c = carry

        @pl.when(j_a == 0)
        def _():
            q0 = pl.multiple_of(jnp.minimum(i_a, nq - 1) * tq, tq)
            qt = qt_ref[:, pl.ds(q0, tq)].astype(jnp.float32)
            q_lo = jnp.where(row < HEAD_DIM, qt, 0.0).astype(jnp.bfloat16)
            q_hi = jnp.where(row < HEAD_DIM, 0.0, qt).astype(jnp.bfloat16)
            w_buf[:, 0:tq] = jnp.concatenate([q_lo, pick[0]], axis=0)
            w_buf[:, tq:2 * tq] = jnp.concatenate([q_hi, pick[1]], axis=0)

        k0 = pl.multiple_of(jnp.where(i_a < nq, j_a, 0) * tk, tk)
        ka = jnp.concatenate([k_ref[pl.ds(k0, tk), :], ca_ref[pl.ds(k0, tk), :]], axis=1)
        s_new = _bdot(ka, w_buf[...])
        if masked:
            bias = bias_buf[jnp.where(i_a < nq, j_a - per_q * i_a, all_masked)]
            s_new = s_new + jnp.concatenate([bias, bias], axis=1)
        s_cur[...] = s_new
        smax_new = tuple(jnp.max(s_new[:, hh * tq:(hh + 1) * tq], axis=0, keepdims=True) for hh in range(2))

        first = j_b == 0
        s_prev = s_prv[...]
        m_new, alpha = [], []
        for hh in range(2):
            m_prev = jnp.where(first, NEG_INF, m[hh])
            mn = jnp.maximum(m_prev, smax[hh])
            p = jnp.exp2(s_prev[:, hh * tq:(hh + 1) * tq] - mn)
            p_cur[:, hh * tq:(hh + 1) * tq] = p.astype(jnp.bfloat16)
            m_new.append(mn)
            alpha.append(jnp.exp2(m_prev - mn))

        vt = vt_ref[0, j_c]
        p_prev = p_prv[...]
        for hh in range(2):
            lhs = jnp.concatenate([vt[hh * HEAD_DIM:(hh + 1) * HEAD_DIM, :], ones_rows], axis=0)
            acc_buf[hh] = alpha_c[hh] * acc_buf[hh] + _bdot(lhs, p_prev[:, hh * tq:(hh + 1) * tq])

        @pl.when(v_c & (j_c == per_q * (i_c + 1) - 1))
        def _():
            q0 = pl.multiple_of(i_c * tq, tq)
            out_t = jnp.concatenate([acc_buf[hh, 0:HEAD_DIM, :] / acc_buf[hh, HEAD_DIM:HEAD_DIM + 1, :]
                                     for hh in range(2)], axis=0)
            y = out_t.T * _silu(g_ref[pl.ds(q0, tq), :])
            o_ref[pl.ds(q0, tq), :] = y.astype(o_ref.dtype)

        last_a = j_a == per_q * (i_a + 1) - 1
        nxt_a = (jnp.where(last_a, i_a + 1, i_a), jnp.where(last_a, 0, j_a + 1))
        return (nxt_a, (i_a, j_a), (i_b, j_b, i_b < nq), smax_new, tuple(m_new), tuple(alpha))

    def either(carry, bufs):
        i_a, j_a = carry[0]
        needs_mask = (j_a >= per_q * i_a) | (i_a >= nq)
        return lax.cond(needs_mask, lambda c: step(c, *bufs, True), lambda c: step(c, *bufs, False), carry)

    def body(_, carry):
        carry = either(carry, (s_buf0, s_buf1, p_buf0, p_buf1))
        return either(carry, (s_buf1, s_buf0, p_buf1, p_buf0))

    zi = jnp.int32(0)
    row_vec = lambda v: tuple(jnp.full((1, tq), v, jnp.float32) for _ in range(2))
    init = ((zi, zi), (jnp.int32(nq), zi), (jnp.int32(nq), zi, False), row_vec(0.0), row_vec(0.0), row_vec(1.0))
    lax.fori_loop(0, (n_tiles + 3) // 2, body, init)


def _fox(qbt, kb, caug, vbt, g, batch, seq):
    npair = FOX_HEADS // 2
    nkt = seq // FOX_TK
    gcol0 = SWA_W // LANES
    return pl.pallas_call(
        _fox_kernel,
        grid=(batch, npair),
        in_specs=[pl.BlockSpec((LANES, seq), lambda b, h: (h, b)),
                  pl.BlockSpec((seq, LANES), lambda b, h: (b, h)),
                  pl.BlockSpec((seq, LANES), lambda b, h: (b, h)),
                  pl.BlockSpec((1, nkt, LANES, FOX_TK), lambda b, h: (h, b, 0, 0)),
                  pl.BlockSpec((seq, LANES), lambda b, h: (b, gcol0 + h))],
        out_specs=pl.BlockSpec((seq, LANES), lambda b, h: (b, h)),
        out_shape=jax.ShapeDtypeStruct((batch * seq, FOX_W), jnp.bfloat16),
        scratch_shapes=[pltpu.VMEM((2 * LANES, 2 * FOX_TQ), jnp.bfloat16),
                        pltpu.VMEM((FOX_TK, 2 * FOX_TQ), jnp.float32),
                        pltpu.VMEM((FOX_TK, 2 * FOX_TQ), jnp.float32),
                        pltpu.VMEM((FOX_TK, 2 * FOX_TQ), jnp.bfloat16),
                        pltpu.VMEM((FOX_TK, 2 * FOX_TQ), jnp.bfloat16),
                        pltpu.VMEM((2, HEAD_DIM + 16, FOX_TQ), jnp.float32),
                        pltpu.VMEM((FOX_TQ // FOX_TK + 1, FOX_TK, FOX_TQ), jnp.float32)],
        compiler_params=pltpu.CompilerParams(dimension_semantics=("arbitrary", "arbitrary"),
                                             vmem_limit_bytes=_VMEM_LIMIT),
        name="fox",
    )(qbt, kb, caug, vbt, g)


def _split3(x):
    hi = x.astype(jnp.bfloat16)
    r1 = x - hi.astype(jnp.float32)
    mid = r1.astype(jnp.bfloat16)
    lo = (r1 - mid.astype(jnp.float32)).astype(jnp.bfloat16)
    return hi, mid, lo


def _ssd_kernel(xbc_ref, z_ref, fdt_ref, cw_ref, cb_ref, bias_ref, alog_ref, dskip_ref, nw_ref,
                y_ref, ca_ref, state_ref, tail_ref, ccarry_ref):
    ci = pl.program_id(1)
    L = BLOCK

    @pl.when(ci == 0)
    def _():
        state_ref[...] = jnp.zeros_like(state_ref)
        tail_ref[...] = jnp.zeros_like(tail_ref)
        ccarry_ref[...] = jnp.zeros_like(ccarry_ref)

    u = xbc_ref[...]
    tail_ref[8:8 + L, :] = u
    conv = cb_ref[...] + cw_ref[SSM_CONV - 1:SSM_CONV, :] * u
    for k in range(1, SSM_CONV):
        conv = conv + cw_ref[SSM_CONV - 1 - k:SSM_CONV - k, :] * tail_ref[8 - k:8 - k + L, :]
    tail_ref[0:8, :] = u[L - 8:L]
    act = _silu(conv)
    xs = act[:, :SSM_W]
    bm = act[:, SSM_W:SSM_W + SSM_BC_W].astype(jnp.bfloat16)
    cm = act[:, SSM_W + SSM_BC_W:].astype(jnp.bfloat16)

    lane = lax.broadcasted_iota(jnp.int32, (L, LANES), 1)
    vals = fdt_ref[...] + bias_ref[...]
    is_f = lane < DT_LANE0
    is_dt = (lane >= DT_LANE0) & (lane < DT_LANE0 + SSM_HEADS)
    sp = _softplus(jnp.where(is_f, -vals, vals))
    a_row = jnp.where(is_dt[0:1], -jnp.exp(alog_ref[...]), 0.0)
    dt = jnp.where(is_dt, sp, 0.0)
    scan_in = jnp.where(is_f, -sp, dt * a_row)

    ri = lax.broadcasted_iota(jnp.int32, (L, L), 0)
    cj = lax.broadcasted_iota(jnp.int32, (L, L), 1)
    causal = cj <= ri
    tri = jnp.where(causal, 1.0, 0.0).astype(jnp.bfloat16)
    hi, mid, lo = _split3(scan_in)
    cs = _bdot(tri, hi) + _bdot(tri, mid) + _bdot(tri, lo)
    cs_t = cs.T

    c_full = cs + ccarry_ref[...]
    ccarry_ref[...] = jnp.where(is_f[0:1], c_full[L - 1:L, :], 0.0)
    lane64 = lane % HEAD_DIM
    ca_cols = []
    for hp in range(FOX_HEADS // 2):
        negc = jnp.where(lane < HEAD_DIM, c_full[:, 2 * hp:2 * hp + 1], c_full[:, 2 * hp + 1:2 * hp + 2]) * (-LOG2E)
        hi, mid, lo = (term.astype(jnp.float32) for term in _split3(negc))
        terms = jnp.where(lane64 == 0, hi, jnp.where(lane64 == 1, mid, jnp.where(lane64 == 2, lo, 0.0)))
        ca_cols.append(terms.astype(jnp.bfloat16))
    ca_ref[...] = jnp.concatenate(ca_cols, axis=1)

    rowp = lax.broadcasted_iota(jnp.int32, (L, LANES), 0)
    low_l = lane < HEAD_DIM
    low_r = rowp < HEAD_DIM
    hpg = SSM_HEADS // 2
    ys = []
    cb = [None, None]
    for pair in range(SSM_HEADS // 2):
        grp = (2 * pair) // hpg
        bm_g = bm[:, grp * SSM_STATE:(grp + 1) * SSM_STATE]
        cm_g = cm[:, grp * SSM_STATE:(grp + 1) * SSM_STATE]
        if cb[grp] is None:
            cb[grp] = _bdot_nt(cm_g, bm_g)
        x_pair = xs[:, pair * LANES:(pair + 1) * LANES]
        ha, hb = DT_LANE0 + 2 * pair, DT_LANE0 + 2 * pair + 1
        acs_col = (cs[:, ha:ha + 1], cs[:, hb:hb + 1])
        acs_row = (cs_t[ha:ha + 1, :], cs_t[hb:hb + 1, :])
        dt_pair = jnp.where(low_l, dt[:, ha:ha + 1], dt[:, hb:hb + 1])
        xdt = x_pair * dt_pair
        xdt_b = xdt.astype(jnp.bfloat16)
        y_diag = []
        for hh in range(2):
            diff = acs_col[hh] - acs_row[hh]
            decay = jnp.exp(jnp.where(causal, diff, NEG_INF))
            gmat = (cb[grp] * decay).astype(jnp.bfloat16)
            y_diag.append(_bdot(gmat, xdt_b))
        y_pair = jnp.where(low_l, y_diag[0], y_diag[1])
        prev = state_ref[pair]
        y_off = _bdot_nt(cm_g, prev.astype(jnp.bfloat16))
        y_pair = y_pair + y_off * jnp.exp(jnp.where(low_l, acs_col[0], acs_col[1]))
        last = (acs_col[0][L - 1:L, :], acs_col[1][L - 1:L, :])
        dst = jnp.exp(jnp.where(low_l, last[0] - acs_col[0], last[1] - acs_col[1]))
        st_new = _bdot((xdt * dst).T.astype(jnp.bfloat16), bm_g)
        cdec = jnp.exp(jnp.where(low_r, last[0], last[1]))
        state_ref[pair] = prev * cdec + st_new
        ys.append(y_pair)
    y = jnp.concatenate(ys, axis=1) + dskip_ref[...] * xs
    y_ref[...] = _rms(y * _silu(z_ref[...]), nw_ref[...]).astype(y_ref.dtype)


def _ssd(xbc, z, fdt, conv_w, conv_b, b_forget, dt_bias, a_log, d_skip, norm_w, batch, seq):
    nc = seq // BLOCK
    row = lambda b, c: (b * nc + c, 0)
    const = lambda b, c: (0, 0)
    zpad = jnp.zeros((LANES - FOX_HEADS - SSM_HEADS,), jnp.float32)
    bias_row = jnp.concatenate([b_forget, dt_bias, zpad]).reshape(1, LANES)
    alog_row = jnp.concatenate([jnp.zeros((FOX_HEADS,), jnp.float32), a_log, zpad]).reshape(1, LANES)
    dskip_row = jnp.repeat(d_skip, HEAD_DIM).reshape(1, SSM_W)
    return pl.pallas_call(
        _ssd_kernel,
        grid=(batch, nc),
        in_specs=[pl.BlockSpec((BLOCK, SSM_CONV_W), row),
                  pl.BlockSpec((BLOCK, SSM_W), row),
                  pl.BlockSpec((BLOCK, LANES), row),
                  pl.BlockSpec((SSM_CONV, SSM_CONV_W), const),
                  pl.BlockSpec((1, SSM_CONV_W), const),
                  pl.BlockSpec((1, LANES), const),
                  pl.BlockSpec((1, LANES), const),
                  pl.BlockSpec((1, SSM_W), const),
                  pl.BlockSpec((1, SSM_W), const)],
        out_specs=[pl.BlockSpec((BLOCK, SSM_W), row),
                   pl.BlockSpec((BLOCK, FOX_W), row)],
        out_shape=[jax.ShapeDtypeStruct((batch * seq, SSM_W), jnp.bfloat16),
                   jax.ShapeDtypeStruct((batch * seq, FOX_W), jnp.bfloat16)],
        scratch_shapes=[pltpu.VMEM((SSM_HEADS // 2, 2 * HEAD_DIM, SSM_STATE), jnp.float32),
                        pltpu.VMEM((8 + BLOCK, SSM_CONV_W), jnp.float32),
                        pltpu.VMEM((1, LANES), jnp.float32)],
        compiler_params=pltpu.CompilerParams(dimension_semantics=("arbitrary", "arbitrary")),
        name="ssd",
    )(xbc, z, fdt, conv_w, conv_b.reshape(1, SSM_CONV_W), bias_row, alog_row, dskip_row,
      norm_w.reshape(1, SSM_W))


def _memkv_kernel(mem_ref, nw_ref, wk_ref, wv_ref, k_ref, v_ref):
    mn = _rms(mem_ref[...], nw_ref[...]).astype(jnp.bfloat16)
    k_ref[...] = _bdot(mn, wk_ref[...]).astype(k_ref.dtype)
    v_ref[...] = _bdot(mn, wv_ref[...]).astype(v_ref.dtype)


def _memkv(mem2, norm_w, wk, wv, mem_tokens):
    t = mem2.shape[0]
    row = lambda b: (b, 0)
    const = lambda b: (0, 0)
    return pl.pallas_call(
        _memkv_kernel,
        grid=(t // mem_tokens,),
        in_specs=[pl.BlockSpec((mem_tokens, D_MODEL), row),
                  pl.BlockSpec((1, D_MODEL), const),
                  pl.BlockSpec((D_MODEL, D_MODEL), const),
                  pl.BlockSpec((D_MODEL, D_MODEL), const)],
        out_specs=[pl.BlockSpec((mem_tokens, D_MODEL), row)] * 2,
        out_shape=[jax.ShapeDtypeStruct((t, D_MODEL), jnp.bfloat16)] * 2,
        compiler_params=pltpu.CompilerParams(dimension_semantics=("arbitrary",)),
        name="memkv",
    )(mem2, norm_w.reshape(1, D_MODEL), wk.astype(jnp.bfloat16), wv.astype(jnp.bfloat16))


def _out_kernel(x_ref, ya_ref, yb_ref, yc_ref, wo_ref, nq_ref, wq_ref, k_ref, v_ref, wmo_ref, fn_ref,
                o_ref, *, final_norm):
    x1 = (x_ref[...]
          + _bdot(ya_ref[...], wo_ref[0:SWA_W, :])
          + _bdot(yb_ref[...], wo_ref[SWA_W:SWA_W + FOX_W, :])
          + _bdot(yc_ref[...], wo_ref[SWA_W + FOX_W:, :]))
    hq = _rms(x1, nq_ref[...]).astype(jnp.bfloat16)
    q = (_bdot(hq, wq_ref[...]) * (MEM_HEAD_DIM ** -0.5)).astype(jnp.bfloat16)
    heads = []
    for h in range(MEM_HEADS):
        sl = slice(h * MEM_HEAD_DIM, (h + 1) * MEM_HEAD_DIM)
        s = _bdot_nt(q[:, sl], k_ref[:, sl])
        m = jnp.max(s, axis=-1, keepdims=True)
        p = jnp.exp(s - m)
        probs = (p / jnp.sum(p, axis=-1, keepdims=True)).astype(jnp.bfloat16)
        heads.append(_bdot(probs, v_ref[:, sl]).astype(jnp.bfloat16))
    att = jnp.concatenate(heads, axis=1)
    x2 = x1 + _bdot(att, wmo_ref[...])
    if final_norm:
        x2 = _rms(x2, fn_ref[...])
    o_ref[...] = x2


def _out_block(x2, ya, yb, yc, w_out, norm_xq_w, w_mq, kmem, vmem, w_mo, final_w, seq, mem_tokens, tm,
               final_norm):
    t = x2.shape[0]
    nblk_s = seq // tm
    row = lambda i: (i, 0)
    const = lambda i: (0, 0)
    memrow = lambda i: (i // nblk_s, 0)
    bf = jnp.bfloat16
    return pl.pallas_call(
        functools.partial(_out_kernel, final_norm=final_norm),
        grid=(t // tm,),
        in_specs=[pl.BlockSpec((tm, D_MODEL), row),
                  pl.BlockSpec((tm, SWA_W), row),
                  pl.BlockSpec((tm, FOX_W), row),
                  pl.BlockSpec((tm, SSM_W), row),
                  pl.BlockSpec((SWA_W + FOX_W + SSM_W, D_MODEL), const),
                  pl.BlockSpec((1, D_MODEL), const),
                  pl.BlockSpec((D_MODEL, D_MODEL), const),
                  pl.BlockSpec((mem_tokens, D_MODEL), memrow),
                  pl.BlockSpec((mem_tokens, D_MODEL), memrow),
                  pl.BlockSpec((D_MODEL, D_MODEL), const),
                  pl.BlockSpec((1, D_MODEL), const)],
        out_specs=pl.BlockSpec((tm, D_MODEL), row),
        out_shape=jax.ShapeDtypeStruct((t, D_MODEL), jnp.float32),
        compiler_params=pltpu.CompilerParams(dimension_semantics=("arbitrary",),
                                             vmem_limit_bytes=_VMEM_LIMIT),
        name="outproj_mem",
    )(x2, ya, yb, yc, w_out.astype(bf), norm_xq_w.reshape(1, D_MODEL), w_mq.astype(bf), kmem, vmem,
      w_mo.astype(bf), final_w.reshape(1, D_MODEL))


def _row_tile(seq):
    return min(512, seq)


def kernel(x, mem, norm_mix_w, w_in, b_forget, swa_sinks, conv_w, conv_b, dt_bias, a_log, d_skip, ssm_norm_w, w_out, norm_xq_w, norm_mem_w, w_mq, w_mk, w_mv, w_mo, final_norm_w):
    batch, seq, _ = x.shape
    mem_tokens = mem.shape[1]
    depth = w_in.shape[0]
    assert seq % BLOCK == 0
    tm = _row_tile(seq)
    assert seq % tm == 0 and seq % FOX_TQ == 0 and seq % SWA_TQ == 0
    rope = _rope_tables(seq)
    x2 = x.reshape(batch * seq, D_MODEL)
    mem2 = mem.reshape(batch * mem_tokens, D_MODEL)
    for l in range(depth):
        w_r, w_t = _arrange_w_in(w_in[l])
        ka, g, kb, z, xbc, fdt, qbt, vbt, qat, vat = _inproj(x2, norm_mix_w[l], w_r, w_t, rope, seq, tm)
        yc, caug = _ssd(xbc, z, fdt, conv_w[l], conv_b[l], b_forget[l], dt_bias[l], a_log[l], d_skip[l],
                       ssm_norm_w[l], batch, seq)
        ya = _swa(qat, ka, vat, g, swa_sinks[l], batch, seq)
        yb = _fox(qbt, kb, caug, vbt, g, batch, seq)
        kmem, vmem = _memkv(mem2, norm_mem_w[l], w_mk[l], w_mv[l], mem_tokens)
        x2 = _out_block(x2, ya, yb, yc, w_out[l], norm_xq_w[l], w_mq[l], kmem, vmem, w_mo[l],
                        final_norm_w, seq, mem_tokens, tm, final_norm=(l == depth - 1))
    return x2.reshape(batch, seq, D_MODEL)
```

```python
import functools

import jax
import jax.numpy as jnp
from jax import lax
from jax.experimental import pallas as pl
from jax.experimental.pallas import tpu as pltpu

D_MODEL = 1024
HEAD_DIM = 64
BLOCK = 128
SWA_HEADS = 8
SWA_KV_HEADS = 2
FOX_HEADS = 8
SSM_HEADS = 16
SSM_STATE = 128
SSM_CONV = 4
MEM_HEADS = 4
MEM_HEAD_DIM = 256
ROPE_THETA = 10000.0
EPS = 1e-6
NEG_INF = -1e30

SWA_W = SWA_HEADS * HEAD_DIM
SWA_KV_W = SWA_KV_HEADS * HEAD_DIM
FOX_W = FOX_HEADS * HEAD_DIM
SSM_W = SSM_HEADS * HEAD_DIM
SSM_BC_W = 2 * SSM_STATE
SSM_CONV_W = SSM_W + 2 * SSM_BC_W
LANES = 128
LOG2E = 1.4426950408889634
SWA_TQ = 512
FOX_TQ = 512
FOX_TK = 512
DT_LANE0 = FOX_HEADS

_O_QA = 0
_O_KA = _O_QA + SWA_W
_O_VA = _O_KA + SWA_KV_W
_O_GA = _O_VA + SWA_KV_W
_O_QB = _O_GA + SWA_W
_O_KB = _O_QB + FOX_W
_O_VB = _O_KB + FOX_W
_O_FB = _O_VB + FOX_W
_O_GB = _O_FB + FOX_HEADS
_O_ZC = _O_GB + FOX_W
_O_XBC = _O_ZC + SSM_W
_O_DT = _O_XBC + SSM_CONV_W
_IN_W = _O_DT + SSM_HEADS

_SEG_W = (2 * SWA_KV_W, 2 * SWA_W, FOX_W, SSM_W, SSM_CONV_W, LANES)
_SEG_OFF = tuple(sum(_SEG_W[:i]) for i in range(len(_SEG_W)))
_PROJ_W = sum(_SEG_W)

_VMEM_LIMIT = 56 * 1024 * 1024


def _bdot(a, b):
    return jnp.dot(a, b, preferred_element_type=jnp.float32)


def _bdot_nt(a, b):
    return lax.dot_general(a, b, (((1,), (1,)), ((), ())), preferred_element_type=jnp.float32)


def _silu(x):
    h = 0.5 * x
    return h + h * jnp.tanh(h)


def _softplus(x):
    return jnp.maximum(x, 0.0) + jnp.log(1.0 + jnp.exp(-jnp.abs(x)))


def _rms(x, w):
    return x * lax.rsqrt(jnp.mean(x * x, axis=-1, keepdims=True) + EPS) * w


def _rope(x, cos, sin_signed):
    width = x.shape[1]
    reps = width // LANES
    lane = lax.broadcasted_iota(jnp.int32, x.shape, 1)
    first_half = (lane % HEAD_DIM) < (HEAD_DIM // 2)
    swapped = jnp.where(first_half,
                        pltpu.roll(x, width - HEAD_DIM // 2, 1),
                        pltpu.roll(x, HEAD_DIM // 2, 1))
    cos_t = jnp.concatenate([cos] * reps, axis=1)
    sin_t = jnp.concatenate([sin_signed] * reps, axis=1)
    return x * cos_t + swapped * sin_t


def _rope_t(x, cos, sin_signed):
    rows = x.shape[0]
    reps = rows // LANES
    r = lax.broadcasted_iota(jnp.int32, x.shape, 0)
    first_half = (r % HEAD_DIM) < (HEAD_DIM // 2)
    swapped = jnp.where(first_half,
                        pltpu.roll(x, rows - HEAD_DIM // 2, 0),
                        pltpu.roll(x, HEAD_DIM // 2, 0))
    cos_t = jnp.concatenate([cos] * reps, axis=0)
    sin_t = jnp.concatenate([sin_signed] * reps, axis=0)
    return x * cos_t + swapped * sin_t


def _inproj_kernel(x_ref, nw_ref, w_ref, wt_ref, cos_ref, sin_ref, cost_ref, sint_ref,
                   ka_ref, g_ref, kb_ref, z_ref, xbc_ref, fdt_ref, qbt_ref, vbt_ref, qat_ref, vat_ref):
    h = _rms(x_ref[...], nw_ref[...]).astype(jnp.bfloat16)

    def seg(i):
        return _bdot(h, w_ref[:, _SEG_OFF[i]:_SEG_OFF[i] + _SEG_W[i]])

    scale = HEAD_DIM ** -0.5 * LOG2E
    ka_ref[...] = _rope(seg(0), cos_ref[...], sin_ref[...]).astype(ka_ref.dtype)
    g_ref[...] = seg(1)
    kb_ref[...] = seg(2).astype(kb_ref.dtype)
    z_ref[...] = seg(3)
    xbc_ref[...] = seg(4)
    fdt_ref[...] = seg(5)
    o_vb, o_qa, o_va = FOX_W, 2 * FOX_W, 2 * FOX_W + SWA_W
    qbt_ref[...] = (_bdot_nt(wt_ref[0:o_vb, :], h) * scale).astype(qbt_ref.dtype)
    qat = _rope_t(_bdot_nt(wt_ref[o_qa:o_va, :], h), cost_ref[...], sint_ref[...])
    qat_ref[...] = (qat * scale).astype(qat_ref.dtype)
    vat_ref[...] = _bdot_nt(wt_ref[o_va:, :], h).astype(vat_ref.dtype)
    vbt = _bdot_nt(wt_ref[o_vb:o_qa, :], h).astype(vbt_ref.dtype)
    for hp in range(FOX_W // LANES):
        for c in range(vbt.shape[1] // FOX_TK):
            vbt_ref[hp, c] = vbt[hp * LANES:(hp + 1) * LANES, c * FOX_TK:(c + 1) * FOX_TK]


def _arrange_w_in(w_in):
    def cols(o, n):
        return w_in[:, o:o + n]
    k0, k1 = cols(_O_KA, HEAD_DIM), cols(_O_KA + HEAD_DIM, HEAD_DIM)
    pad = jnp.zeros((w_in.shape[0], LANES - FOX_HEADS - SSM_HEADS), w_in.dtype)
    parts = [k0, k1, k1, k0,
             cols(_O_GA, SWA_W), cols(_O_GB, FOX_W),
             cols(_O_KB, FOX_W),
             cols(_O_ZC, SSM_W), cols(_O_XBC, SSM_CONV_W),
             cols(_O_FB, FOX_HEADS), cols(_O_DT, SSM_HEADS), pad]
    w_r = jnp.concatenate(parts, axis=1).astype(jnp.bfloat16)
    w_t = jnp.concatenate([cols(_O_QB, FOX_W), cols(_O_VB, FOX_W), cols(_O_QA, SWA_W), cols(_O_VA, SWA_KV_W)],
                          axis=1).T.astype(jnp.bfloat16)
    return w_r, w_t


def _inproj(x2, norm_w, w_r, w_t, rope, seq, tm):
    t = x2.shape[0]
    nblk_s = seq // tm
    row = lambda i: (i, 0)
    const = lambda i: (0, 0)
    pos = lambda i: (i % nblk_s, 0)
    pos_t = lambda i: (0, i % nblk_s)
    col = lambda i: (0, i)
    bf, f32 = jnp.bfloat16, jnp.float32
    outs = [(2 * SWA_KV_W, bf), (2 * SWA_W, f32), (FOX_W, bf), (SSM_W, f32), (SSM_CONV_W, f32), (LANES, f32)]
    npair = FOX_W // LANES
    return pl.pallas_call(
        _inproj_kernel,
        grid=(t // tm,),
        in_specs=[pl.BlockSpec((tm, D_MODEL), row),
                  pl.BlockSpec((1, D_MODEL), const),
                  pl.BlockSpec((D_MODEL, _PROJ_W), const, pipeline_mode=pl.Buffered(1)),
                  pl.BlockSpec(w_t.shape, const, pipeline_mode=pl.Buffered(1)),
                  pl.BlockSpec((tm, LANES), pos),
                  pl.BlockSpec((tm, LANES), pos),
                  pl.BlockSpec((LANES, tm), pos_t),
                  pl.BlockSpec((LANES, tm), pos_t)],
        out_specs=[pl.BlockSpec((tm, w), row) for w, _ in outs]
        + [pl.BlockSpec((FOX_W, tm), col),
           pl.BlockSpec((npair, tm // FOX_TK, LANES, FOX_TK), lambda i: (0, i, 0, 0)),
           pl.BlockSpec((SWA_W, tm), col),
           pl.BlockSpec((SWA_KV_W, tm), col)],
        out_shape=[jax.ShapeDtypeStruct((t, w), d) for w, d in outs]
        + [jax.ShapeDtypeStruct((FOX_W, t), bf),
           jax.ShapeDtypeStruct((npair, t // FOX_TK, LANES, FOX_TK), bf),
           jax.ShapeDtypeStruct((SWA_W, t), bf),
           jax.ShapeDtypeStruct((SWA_KV_W, t), bf)],
        compiler_params=pltpu.CompilerParams(dimension_semantics=("arbitrary",),
                                             vmem_limit_bytes=_VMEM_LIMIT),
        name="inproj",
    )(x2, norm_w.reshape(1, D_MODEL), w_r, w_t, *rope)


def _rope_tables(seq):
    pos = jnp.arange(seq, dtype=jnp.float32)
    inv = 1.0 / (ROPE_THETA ** (jnp.arange(0, HEAD_DIM, 2, dtype=jnp.float32) / HEAD_DIM))
    ang = pos[:, None] * inv[None, :]
    cos, sin = jnp.cos(ang), jnp.sin(ang)
    cos_t = jnp.concatenate([cos, cos, cos, cos], axis=1)
    sin_t = jnp.concatenate([-sin, sin, -sin, sin], axis=1)
    return cos_t, sin_t, cos_t.T, sin_t.T


def _swa_kernel(sink_ref, qt_ref, kc_ref, kp_ref, vc_ref, vp_ref, g_ref, o_ref):
    n = pl.program_id(1)
    nsub = SWA_TQ // BLOCK
    key = lax.broadcasted_iota(jnp.int32, (2 * BLOCK, BLOCK), 0)
    qry = lax.broadcasted_iota(jnp.int32, (2 * BLOCK, BLOCK), 1)
    band = jnp.where((key > qry) & (key <= qry + BLOCK), 0.0, NEG_INF)
    band_first = jnp.where(key < BLOCK, NEG_INF, band)
    row = lax.broadcasted_iota(jnp.int32, (LANES, BLOCK), 0)
    ones_rows = jnp.ones((16, 2 * BLOCK), jnp.bfloat16)

    def scores(u):
        if u == 0:
            kcat = jnp.concatenate([kp_ref[...], kc_ref[0:BLOCK, :]], axis=0)
            vcat = jnp.concatenate([vp_ref[...], vc_ref[:, 0:BLOCK]], axis=1)
            bias = jnp.where(n > 0, band, band_first)
        else:
            kcat = kc_ref[(u - 1) * BLOCK:(u + 1) * BLOCK, :]
            vcat = vc_ref[:, (u - 1) * BLOCK:(u + 1) * BLOCK]
            bias = band
        tiles = []
        for c in range(SWA_HEADS // 2):
            qt = qt_ref[c * LANES:(c + 1) * LANES, u * BLOCK:(u + 1) * BLOCK].astype(jnp.float32)
            for half in range(2):
                kv = (2 * c + half) // (SWA_HEADS // SWA_KV_HEADS)
                sel = 0 if kv == half else 1
                in_half = (row < HEAD_DIM) if half == 0 else (row >= HEAD_DIM)
                w = jnp.where(in_half, qt, 0.0).astype(jnp.bfloat16)
                tiles.append(_bdot(kcat[:, sel * LANES:(sel + 1) * LANES], w) + bias)
        return tiles, vcat

    def finish(u, tiles, vcat):
        probs, sink_terms = [], []
        for head, s in enumerate(tiles):
            sink = sink_ref[0, head] * LOG2E
            m = jnp.maximum(jnp.max(s, axis=0, keepdims=True), sink)
            probs.append(jnp.exp2(s - m).astype(jnp.bfloat16))
            sink_terms.append(jnp.exp2(sink - m))
        outs = []
        for head, p in enumerate(probs):
            kv = head // (SWA_HEADS // SWA_KV_HEADS)
            lhs = jnp.concatenate([vcat[kv * HEAD_DIM:(kv + 1) * HEAD_DIM, :], ones_rows], axis=0)
            pv = _bdot(lhs, p)
            outs.append(pv[0:HEAD_DIM, :] / (pv[HEAD_DIM:HEAD_DIM + 1, :] + sink_terms[head]))
        y = jnp.concatenate(outs, axis=0).T
        rows = slice(u * BLOCK, (u + 1) * BLOCK)
        o_ref[rows, :] = (y * _silu(g_ref[rows, :])).astype(o_ref.dtype)

    pending = scores(0)
    for u in range(nsub):
        nxt = scores(u + 1) if u + 1 < nsub else None
        finish(u, *pending)
        pending = nxt


def _swa(qat, ka, vat, g, sinks, batch, seq):
    n = seq // SWA_TQ
    nsub = SWA_TQ // BLOCK
    cur = lambda b, i: (b * n + i, 0)
    cur_t = lambda b, i: (0, b * n + i)
    prev = lambda b, i: ((b * n + i) * nsub - jnp.minimum(i, 1), 0)
    prev_t = lambda b, i: (0, (b * n + i) * nsub - jnp.minimum(i, 1))
    return pl.pallas_call(
        _swa_kernel,
        grid=(batch, n),
        in_specs=[pl.BlockSpec(memory_space=pltpu.SMEM),
                  pl.BlockSpec((SWA_W, SWA_TQ), cur_t),
                  pl.BlockSpec((SWA_TQ, 2 * SWA_KV_W), cur),
                  pl.BlockSpec((BLOCK, 2 * SWA_KV_W), prev),
                  pl.BlockSpec((SWA_KV_W, SWA_TQ), cur_t),
                  pl.BlockSpec((SWA_KV_W, BLOCK), prev_t),
                  pl.BlockSpec((SWA_TQ, SWA_W), cur)],
        out_specs=pl.BlockSpec((SWA_TQ, SWA_W), cur),
        out_shape=jax.ShapeDtypeStruct((batch * seq, SWA_W), jnp.bfloat16),
        compiler_params=pltpu.CompilerParams(dimension_semantics=("arbitrary", "arbitrary")),
        name="swa",
    )(sinks.reshape(1, SWA_HEADS).astype(jnp.float32), qat, ka, ka, vat, vat, g)


def _fox_kernel(qt_ref, k_ref, ca_ref, vt_ref, g_ref, o_ref, w_buf, s_buf0, s_buf1, acc_buf, bias_buf):
    tq, tk = FOX_TQ, FOX_TK
    per_q = tq // tk
    nq = qt_ref.shape[1] // tq
    n_tiles = per_q * nq * (nq + 1) // 2
    all_masked = per_q
    acc_rows = acc_buf.shape[1]

    krow = lax.broadcasted_iota(jnp.int32, (tk, tq), 0)
    qcol = lax.broadcasted_iota(jnp.int32, (tk, tq), 1)
    for d in range(per_q):
        bias_buf[d] = jnp.where(krow + d * tk <= qcol, 0.0, NEG_INF)
    bias_buf[all_masked] = jnp.full((tk, tq), NEG_INF, jnp.float32)
    for buf in (s_buf0, s_buf1, acc_buf):
        buf[...] = jnp.zeros_like(buf)

    row = lax.broadcasted_iota(jnp.int32, (LANES, tq), 0)
    nsplit = 3
    pick = (jnp.where(row < nsplit, 1.0, 0.0).astype(jnp.bfloat16),
            jnp.where((row >= HEAD_DIM) & (row < HEAD_DIM + nsplit), 1.0, 0.0).astype(jnp.bfloat16))
    ones_rows = jnp.ones((acc_rows - HEAD_DIM, tk), jnp.bfloat16)

    def step(carry, s_cur, s_prv, masked):
        (i_a, j_a), (i_b, j_b, v_b), smax, m = carry

        @pl.when(j_a == 0)
        def _():
            q0 = pl.multiple_of(jnp.minimum(i_a, nq - 1) * tq, tq)
            qt = qt_ref[:, pl.ds(q0, tq)].astype(jnp.float32)
            q_lo = jnp.where(row < HEAD_DIM, qt, 0.0).astype(jnp.bfloat16)
            q_hi = jnp.where(row < HEAD_DIM, 0.0, qt).astype(jnp.bfloat16)
            w_buf[:, 0:tq] = jnp.concatenate([q_lo, pick[0]], axis=0)
            w_buf[:, tq:2 * tq] = jnp.concatenate([q_hi, pick[1]], axis=0)

        k0 = pl.multiple_of(jnp.where(i_a < nq, j_a, 0) * tk, tk)
        ka = jnp.concatenate([k_ref[pl.ds(k0, tk), :], ca_ref[pl.ds(k0, tk), :]], axis=1)
        s_new = _bdot(ka, w_buf[...])
        if masked:
            bias = bias_buf[jnp.where(i_a < nq, j_a - per_q * i_a, all_masked)]
            s_new = s_new + jnp.concatenate([bias, bias], axis=1)
        s_cur[...] = s_new
        smax_new = tuple(jnp.max(s_new[:, hh * tq:(hh + 1) * tq], axis=0, keepdims=True) for hh in range(2))

        first = j_b == 0
        s_prev = s_prv[...]
        vt = vt_ref[0, j_b]
        m_new = []
        for hh in range(2):
            m_prev = jnp.where(first, NEG_INF, m[hh])
            mn = jnp.maximum(m_prev, smax[hh])
            p = jnp.exp2(s_prev[:, hh * tq:(hh + 1) * tq] - mn).astype(jnp.bfloat16)
            lhs = jnp.concatenate([vt[hh * HEAD_DIM:(hh + 1) * HEAD_DIM, :], ones_rows], axis=0)
            acc_buf[hh] = jnp.exp2(m_prev - mn) * acc_buf[hh] + _bdot(lhs, p)
            m_new.append(mn)

        @pl.when(v_b & (j_b == per_q * (i_b + 1) - 1))
        def _():
            q0 = pl.multiple_of(i_b * tq, tq)
            out_t = jnp.concatenate([acc_buf[hh, 0:HEAD_DIM, :] / acc_buf[hh, HEAD_DIM:HEAD_DIM + 1, :]
                                     for hh in range(2)], axis=0)
            y = out_t.T * _silu(g_ref[pl.ds(q0, tq), :])
            o_ref[pl.ds(q0, tq), :] = y.astype(o_ref.dtype)

        last_a = j_a == per_q * (i_a + 1) - 1
        nxt_a = (jnp.where(last_a, i_a + 1, i_a), jnp.where(last_a, 0, j_a + 1))
        return (nxt_a, (i_a, jnp.where(i_a < nq, j_a, 0), i_a < nq), smax_new, tuple(m_new))

    def either(carry, bufs):
        i_a, j_a = carry[0]
        needs_mask = (j_a >= per_q * i_a) | (i_a >= nq)
        return lax.cond(needs_mask, lambda c: step(c, *bufs, True), lambda c: step(c, *bufs, False), carry)

    def body(_, carry):
        carry = either(carry, (s_buf0, s_buf1))
        return either(carry, (s_buf1, s_buf0))

    zi = jnp.int32(0)
    row_vec = lambda v: tuple(jnp.full((1, tq), v, jnp.float32) for _ in range(2))
    init = ((zi, zi), (zi, zi, False), row_vec(0.0), row_vec(0.0))
    lax.fori_loop(0, (n_tiles + 2) // 2, body, init)


def _fox(qbt, kb, caug, vbt, g, batch, seq):
    npair = FOX_HEADS // 2
    nkt = seq // FOX_TK
    gcol0 = SWA_W // LANES
    return pl.pallas_call(
        _fox_kernel,
        grid=(batch, npair),
        in_specs=[pl.BlockSpec((LANES, seq), lambda b, h: (h, b)),
                  pl.BlockSpec((seq, LANES), lambda b, h: (b, h)),
                  pl.BlockSpec((seq, LANES), lambda b, h: (b, h)),
                  pl.BlockSpec((1, nkt, LANES, FOX_TK), lambda b, h: (h, b, 0, 0)),
                  pl.BlockSpec((seq, LANES), lambda b, h: (b, gcol0 + h))],
        out_specs=pl.BlockSpec((seq, LANES), lambda b, h: (b, h)),
        out_shape=jax.ShapeDtypeStruct((batch * seq, FOX_W), jnp.bfloat16),
        scratch_shapes=[pltpu.VMEM((2 * LANES, 2 * FOX_TQ), jnp.bfloat16),
                        pltpu.VMEM((FOX_TK, 2 * FOX_TQ), jnp.float32),
                        pltpu.VMEM((FOX_TK, 2 * FOX_TQ), jnp.float32),
                        pltpu.VMEM((2, HEAD_DIM + 16, FOX_TQ), jnp.float32),
                        pltpu.VMEM((FOX_TQ // FOX_TK + 1, FOX_TK, FOX_TQ), jnp.float32)],
        compiler_params=pltpu.CompilerParams(dimension_semantics=("arbitrary", "arbitrary"),
                                             vmem_limit_bytes=_VMEM_LIMIT),
        name="fox",
    )(qbt, kb, caug, vbt, g)


def _split3(x):
    hi = x.astype(jnp.bfloat16)
    r1 = x - hi.astype(jnp.float32)
    mid = r1.astype(jnp.bfloat16)
    lo = (r1 - mid.astype(jnp.float32)).astype(jnp.bfloat16)
    return hi, mid, lo


def _ssd_kernel(xbc_ref, z_ref, fdt_ref, cw_ref, cb_ref, bias_ref, alog_ref, dskip_ref, nw_ref,
                y_ref, ca_ref, state_ref, tail_ref, ccarry_ref):
    ci = pl.program_id(1)
    L = BLOCK

    @pl.when(ci == 0)
    def _():
        state_ref[...] = jnp.zeros_like(state_ref)
        tail_ref[...] = jnp.zeros_like(tail_ref)
        ccarry_ref[...] = jnp.zeros_like(ccarry_ref)

    u = xbc_ref[...]
    tail_ref[8:8 + L, :] = u
    conv = cb_ref[...] + cw_ref[SSM_CONV - 1:SSM_CONV, :] * u
    for k in range(1, SSM_CONV):
        conv = conv + cw_ref[SSM_CONV - 1 - k:SSM_CONV - k, :] * tail_ref[8 - k:8 - k + L, :]
    tail_ref[0:8, :] = u[L - 8:L]
    act = _silu(conv)
    xs = act[:, :SSM_W]
    bm = act[:, SSM_W:SSM_W + SSM_BC_W].astype(jnp.bfloat16)
    cm = act[:, SSM_W + SSM_BC_W:].astype(jnp.bfloat16)

    lane = lax.broadcasted_iota(jnp.int32, (L, LANES), 1)
    vals = fdt_ref[...] + bias_ref[...]
    is_f = lane < DT_LANE0
    is_dt = (lane >= DT_LANE0) & (lane < DT_LANE0 + SSM_HEADS)
    sp = _softplus(jnp.where(is_f, -vals, vals))
    a_row = jnp.where(is_dt[0:1], -jnp.exp(alog_ref[...]), 0.0)
    dt = jnp.where(is_dt, sp, 0.0)
    scan_in = jnp.where(is_f, -sp, dt * a_row)

    ri = lax.broadcasted_iota(jnp.int32, (L, L), 0)
    cj = lax.broadcasted_iota(jnp.int32, (L, L), 1)
    causal = cj <= ri
    tri = jnp.where(causal, 1.0, 0.0).astype(jnp.bfloat16)
    hi, mid, lo = _split3(scan_in)
    cs = _bdot(tri, hi) + _bdot(tri, mid) + _bdot(tri, lo)
    cs_t = cs.T

    c_full = cs + ccarry_ref[...]
    ccarry_ref[...] = jnp.where(is_f[0:1], c_full[L - 1:L, :], 0.0)
    lane64 = lane % HEAD_DIM
    ca_cols = []
    for hp in range(FOX_HEADS // 2):
        negc = jnp.where(lane < HEAD_DIM, c_full[:, 2 * hp:2 * hp + 1], c_full[:, 2 * hp + 1:2 * hp + 2]) * (-LOG2E)
        hi, mid, lo = (term.astype(jnp.float32) for term in _split3(negc))
        terms = jnp.where(lane64 == 0, hi, jnp.where(lane64 == 1, mid, jnp.where(lane64 == 2, lo, 0.0)))
        ca_cols.append(terms.astype(jnp.bfloat16))
    ca_ref[...] = jnp.concatenate(ca_cols, axis=1)

    rowp = lax.broadcasted_iota(jnp.int32, (L, LANES), 0)
    low_l = lane < HEAD_DIM
    low_r = rowp < HEAD_DIM
    hpg = SSM_HEADS // 2
    ys = []
    cb = [None, None]
    for pair in range(SSM_HEADS // 2):
        grp = (2 * pair) // hpg
        bm_g = bm[:, grp * SSM_STATE:(grp + 1) * SSM_STATE]
        cm_g = cm[:, grp * SSM_STATE:(grp + 1) * SSM_STATE]
        if cb[grp] is None:
            cb[grp] = _bdot_nt(cm_g, bm_g)
        x_pair = xs[:, pair * LANES:(pair + 1) * LANES]
        ha, hb = DT_LANE0 + 2 * pair, DT_LANE0 + 2 * pair + 1
        acs_col = (cs[:, ha:ha + 1], cs[:, hb:hb + 1])
        acs_row = (cs_t[ha:ha + 1, :], cs_t[hb:hb + 1, :])
        dt_pair = jnp.where(low_l, dt[:, ha:ha + 1], dt[:, hb:hb + 1])
        xdt = x_pair * dt_pair
        xdt_b = xdt.astype(jnp.bfloat16)
        y_diag = []
        for hh in range(2):
            diff = acs_col[hh] - acs_row[hh]
            decay = jnp.exp(jnp.where(causal, diff, NEG_INF))
            gmat = (cb[grp] * decay).astype(jnp.bfloat16)
            y_diag.append(_bdot(gmat, xdt_b))
        y_pair = jnp.where(low_l, y_diag[0], y_diag[1])
        prev = state_ref[pair]
        y_off = _bdot_nt(cm_g, prev.astype(jnp.bfloat16))
        y_pair = y_pair + y_off * jnp.exp(jnp.where(low_l, acs_col[0], acs_col[1]))
        last = (acs_col[0][L - 1:L, :], acs_col[1][L - 1:L, :])
        dst = jnp.exp(jnp.where(low_l, last[0] - acs_col[0], last[1] - acs_col[1]))
        st_new = _bdot((xdt * dst).T.astype(jnp.bfloat16), bm_g)
        cdec = jnp.exp(jnp.where(low_r, last[0], last[1]))
        state_ref[pair] = prev * cdec + st_new
        ys.append(y_pair)
    y = jnp.concatenate(ys, axis=1) + dskip_ref[...] * xs
    y_ref[...] = _rms(y * _silu(z_ref[...]), nw_ref[...]).astype(y_ref.dtype)


def _ssd(xbc, z, fdt, conv_w, conv_b, b_forget, dt_bias, a_log, d_skip, norm_w, batch, seq):
    nc = seq // BLOCK
    row = lambda b, c: (b * nc + c, 0)
    const = lambda b, c: (0, 0)
    zpad = jnp.zeros((LANES - FOX_HEADS - SSM_HEADS,), jnp.float32)
    bias_row = jnp.concatenate([b_forget, dt_bias, zpad]).reshape(1, LANES)
    alog_row = jnp.concatenate([jnp.zeros((FOX_HEADS,), jnp.float32), a_log, zpad]).reshape(1, LANES)
    dskip_row = jnp.repeat(d_skip, HEAD_DIM).reshape(1, SSM_W)
    return pl.pallas_call(
        _ssd_kernel,
        grid=(batch, nc),
        in_specs=[pl.BlockSpec((BLOCK, SSM_CONV_W), row),
                  pl.BlockSpec((BLOCK, SSM_W), row),
                  pl.BlockSpec((BLOCK, LANES), row),
                  pl.BlockSpec((SSM_CONV, SSM_CONV_W), const),
                  pl.BlockSpec((1, SSM_CONV_W), const),
                  pl.BlockSpec((1, LANES), const),
                  pl.BlockSpec((1, LANES), const),
                  pl.BlockSpec((1, SSM_W), const),
                  pl.BlockSpec((1, SSM_W), const)],
        out_specs=[pl.BlockSpec((BLOCK, SSM_W), row),
                   pl.BlockSpec((BLOCK, FOX_W), row)],
        out_shape=[jax.ShapeDtypeStruct((batch * seq, SSM_W), jnp.bfloat16),
                   jax.ShapeDtypeStruct((batch * seq, FOX_W), jnp.bfloat16)],
        scratch_shapes=[pltpu.VMEM((SSM_HEADS // 2, 2 * HEAD_DIM, SSM_STATE), jnp.float32),
                        pltpu.VMEM((8 + BLOCK, SSM_CONV_W), jnp.float32),
                        pltpu.VMEM((1, LANES), jnp.float32)],
        compiler_params=pltpu.CompilerParams(dimension_semantics=("arbitrary", "arbitrary")),
        name="ssd",
    )(xbc, z, fdt, conv_w, conv_b.reshape(1, SSM_CONV_W), bias_row, alog_row, dskip_row,
      norm_w.reshape(1, SSM_W))


def _memkv_kernel(mem_ref, nw_ref, wk_ref, wv_ref, k_ref, v_ref):
    mn = _rms(mem_ref[...], nw_ref[...]).astype(jnp.bfloat16)
    k_ref[...] = _bdot(mn, wk_ref[...]).astype(k_ref.dtype)
    v_ref[...] = _bdot(mn, wv_ref[...]).astype(v_ref.dtype)


def _memkv(mem2, norm_w, wk, wv, mem_tokens):
    t = mem2.shape[0]
    row = lambda b: (b, 0)
    const = lambda b: (0, 0)
    return pl.pallas_call(
        _memkv_kernel,
        grid=(t // mem_tokens,),
        in_specs=[pl.BlockSpec((mem_tokens, D_MODEL), row),
                  pl.BlockSpec((1, D_MODEL), const),
                  pl.BlockSpec((D_MODEL, D_MODEL), const),
                  pl.BlockSpec((D_MODEL, D_MODEL), const)],
        out_specs=[pl.BlockSpec((mem_tokens, D_MODEL), row)] * 2,
        out_shape=[jax.ShapeDtypeStruct((t, D_MODEL), jnp.bfloat16)] * 2,
        compiler_params=pltpu.CompilerParams(dimension_semantics=("arbitrary",)),
        name="memkv",
    )(mem2, norm_w.reshape(1, D_MODEL), wk.astype(jnp.bfloat16), wv.astype(jnp.bfloat16))


def _out_kernel(x_ref, ya_ref, yb_ref, yc_ref, wo_ref, nq_ref, wq_ref, k_ref, v_ref, wmo_ref, fn_ref,
                o_ref, *, final_norm):
    x1 = (x_ref[...]
          + _bdot(ya_ref[...], wo_ref[0:SWA_W, :])
          + _bdot(yb_ref[...], wo_ref[SWA_W:SWA_W + FOX_W, :])
          + _bdot(yc_ref[...], wo_ref[SWA_W + FOX_W:, :]))
    hq = _rms(x1, nq_ref[...]).astype(jnp.bfloat16)
    q = (_bdot(hq, wq_ref[...]) * (MEM_HEAD_DIM ** -0.5)).astype(jnp.bfloat16)
    heads = []
    for h in range(MEM_HEADS):
        sl = slice(h * MEM_HEAD_DIM, (h + 1) * MEM_HEAD_DIM)
        s = _bdot_nt(q[:, sl], k_ref[:, sl])
        m = jnp.max(s, axis=-1, keepdims=True)
        p = jnp.exp(s - m)
        probs = (p / jnp.sum(p, axis=-1, keepdims=True)).astype(jnp.bfloat16)
        heads.append(_bdot(probs, v_ref[:, sl]).astype(jnp.bfloat16))
    att = jnp.concatenate(heads, axis=1)
    x2 = x1 + _bdot(att, wmo_ref[...])
    if final_norm:
        x2 = _rms(x2, fn_ref[...])
    o_ref[...] = x2


def _out_block(x2, ya, yb, yc, w_out, norm_xq_w, w_mq, kmem, vmem, w_mo, final_w, seq, mem_tokens, tm,
               final_norm):
    t = x2.shape[0]
    nblk_s = seq // tm
    row = lambda i: (i, 0)
    const = lambda i: (0, 0)
    memrow = lambda i: (i // nblk_s, 0)
    bf = jnp.bfloat16
    return pl.pallas_call(
        functools.partial(_out_kernel, final_norm=final_norm),
        grid=(t // tm,),
        in_specs=[pl.BlockSpec((tm, D_MODEL), row),
                  pl.BlockSpec((tm, SWA_W), row),
                  pl.BlockSpec((tm, FOX_W), row),
                  pl.BlockSpec((tm, SSM_W), row),
                  pl.BlockSpec((SWA_W + FOX_W + SSM_W, D_MODEL), const),
                  pl.BlockSpec((1, D_MODEL), const),
                  pl.BlockSpec((D_MODEL, D_MODEL), const),
                  pl.BlockSpec((mem_tokens, D_MODEL), memrow),
                  pl.BlockSpec((mem_tokens, D_MODEL), memrow),
                  pl.BlockSpec((D_MODEL, D_MODEL), const),
                  pl.BlockSpec((1, D_MODEL), const)],
        out_specs=pl.BlockSpec((tm, D_MODEL), row),
        out_shape=jax.ShapeDtypeStruct((t, D_MODEL), jnp.float32),
        compiler_params=pltpu.CompilerParams(dimension_semantics=("arbitrary",),
                                             vmem_limit_bytes=_VMEM_LIMIT),
        name="outproj_mem",
    )(x2, ya, yb, yc, w_out.astype(bf), norm_xq_w.reshape(1, D_MODEL), w_mq.astype(bf), kmem, vmem,
      w_mo.astype(bf), final_w.reshape(1, D_MODEL))


def _row_tile(seq):
    return min(512, seq)


def kernel(x, mem, norm_mix_w, w_in, b_forget, swa_sinks, conv_w, conv_b, dt_bias, a_log, d_skip, ssm_norm_w, w_out, norm_xq_w, norm_mem_w, w_mq, w_mk, w_mv, w_mo, final_norm_w):
    batch, seq, _ = x.shape
    mem_tokens = mem.shape[1]
    depth = w_in.shape[0]
    assert seq % BLOCK == 0
    tm = _row_tile(seq)
    assert seq % tm == 0 and seq % FOX_TQ == 0 and seq % SWA_TQ == 0
    rope = _rope_tables(seq)
    x2 = x.reshape(batch * seq, D_MODEL)
    mem2 = mem.reshape(batch * mem_tokens, D_MODEL)
    for l in range(depth):
        w_r, w_t = _arrange_w_in(w_in[l])
        ka, g, kb, z, xbc, fdt, qbt, vbt, qat, vat = _inproj(x2, norm_mix_w[l], w_r, w_t, rope, seq, tm)
        yc, caug = _ssd(xbc, z, fdt, conv_w[l], conv_b[l], b_forget[l], dt_bias[l], a_log[l], d_skip[l],
                       ssm_norm_w[l], batch, seq)
        ya = _swa(qat, ka, vat, g, swa_sinks[l], batch, seq)
        yb = _fox(qbt, kb, caug, vbt, g, batch, seq)
        kmem, vmem = _memkv(mem2, norm_mem_w[l], w_mk[l], w_mv[l], mem_tokens)
        x2 = _out_block(x2, ya, yb, yc, w_out[l], norm_xq_w[l], w_mq[l], kmem, vmem, w_mo[l],
                        final_norm_w, seq, mem_tokens, tm, final_norm=(l == depth - 1))
    return x2.reshape(batch, seq, D_MODEL)
```

```python
import functools

import jax
import jax.numpy as jnp
from jax import lax
from jax.experimental import pallas as pl
from jax.experimental.pallas import tpu as pltpu

D_MODEL = 1024
HEAD_DIM = 64
BLOCK = 128
SWA_HEADS = 8
SWA_KV_HEADS = 2
FOX_HEADS = 8
SSM_HEADS = 16
SSM_STATE = 128
SSM_CONV = 4
MEM_HEADS = 4
MEM_HEAD_DIM = 256
ROPE_THETA = 10000.0
EPS = 1e-6
NEG_INF = -1e30

SWA_W = SWA_HEADS * HEAD_DIM
SWA_KV_W = SWA_KV_HEADS * HEAD_DIM
FOX_W = FOX_HEADS * HEAD_DIM
SSM_W = SSM_HEADS * HEAD_DIM
SSM_BC_W = 2 * SSM_STATE
SSM_CONV_W = SSM_W + 2 * SSM_BC_W
LANES = 128
LOG2E = 1.4426950408889634
SWA_TQ = 512
FOX_TQ = 512
FOX_TK = 512
DT_LANE0 = FOX_HEADS

_O_QA = 0
_O_KA = _O_QA + SWA_W
_O_VA = _O_KA + SWA_KV_W
_O_GA = _O_VA + SWA_KV_W
_O_QB = _O_GA + SWA_W
_O_KB = _O_QB + FOX_W
_O_VB = _O_KB + FOX_W
_O_FB = _O_VB + FOX_W
_O_GB = _O_FB + FOX_HEADS
_O_ZC = _O_GB + FOX_W
_O_XBC = _O_ZC + SSM_W
_O_DT = _O_XBC + SSM_CONV_W
_IN_W = _O_DT + SSM_HEADS

_SEG_W = (2 * SWA_KV_W, 2 * SWA_W, FOX_W, SSM_W, SSM_CONV_W, LANES)
_SEG_OFF = tuple(sum(_SEG_W[:i]) for i in range(len(_SEG_W)))
_PROJ_W = sum(_SEG_W)

_VMEM_LIMIT = 56 * 1024 * 1024


def _bdot(a, b):
    return jnp.dot(a, b, preferred_element_type=jnp.float32)


def _bdot_nt(a, b):
    return lax.dot_general(a, b, (((1,), (1,)), ((), ())), preferred_element_type=jnp.float32)


def _silu(x):
    h = 0.5 * x
    return h + h * jnp.tanh(h)


def _softplus(x):
    return jnp.maximum(x, 0.0) + jnp.log(1.0 + jnp.exp(-jnp.abs(x)))


def _rms(x, w):
    return x * lax.rsqrt(jnp.mean(x * x, axis=-1, keepdims=True) + EPS) * w


def _rope(x, cos, sin_signed):
    width = x.shape[1]
    reps = width // LANES
    lane = lax.broadcasted_iota(jnp.int32, x.shape, 1)
    first_half = (lane % HEAD_DIM) < (HEAD_DIM // 2)
    swapped = jnp.where(first_half,
                        pltpu.roll(x, width - HEAD_DIM // 2, 1),
                        pltpu.roll(x, HEAD_DIM // 2, 1))
    cos_t = jnp.concatenate([cos] * reps, axis=1)
    sin_t = jnp.concatenate([sin_signed] * reps, axis=1)
    return x * cos_t + swapped * sin_t


def _rope_t(x, cos, sin_signed):
    rows = x.shape[0]
    reps = rows // LANES
    r = lax.broadcasted_iota(jnp.int32, x.shape, 0)
    first_half = (r % HEAD_DIM) < (HEAD_DIM // 2)
    swapped = jnp.where(first_half,
                        pltpu.roll(x, rows - HEAD_DIM // 2, 0),
                        pltpu.roll(x, HEAD_DIM // 2, 0))
    cos_t = jnp.concatenate([cos] * reps, axis=0)
    sin_t = jnp.concatenate([sin_signed] * reps, axis=0)
    return x * cos_t + swapped * sin_t


def _inproj_kernel(x_ref, nw_ref, w_ref, wt_ref, cos_ref, sin_ref, cost_ref, sint_ref,
                   ka_ref, g_ref, kb_ref, z_ref, xbc_ref, fdt_ref, qbt_ref, vbt_ref, qat_ref, vat_ref):
    h = _rms(x_ref[...], nw_ref[...]).astype(jnp.bfloat16)

    def seg(i):
        return _bdot(h, w_ref[:, _SEG_OFF[i]:_SEG_OFF[i] + _SEG_W[i]])

    scale = HEAD_DIM ** -0.5 * LOG2E
    ka_ref[...] = _rope(seg(0), cos_ref[...], sin_ref[...]).astype(ka_ref.dtype)
    g_ref[...] = seg(1)
    kb_ref[...] = seg(2).astype(kb_ref.dtype)
    z_ref[...] = seg(3)
    xbc_ref[...] = seg(4)
    fdt_ref[...] = seg(5)
    o_vb, o_qa, o_va = FOX_W, 2 * FOX_W, 2 * FOX_W + SWA_W
    qbt_ref[...] = (_bdot_nt(wt_ref[0:o_vb, :], h) * scale).astype(qbt_ref.dtype)
    qat = _rope_t(_bdot_nt(wt_ref[o_qa:o_va, :], h), cost_ref[...], sint_ref[...])
    qat_ref[...] = (qat * scale).astype(qat_ref.dtype)
    vat_ref[...] = _bdot_nt(wt_ref[o_va:, :], h).astype(vat_ref.dtype)
    vbt = _bdot_nt(wt_ref[o_vb:o_qa, :], h).astype(vbt_ref.dtype)
    for hp in range(FOX_W // LANES):
        for c in range(vbt.shape[1] // FOX_TK):
            vbt_ref[hp, c] = vbt[hp * LANES:(hp + 1) * LANES, c * FOX_TK:(c + 1) * FOX_TK]


def _arrange_w_in(w_in):
    def cols(o, n):
        return w_in[:, o:o + n]
    k0, k1 = cols(_O_KA, HEAD_DIM), cols(_O_KA + HEAD_DIM, HEAD_DIM)
    pad = jnp.zeros((w_in.shape[0], LANES - FOX_HEADS - SSM_HEADS), w_in.dtype)
    parts = [k0, k1, k1, k0,
             cols(_O_GA, SWA_W), cols(_O_GB, FOX_W),
             cols(_O_KB, FOX_W),
             cols(_O_ZC, SSM_W), cols(_O_XBC, SSM_CONV_W),
             cols(_O_FB, FOX_HEADS), cols(_O_DT, SSM_HEADS), pad]
    w_r = jnp.concatenate(parts, axis=1).astype(jnp.bfloat16)
    w_t = jnp.concatenate([cols(_O_QB, FOX_W), cols(_O_VB, FOX_W), cols(_O_QA, SWA_W), cols(_O_VA, SWA_KV_W)],
                          axis=1).T.astype(jnp.bfloat16)
    return w_r, w_t


def _inproj(x2, norm_w, w_r, w_t, rope, seq, tm):
    t = x2.shape[0]
    nblk_s = seq // tm
    row = lambda i: (i, 0)
    const = lambda i: (0, 0)
    pos = lambda i: (i % nblk_s, 0)
    pos_t = lambda i: (0, i % nblk_s)
    col = lambda i: (0, i)
    bf, f32 = jnp.bfloat16, jnp.float32
    outs = [(2 * SWA_KV_W, bf), (2 * SWA_W, f32), (FOX_W, bf), (SSM_W, f32), (SSM_CONV_W, f32), (LANES, f32)]
    npair = FOX_W // LANES
    return pl.pallas_call(
        _inproj_kernel,
        grid=(t // tm,),
        in_specs=[pl.BlockSpec((tm, D_MODEL), row),
                  pl.BlockSpec((1, D_MODEL), const),
                  pl.BlockSpec((D_MODEL, _PROJ_W), const, pipeline_mode=pl.Buffered(1)),
                  pl.BlockSpec(w_t.shape, const, pipeline_mode=pl.Buffered(1)),
                  pl.BlockSpec((tm, LANES), pos),
                  pl.BlockSpec((tm, LANES), pos),
                  pl.BlockSpec((LANES, tm), pos_t),
                  pl.BlockSpec((LANES, tm), pos_t)],
        out_specs=[pl.BlockSpec((tm, w), row) for w, _ in outs]
        + [pl.BlockSpec((FOX_W, tm), col),
           pl.BlockSpec((npair, tm // FOX_TK, LANES, FOX_TK), lambda i: (0, i, 0, 0)),
           pl.BlockSpec((SWA_W, tm), col),
           pl.BlockSpec((SWA_KV_W, tm), col)],
        out_shape=[jax.ShapeDtypeStruct((t, w), d) for w, d in outs]
        + [jax.ShapeDtypeStruct((FOX_W, t), bf),
           jax.ShapeDtypeStruct((npair, t // FOX_TK, LANES, FOX_TK), bf),
           jax.ShapeDtypeStruct((SWA_W, t), bf),
           jax.ShapeDtypeStruct((SWA_KV_W, t), bf)],
        compiler_params=pltpu.CompilerParams(dimension_semantics=("arbitrary",),
                                             vmem_limit_bytes=_VMEM_LIMIT),
        name="inproj",
    )(x2, norm_w.reshape(1, D_MODEL), w_r, w_t, *rope)


def _rope_tables(seq):
    pos = jnp.arange(seq, dtype=jnp.float32)
    inv = 1.0 / (ROPE_THETA ** (jnp.arange(0, HEAD_DIM, 2, dtype=jnp.float32) / HEAD_DIM))
    ang = pos[:, None] * inv[None, :]
    cos, sin = jnp.cos(ang), jnp.sin(ang)
    cos_t = jnp.concatenate([cos, cos, cos, cos], axis=1)
    sin_t = jnp.concatenate([-sin, sin, -sin, sin], axis=1)
    return cos_t, sin_t, cos_t.T, sin_t.T


def _swa_kernel(sink_ref, qt_ref, kc_ref, kp_ref, vc_ref, vp_ref, g_ref, o_ref):
    n = pl.program_id(1)
    nsub = SWA_TQ // BLOCK
    key = lax.broadcasted_iota(jnp.int32, (2 * BLOCK, BLOCK), 0)
    qry = lax.broadcasted_iota(jnp.int32, (2 * BLOCK, BLOCK), 1)
    band = jnp.where((key > qry) & (key <= qry + BLOCK), 0.0, NEG_INF)
    band_first = jnp.where(key < BLOCK, NEG_INF, band)
    row = lax.broadcasted_iota(jnp.int32, (LANES, BLOCK), 0)
    ones_rows = jnp.ones((16, 2 * BLOCK), jnp.bfloat16)

    def scores(u):
        if u == 0:
            kcat = jnp.concatenate([kp_ref[...], kc_ref[0:BLOCK, :]], axis=0)
            vcat = jnp.concatenate([vp_ref[...], vc_ref[:, 0:BLOCK]], axis=1)
            bias = jnp.where(n > 0, band, band_first)
        else:
            kcat = kc_ref[(u - 1) * BLOCK:(u + 1) * BLOCK, :]
            vcat = vc_ref[:, (u - 1) * BLOCK:(u + 1) * BLOCK]
            bias = band
        tiles = []
        for c in range(SWA_HEADS // 2):
            qt = qt_ref[c * LANES:(c + 1) * LANES, u * BLOCK:(u + 1) * BLOCK].astype(jnp.float32)
            for half in range(2):
                kv = (2 * c + half) // (SWA_HEADS // SWA_KV_HEADS)
                sel = 0 if kv == half else 1
                in_half = (row < HEAD_DIM) if half == 0 else (row >= HEAD_DIM)
                w = jnp.where(in_half, qt, 0.0).astype(jnp.bfloat16)
                tiles.append(_bdot(kcat[:, sel * LANES:(sel + 1) * LANES], w) + bias)
        return tiles, vcat

    def finish(u, tiles, vcat):
        probs, sink_terms = [], []
        for head, s in enumerate(tiles):
            sink = sink_ref[0, head] * LOG2E
            m = jnp.maximum(jnp.max(s, axis=0, keepdims=True), sink)
            probs.append(jnp.exp2(s - m).astype(jnp.bfloat16))
            sink_terms.append(jnp.exp2(sink - m))
        outs = []
        for head, p in enumerate(probs):
            kv = head // (SWA_HEADS // SWA_KV_HEADS)
            lhs = jnp.concatenate([vcat[kv * HEAD_DIM:(kv + 1) * HEAD_DIM, :], ones_rows], axis=0)
            pv = _bdot(lhs, p)
            outs.append(pv[0:HEAD_DIM, :] / (pv[HEAD_DIM:HEAD_DIM + 1, :] + sink_terms[head]))
        y = jnp.concatenate(outs, axis=0).T
        rows = slice(u * BLOCK, (u + 1) * BLOCK)
        o_ref[rows, :] = (y * _silu(g_ref[rows, :])).astype(o_ref.dtype)

    pending = scores(0)
    for u in range(nsub):
        nxt = scores(u + 1) if u + 1 < nsub else None
        finish(u, *pending)
        pending = nxt


def _swa(qat, ka, vat, g, sinks, batch, seq):
    n = seq // SWA_TQ
    nsub = SWA_TQ // BLOCK
    cur = lambda b, i: (b * n + i, 0)
    cur_t = lambda b, i: (0, b * n + i)
    prev = lambda b, i: ((b * n + i) * nsub - jnp.minimum(i, 1), 0)
    prev_t = lambda b, i: (0, (b * n + i) * nsub - jnp.minimum(i, 1))
    return pl.pallas_call(
        _swa_kernel,
        grid=(batch, n),
        in_specs=[pl.BlockSpec(memory_space=pltpu.SMEM),
                  pl.BlockSpec((SWA_W, SWA_TQ), cur_t),
                  pl.BlockSpec((SWA_TQ, 2 * SWA_KV_W), cur),
                  pl.BlockSpec((BLOCK, 2 * SWA_KV_W), prev),
                  pl.BlockSpec((SWA_KV_W, SWA_TQ), cur_t),
                  pl.BlockSpec((SWA_KV_W, BLOCK), prev_t),
                  pl.BlockSpec((SWA_TQ, SWA_W), cur)],
        out_specs=pl.BlockSpec((SWA_TQ, SWA_W), cur),
        out_shape=jax.ShapeDtypeStruct((batch * seq, SWA_W), jnp.bfloat16),
        compiler_params=pltpu.CompilerParams(dimension_semantics=("arbitrary", "arbitrary")),
        name="swa",
    )(sinks.reshape(1, SWA_HEADS).astype(jnp.float32), qat, ka, ka, vat, vat, g)


def _fox_kernel(qt_ref, k_ref, ca_ref, vt_ref, g_ref, o_ref, w_buf, s_buf0, s_buf1, acc_buf, bias_buf):
    tq, tk = FOX_TQ, FOX_TK
    per_q = tq // tk
    nq = qt_ref.shape[1] // tq
    n_tiles = per_q * nq * (nq + 1) // 2
    all_masked = per_q + 1
    acc_rows = acc_buf.shape[1]

    krow = lax.broadcasted_iota(jnp.int32, (tk, tq), 0)
    qcol = lax.broadcasted_iota(jnp.int32, (tk, tq), 1)
    for d in range(per_q):
        bias_buf[1 + d] = jnp.where(krow + d * tk <= qcol, 0.0, NEG_INF)
    bias_buf[0] = jnp.zeros((tk, tq), jnp.float32)
    bias_buf[all_masked] = jnp.full((tk, tq), NEG_INF, jnp.float32)
    for buf in (s_buf0, s_buf1, acc_buf):
        buf[...] = jnp.zeros_like(buf)

    row = lax.broadcasted_iota(jnp.int32, (LANES, tq), 0)
    nsplit = 3
    pick = (jnp.where(row < nsplit, 1.0, 0.0).astype(jnp.bfloat16),
            jnp.where((row >= HEAD_DIM) & (row < HEAD_DIM + nsplit), 1.0, 0.0).astype(jnp.bfloat16))
    ones_rows = jnp.ones((acc_rows - HEAD_DIM, tk), jnp.bfloat16)

    def step(carry, s_cur, s_prv, masked):
        (i_a, j_a), (i_b, j_b, v_b), smax, m = carry

        @pl.when(j_a == 0)
        def _():
            q0 = pl.multiple_of(jnp.minimum(i_a, nq - 1) * tq, tq)
            qt = qt_ref[:, pl.ds(q0, tq)].astype(jnp.float32)
            q_lo = jnp.where(row < HEAD_DIM, qt, 0.0).astype(jnp.bfloat16)
            q_hi = jnp.where(row < HEAD_DIM, 0.0, qt).astype(jnp.bfloat16)
            w_buf[:, 0:tq] = jnp.concatenate([q_lo, pick[0]], axis=0)
            w_buf[:, tq:2 * tq] = jnp.concatenate([q_hi, pick[1]], axis=0)

        k0 = pl.multiple_of(jnp.where(i_a < nq, j_a, 0) * tk, tk)
        ka = jnp.concatenate([k_ref[pl.ds(k0, tk), :], ca_ref[pl.ds(k0, tk), :]], axis=1)
        s_new = _bdot(ka, w_buf[...])
        if masked:
            bias = bias_buf[jnp.where(i_a < nq, jnp.maximum(j_a - per_q * i_a + 1, 0), all_masked)]
            s_new = s_new + jnp.concatenate([bias, bias], axis=1)
        s_cur[...] = s_new
        smax_new = tuple(jnp.max(s_new[:, hh * tq:(hh + 1) * tq], axis=0, keepdims=True) for hh in range(2))

        first = j_b == 0
        s_prev = s_prv[...]
        vt = vt_ref[0, j_b]
        m_new = []
        for hh in range(2):
            m_prev = jnp.where(first, NEG_INF, m[hh])
            mn = jnp.maximum(m_prev, smax[hh])
            p = jnp.exp2(s_prev[:, hh * tq:(hh + 1) * tq] - mn).astype(jnp.bfloat16)
            lhs = jnp.concatenate([vt[hh * HEAD_DIM:(hh + 1) * HEAD_DIM, :], ones_rows], axis=0)
            acc_buf[hh] = jnp.exp2(m_prev - mn) * acc_buf[hh] + _bdot(lhs, p)
            m_new.append(mn)

        @pl.when(v_b & (j_b == per_q * (i_b + 1) - 1))
        def _():
            q0 = pl.multiple_of(i_b * tq, tq)
            out_t = jnp.concatenate([acc_buf[hh, 0:HEAD_DIM, :] / acc_buf[hh, HEAD_DIM:HEAD_DIM + 1, :]
                                     for hh in range(2)], axis=0)
            y = out_t.T * _silu(g_ref[pl.ds(q0, tq), :])
            o_ref[pl.ds(q0, tq), :] = y.astype(o_ref.dtype)

        last_a = j_a == per_q * (i_a + 1) - 1
        nxt_a = (jnp.where(last_a, i_a + 1, i_a), jnp.where(last_a, 0, j_a + 1))
        return (nxt_a, (i_a, jnp.where(i_a < nq, j_a, 0), i_a < nq), smax_new, tuple(m_new))

    def either(carry, bufs):
        i_a, j_a = carry[0]
        needs_mask = (j_a >= per_q * i_a) | (i_a >= nq)
        return step(carry, *bufs, True)

    def body(_, carry):
        carry = either(carry, (s_buf0, s_buf1))
        return either(carry, (s_buf1, s_buf0))

    zi = jnp.int32(0)
    row_vec = lambda v: tuple(jnp.full((1, tq), v, jnp.float32) for _ in range(2))
    init = ((zi, zi), (zi, zi, False), row_vec(0.0), row_vec(0.0))
    lax.fori_loop(0, (n_tiles + 2) // 2, body, init)


def _fox(qbt, kb, caug, vbt, g, batch, seq):
    npair = FOX_HEADS // 2
    nkt = seq // FOX_TK
    gcol0 = SWA_W // LANES
    return pl.pallas_call(
        _fox_kernel,
        grid=(batch, npair),
        in_specs=[pl.BlockSpec((LANES, seq), lambda b, h: (h, b)),
                  pl.BlockSpec((seq, LANES), lambda b, h: (b, h)),
                  pl.BlockSpec((seq, LANES), lambda b, h: (b, h)),
                  pl.BlockSpec((1, nkt, LANES, FOX_TK), lambda b, h: (h, b, 0, 0)),
                  pl.BlockSpec((seq, LANES), lambda b, h: (b, gcol0 + h))],
        out_specs=pl.BlockSpec((seq, LANES), lambda b, h: (b, h)),
        out_shape=jax.ShapeDtypeStruct((batch * seq, FOX_W), jnp.bfloat16),
        scratch_shapes=[pltpu.VMEM((2 * LANES, 2 * FOX_TQ), jnp.bfloat16),
                        pltpu.VMEM((FOX_TK, 2 * FOX_TQ), jnp.float32),
                        pltpu.VMEM((FOX_TK, 2 * FOX_TQ), jnp.float32),
                        pltpu.VMEM((2, HEAD_DIM + 16, FOX_TQ), jnp.float32),
                        pltpu.VMEM((FOX_TQ // FOX_TK + 2, FOX_TK, FOX_TQ), jnp.float32)],
        compiler_params=pltpu.CompilerParams(dimension_semantics=("arbitrary", "arbitrary"),
                                             vmem_limit_bytes=_VMEM_LIMIT),
        name="fox",
    )(qbt, kb, caug, vbt, g)


def _split3(x):
    hi = x.astype(jnp.bfloat16)
    r1 = x - hi.astype(jnp.float32)
    mid = r1.astype(jnp.bfloat16)
    lo = (r1 - mid.astype(jnp.float32)).astype(jnp.bfloat16)
    return hi, mid, lo


def _ssd_kernel(xbc_ref, z_ref, fdt_ref, cw_ref, cb_ref, bias_ref, alog_ref, dskip_ref, nw_ref,
                y_ref, ca_ref, state_ref, tail_ref, ccarry_ref):
    ci = pl.program_id(1)
    L = BLOCK

    @pl.when(ci == 0)
    def _():
        state_ref[...] = jnp.zeros_like(state_ref)
        tail_ref[...] = jnp.zeros_like(tail_ref)
        ccarry_ref[...] = jnp.zeros_like(ccarry_ref)

    u = xbc_ref[...]
    tail_ref[8:8 + L, :] = u
    conv = cb_ref[...] + cw_ref[SSM_CONV - 1:SSM_CONV, :] * u
    for k in range(1, SSM_CONV):
        conv = conv + cw_ref[SSM_CONV - 1 - k:SSM_CONV - k, :] * tail_ref[8 - k:8 - k + L, :]
    tail_ref[0:8, :] = u[L - 8:L]
    act = _silu(conv)
    xs = act[:, :SSM_W]
    bm = act[:, SSM_W:SSM_W + SSM_BC_W].astype(jnp.bfloat16)
    cm = act[:, SSM_W + SSM_BC_W:].astype(jnp.bfloat16)

    lane = lax.broadcasted_iota(jnp.int32, (L, LANES), 1)
    vals = fdt_ref[...] + bias_ref[...]
    is_f = lane < DT_LANE0
    is_dt = (lane >= DT_LANE0) & (lane < DT_LANE0 + SSM_HEADS)
    sp = _softplus(jnp.where(is_f, -vals, vals))
    a_row = jnp.where(is_dt[0:1], -jnp.exp(alog_ref[...]), 0.0)
    dt = jnp.where(is_dt, sp, 0.0)
    scan_in = jnp.where(is_f, -sp, dt * a_row)

    ri = lax.broadcasted_iota(jnp.int32, (L, L), 0)
    cj = lax.broadcasted_iota(jnp.int32, (L, L), 1)
    causal = cj <= ri
    tri = jnp.where(causal, 1.0, 0.0).astype(jnp.bfloat16)
    hi, mid, lo = _split3(scan_in)
    cs = _bdot(tri, hi) + _bdot(tri, mid) + _bdot(tri, lo)
    cs_t = cs.T

    c_full = cs + ccarry_ref[...]
    ccarry_ref[...] = jnp.where(is_f[0:1], c_full[L - 1:L, :], 0.0)
    lane64 = lane % HEAD_DIM
    ca_cols = []
    for hp in range(FOX_HEADS // 2):
        negc = jnp.where(lane < HEAD_DIM, c_full[:, 2 * hp:2 * hp + 1], c_full[:, 2 * hp + 1:2 * hp + 2]) * (-LOG2E)
        hi, mid, lo = (term.astype(jnp.float32) for term in _split3(negc))
        terms = jnp.where(lane64 == 0, hi, jnp.where(lane64 == 1, mid, jnp.where(lane64 == 2, lo, 0.0)))
        ca_cols.append(terms.astype(jnp.bfloat16))
    ca_ref[...] = jnp.concatenate(ca_cols, axis=1)

    rowp = lax.broadcasted_iota(jnp.int32, (L, LANES), 0)
    low_l = lane < HEAD_DIM
    low_r = rowp < HEAD_DIM
    hpg = SSM_HEADS // 2
    ys = []
    cb = [None, None]
    for pair in range(SSM_HEADS // 2):
        grp = (2 * pair) // hpg
        bm_g = bm[:, grp * SSM_STATE:(grp + 1) * SSM_STATE]
        cm_g = cm[:, grp * SSM_STATE:(grp + 1) * SSM_STATE]
        if cb[grp] is None:
            cb[grp] = _bdot_nt(cm_g, bm_g)
        x_pair = xs[:, pair * LANES:(pair + 1) * LANES]
        ha, hb = DT_LANE0 + 2 * pair, DT_LANE0 + 2 * pair + 1
        acs_col = (cs[:, ha:ha + 1], cs[:, hb:hb + 1])
        acs_row = (cs_t[ha:ha + 1, :], cs_t[hb:hb + 1, :])
        dt_pair = jnp.where(low_l, dt[:, ha:ha + 1], dt[:, hb:hb + 1])
        xdt = x_pair * dt_pair
        xdt_b = xdt.astype(jnp.bfloat16)
        y_diag = []
        for hh in range(2):
            diff = acs_col[hh] - acs_row[hh]
            decay = jnp.exp(jnp.where(causal, diff, NEG_INF))
            gmat = (cb[grp] * decay).astype(jnp.bfloat16)
            y_diag.append(_bdot(gmat, xdt_b))
        y_pair = jnp.where(low_l, y_diag[0], y_diag[1])
        prev = state_ref[pair]
        y_off = _bdot_nt(cm_g, prev.astype(jnp.bfloat16))
        y_pair = y_pair + y_off * jnp.exp(jnp.where(low_l, acs_col[0], acs_col[1]))
        last = (acs_col[0][L - 1:L, :], acs_col[1][L - 1:L, :])
        dst = jnp.exp(jnp.where(low_l, last[0] - acs_col[0], last[1] - acs_col[1]))
        st_new = _bdot((xdt * dst).T.astype(jnp.bfloat16), bm_g)
        cdec = jnp.exp(jnp.where(low_r, last[0], last[1]))
        state_ref[pair] = prev * cdec + st_new
        ys.append(y_pair)
    y = jnp.concatenate(ys, axis=1) + dskip_ref[...] * xs
    y_ref[...] = _rms(y * _silu(z_ref[...]), nw_ref[...]).astype(y_ref.dtype)


def _ssd(xbc, z, fdt, conv_w, conv_b, b_forget, dt_bias, a_log, d_skip, norm_w, batch, seq):
    nc = seq // BLOCK
    row = lambda b, c: (b * nc + c, 0)
    const = lambda b, c: (0, 0)
    zpad = jnp.zeros((LANES - FOX_HEADS - SSM_HEADS,), jnp.float32)
    bias_row = jnp.concatenate([b_forget, dt_bias, zpad]).reshape(1, LANES)
    alog_row = jnp.concatenate([jnp.zeros((FOX_HEADS,), jnp.float32), a_log, zpad]).reshape(1, LANES)
    dskip_row = jnp.repeat(d_skip, HEAD_DIM).reshape(1, SSM_W)
    return pl.pallas_call(
        _ssd_kernel,
        grid=(batch, nc),
        in_specs=[pl.BlockSpec((BLOCK, SSM_CONV_W), row),
                  pl.BlockSpec((BLOCK, SSM_W), row),
                  pl.BlockSpec((BLOCK, LANES), row),
                  pl.BlockSpec((SSM_CONV, SSM_CONV_W), const),
                  pl.BlockSpec((1, SSM_CONV_W), const),
                  pl.BlockSpec((1, LANES), const),
                  pl.BlockSpec((1, LANES), const),
                  pl.BlockSpec((1, SSM_W), const),
                  pl.BlockSpec((1, SSM_W), const)],
        out_specs=[pl.BlockSpec((BLOCK, SSM_W), row),
                   pl.BlockSpec((BLOCK, FOX_W), row)],
        out_shape=[jax.ShapeDtypeStruct((batch * seq, SSM_W), jnp.bfloat16),
                   jax.ShapeDtypeStruct((batch * seq, FOX_W), jnp.bfloat16)],
        scratch_shapes=[pltpu.VMEM((SSM_HEADS // 2, 2 * HEAD_DIM, SSM_STATE), jnp.float32),
                        pltpu.VMEM((8 + BLOCK, SSM_CONV_W), jnp.float32),
                        pltpu.VMEM((1, LANES), jnp.float32)],
        compiler_params=pltpu.CompilerParams(dimension_semantics=("arbitrary", "arbitrary")),
        name="ssd",
    )(xbc, z, fdt, conv_w, conv_b.reshape(1, SSM_CONV_W), bias_row, alog_row, dskip_row,
      norm_w.reshape(1, SSM_W))


def _memkv_kernel(mem_ref, nw_ref, wk_ref, wv_ref, k_ref, v_ref):
    mn = _rms(mem_ref[...], nw_ref[...]).astype(jnp.bfloat16)
    k_ref[...] = _bdot(mn, wk_ref[...]).astype(k_ref.dtype)
    v_ref[...] = _bdot(mn, wv_ref[...]).astype(v_ref.dtype)


def _memkv(mem2, norm_w, wk, wv, mem_tokens):
    t = mem2.shape[0]
    row = lambda b: (b, 0)
    const = lambda b: (0, 0)
    return pl.pallas_call(
        _memkv_kernel,
        grid=(t // mem_tokens,),
        in_specs=[pl.BlockSpec((mem_tokens, D_MODEL), row),
                  pl.BlockSpec((1, D_MODEL), const),
                  pl.BlockSpec((D_MODEL, D_MODEL), const),
                  pl.BlockSpec((D_MODEL, D_MODEL), const)],
        out_specs=[pl.BlockSpec((mem_tokens, D_MODEL), row)] * 2,
        out_shape=[jax.ShapeDtypeStruct((t, D_MODEL), jnp.bfloat16)] * 2,
        compiler_params=pltpu.CompilerParams(dimension_semantics=("arbitrary",)),
        name="memkv",
    )(mem2, norm_w.reshape(1, D_MODEL), wk.astype(jnp.bfloat16), wv.astype(jnp.bfloat16))


def _out_kernel(x_ref, ya_ref, yb_ref, yc_ref, wo_ref, nq_ref, wq_ref, k_ref, v_ref, wmo_ref, fn_ref,
                o_ref, *, final_norm):
    x1 = (x_ref[...]
          + _bdot(ya_ref[...], wo_ref[0:SWA_W, :])
          + _bdot(yb_ref[...], wo_ref[SWA_W:SWA_W + FOX_W, :])
          + _bdot(yc_ref[...], wo_ref[SWA_W + FOX_W:, :]))
    hq = _rms(x1, nq_ref[...]).astype(jnp.bfloat16)
    q = (_bdot(hq, wq_ref[...]) * (MEM_HEAD_DIM ** -0.5)).astype(jnp.bfloat16)
    heads = []
    for h in range(MEM_HEADS):
        sl = slice(h * MEM_HEAD_DIM, (h + 1) * MEM_HEAD_DIM)
        s = _bdot_nt(q[:, sl], k_ref[:, sl])
        m = jnp.max(s, axis=-1, keepdims=True)
        p = jnp.exp(s - m)
        probs = (p / jnp.sum(p, axis=-1, keepdims=True)).astype(jnp.bfloat16)
        heads.append(_bdot(probs, v_ref[:, sl]).astype(jnp.bfloat16))
    att = jnp.concatenate(heads, axis=1)
    x2 = x1 + _bdot(att, wmo_ref[...])
    if final_norm:
        x2 = _rms(x2, fn_ref[...])
    o_ref[...] = x2


def _out_block(x2, ya, yb, yc, w_out, norm_xq_w, w_mq, kmem, vmem, w_mo, final_w, seq, mem_tokens, tm,
               final_norm):
    t = x2.shape[0]
    nblk_s = seq // tm
    row = lambda i: (i, 0)
    const = lambda i: (0, 0)
    memrow = lambda i: (i // nblk_s, 0)
    bf = jnp.bfloat16
    return pl.pallas_call(
        functools.partial(_out_kernel, final_norm=final_norm),
        grid=(t // tm,),
        in_specs=[pl.BlockSpec((tm, D_MODEL), row),
                  pl.BlockSpec((tm, SWA_W), row),
                  pl.BlockSpec((tm, FOX_W), row),
                  pl.BlockSpec((tm, SSM_W), row),
                  pl.BlockSpec((SWA_W + FOX_W + SSM_W, D_MODEL), const),
                  pl.BlockSpec((1, D_MODEL), const),
                  pl.BlockSpec((D_MODEL, D_MODEL), const),
                  pl.BlockSpec((mem_tokens, D_MODEL), memrow),
                  pl.BlockSpec((mem_tokens, D_MODEL), memrow),
                  pl.BlockSpec((D_MODEL, D_MODEL), const),
                  pl.BlockSpec((1, D_MODEL), const)],
        out_specs=pl.BlockSpec((tm, D_MODEL), row),
        out_shape=jax.ShapeDtypeStruct((t, D_MODEL), jnp.float32),
        compiler_params=pltpu.CompilerParams(dimension_semantics=("arbitrary",),
                                             vmem_limit_bytes=_VMEM_LIMIT),
        name="outproj_mem",
    )(x2, ya, yb, yc, w_out.astype(bf), norm_xq_w.reshape(1, D_MODEL), w_mq.astype(bf), kmem, vmem,
      w_mo.astype(bf), final_w.reshape(1, D_MODEL))


def _row_tile(seq):
    return min(512, seq)


def kernel(x, mem, norm_mix_w, w_in, b_forget, swa_sinks, conv_w, conv_b, dt_bias, a_log, d_skip, ssm_norm_w, w_out, norm_xq_w, norm_mem_w, w_mq, w_mk, w_mv, w_mo, final_norm_w):
    batch, seq, _ = x.shape
    mem_tokens = mem.shape[1]
    depth = w_in.shape[0]
    assert seq % BLOCK == 0
    tm = _row_tile(seq)
    assert seq % tm == 0 and seq % FOX_TQ == 0 and seq % SWA_TQ == 0
    rope = _rope_tables(seq)
    x2 = x.reshape(batch * seq, D_MODEL)
    mem2 = mem.reshape(batch * mem_tokens, D_MODEL)
    for l in range(depth):
        w_r, w_t = _arrange_w_in(w_in[l])
        ka, g, kb, z, xbc, fdt, qbt, vbt, qat, vat = _inproj(x2, norm_mix_w[l], w_r, w_t, rope, seq, tm)
        yc, caug = _ssd(xbc, z, fdt, conv_w[l], conv_b[l], b_forget[l], dt_bias[l], a_log[l], d_skip[l],
                       ssm_norm_w[l], batch, seq)
        ya = _swa(qat, ka, vat, g, swa_sinks[l], batch, seq)
        yb = _fox(qbt, kb, caug, vbt, g, batch, seq)
        kmem, vmem = _memkv(mem2, norm_mem_w[l], w_mk[l], w_mv[l], mem_tokens)
        x2 = _out_block(x2, ya, yb, yc, w_out[l], norm_xq_w[l], w_mq[l], kmem, vmem, w_mo[l],
                        final_norm_w, seq, mem_tokens, tm, final_norm=(l == depth - 1))
    return x2.reshape(batch, seq, D_MODEL)
```

```python
import functools

import jax
import jax.numpy as jnp
from jax import lax
from jax.experimental import pallas as pl
from jax.experimental.pallas import tpu as pltpu

D_MODEL = 1024
HEAD_DIM = 64
BLOCK = 128
SWA_HEADS = 8
SWA_KV_HEADS = 2
FOX_HEADS = 8
SSM_HEADS = 16
SSM_STATE = 128
SSM_CONV = 4
MEM_HEADS = 4
MEM_HEAD_DIM = 256
ROPE_THETA = 10000.0
EPS = 1e-6
NEG_INF = -1e30

SWA_W = SWA_HEADS * HEAD_DIM
SWA_KV_W = SWA_KV_HEADS * HEAD_DIM
FOX_W = FOX_HEADS * HEAD_DIM
SSM_W = SSM_HEADS * HEAD_DIM
SSM_BC_W = 2 * SSM_STATE
SSM_CONV_W = SSM_W + 2 * SSM_BC_W
LANES = 128
LOG2E = 1.4426950408889634
SWA_TQ = 512
FOX_TQ = 512
FOX_TK = 512
DT_LANE0 = FOX_HEADS

_O_QA = 0
_O_KA = _O_QA + SWA_W
_O_VA = _O_KA + SWA_KV_W
_O_GA = _O_VA + SWA_KV_W
_O_QB = _O_GA + SWA_W
_O_KB = _O_QB + FOX_W
_O_VB = _O_KB + FOX_W
_O_FB = _O_VB + FOX_W
_O_GB = _O_FB + FOX_HEADS
_O_ZC = _O_GB + FOX_W
_O_XBC = _O_ZC + SSM_W
_O_DT = _O_XBC + SSM_CONV_W
_IN_W = _O_DT + SSM_HEADS

_SEG_W = (2 * SWA_KV_W, 2 * SWA_W, FOX_W, SSM_W, SSM_CONV_W, LANES)
_SEG_OFF = tuple(sum(_SEG_W[:i]) for i in range(len(_SEG_W)))
_PROJ_W = sum(_SEG_W)

_VMEM_LIMIT = 56 * 1024 * 1024


def _bdot(a, b):
    return jnp.dot(a, b, preferred_element_type=jnp.float32)


def _bdot_nt(a, b):
    return lax.dot_general(a, b, (((1,), (1,)), ((), ())), preferred_element_type=jnp.float32)


def _silu(x):
    h = 0.5 * x
    return h + h * jnp.tanh(h)


def _softplus(x):
    return jnp.maximum(x, 0.0) + jnp.log(1.0 + jnp.exp(-jnp.abs(x)))


def _rms(x, w):
    return x * lax.rsqrt(jnp.mean(x * x, axis=-1, keepdims=True) + EPS) * w


def _rope(x, cos, sin_signed):
    width = x.shape[1]
    reps = width // LANES
    lane = lax.broadcasted_iota(jnp.int32, x.shape, 1)
    first_half = (lane % HEAD_DIM) < (HEAD_DIM // 2)
    swapped = jnp.where(first_half,
                        pltpu.roll(x, width - HEAD_DIM // 2, 1),
                        pltpu.roll(x, HEAD_DIM // 2, 1))
    cos_t = jnp.concatenate([cos] * reps, axis=1)
    sin_t = jnp.concatenate([sin_signed] * reps, axis=1)
    return x * cos_t + swapped * sin_t


def _rope_t(x, cos, sin_signed):
    rows = x.shape[0]
    reps = rows // LANES
    r = lax.broadcasted_iota(jnp.int32, x.shape, 0)
    first_half = (r % HEAD_DIM) < (HEAD_DIM // 2)
    swapped = jnp.where(first_half,
                        pltpu.roll(x, rows - HEAD_DIM // 2, 0),
                        pltpu.roll(x, HEAD_DIM // 2, 0))
    cos_t = jnp.concatenate([cos] * reps, axis=0)
    sin_t = jnp.concatenate([sin_signed] * reps, axis=0)
    return x * cos_t + swapped * sin_t


def _inproj_kernel(x_ref, nw_ref, w_ref, wt_ref, cos_ref, sin_ref, cost_ref, sint_ref,
                   ka_ref, g_ref, kb_ref, z_ref, xbc_ref, fdt_ref, qbt_ref, vbt_ref, qat_ref, vat_ref):
    h = _rms(x_ref[...], nw_ref[...]).astype(jnp.bfloat16)

    def seg(i):
        return _bdot(h, w_ref[:, _SEG_OFF[i]:_SEG_OFF[i] + _SEG_W[i]])

    scale = HEAD_DIM ** -0.5 * LOG2E
    ka_ref[...] = _rope(seg(0), cos_ref[...], sin_ref[...]).astype(ka_ref.dtype)
    g_ref[...] = seg(1)
    kb_ref[...] = seg(2).astype(kb_ref.dtype)
    z_ref[...] = seg(3)
    xbc_ref[...] = seg(4)
    fdt_ref[...] = seg(5)
    o_vb, o_qa, o_va = FOX_W, 2 * FOX_W, 2 * FOX_W + SWA_W
    qbt_ref[...] = (_bdot_nt(wt_ref[0:o_vb, :], h) * scale).astype(qbt_ref.dtype)
    qat = _rope_t(_bdot_nt(wt_ref[o_qa:o_va, :], h), cost_ref[...], sint_ref[...])
    qat_ref[...] = (qat * scale).astype(qat_ref.dtype)
    vat_ref[...] = _bdot_nt(wt_ref[o_va:, :], h).astype(vat_ref.dtype)
    vbt = _bdot_nt(wt_ref[o_vb:o_qa, :], h).astype(vbt_ref.dtype)
    for hp in range(FOX_W // LANES):
        for c in range(vbt.shape[1] // FOX_TK):
            vbt_ref[hp, c] = vbt[hp * LANES:(hp + 1) * LANES, c * FOX_TK:(c + 1) * FOX_TK]


def _arrange_w_in(w_in):
    def cols(o, n):
        return w_in[:, o:o + n]
    k0, k1 = cols(_O_KA, HEAD_DIM), cols(_O_KA + HEAD_DIM, HEAD_DIM)
    pad = jnp.zeros((w_in.shape[0], LANES - FOX_HEADS - SSM_HEADS), w_in.dtype)
    parts = [k0, k1, k1, k0,
             cols(_O_GA, SWA_W), cols(_O_GB, FOX_W),
             cols(_O_KB, FOX_W),
             cols(_O_ZC, SSM_W), cols(_O_XBC, SSM_CONV_W),
             cols(_O_FB, FOX_HEADS), cols(_O_DT, SSM_HEADS), pad]
    w_r = jnp.concatenate(parts, axis=1).astype(jnp.bfloat16)
    w_t = jnp.concatenate([cols(_O_QB, FOX_W), cols(_O_VB, FOX_W), cols(_O_QA, SWA_W), cols(_O_VA, SWA_KV_W)],
                          axis=1).T.astype(jnp.bfloat16)
    return w_r, w_t


def _inproj(x2, norm_w, w_r, w_t, rope, seq, tm):
    t = x2.shape[0]
    nblk_s = seq // tm
    row = lambda i: (i, 0)
    const = lambda i: (0, 0)
    pos = lambda i: (i % nblk_s, 0)
    pos_t = lambda i: (0, i % nblk_s)
    col = lambda i: (0, i)
    bf, f32 = jnp.bfloat16, jnp.float32
    outs = [(2 * SWA_KV_W, bf), (2 * SWA_W, f32), (FOX_W, bf), (SSM_W, f32), (SSM_CONV_W, f32), (LANES, f32)]
    npair = FOX_W // LANES
    return pl.pallas_call(
        _inproj_kernel,
        grid=(t // tm,),
        in_specs=[pl.BlockSpec((tm, D_MODEL), row),
                  pl.BlockSpec((1, D_MODEL), const),
                  pl.BlockSpec((D_MODEL, _PROJ_W), const, pipeline_mode=pl.Buffered(1)),
                  pl.BlockSpec(w_t.shape, const, pipeline_mode=pl.Buffered(1)),
                  pl.BlockSpec((tm, LANES), pos),
                  pl.BlockSpec((tm, LANES), pos),
                  pl.BlockSpec((LANES, tm), pos_t),
                  pl.BlockSpec((LANES, tm), pos_t)],
        out_specs=[pl.BlockSpec((tm, w), row) for w, _ in outs]
        + [pl.BlockSpec((FOX_W, tm), col),
           pl.BlockSpec((npair, tm // FOX_TK, LANES, FOX_TK), lambda i: (0, i, 0, 0)),
           pl.BlockSpec((SWA_W, tm), col),
           pl.BlockSpec((SWA_KV_W, tm), col)],
        out_shape=[jax.ShapeDtypeStruct((t, w), d) for w, d in outs]
        + [jax.ShapeDtypeStruct((FOX_W, t), bf),
           jax.ShapeDtypeStruct((npair, t // FOX_TK, LANES, FOX_TK), bf),
           jax.ShapeDtypeStruct((SWA_W, t), bf),
           jax.ShapeDtypeStruct((SWA_KV_W, t), bf)],
        compiler_params=pltpu.CompilerParams(dimension_semantics=("arbitrary",),
                                             vmem_limit_bytes=_VMEM_LIMIT),
        name="inproj",
    )(x2, norm_w.reshape(1, D_MODEL), w_r, w_t, *rope)


def _rope_tables(seq):
    pos = jnp.arange(seq, dtype=jnp.float32)
    inv = 1.0 / (ROPE_THETA ** (jnp.arange(0, HEAD_DIM, 2, dtype=jnp.float32) / HEAD_DIM))
    ang = pos[:, None] * inv[None, :]
    cos, sin = jnp.cos(ang), jnp.sin(ang)
    cos_t = jnp.concatenate([cos, cos, cos, cos], axis=1)
    sin_t = jnp.concatenate([-sin, sin, -sin, sin], axis=1)
    return cos_t, sin_t, cos_t.T, sin_t.T


def _swa_kernel(sink_ref, qt_ref, kc_ref, kp_ref, vc_ref, vp_ref, g_ref, o_ref):
    n = pl.program_id(1)
    nsub = SWA_TQ // BLOCK
    key = lax.broadcasted_iota(jnp.int32, (2 * BLOCK, BLOCK), 0)
    qry = lax.broadcasted_iota(jnp.int32, (2 * BLOCK, BLOCK), 1)
    band = jnp.where((key > qry) & (key <= qry + BLOCK), 0.0, NEG_INF)
    band_first = jnp.where(key < BLOCK, NEG_INF, band)
    row = lax.broadcasted_iota(jnp.int32, (LANES, BLOCK), 0)
    ones_rows = jnp.ones((16, 2 * BLOCK), jnp.bfloat16)

    def scores(u):
        if u == 0:
            kcat = jnp.concatenate([kp_ref[...], kc_ref[0:BLOCK, :]], axis=0)
            vcat = jnp.concatenate([vp_ref[...], vc_ref[:, 0:BLOCK]], axis=1)
            bias = jnp.where(n > 0, band, band_first)
        else:
            kcat = kc_ref[(u - 1) * BLOCK:(u + 1) * BLOCK, :]
            vcat = vc_ref[:, (u - 1) * BLOCK:(u + 1) * BLOCK]
            bias = band
        tiles = []
        for c in range(SWA_HEADS // 2):
            qt = qt_ref[c * LANES:(c + 1) * LANES, u * BLOCK:(u + 1) * BLOCK].astype(jnp.float32)
            for half in range(2):
                kv = (2 * c + half) // (SWA_HEADS // SWA_KV_HEADS)
                sel = 0 if kv == half else 1
                in_half = (row < HEAD_DIM) if half == 0 else (row >= HEAD_DIM)
                w = jnp.where(in_half, qt, 0.0).astype(jnp.bfloat16)
                tiles.append(_bdot(kcat[:, sel * LANES:(sel + 1) * LANES], w) + bias)
        return tiles, vcat

    def finish(u, tiles, vcat):
        probs, sink_terms = [], []
        for head, s in enumerate(tiles):
            sink = sink_ref[0, head] * LOG2E
            m = jnp.maximum(jnp.max(s, axis=0, keepdims=True), sink)
            probs.append(jnp.exp2(s - m).astype(jnp.bfloat16))
            sink_terms.append(jnp.exp2(sink - m))
        outs = []
        for head, p in enumerate(probs):
            kv = head // (SWA_HEADS // SWA_KV_HEADS)
            lhs = jnp.concatenate([vcat[kv * HEAD_DIM:(kv + 1) * HEAD_DIM, :], ones_rows], axis=0)
            pv = _bdot(lhs, p)
            outs.append(pv[0:HEAD_DIM, :] / (pv[HEAD_DIM:HEAD_DIM + 1, :] + sink_terms[head]))
        y = jnp.concatenate(outs, axis=0).T
        rows = slice(u * BLOCK, (u + 1) * BLOCK)
        o_ref[rows, :] = (y * _silu(g_ref[rows, :])).astype(o_ref.dtype)

    pending = scores(0)
    for u in range(nsub):
        nxt = scores(u + 1) if u + 1 < nsub else None
        finish(u, *pending)
        pending = nxt


def _swa(qat, ka, vat, g, sinks, batch, seq):
    n = seq // SWA_TQ
    nsub = SWA_TQ // BLOCK
    cur = lambda b, i: (b * n + i, 0)
    cur_t = lambda b, i: (0, b * n + i)
    prev = lambda b, i: ((b * n + i) * nsub - jnp.minimum(i, 1), 0)
    prev_t = lambda b, i: (0, (b * n + i) * nsub - jnp.minimum(i, 1))
    return pl.pallas_call(
        _swa_kernel,
        grid=(batch, n),
        in_specs=[pl.BlockSpec(memory_space=pltpu.SMEM),
                  pl.BlockSpec((SWA_W, SWA_TQ), cur_t),
                  pl.BlockSpec((SWA_TQ, 2 * SWA_KV_W), cur),
                  pl.BlockSpec((BLOCK, 2 * SWA_KV_W), prev),
                  pl.BlockSpec((SWA_KV_W, SWA_TQ), cur_t),
                  pl.BlockSpec((SWA_KV_W, BLOCK), prev_t),
                  pl.BlockSpec((SWA_TQ, SWA_W), cur)],
        out_specs=pl.BlockSpec((SWA_TQ, SWA_W), cur),
        out_shape=jax.ShapeDtypeStruct((batch * seq, SWA_W), jnp.bfloat16),
        compiler_params=pltpu.CompilerParams(dimension_semantics=("arbitrary", "arbitrary")),
        name="swa",
    )(sinks.reshape(1, SWA_HEADS).astype(jnp.float32), qat, ka, ka, vat, vat, g)


def _fox_kernel(qt_ref, k_ref, ca_ref, vt_ref, g_ref, o_ref, w_buf, s_buf0, s_buf1, acc_buf, bias_buf):
    tq, tk = FOX_TQ, FOX_TK
    per_q = tq // tk
    nq = qt_ref.shape[1] // tq
    n_tiles = per_q * nq * (nq + 1) // 2
    all_masked = per_q
    acc_rows = acc_buf.shape[1]
    s_stride = s_buf0.shape[1] - tq

    krow = lax.broadcasted_iota(jnp.int32, (tk, tq), 0)
    qcol = lax.broadcasted_iota(jnp.int32, (tk, tq), 1)
    for d in range(per_q):
        bias_buf[d] = jnp.where(krow + d * tk <= qcol, 0.0, NEG_INF)
    bias_buf[all_masked] = jnp.full((tk, tq), NEG_INF, jnp.float32)
    for buf in (s_buf0, s_buf1, acc_buf):
        buf[...] = jnp.zeros_like(buf)

    row = lax.broadcasted_iota(jnp.int32, (LANES, tq), 0)
    nsplit = 3
    pick = (jnp.where(row < nsplit, 1.0, 0.0).astype(jnp.bfloat16),
            jnp.where((row >= HEAD_DIM) & (row < HEAD_DIM + nsplit), 1.0, 0.0).astype(jnp.bfloat16))
    ones_rows = jnp.ones((acc_rows - HEAD_DIM, tk), jnp.bfloat16)

    def step(carry, s_cur, s_prv, masked):
        (i_a, j_a), (i_b, j_b, v_b), smax, m = carry

        @pl.when(j_a == 0)
        def _():
            q0 = pl.multiple_of(jnp.minimum(i_a, nq - 1) * tq, tq)
            qt = qt_ref[:, pl.ds(q0, tq)].astype(jnp.float32)
            q_lo = jnp.where(row < HEAD_DIM, qt, 0.0).astype(jnp.bfloat16)
            q_hi = jnp.where(row < HEAD_DIM, 0.0, qt).astype(jnp.bfloat16)
            w_buf[:, 0:tq] = jnp.concatenate([q_lo, pick[0]], axis=0)
            w_buf[:, tq:2 * tq] = jnp.concatenate([q_hi, pick[1]], axis=0)

        k0 = pl.multiple_of(jnp.where(i_a < nq, j_a, 0) * tk, tk)
        ka = jnp.concatenate([k_ref[pl.ds(k0, tk), :], ca_ref[pl.ds(k0, tk), :]], axis=1)
        s_new = _bdot(ka, w_buf[...])
        if masked:
            bias = bias_buf[jnp.where(i_a < nq, j_a - per_q * i_a, all_masked)]
            s_new = s_new + jnp.concatenate([bias, bias], axis=1)
        for hh in range(2):
            s_cur[:, hh * s_stride:hh * s_stride + tq] = s_new[:, hh * tq:(hh + 1) * tq]
        smax_new = tuple(jnp.max(s_new[:, hh * tq:(hh + 1) * tq], axis=0, keepdims=True) for hh in range(2))

        first = j_b == 0
        vt = vt_ref[0, j_b]
        m_new = []
        for hh in range(2):
            m_prev = jnp.where(first, NEG_INF, m[hh])
            mn = jnp.maximum(m_prev, smax[hh])
            p = jnp.exp2(s_prv[:, hh * s_stride:hh * s_stride + tq] - mn).astype(jnp.bfloat16)
            lhs = jnp.concatenate([vt[hh * HEAD_DIM:(hh + 1) * HEAD_DIM, :], ones_rows], axis=0)
            acc_buf[hh] = jnp.exp2(m_prev - mn) * acc_buf[hh] + _bdot(lhs, p)
            m_new.append(mn)

        @pl.when(v_b & (j_b == per_q * (i_b + 1) - 1))
        def _():
            q0 = pl.multiple_of(i_b * tq, tq)
            out_t = jnp.concatenate([acc_buf[hh, 0:HEAD_DIM, :] / acc_buf[hh, HEAD_DIM:HEAD_DIM + 1, :]
                                     for hh in range(2)], axis=0)
            y = out_t.T * _silu(g_ref[pl.ds(q0, tq), :])
            o_ref[pl.ds(q0, tq), :] = y.astype(o_ref.dtype)

        last_a = j_a == per_q * (i_a + 1) - 1
        nxt_a = (jnp.where(last_a, i_a + 1, i_a), jnp.where(last_a, 0, j_a + 1))
        return (nxt_a, (i_a, jnp.where(i_a < nq, j_a, 0), i_a < nq), smax_new, tuple(m_new))

    def either(carry, bufs):
        i_a, j_a = carry[0]
        needs_mask = (j_a >= per_q * i_a) | (i_a >= nq)
        return lax.cond(needs_mask, lambda c: step(c, *bufs, True), lambda c: step(c, *bufs, False), carry)

    def body(_, carry):
        carry = either(carry, (s_buf0, s_buf1))
        return either(carry, (s_buf1, s_buf0))

    zi = jnp.int32(0)
    row_vec = lambda v: tuple(jnp.full((1, tq), v, jnp.float32) for _ in range(2))
    init = ((zi, zi), (zi, zi, False), row_vec(0.0), row_vec(0.0))
    lax.fori_loop(0, (n_tiles + 2) // 2, body, init)


def _fox(qbt, kb, caug, vbt, g, batch, seq):
    npair = FOX_HEADS // 2
    nkt = seq // FOX_TK
    gcol0 = SWA_W // LANES
    return pl.pallas_call(
        _fox_kernel,
        grid=(batch, npair),
        in_specs=[pl.BlockSpec((LANES, seq), lambda b, h: (h, b)),
                  pl.BlockSpec((seq, LANES), lambda b, h: (b, h)),
                  pl.BlockSpec((seq, LANES), lambda b, h: (b, h)),
                  pl.BlockSpec((1, nkt, LANES, FOX_TK), lambda b, h: (h, b, 0, 0)),
                  pl.BlockSpec((seq, LANES), lambda b, h: (b, gcol0 + h))],
        out_specs=pl.BlockSpec((seq, LANES), lambda b, h: (b, h)),
        out_shape=jax.ShapeDtypeStruct((batch * seq, FOX_W), jnp.bfloat16),
        scratch_shapes=[pltpu.VMEM((2 * LANES, 2 * FOX_TQ), jnp.bfloat16),
                        pltpu.VMEM((FOX_TK, 2 * FOX_TQ + LANES), jnp.float32),
                        pltpu.VMEM((FOX_TK, 2 * FOX_TQ + LANES), jnp.float32),
                        pltpu.VMEM((2, HEAD_DIM + 16, FOX_TQ), jnp.float32),
                        pltpu.VMEM((FOX_TQ // FOX_TK + 1, FOX_TK, FOX_TQ), jnp.float32)],
        compiler_params=pltpu.CompilerParams(dimension_semantics=("arbitrary", "arbitrary"),
                                             vmem_limit_bytes=_VMEM_LIMIT),
        name="fox",
    )(qbt, kb, caug, vbt, g)


def _split3(x):
    hi = x.astype(jnp.bfloat16)
    r1 = x - hi.astype(jnp.float32)
    mid = r1.astype(jnp.bfloat16)
    lo = (r1 - mid.astype(jnp.float32)).astype(jnp.bfloat16)
    return hi, mid, lo


def _ssd_kernel(xbc_ref, z_ref, fdt_ref, cw_ref, cb_ref, bias_ref, alog_ref, dskip_ref, nw_ref,
                y_ref, ca_ref, state_ref, tail_ref, ccarry_ref):
    ci = pl.program_id(1)
    L = BLOCK

    @pl.when(ci == 0)
    def _():
        state_ref[...] = jnp.zeros_like(state_ref)
        tail_ref[...] = jnp.zeros_like(tail_ref)
        ccarry_ref[...] = jnp.zeros_like(ccarry_ref)

    u = xbc_ref[...]
    tail_ref[8:8 + L, :] = u
    conv = cb_ref[...] + cw_ref[SSM_CONV - 1:SSM_CONV, :] * u
    for k in range(1, SSM_CONV):
        conv = conv + cw_ref[SSM_CONV - 1 - k:SSM_CONV - k, :] * tail_ref[8 - k:8 - k + L, :]
    tail_ref[0:8, :] = u[L - 8:L]
    act = _silu(conv)
    xs = act[:, :SSM_W]
    bm = act[:, SSM_W:SSM_W + SSM_BC_W].astype(jnp.bfloat16)
    cm = act[:, SSM_W + SSM_BC_W:].astype(jnp.bfloat16)

    lane = lax.broadcasted_iota(jnp.int32, (L, LANES), 1)
    vals = fdt_ref[...] + bias_ref[...]
    is_f = lane < DT_LANE0
    is_dt = (lane >= DT_LANE0) & (lane < DT_LANE0 + SSM_HEADS)
    sp = _softplus(jnp.where(is_f, -vals, vals))
    a_row = jnp.where(is_dt[0:1], -jnp.exp(alog_ref[...]), 0.0)
    dt = jnp.where(is_dt, sp, 0.0)
    scan_in = jnp.where(is_f, -sp, dt * a_row)

    ri = lax.broadcasted_iota(jnp.int32, (L, L), 0)
    cj = lax.broadcasted_iota(jnp.int32, (L, L), 1)
    causal = cj <= ri
    tri = jnp.where(causal, 1.0, 0.0).astype(jnp.bfloat16)
    hi, mid, lo = _split3(scan_in)
    cs = _bdot(tri, hi) + _bdot(tri, mid) + _bdot(tri, lo)
    cs_t = cs.T

    c_full = cs + ccarry_ref[...]
    ccarry_ref[...] = jnp.where(is_f[0:1], c_full[L - 1:L, :], 0.0)
    lane64 = lane % HEAD_DIM
    ca_cols = []
    for hp in range(FOX_HEADS // 2):
        negc = jnp.where(lane < HEAD_DIM, c_full[:, 2 * hp:2 * hp + 1], c_full[:, 2 * hp + 1:2 * hp + 2]) * (-LOG2E)
        hi, mid, lo = (term.astype(jnp.float32) for term in _split3(negc))
        terms = jnp.where(lane64 == 0, hi, jnp.where(lane64 == 1, mid, jnp.where(lane64 == 2, lo, 0.0)))
        ca_cols.append(terms.astype(jnp.bfloat16))
    ca_ref[...] = jnp.concatenate(ca_cols, axis=1)

    rowp = lax.broadcasted_iota(jnp.int32, (L, LANES), 0)
    low_l = lane < HEAD_DIM
    low_r = rowp < HEAD_DIM
    hpg = SSM_HEADS // 2
    ys = []
    cb = [None, None]
    for pair in range(SSM_HEADS // 2):
        grp = (2 * pair) // hpg
        bm_g = bm[:, grp * SSM_STATE:(grp + 1) * SSM_STATE]
        cm_g = cm[:, grp * SSM_STATE:(grp + 1) * SSM_STATE]
        if cb[grp] is None:
            cb[grp] = _bdot_nt(cm_g, bm_g)
        x_pair = xs[:, pair * LANES:(pair + 1) * LANES]
        ha, hb = DT_LANE0 + 2 * pair, DT_LANE0 + 2 * pair + 1
        acs_col = (cs[:, ha:ha + 1], cs[:, hb:hb + 1])
        acs_row = (cs_t[ha:ha + 1, :], cs_t[hb:hb + 1, :])
        dt_pair = jnp.where(low_l, dt[:, ha:ha + 1], dt[:, hb:hb + 1])
        xdt = x_pair * dt_pair
        xdt_b = xdt.astype(jnp.bfloat16)
        y_diag = []
        for hh in range(2):
            diff = acs_col[hh] - acs_row[hh]
            decay = jnp.exp(jnp.where(causal, diff, NEG_INF))
            gmat = (cb[grp] * decay).astype(jnp.bfloat16)
            y_diag.append(_bdot(gmat, xdt_b))
        y_pair = jnp.where(low_l, y_diag[0], y_diag[1])
        prev = state_ref[pair]
        y_off = _bdot_nt(cm_g, prev.astype(jnp.bfloat16))
        y_pair = y_pair + y_off * jnp.exp(jnp.where(low_l, acs_col[0], acs_col[1]))
        last = (acs_col[0][L - 1:L, :], acs_col[1][L - 1:L, :])
        dst = jnp.exp(jnp.where(low_l, last[0] - acs_col[0], last[1] - acs_col[1]))
        st_new = _bdot((xdt * dst).T.astype(jnp.bfloat16), bm_g)
        cdec = jnp.exp(jnp.where(low_r, last[0], last[1]))
        state_ref[pair] = prev * cdec + st_new
        ys.append(y_pair)
    y = jnp.concatenate(ys, axis=1) + dskip_ref[...] * xs
    y_ref[...] = _rms(y * _silu(z_ref[...]), nw_ref[...]).astype(y_ref.dtype)


def _ssd(xbc, z, fdt, conv_w, conv_b, b_forget, dt_bias, a_log, d_skip, norm_w, batch, seq):
    nc = seq // BLOCK
    row = lambda b, c: (b * nc + c, 0)
    const = lambda b, c: (0, 0)
    zpad = jnp.zeros((LANES - FOX_HEADS - SSM_HEADS,), jnp.float32)
    bias_row = jnp.concatenate([b_forget, dt_bias, zpad]).reshape(1, LANES)
    alog_row = jnp.concatenate([jnp.zeros((FOX_HEADS,), jnp.float32), a_log, zpad]).reshape(1, LANES)
    dskip_row = jnp.repeat(d_skip, HEAD_DIM).reshape(1, SSM_W)
    return pl.pallas_call(
        _ssd_kernel,
        grid=(batch, nc),
        in_specs=[pl.BlockSpec((BLOCK, SSM_CONV_W), row),
                  pl.BlockSpec((BLOCK, SSM_W), row),
                  pl.BlockSpec((BLOCK, LANES), row),
                  pl.BlockSpec((SSM_CONV, SSM_CONV_W), const),
                  pl.BlockSpec((1, SSM_CONV_W), const),
                  pl.BlockSpec((1, LANES), const),
                  pl.BlockSpec((1, LANES), const),
                  pl.BlockSpec((1, SSM_W), const),
                  pl.BlockSpec((1, SSM_W), const)],
        out_specs=[pl.BlockSpec((BLOCK, SSM_W), row),
                   pl.BlockSpec((BLOCK, FOX_W), row)],
        out_shape=[jax.ShapeDtypeStruct((batch * seq, SSM_W), jnp.bfloat16),
                   jax.ShapeDtypeStruct((batch * seq, FOX_W), jnp.bfloat16)],
        scratch_shapes=[pltpu.VMEM((SSM_HEADS // 2, 2 * HEAD_DIM, SSM_STATE), jnp.float32),
                        pltpu.VMEM((8 + BLOCK, SSM_CONV_W), jnp.float32),
                        pltpu.VMEM((1, LANES), jnp.float32)],
        compiler_params=pltpu.CompilerParams(dimension_semantics=("arbitrary", "arbitrary")),
        name="ssd",
    )(xbc, z, fdt, conv_w, conv_b.reshape(1, SSM_CONV_W), bias_row, alog_row, dskip_row,
      norm_w.reshape(1, SSM_W))


def _memkv_kernel(mem_ref, nw_ref, wk_ref, wv_ref, k_ref, v_ref):
    mn = _rms(mem_ref[...], nw_ref[...]).astype(jnp.bfloat16)
    k_ref[...] = _bdot(mn, wk_ref[...]).astype(k_ref.dtype)
    v_ref[...] = _bdot(mn, wv_ref[...]).astype(v_ref.dtype)


def _memkv(mem2, norm_w, wk, wv, mem_tokens):
    t = mem2.shape[0]
    row = lambda b: (b, 0)
    const = lambda b: (0, 0)
    return pl.pallas_call(
        _memkv_kernel,
        grid=(t // mem_tokens,),
        in_specs=[pl.BlockSpec((mem_tokens, D_MODEL), row),
                  pl.BlockSpec((1, D_MODEL), const),
                  pl.BlockSpec((D_MODEL, D_MODEL), const),
                  pl.BlockSpec((D_MODEL, D_MODEL), const)],
        out_specs=[pl.BlockSpec((mem_tokens, D_MODEL), row)] * 2,
        out_shape=[jax.ShapeDtypeStruct((t, D_MODEL), jnp.bfloat16)] * 2,
        compiler_params=pltpu.CompilerParams(dimension_semantics=("arbitrary",)),
        name="memkv",
    )(mem2, norm_w.reshape(1, D_MODEL), wk.astype(jnp.bfloat16), wv.astype(jnp.bfloat16))


def _out_kernel(x_ref, ya_ref, yb_ref, yc_ref, wo_ref, nq_ref, wq_ref, k_ref, v_ref, wmo_ref, fn_ref,
                o_ref, *, final_norm):
    x1 = (x_ref[...]
          + _bdot(ya_ref[...], wo_ref[0:SWA_W, :])
          + _bdot(yb_ref[...], wo_ref[SWA_W:SWA_W + FOX_W, :])
          + _bdot(yc_ref[...], wo_ref[SWA_W + FOX_W:, :]))
    hq = _rms(x1, nq_ref[...]).astype(jnp.bfloat16)
    q = (_bdot(hq, wq_ref[...]) * (MEM_HEAD_DIM ** -0.5)).astype(jnp.bfloat16)
    heads = []
    for h in range(MEM_HEADS):
        sl = slice(h * MEM_HEAD_DIM, (h + 1) * MEM_HEAD_DIM)
        s = _bdot_nt(q[:, sl], k_ref[:, sl])
        m = jnp.max(s, axis=-1, keepdims=True)
        p = jnp.exp(s - m)
        probs = (p / jnp.sum(p, axis=-1, keepdims=True)).astype(jnp.bfloat16)
        heads.append(_bdot(probs, v_ref[:, sl]).astype(jnp.bfloat16))
    att = jnp.concatenate(heads, axis=1)
    x2 = x1 + _bdot(att, wmo_ref[...])
    if final_norm:
        x2 = _rms(x2, fn_ref[...])
    o_ref[...] = x2


def _out_block(x2, ya, yb, yc, w_out, norm_xq_w, w_mq, kmem, vmem, w_mo, final_w, seq, mem_tokens, tm,
               final_norm):
    t = x2.shape[0]
    nblk_s = seq // tm
    row = lambda i: (i, 0)
    const = lambda i: (0, 0)
    memrow = lambda i: (i // nblk_s, 0)
    bf = jnp.bfloat16
    return pl.pallas_call(
        functools.partial(_out_kernel, final_norm=final_norm),
        grid=(t // tm,),
        in_specs=[pl.BlockSpec((tm, D_MODEL), row),
                  pl.BlockSpec((tm, SWA_W), row),
                  pl.BlockSpec((tm, FOX_W), row),
                  pl.BlockSpec((tm, SSM_W), row),
                  pl.BlockSpec((SWA_W + FOX_W + SSM_W, D_MODEL), const),
                  pl.BlockSpec((1, D_MODEL), const),
                  pl.BlockSpec((D_MODEL, D_MODEL), const),
                  pl.BlockSpec((mem_tokens, D_MODEL), memrow),
                  pl.BlockSpec((mem_tokens, D_MODEL), memrow),
                  pl.BlockSpec((D_MODEL, D_MODEL), const),
                  pl.BlockSpec((1, D_MODEL), const)],
        out_specs=pl.BlockSpec((tm, D_MODEL), row),
        out_shape=jax.ShapeDtypeStruct((t, D_MODEL), jnp.float32),
        compiler_params=pltpu.CompilerParams(dimension_semantics=("arbitrary",),
                                             vmem_limit_bytes=_VMEM_LIMIT),
        name="outproj_mem",
    )(x2, ya, yb, yc, w_out.astype(bf), norm_xq_w.reshape(1, D_MODEL), w_mq.astype(bf), kmem, vmem,
      w_mo.astype(bf), final_w.reshape(1, D_MODEL))


def _row_tile(seq):
    return min(512, seq)


def kernel(x, mem, norm_mix_w, w_in, b_forget, swa_sinks, conv_w, conv_b, dt_bias, a_log, d_skip, ssm_norm_w, w_out, norm_xq_w, norm_mem_w, w_mq, w_mk, w_mv, w_mo, final_norm_w):
    batch, seq, _ = x.shape
    mem_tokens = mem.shape[1]
    depth = w_in.shape[0]
    assert seq % BLOCK == 0
    tm = _row_tile(seq)
    assert seq % tm == 0 and seq % FOX_TQ == 0 and seq % SWA_TQ == 0
    rope = _rope_tables(seq)
    x2 = x.reshape(batch * seq, D_MODEL)
    mem2 = mem.reshape(batch * mem_tokens, D_MODEL)
    for l in range(depth):
        w_r, w_t = _arrange_w_in(w_in[l])
        ka, g, kb, z, xbc, fdt, qbt, vbt, qat, vat = _inproj(x2, norm_mix_w[l], w_r, w_t, rope, seq, tm)
        yc, caug = _ssd(xbc, z, fdt, conv_w[l], conv_b[l], b_forget[l], dt_bias[l], a_log[l], d_skip[l],
                       ssm_norm_w[l], batch, seq)
        ya = _swa(qat, ka, vat, g, swa_sinks[l], batch, seq)
        yb = _fox(qbt, kb, caug, vbt, g, batch, seq)
        kmem, vmem = _memkv(mem2, norm_mem_w[l], w_mk[l], w_mv[l], mem_tokens)
        x2 = _out_block(x2, ya, yb, yc, w_out[l], norm_xq_w[l], w_mq[l], kmem, vmem, w_mo[l],
                        final_norm_w, seq, mem_tokens, tm, final_norm=(l == depth - 1))
    return x2.reshape(batch, seq, D_MODEL)
```

```python
import functools

import jax
import jax.numpy as jnp
from jax import lax
from jax.experimental import pallas as pl
from jax.experimental.pallas import tpu as pltpu

D_MODEL = 1024
HEAD_DIM = 64
BLOCK = 128
SWA_HEADS = 8
SWA_KV_HEADS = 2
FOX_HEADS = 8
SSM_HEADS = 16
SSM_STATE = 128
SSM_CONV = 4
MEM_HEADS = 4
MEM_HEAD_DIM = 256
ROPE_THETA = 10000.0
EPS = 1e-6
NEG_INF = -1e30

SWA_W = SWA_HEADS * HEAD_DIM
SWA_KV_W = SWA_KV_HEADS * HEAD_DIM
FOX_W = FOX_HEADS * HEAD_DIM
SSM_W = SSM_HEADS * HEAD_DIM
SSM_BC_W = 2 * SSM_STATE
SSM_CONV_W = SSM_W + 2 * SSM_BC_W
LANES = 128
LOG2E = 1.4426950408889634
SWA_TQ = 512
FOX_TQ = 1024
FOX_CHUNK = 256
FOX_TK = 512
DT_LANE0 = FOX_HEADS

_O_QA = 0
_O_KA = _O_QA + SWA_W
_O_VA = _O_KA + SWA_KV_W
_O_GA = _O_VA + SWA_KV_W
_O_QB = _O_GA + SWA_W
_O_KB = _O_QB + FOX_W
_O_VB = _O_KB + FOX_W
_O_FB = _O_VB + FOX_W
_O_GB = _O_FB + FOX_HEADS
_O_ZC = _O_GB + FOX_W
_O_XBC = _O_ZC + SSM_W
_O_DT = _O_XBC + SSM_CONV_W
_IN_W = _O_DT + SSM_HEADS

_SEG_W = (2 * SWA_KV_W, 2 * SWA_W, FOX_W, SSM_W, SSM_CONV_W, LANES)
_SEG_OFF = tuple(sum(_SEG_W[:i]) for i in range(len(_SEG_W)))
_PROJ_W = sum(_SEG_W)

_VMEM_LIMIT = 56 * 1024 * 1024


def _bdot(a, b):
    return jnp.dot(a, b, preferred_element_type=jnp.float32)


def _bdot_nt(a, b):
    return lax.dot_general(a, b, (((1,), (1,)), ((), ())), preferred_element_type=jnp.float32)


def _silu(x):
    h = 0.5 * x
    return h + h * jnp.tanh(h)


def _softplus(x):
    return jnp.maximum(x, 0.0) + jnp.log(1.0 + jnp.exp(-jnp.abs(x)))


def _rms(x, w):
    return x * lax.rsqrt(jnp.mean(x * x, axis=-1, keepdims=True) + EPS) * w


def _rope(x, cos, sin_signed):
    width = x.shape[1]
    reps = width // LANES
    lane = lax.broadcasted_iota(jnp.int32, x.shape, 1)
    first_half = (lane % HEAD_DIM) < (HEAD_DIM // 2)
    swapped = jnp.where(first_half,
                        pltpu.roll(x, width - HEAD_DIM // 2, 1),
                        pltpu.roll(x, HEAD_DIM // 2, 1))
    cos_t = jnp.concatenate([cos] * reps, axis=1)
    sin_t = jnp.concatenate([sin_signed] * reps, axis=1)
    return x * cos_t + swapped * sin_t


def _rope_t(x, cos, sin_signed):
    rows = x.shape[0]
    reps = rows // LANES
    r = lax.broadcasted_iota(jnp.int32, x.shape, 0)
    first_half = (r % HEAD_DIM) < (HEAD_DIM // 2)
    swapped = jnp.where(first_half,
                        pltpu.roll(x, rows - HEAD_DIM // 2, 0),
                        pltpu.roll(x, HEAD_DIM // 2, 0))
    cos_t = jnp.concatenate([cos] * reps, axis=0)
    sin_t = jnp.concatenate([sin_signed] * reps, axis=0)
    return x * cos_t + swapped * sin_t


def _inproj_kernel(x_ref, nw_ref, w_ref, wt_ref, cos_ref, sin_ref, cost_ref, sint_ref,
                   ka_ref, g_ref, kb_ref, z_ref, xbc_ref, fdt_ref, qbt_ref, vbt_ref, qat_ref, vat_ref):
    h = _rms(x_ref[...], nw_ref[...]).astype(jnp.bfloat16)

    def seg(i):
        return _bdot(h, w_ref[:, _SEG_OFF[i]:_SEG_OFF[i] + _SEG_W[i]])

    scale = HEAD_DIM ** -0.5 * LOG2E
    ka_ref[...] = _rope(seg(0), cos_ref[...], sin_ref[...]).astype(ka_ref.dtype)
    g_ref[...] = seg(1)
    kb_ref[...] = seg(2).astype(kb_ref.dtype)
    z_ref[...] = seg(3)
    xbc_ref[...] = seg(4)
    fdt_ref[...] = seg(5)
    o_vb, o_qa, o_va = FOX_W, 2 * FOX_W, 2 * FOX_W + SWA_W
    qbt_ref[...] = (_bdot_nt(wt_ref[0:o_vb, :], h) * scale).astype(qbt_ref.dtype)
    qat = _rope_t(_bdot_nt(wt_ref[o_qa:o_va, :], h), cost_ref[...], sint_ref[...])
    qat_ref[...] = (qat * scale).astype(qat_ref.dtype)
    vat_ref[...] = _bdot_nt(wt_ref[o_va:, :], h).astype(vat_ref.dtype)
    vbt = _bdot_nt(wt_ref[o_vb:o_qa, :], h).astype(vbt_ref.dtype)
    for hp in range(FOX_W // LANES):
        for c in range(vbt.shape[1] // FOX_TK):
            vbt_ref[hp, c] = vbt[hp * LANES:(hp + 1) * LANES, c * FOX_TK:(c + 1) * FOX_TK]


def _arrange_w_in(w_in):
    def cols(o, n):
        return w_in[:, o:o + n]
    k0, k1 = cols(_O_KA, HEAD_DIM), cols(_O_KA + HEAD_DIM, HEAD_DIM)
    pad = jnp.zeros((w_in.shape[0], LANES - FOX_HEADS - SSM_HEADS), w_in.dtype)
    parts = [k0, k1, k1, k0,
             cols(_O_GA, SWA_W), cols(_O_GB, FOX_W),
             cols(_O_KB, FOX_W),
             cols(_O_ZC, SSM_W), cols(_O_XBC, SSM_CONV_W),
             cols(_O_FB, FOX_HEADS), cols(_O_DT, SSM_HEADS), pad]
    w_r = jnp.concatenate(parts, axis=1).astype(jnp.bfloat16)
    w_t = jnp.concatenate([cols(_O_QB, FOX_W), cols(_O_VB, FOX_W), cols(_O_QA, SWA_W), cols(_O_VA, SWA_KV_W)],
                          axis=1).T.astype(jnp.bfloat16)
    return w_r, w_t


def _inproj(x2, norm_w, w_r, w_t, rope, seq, tm):
    t = x2.shape[0]
    nblk_s = seq // tm
    row = lambda i: (i, 0)
    const = lambda i: (0, 0)
    pos = lambda i: (i % nblk_s, 0)
    pos_t = lambda i: (0, i % nblk_s)
    col = lambda i: (0, i)
    bf, f32 = jnp.bfloat16, jnp.float32
    outs = [(2 * SWA_KV_W, bf), (2 * SWA_W, f32), (FOX_W, bf), (SSM_W, f32), (SSM_CONV_W, f32), (LANES, f32)]
    npair = FOX_W // LANES
    return pl.pallas_call(
        _inproj_kernel,
        grid=(t // tm,),
        in_specs=[pl.BlockSpec((tm, D_MODEL), row),
                  pl.BlockSpec((1, D_MODEL), const),
                  pl.BlockSpec((D_MODEL, _PROJ_W), const, pipeline_mode=pl.Buffered(1)),
                  pl.BlockSpec(w_t.shape, const, pipeline_mode=pl.Buffered(1)),
                  pl.BlockSpec((tm, LANES), pos),
                  pl.BlockSpec((tm, LANES), pos),
                  pl.BlockSpec((LANES, tm), pos_t),
                  pl.BlockSpec((LANES, tm), pos_t)],
        out_specs=[pl.BlockSpec((tm, w), row) for w, _ in outs]
        + [pl.BlockSpec((FOX_W, tm), col),
           pl.BlockSpec((npair, tm // FOX_TK, LANES, FOX_TK), lambda i: (0, i, 0, 0)),
           pl.BlockSpec((SWA_W, tm), col),
           pl.BlockSpec((SWA_KV_W, tm), col)],
        out_shape=[jax.ShapeDtypeStruct((t, w), d) for w, d in outs]
        + [jax.ShapeDtypeStruct((FOX_W, t), bf),
           jax.ShapeDtypeStruct((npair, t // FOX_TK, LANES, FOX_TK), bf),
           jax.ShapeDtypeStruct((SWA_W, t), bf),
           jax.ShapeDtypeStruct((SWA_KV_W, t), bf)],
        compiler_params=pltpu.CompilerParams(dimension_semantics=("arbitrary",),
                                             vmem_limit_bytes=_VMEM_LIMIT),
        name="inproj",
    )(x2, norm_w.reshape(1, D_MODEL), w_r, w_t, *rope)


def _rope_tables(seq):
    pos = jnp.arange(seq, dtype=jnp.float32)
    inv = 1.0 / (ROPE_THETA ** (jnp.arange(0, HEAD_DIM, 2, dtype=jnp.float32) / HEAD_DIM))
    ang = pos[:, None] * inv[None, :]
    cos, sin = jnp.cos(ang), jnp.sin(ang)
    cos_t = jnp.concatenate([cos, cos, cos, cos], axis=1)
    sin_t = jnp.concatenate([-sin, sin, -sin, sin], axis=1)
    return cos_t, sin_t, cos_t.T, sin_t.T


def _swa_kernel(sink_ref, qt_ref, kc_ref, kp_ref, vc_ref, vp_ref, g_ref, o_ref):
    n = pl.program_id(1)
    nsub = SWA_TQ // BLOCK
    key = lax.broadcasted_iota(jnp.int32, (2 * BLOCK, BLOCK), 0)
    qry = lax.broadcasted_iota(jnp.int32, (2 * BLOCK, BLOCK), 1)
    band = jnp.where((key > qry) & (key <= qry + BLOCK), 0.0, NEG_INF)
    band_first = jnp.where(key < BLOCK, NEG_INF, band)
    row = lax.broadcasted_iota(jnp.int32, (LANES, BLOCK), 0)
    ones_rows = jnp.ones((16, 2 * BLOCK), jnp.bfloat16)

    def scores(u):
        if u == 0:
            kcat = jnp.concatenate([kp_ref[...], kc_ref[0:BLOCK, :]], axis=0)
            vcat = jnp.concatenate([vp_ref[...], vc_ref[:, 0:BLOCK]], axis=1)
            bias = jnp.where(n > 0, band, band_first)
        else:
            kcat = kc_ref[(u - 1) * BLOCK:(u + 1) * BLOCK, :]
            vcat = vc_ref[:, (u - 1) * BLOCK:(u + 1) * BLOCK]
            bias = band
        tiles = []
        for c in range(SWA_HEADS // 2):
            qt = qt_ref[c * LANES:(c + 1) * LANES, u * BLOCK:(u + 1) * BLOCK].astype(jnp.float32)
            for half in range(2):
                kv = (2 * c + half) // (SWA_HEADS // SWA_KV_HEADS)
                sel = 0 if kv == half else 1
                in_half = (row < HEAD_DIM) if half == 0 else (row >= HEAD_DIM)
                w = jnp.where(in_half, qt, 0.0).astype(jnp.bfloat16)
                tiles.append(_bdot(kcat[:, sel * LANES:(sel + 1) * LANES], w) + bias)
        return tiles, vcat

    def finish(u, tiles, vcat):
        probs, sink_terms = [], []
        for head, s in enumerate(tiles):
            sink = sink_ref[0, head] * LOG2E
            m = jnp.maximum(jnp.max(s, axis=0, keepdims=True), sink)
            probs.append(jnp.exp2(s - m).astype(jnp.bfloat16))
            sink_terms.append(jnp.exp2(sink - m))
        outs = []
        for head, p in enumerate(probs):
            kv = head // (SWA_HEADS // SWA_KV_HEADS)
            lhs = jnp.concatenate([vcat[kv * HEAD_DIM:(kv + 1) * HEAD_DIM, :], ones_rows], axis=0)
            pv = _bdot(lhs, p)
            outs.append(pv[0:HEAD_DIM, :] / (pv[HEAD_DIM:HEAD_DIM + 1, :] + sink_terms[head]))
        y = jnp.concatenate(outs, axis=0).T
        rows = slice(u * BLOCK, (u + 1) * BLOCK)
        o_ref[rows, :] = (y * _silu(g_ref[rows, :])).astype(o_ref.dtype)

    pending = scores(0)
    for u in range(nsub):
        nxt = scores(u + 1) if u + 1 < nsub else None
        finish(u, *pending)
        pending = nxt


def _swa(qat, ka, vat, g, sinks, batch, seq):
    n = seq // SWA_TQ
    nsub = SWA_TQ // BLOCK
    cur = lambda b, i: (b * n + i, 0)
    cur_t = lambda b, i: (0, b * n + i)
    prev = lambda b, i: ((b * n + i) * nsub - jnp.minimum(i, 1), 0)
    prev_t = lambda b, i: (0, (b * n + i) * nsub - jnp.minimum(i, 1))
    return pl.pallas_call(
        _swa_kernel,
        grid=(batch, n),
        in_specs=[pl.BlockSpec(memory_space=pltpu.SMEM),
                  pl.BlockSpec((SWA_W, SWA_TQ), cur_t),
                  pl.BlockSpec((SWA_TQ, 2 * SWA_KV_W), cur),
                  pl.BlockSpec((BLOCK, 2 * SWA_KV_W), prev),
                  pl.BlockSpec((SWA_KV_W, SWA_TQ), cur_t),
                  pl.BlockSpec((SWA_KV_W, BLOCK), prev_t),
                  pl.BlockSpec((SWA_TQ, SWA_W), cur)],
        out_specs=pl.BlockSpec((SWA_TQ, SWA_W), cur),
        out_shape=jax.ShapeDtypeStruct((batch * seq, SWA_W), jnp.bfloat16),
        compiler_params=pltpu.CompilerParams(dimension_semantics=("arbitrary", "arbitrary")),
        name="swa",
    )(sinks.reshape(1, SWA_HEADS).astype(jnp.float32), qat, ka, ka, vat, vat, g)


def _fox_kernel(qt_ref, k_ref, ca_ref, vt_ref, g_ref, o_ref, w_buf, s_buf0, s_buf1, acc_buf, bias_buf):
    tq, tk = FOX_TQ, FOX_TK
    per_q = tq // tk
    nq = qt_ref.shape[1] // tq
    n_tiles = per_q * nq * (nq + 1) // 2
    all_masked = per_q
    acc_rows = acc_buf.shape[1]

    krow = lax.broadcasted_iota(jnp.int32, (tk, tq), 0)
    qcol = lax.broadcasted_iota(jnp.int32, (tk, tq), 1)
    for d in range(per_q):
        bias_buf[d] = jnp.where(krow + d * tk <= qcol, 0.0, NEG_INF)
    bias_buf[all_masked] = jnp.full((tk, tq), NEG_INF, jnp.float32)
    for buf in (s_buf0, s_buf1, acc_buf):
        buf[...] = jnp.zeros_like(buf)

    row = lax.broadcasted_iota(jnp.int32, (LANES, tq), 0)
    nsplit = 3
    pick = (jnp.where(row < nsplit, 1.0, 0.0).astype(jnp.bfloat16),
            jnp.where((row >= HEAD_DIM) & (row < HEAD_DIM + nsplit), 1.0, 0.0).astype(jnp.bfloat16))
    ones_rows = jnp.ones((acc_rows - HEAD_DIM, tk), jnp.bfloat16)

    def step(carry, s_cur, s_prv, masked):
        (i_a, j_a), (i_b, j_b, v_b), smax, m = carry

        @pl.when(j_a == 0)
        def _():
            q0 = pl.multiple_of(jnp.minimum(i_a, nq - 1) * tq, tq)
            qt = qt_ref[:, pl.ds(q0, tq)].astype(jnp.float32)
            q_lo = jnp.where(row < HEAD_DIM, qt, 0.0).astype(jnp.bfloat16)
            q_hi = jnp.where(row < HEAD_DIM, 0.0, qt).astype(jnp.bfloat16)
            w_buf[:, 0:tq] = jnp.concatenate([q_lo, pick[0]], axis=0)
            w_buf[:, tq:2 * tq] = jnp.concatenate([q_hi, pick[1]], axis=0)

        k0 = pl.multiple_of(jnp.where(i_a < nq, j_a, 0) * tk, tk)
        ka = jnp.concatenate([k_ref[pl.ds(k0, tk), :], ca_ref[pl.ds(k0, tk), :]], axis=1)
        if masked:
            bias_slot = jnp.where(i_a < nq, j_a - per_q * i_a, all_masked)
        first = j_b == 0
        vt = vt_ref[0, j_b]
        n_chunks = 2 * tq // FOX_CHUNK
        smax_new, m_new = [], []

        def chunk_cols(ch):
            hh, start = divmod(ch * FOX_CHUNK, tq)
            return slice(ch * FOX_CHUNK, (ch + 1) * FOX_CHUNK), hh, slice(start, start + FOX_CHUNK)

        def stage_a(ch):
            cols, _, hcols = chunk_cols(ch)
            s_new = _bdot(ka, w_buf[:, cols])
            if masked:
                s_new = s_new + bias_buf[bias_slot, :, hcols]
            s_cur[:, cols] = s_new
            smax_new.append(jnp.max(s_new, axis=0, keepdims=True))

        def stage_b(ch):
            cols, hh, hcols = chunk_cols(ch)
            m_prev = jnp.where(first, NEG_INF, m[ch])
            mn = jnp.maximum(m_prev, smax[ch])
            p = jnp.exp2(s_prv[:, cols] - mn).astype(jnp.bfloat16)
            lhs = jnp.concatenate([vt[hh * HEAD_DIM:(hh + 1) * HEAD_DIM, :], ones_rows], axis=0)
            acc_buf[hh, :, hcols] = jnp.exp2(m_prev - mn) * acc_buf[hh, :, hcols] + _bdot(lhs, p)
            m_new.append(mn)

        stage_a(0)
        for ch in range(n_chunks):
            if ch + 1 < n_chunks:
                stage_a(ch + 1)
            stage_b(ch)
        smax_new = tuple(smax_new)

        @pl.when(v_b & (j_b == per_q * (i_b + 1) - 1))
        def _():
            q0 = pl.multiple_of(i_b * tq, tq)
            out_t = jnp.concatenate([acc_buf[hh, 0:HEAD_DIM, :] / acc_buf[hh, HEAD_DIM:HEAD_DIM + 1, :]
                                     for hh in range(2)], axis=0)
            y = out_t.T * _silu(g_ref[pl.ds(q0, tq), :])
            o_ref[pl.ds(q0, tq), :] = y.astype(o_ref.dtype)

        last_a = j_a == per_q * (i_a + 1) - 1
        nxt_a = (jnp.where(last_a, i_a + 1, i_a), jnp.where(last_a, 0, j_a + 1))
        return (nxt_a, (i_a, jnp.where(i_a < nq, j_a, 0), i_a < nq), smax_new, tuple(m_new))

    def either(carry, bufs):
        i_a, j_a = carry[0]
        needs_mask = (j_a >= per_q * i_a) | (i_a >= nq)
        return lax.cond(needs_mask, lambda c: step(c, *bufs, True), lambda c: step(c, *bufs, False), carry)

    def body(_, carry):
        carry = either(carry, (s_buf0, s_buf1))
        return either(carry, (s_buf1, s_buf0))

    zi = jnp.int32(0)
    row_vec = lambda v: tuple(jnp.full((1, FOX_CHUNK), v, jnp.float32) for _ in range(2 * tq // FOX_CHUNK))
    init = ((zi, zi), (zi, zi, False), row_vec(0.0), row_vec(0.0))
    lax.fori_loop(0, (n_tiles + 2) // 2, body, init)


def _fox(qbt, kb, caug, vbt, g, batch, seq):
    npair = FOX_HEADS // 2
    nkt = seq // FOX_TK
    gcol0 = SWA_W // LANES
    return pl.pallas_call(
        _fox_kernel,
        grid=(batch, npair),
        in_specs=[pl.BlockSpec((LANES, seq), lambda b, h: (h, b)),
                  pl.BlockSpec((seq, LANES), lambda b, h: (b, h)),
                  pl.BlockSpec((seq, LANES), lambda b, h: (b, h)),
                  pl.BlockSpec((1, nkt, LANES, FOX_TK), lambda b, h: (h, b, 0, 0)),
                  pl.BlockSpec((seq, LANES), lambda b, h: (b, gcol0 + h))],
        out_specs=pl.BlockSpec((seq, LANES), lambda b, h: (b, h)),
        out_shape=jax.ShapeDtypeStruct((batch * seq, FOX_W), jnp.bfloat16),
        scratch_shapes=[pltpu.VMEM((2 * LANES, 2 * FOX_TQ), jnp.bfloat16),
                        pltpu.VMEM((FOX_TK, 2 * FOX_TQ), jnp.float32),
                        pltpu.VMEM((FOX_TK, 2 * FOX_TQ), jnp.float32),
                        pltpu.VMEM((2, HEAD_DIM + 16, FOX_TQ), jnp.float32),
                        pltpu.VMEM((FOX_TQ // FOX_TK + 1, FOX_TK, FOX_TQ), jnp.float32)],
        compiler_params=pltpu.CompilerParams(dimension_semantics=("arbitrary", "arbitrary"),
                                             vmem_limit_bytes=_VMEM_LIMIT),
        name="fox",
    )(qbt, kb, caug, vbt, g)


def _split3(x):
    hi = x.astype(jnp.bfloat16)
    r1 = x - hi.astype(jnp.float32)
    mid = r1.astype(jnp.bfloat16)
    lo = (r1 - mid.astype(jnp.float32)).astype(jnp.bfloat16)
    return hi, mid, lo


def _ssd_kernel(xbc_ref, z_ref, fdt_ref, cw_ref, cb_ref, bias_ref, alog_ref, dskip_ref, nw_ref,
                y_ref, ca_ref, state_ref, tail_ref, ccarry_ref):
    ci = pl.program_id(1)
    L = BLOCK

    @pl.when(ci == 0)
    def _():
        state_ref[...] = jnp.zeros_like(state_ref)
        tail_ref[...] = jnp.zeros_like(tail_ref)
        ccarry_ref[...] = jnp.zeros_like(ccarry_ref)

    u = xbc_ref[...]
    tail_ref[8:8 + L, :] = u
    conv = cb_ref[...] + cw_ref[SSM_CONV - 1:SSM_CONV, :] * u
    for k in range(1, SSM_CONV):
        conv = conv + cw_ref[SSM_CONV - 1 - k:SSM_CONV - k, :] * tail_ref[8 - k:8 - k + L, :]
    tail_ref[0:8, :] = u[L - 8:L]
    act = _silu(conv)
    xs = act[:, :SSM_W]
    bm = act[:, SSM_W:SSM_W + SSM_BC_W].astype(jnp.bfloat16)
    cm = act[:, SSM_W + SSM_BC_W:].astype(jnp.bfloat16)

    lane = lax.broadcasted_iota(jnp.int32, (L, LANES), 1)
    vals = fdt_ref[...] + bias_ref[...]
    is_f = lane < DT_LANE0
    is_dt = (lane >= DT_LANE0) & (lane < DT_LANE0 + SSM_HEADS)
    sp = _softplus(jnp.where(is_f, -vals, vals))
    a_row = jnp.where(is_dt[0:1], -jnp.exp(alog_ref[...]), 0.0)
    dt = jnp.where(is_dt, sp, 0.0)
    scan_in = jnp.where(is_f, -sp, dt * a_row)

    ri = lax.broadcasted_iota(jnp.int32, (L, L), 0)
    cj = lax.broadcasted_iota(jnp.int32, (L, L), 1)
    causal = cj <= ri
    tri = jnp.where(causal, 1.0, 0.0).astype(jnp.bfloat16)
    hi, mid, lo = _split3(scan_in)
    cs = _bdot(tri, hi) + _bdot(tri, mid) + _bdot(tri, lo)
    cs_t = cs.T

    c_full = cs + ccarry_ref[...]
    ccarry_ref[...] = jnp.where(is_f[0:1], c_full[L - 1:L, :], 0.0)
    lane64 = lane % HEAD_DIM
    ca_cols = []
    for hp in range(FOX_HEADS // 2):
        negc = jnp.where(lane < HEAD_DIM, c_full[:, 2 * hp:2 * hp + 1], c_full[:, 2 * hp + 1:2 * hp + 2]) * (-LOG2E)
        hi, mid, lo = (term.astype(jnp.float32) for term in _split3(negc))
        terms = jnp.where(lane64 == 0, hi, jnp.where(lane64 == 1, mid, jnp.where(lane64 == 2, lo, 0.0)))
        ca_cols.append(terms.astype(jnp.bfloat16))
    ca_ref[...] = jnp.concatenate(ca_cols, axis=1)

    rowp = lax.broadcasted_iota(jnp.int32, (L, LANES), 0)
    low_l = lane < HEAD_DIM
    low_r = rowp < HEAD_DIM
    hpg = SSM_HEADS // 2
    ys = []
    cb = [None, None]
    for pair in range(SSM_HEADS // 2):
        grp = (2 * pair) // hpg
        bm_g = bm[:, grp * SSM_STATE:(grp + 1) * SSM_STATE]
        cm_g = cm[:, grp * SSM_STATE:(grp + 1) * SSM_STATE]
        if cb[grp] is None:
            cb[grp] = _bdot_nt(cm_g, bm_g)
        x_pair = xs[:, pair * LANES:(pair + 1) * LANES]
        ha, hb = DT_LANE0 + 2 * pair, DT_LANE0 + 2 * pair + 1
        acs_col = (cs[:, ha:ha + 1], cs[:, hb:hb + 1])
        acs_row = (cs_t[ha:ha + 1, :], cs_t[hb:hb + 1, :])
        dt_pair = jnp.where(low_l, dt[:, ha:ha + 1], dt[:, hb:hb + 1])
        xdt = x_pair * dt_pair
        xdt_b = xdt.astype(jnp.bfloat16)
        y_diag = []
        for hh in range(2):
            diff = acs_col[hh] - acs_row[hh]
            decay = jnp.exp(jnp.where(causal, diff, NEG_INF))
            gmat = (cb[grp] * decay).astype(jnp.bfloat16)
            y_diag.append(_bdot(gmat, xdt_b))
        y_pair = jnp.where(low_l, y_diag[0], y_diag[1])
        prev = state_ref[pair]
        y_off = _bdot_nt(cm_g, prev.astype(jnp.bfloat16))
        y_pair = y_pair + y_off * jnp.exp(jnp.where(low_l, acs_col[0], acs_col[1]))
        last = (acs_col[0][L - 1:L, :], acs_col[1][L - 1:L, :])
        dst = jnp.exp(jnp.where(low_l, last[0] - acs_col[0], last[1] - acs_col[1]))
        st_new = _bdot((xdt * dst).T.astype(jnp.bfloat16), bm_g)
        cdec = jnp.exp(jnp.where(low_r, last[0], last[1]))
        state_ref[pair] = prev * cdec + st_new
        ys.append(y_pair)
    y = jnp.concatenate(ys, axis=1) + dskip_ref[...] * xs
    y_ref[...] = _rms(y * _silu(z_ref[...]), nw_ref[...]).astype(y_ref.dtype)


def _ssd(xbc, z, fdt, conv_w, conv_b, b_forget, dt_bias, a_log, d_skip, norm_w, batch, seq):
    nc = seq // BLOCK
    row = lambda b, c: (b * nc + c, 0)
    const = lambda b, c: (0, 0)
    zpad = jnp.zeros((LANES - FOX_HEADS - SSM_HEADS,), jnp.float32)
    bias_row = jnp.concatenate([b_forget, dt_bias, zpad]).reshape(1, LANES)
    alog_row = jnp.concatenate([jnp.zeros((FOX_HEADS,), jnp.float32), a_log, zpad]).reshape(1, LANES)
    dskip_row = jnp.repeat(d_skip, HEAD_DIM).reshape(1, SSM_W)
    return pl.pallas_call(
        _ssd_kernel,
        grid=(batch, nc),
        in_specs=[pl.BlockSpec((BLOCK, SSM_CONV_W), row),
                  pl.BlockSpec((BLOCK, SSM_W), row),
                  pl.BlockSpec((BLOCK, LANES), row),
                  pl.BlockSpec((SSM_CONV, SSM_CONV_W), const),
                  pl.BlockSpec((1, SSM_CONV_W), const),
                  pl.BlockSpec((1, LANES), const),
                  pl.BlockSpec((1, LANES), const),
                  pl.BlockSpec((1, SSM_W), const),
                  pl.BlockSpec((1, SSM_W), const)],
        out_specs=[pl.BlockSpec((BLOCK, SSM_W), row),
                   pl.BlockSpec((BLOCK, FOX_W), row)],
        out_shape=[jax.ShapeDtypeStruct((batch * seq, SSM_W), jnp.bfloat16),
                   jax.ShapeDtypeStruct((batch * seq, FOX_W), jnp.bfloat16)],
        scratch_shapes=[pltpu.VMEM((SSM_HEADS // 2, 2 * HEAD_DIM, SSM_STATE), jnp.float32),
                        pltpu.VMEM((8 + BLOCK, SSM_CONV_W), jnp.float32),
                        pltpu.VMEM((1, LANES), jnp.float32)],
        compiler_params=pltpu.CompilerParams(dimension_semantics=("arbitrary", "arbitrary")),
        name="ssd",
    )(xbc, z, fdt, conv_w, conv_b.reshape(1, SSM_CONV_W), bias_row, alog_row, dskip_row,
      norm_w.reshape(1, SSM_W))


def _memkv_kernel(mem_ref, nw_ref, wk_ref, wv_ref, k_ref, v_ref):
    mn = _rms(mem_ref[...], nw_ref[...]).astype(jnp.bfloat16)
    k_ref[...] = _bdot(mn, wk_ref[...]).astype(k_ref.dtype)
    v_ref[...] = _bdot(mn, wv_ref[...]).astype(v_ref.dtype)


def _memkv(mem2, norm_w, wk, wv, mem_tokens):
    t = mem2.shape[0]
    row = lambda b: (b, 0)
    const = lambda b: (0, 0)
    return pl.pallas_call(
        _memkv_kernel,
        grid=(t // mem_tokens,),
        in_specs=[pl.BlockSpec((mem_tokens, D_MODEL), row),
                  pl.BlockSpec((1, D_MODEL), const),
                  pl.BlockSpec((D_MODEL, D_MODEL), const),
                  pl.BlockSpec((D_MODEL, D_MODEL), const)],
        out_specs=[pl.BlockSpec((mem_tokens, D_MODEL), row)] * 2,
        out_shape=[jax.ShapeDtypeStruct((t, D_MODEL), jnp.bfloat16)] * 2,
        compiler_params=pltpu.CompilerParams(dimension_semantics=("arbitrary",)),
        name="memkv",
    )(mem2, norm_w.reshape(1, D_MODEL), wk.astype(jnp.bfloat16), wv.astype(jnp.bfloat16))


def _out_kernel(x_ref, ya_ref, yb_ref, yc_ref, wo_ref, nq_ref, wq_ref, k_ref, v_ref, wmo_ref, fn_ref,
                o_ref, *, final_norm):
    x1 = (x_ref[...]
          + _bdot(ya_ref[...], wo_ref[0:SWA_W, :])
          + _bdot(yb_ref[...], wo_ref[SWA_W:SWA_W + FOX_W, :])
          + _bdot(yc_ref[...], wo_ref[SWA_W + FOX_W:, :]))
    hq = _rms(x1, nq_ref[...]).astype(jnp.bfloat16)
    q = (_bdot(hq, wq_ref[...]) * (MEM_HEAD_DIM ** -0.5)).astype(jnp.bfloat16)
    heads = []
    for h in range(MEM_HEADS):
        sl = slice(h * MEM_HEAD_DIM, (h + 1) * MEM_HEAD_DIM)
        s = _bdot_nt(q[:, sl], k_ref[:, sl])
        m = jnp.max(s, axis=-1, keepdims=True)
        p = jnp.exp(s - m)
        probs = (p / jnp.sum(p, axis=-1, keepdims=True)).astype(jnp.bfloat16)
        heads.append(_bdot(probs, v_ref[:, sl]).astype(jnp.bfloat16))
    att = jnp.concatenate(heads, axis=1)
    x2 = x1 + _bdot(att, wmo_ref[...])
    if final_norm:
        x2 = _rms(x2, fn_ref[...])
    o_ref[...] = x2


def _out_block(x2, ya, yb, yc, w_out, norm_xq_w, w_mq, kmem, vmem, w_mo, final_w, seq, mem_tokens, tm,
               final_norm):
    t = x2.shape[0]
    nblk_s = seq // tm
    row = lambda i: (i, 0)
    const = lambda i: (0, 0)
    memrow = lambda i: (i // nblk_s, 0)
    bf = jnp.bfloat16
    return pl.pallas_call(
        functools.partial(_out_kernel, final_norm=final_norm),
        grid=(t // tm,),
        in_specs=[pl.BlockSpec((tm, D_MODEL), row),
                  pl.BlockSpec((tm, SWA_W), row),
                  pl.BlockSpec((tm, FOX_W), row),
                  pl.BlockSpec((tm, SSM_W), row),
                  pl.BlockSpec((SWA_W + FOX_W + SSM_W, D_MODEL), const),
                  pl.BlockSpec((1, D_MODEL), const),
                  pl.BlockSpec((D_MODEL, D_MODEL), const),
                  pl.BlockSpec((mem_tokens, D_MODEL), memrow),
                  pl.BlockSpec((mem_tokens, D_MODEL), memrow),
                  pl.BlockSpec((D_MODEL, D_MODEL), const),
                  pl.BlockSpec((1, D_MODEL), const)],
        out_specs=pl.BlockSpec((tm, D_MODEL), row),
        out_shape=jax.ShapeDtypeStruct((t, D_MODEL), jnp.float32),
        compiler_params=pltpu.CompilerParams(dimension_semantics=("arbitrary",),
                                             vmem_limit_bytes=_VMEM_LIMIT),
        name="outproj_mem",
    )(x2, ya, yb, yc, w_out.astype(bf), norm_xq_w.reshape(1, D_MODEL), w_mq.astype(bf), kmem, vmem,
      w_mo.astype(bf), final_w.reshape(1, D_MODEL))


def _row_tile(seq):
    return min(512, seq)


def kernel(x, mem, norm_mix_w, w_in, b_forget, swa_sinks, conv_w, conv_b, dt_bias, a_log, d_skip, ssm_norm_w, w_out, norm_xq_w, norm_mem_w, w_mq, w_mk, w_mv, w_mo, final_norm_w):
    batch, seq, _ = x.shape
    mem_tokens = mem.shape[1]
    depth = w_in.shape[0]
    assert seq % BLOCK == 0
    tm = _row_tile(seq)
    assert seq % tm == 0 and seq % FOX_TQ == 0 and seq % SWA_TQ == 0
    rope = _rope_tables(seq)
    x2 = x.reshape(batch * seq, D_MODEL)
    mem2 = mem.reshape(batch * mem_tokens, D_MODEL)
    for l in range(depth):
        w_r, w_t = _arrange_w_in(w_in[l])
        ka, g, kb, z, xbc, fdt, qbt, vbt, qat, vat = _inproj(x2, norm_mix_w[l], w_r, w_t, rope, seq, tm)
        yc, caug = _ssd(xbc, z, fdt, conv_w[l], conv_b[l], b_forget[l], dt_bias[l], a_log[l], d_skip[l],
                       ssm_norm_w[l], batch, seq)
        ya = _swa(qat, ka, vat, g, swa_sinks[l], batch, seq)
        yb = _fox(qbt, kb, caug, vbt, g, batch, seq)
        kmem, vmem = _memkv(mem2, norm_mem_w[l], w_mk[l], w_mv[l], mem_tokens)
        x2 = _out_block(x2, ya, yb, yc, w_out[l], norm_xq_w[l], w_mq[l], kmem, vmem, w_mo[l],
                        final_norm_w, seq, mem_tokens, tm, final_norm=(l == depth - 1))
    return x2.reshape(batch, seq, D_MODEL)
```

```python
import functools

import jax
import jax.numpy as jnp
from jax import lax
from jax.experimental import pallas as pl
from jax.experimental.pallas import tpu as pltpu

D_MODEL = 1024
HEAD_DIM = 64
BLOCK = 128
SWA_HEADS = 8
SWA_KV_HEADS = 2
FOX_HEADS = 8
SSM_HEADS = 16
SSM_STATE = 128
SSM_CONV = 4
MEM_HEADS = 4
MEM_HEAD_DIM = 256
ROPE_THETA = 10000.0
EPS = 1e-6
NEG_INF = -1e30

SWA_W = SWA_HEADS * HEAD_DIM
SWA_KV_W = SWA_KV_HEADS * HEAD_DIM
FOX_W = FOX_HEADS * HEAD_DIM
SSM_W = SSM_HEADS * HEAD_DIM
SSM_BC_W = 2 * SSM_STATE
SSM_CONV_W = SSM_W + 2 * SSM_BC_W
LANES = 128
LOG2E = 1.4426950408889634
SWA_TQ = 512
FOX_TQ = 1024
FOX_CHUNK = 256
FOX_TK = 512
OUT_ROWS = 256
DT_LANE0 = FOX_HEADS

_O_QA = 0
_O_KA = _O_QA + SWA_W
_O_VA = _O_KA + SWA_KV_W
_O_GA = _O_VA + SWA_KV_W
_O_QB = _O_GA + SWA_W
_O_KB = _O_QB + FOX_W
_O_VB = _O_KB + FOX_W
_O_FB = _O_VB + FOX_W
_O_GB = _O_FB + FOX_HEADS
_O_ZC = _O_GB + FOX_W
_O_XBC = _O_ZC + SSM_W
_O_DT = _O_XBC + SSM_CONV_W
_IN_W = _O_DT + SSM_HEADS

_SEG_W = (2 * SWA_KV_W, 2 * SWA_W, FOX_W, SSM_W, SSM_CONV_W, LANES)
_SEG_OFF = tuple(sum(_SEG_W[:i]) for i in range(len(_SEG_W)))
_PROJ_W = sum(_SEG_W)

_VMEM_LIMIT = 56 * 1024 * 1024


def _bdot(a, b):
    return jnp.dot(a, b, preferred_element_type=jnp.float32)


def _bdot_nt(a, b):
    return lax.dot_general(a, b, (((1,), (1,)), ((), ())), preferred_element_type=jnp.float32)


def _silu(x):
    h = 0.5 * x
    return h + h * jnp.tanh(h)


def _softplus(x):
    return jnp.maximum(x, 0.0) + jnp.log(1.0 + jnp.exp(-jnp.abs(x)))


def _rms(x, w):
    return x * lax.rsqrt(jnp.mean(x * x, axis=-1, keepdims=True) + EPS) * w


def _rope(x, cos, sin_signed):
    width = x.shape[1]
    reps = width // LANES
    lane = lax.broadcasted_iota(jnp.int32, x.shape, 1)
    first_half = (lane % HEAD_DIM) < (HEAD_DIM // 2)
    swapped = jnp.where(first_half,
                        pltpu.roll(x, width - HEAD_DIM // 2, 1),
                        pltpu.roll(x, HEAD_DIM // 2, 1))
    cos_t = jnp.concatenate([cos] * reps, axis=1)
    sin_t = jnp.concatenate([sin_signed] * reps, axis=1)
    return x * cos_t + swapped * sin_t


def _rope_t(x, cos, sin_signed):
    rows = x.shape[0]
    reps = rows // LANES
    r = lax.broadcasted_iota(jnp.int32, x.shape, 0)
    first_half = (r % HEAD_DIM) < (HEAD_DIM // 2)
    swapped = jnp.where(first_half,
                        pltpu.roll(x, rows - HEAD_DIM // 2, 0),
                        pltpu.roll(x, HEAD_DIM // 2, 0))
    cos_t = jnp.concatenate([cos] * reps, axis=0)
    sin_t = jnp.concatenate([sin_signed] * reps, axis=0)
    return x * cos_t + swapped * sin_t


def _inproj_kernel(x_ref, nw_ref, w_ref, wt_ref, cos_ref, sin_ref, cost_ref, sint_ref,
                   ka_ref, g_ref, kb_ref, z_ref, xbc_ref, fdt_ref, qbt_ref, vbt_ref, qat_ref, vat_ref):
    h = _rms(x_ref[...], nw_ref[...]).astype(jnp.bfloat16)

    def seg(i):
        return _bdot(h, w_ref[:, _SEG_OFF[i]:_SEG_OFF[i] + _SEG_W[i]])

    scale = HEAD_DIM ** -0.5 * LOG2E
    ka_ref[...] = _rope(seg(0), cos_ref[...], sin_ref[...]).astype(ka_ref.dtype)
    g_ref[...] = seg(1)
    kb_ref[...] = seg(2).astype(kb_ref.dtype)
    z_ref[...] = seg(3)
    xbc_ref[...] = seg(4)
    fdt_ref[...] = seg(5)
    o_vb, o_qa, o_va = FOX_W, 2 * FOX_W, 2 * FOX_W + SWA_W
    qbt_ref[...] = (_bdot_nt(wt_ref[0:o_vb, :], h) * scale).astype(qbt_ref.dtype)
    qat = _rope_t(_bdot_nt(wt_ref[o_qa:o_va, :], h), cost_ref[...], sint_ref[...])
    qat_ref[...] = (qat * scale).astype(qat_ref.dtype)
    vat_ref[...] = _bdot_nt(wt_ref[o_va:, :], h).astype(vat_ref.dtype)
    vbt = _bdot_nt(wt_ref[o_vb:o_qa, :], h).astype(vbt_ref.dtype)
    for hp in range(FOX_W // LANES):
        for c in range(vbt.shape[1] // FOX_TK):
            vbt_ref[hp, c] = vbt[hp * LANES:(hp + 1) * LANES, c * FOX_TK:(c + 1) * FOX_TK]


def _arrange_w_in(w_in):
    def cols(o, n):
        return w_in[:, o:o + n]
    k0, k1 = cols(_O_KA, HEAD_DIM), cols(_O_KA + HEAD_DIM, HEAD_DIM)
    pad = jnp.zeros((w_in.shape[0], LANES - FOX_HEADS - SSM_HEADS), w_in.dtype)
    parts = [k0, k1, k1, k0,
             cols(_O_GA, SWA_W), cols(_O_GB, FOX_W),
             cols(_O_KB, FOX_W),
             cols(_O_ZC, SSM_W), cols(_O_XBC, SSM_CONV_W),
             cols(_O_FB, FOX_HEADS), cols(_O_DT, SSM_HEADS), pad]
    w_r = jnp.concatenate(parts, axis=1).astype(jnp.bfloat16)
    w_t = jnp.concatenate([cols(_O_QB, FOX_W), cols(_O_VB, FOX_W), cols(_O_QA, SWA_W), cols(_O_VA, SWA_KV_W)],
                          axis=1).T.astype(jnp.bfloat16)
    return w_r, w_t


def _inproj(x2, norm_w, w_r, w_t, rope, seq, tm):
    t = x2.shape[0]
    nblk_s = seq // tm
    row = lambda i: (i, 0)
    const = lambda i: (0, 0)
    pos = lambda i: (i % nblk_s, 0)
    pos_t = lambda i: (0, i % nblk_s)
    col = lambda i: (0, i)
    bf, f32 = jnp.bfloat16, jnp.float32
    outs = [(2 * SWA_KV_W, bf), (2 * SWA_W, f32), (FOX_W, bf), (SSM_W, f32), (SSM_CONV_W, f32), (LANES, f32)]
    npair = FOX_W // LANES
    return pl.pallas_call(
        _inproj_kernel,
        grid=(t // tm,),
        in_specs=[pl.BlockSpec((tm, D_MODEL), row),
                  pl.BlockSpec((1, D_MODEL), const),
                  pl.BlockSpec((D_MODEL, _PROJ_W), const, pipeline_mode=pl.Buffered(1)),
                  pl.BlockSpec(w_t.shape, const, pipeline_mode=pl.Buffered(1)),
                  pl.BlockSpec((tm, LANES), pos),
                  pl.BlockSpec((tm, LANES), pos),
                  pl.BlockSpec((LANES, tm), pos_t),
                  pl.BlockSpec((LANES, tm), pos_t)],
        out_specs=[pl.BlockSpec((tm, w), row) for w, _ in outs]
        + [pl.BlockSpec((FOX_W, tm), col),
           pl.BlockSpec((npair, tm // FOX_TK, LANES, FOX_TK), lambda i: (0, i, 0, 0)),
           pl.BlockSpec((SWA_W, tm), col),
           pl.BlockSpec((SWA_KV_W, tm), col)],
        out_shape=[jax.ShapeDtypeStruct((t, w), d) for w, d in outs]
        + [jax.ShapeDtypeStruct((FOX_W, t), bf),
           jax.ShapeDtypeStruct((npair, t // FOX_TK, LANES, FOX_TK), bf),
           jax.ShapeDtypeStruct((SWA_W, t), bf),
           jax.ShapeDtypeStruct((SWA_KV_W, t), bf)],
        compiler_params=pltpu.CompilerParams(dimension_semantics=("arbitrary",),
                                             vmem_limit_bytes=_VMEM_LIMIT),
        name="inproj",
    )(x2, norm_w.reshape(1, D_MODEL), w_r, w_t, *rope)


def _rope_tables(seq):
    pos = jnp.arange(seq, dtype=jnp.float32)
    inv = 1.0 / (ROPE_THETA ** (jnp.arange(0, HEAD_DIM, 2, dtype=jnp.float32) / HEAD_DIM))
    ang = pos[:, None] * inv[None, :]
    cos, sin = jnp.cos(ang), jnp.sin(ang)
    cos_t = jnp.concatenate([cos, cos, cos, cos], axis=1)
    sin_t = jnp.concatenate([-sin, sin, -sin, sin], axis=1)
    return cos_t, sin_t, cos_t.T, sin_t.T


def _swa_kernel(sink_ref, qt_ref, kc_ref, kp_ref, vc_ref, vp_ref, g_ref, o_ref):
    n = pl.program_id(1)
    nsub = SWA_TQ // BLOCK
    key = lax.broadcasted_iota(jnp.int32, (2 * BLOCK, BLOCK), 0)
    qry = lax.broadcasted_iota(jnp.int32, (2 * BLOCK, BLOCK), 1)
    band = jnp.where((key > qry) & (key <= qry + BLOCK), 0.0, NEG_INF)
    band_first = jnp.where(key < BLOCK, NEG_INF, band)
    row = lax.broadcasted_iota(jnp.int32, (LANES, BLOCK), 0)
    ones_rows = jnp.ones((16, 2 * BLOCK), jnp.bfloat16)

    def scores(u):
        if u == 0:
            kcat = jnp.concatenate([kp_ref[...], kc_ref[0:BLOCK, :]], axis=0)
            vcat = jnp.concatenate([vp_ref[...], vc_ref[:, 0:BLOCK]], axis=1)
            bias = jnp.where(n > 0, band, band_first)
        else:
            kcat = kc_ref[(u - 1) * BLOCK:(u + 1) * BLOCK, :]
            vcat = vc_ref[:, (u - 1) * BLOCK:(u + 1) * BLOCK]
            bias = band
        tiles = []
        for c in range(SWA_HEADS // 2):
            qt = qt_ref[c * LANES:(c + 1) * LANES, u * BLOCK:(u + 1) * BLOCK].astype(jnp.float32)
            for half in range(2):
                kv = (2 * c + half) // (SWA_HEADS // SWA_KV_HEADS)
                sel = 0 if kv == half else 1
                in_half = (row < HEAD_DIM) if half == 0 else (row >= HEAD_DIM)
                w = jnp.where(in_half, qt, 0.0).astype(jnp.bfloat16)
                tiles.append(_bdot(kcat[:, sel * LANES:(sel + 1) * LANES], w) + bias)
        return tiles, vcat

    def finish(u, tiles, vcat):
        probs, sink_terms = [], []
        for head, s in enumerate(tiles):
            sink = sink_ref[0, head] * LOG2E
            m = jnp.maximum(jnp.max(s, axis=0, keepdims=True), sink)
            probs.append(jnp.exp2(s - m).astype(jnp.bfloat16))
            sink_terms.append(jnp.exp2(sink - m))
        outs = []
        for head, p in enumerate(probs):
            kv = head // (SWA_HEADS // SWA_KV_HEADS)
            lhs = jnp.concatenate([vcat[kv * HEAD_DIM:(kv + 1) * HEAD_DIM, :], ones_rows], axis=0)
            pv = _bdot(lhs, p)
            outs.append(pv[0:HEAD_DIM, :] / (pv[HEAD_DIM:HEAD_DIM + 1, :] + sink_terms[head]))
        y = jnp.concatenate(outs, axis=0).T
        rows = slice(u * BLOCK, (u + 1) * BLOCK)
        o_ref[rows, :] = (y * _silu(g_ref[rows, :])).astype(o_ref.dtype)

    pending = scores(0)
    for u in range(nsub):
        nxt = scores(u + 1) if u + 1 < nsub else None
        finish(u, *pending)
        pending = nxt


def _swa(qat, ka, vat, g, sinks, batch, seq):
    n = seq // SWA_TQ
    nsub = SWA_TQ // BLOCK
    cur = lambda b, i: (b * n + i, 0)
    cur_t = lambda b, i: (0, b * n + i)
    prev = lambda b, i: ((b * n + i) * nsub - jnp.minimum(i, 1), 0)
    prev_t = lambda b, i: (0, (b * n + i) * nsub - jnp.minimum(i, 1))
    return pl.pallas_call(
        _swa_kernel,
        grid=(batch, n),
        in_specs=[pl.BlockSpec(memory_space=pltpu.SMEM),
                  pl.BlockSpec((SWA_W, SWA_TQ), cur_t),
                  pl.BlockSpec((SWA_TQ, 2 * SWA_KV_W), cur),
                  pl.BlockSpec((BLOCK, 2 * SWA_KV_W), prev),
                  pl.BlockSpec((SWA_KV_W, SWA_TQ), cur_t),
                  pl.BlockSpec((SWA_KV_W, BLOCK), prev_t),
                  pl.BlockSpec((SWA_TQ, SWA_W), cur)],
        out_specs=pl.BlockSpec((SWA_TQ, SWA_W), cur),
        out_shape=jax.ShapeDtypeStruct((batch * seq, SWA_W), jnp.bfloat16),
        compiler_params=pltpu.CompilerParams(dimension_semantics=("arbitrary", "arbitrary")),
        name="swa",
    )(sinks.reshape(1, SWA_HEADS).astype(jnp.float32), qat, ka, ka, vat, vat, g)


def _fox_kernel(qt_ref, k_ref, ca_ref, vt_ref, g_ref, o_ref, w_buf, s_buf0, s_buf1, acc_buf, bias_buf):
    tq, tk = FOX_TQ, FOX_TK
    per_q = tq // tk
    nq = qt_ref.shape[1] // tq
    n_tiles = per_q * nq * (nq + 1) // 2
    all_masked = per_q
    acc_rows = acc_buf.shape[1]

    krow = lax.broadcasted_iota(jnp.int32, (tk, tq), 0)
    qcol = lax.broadcasted_iota(jnp.int32, (tk, tq), 1)
    for d in range(per_q):
        bias_buf[d] = jnp.where(krow + d * tk <= qcol, 0.0, NEG_INF)
    bias_buf[all_masked] = jnp.full((tk, tq), NEG_INF, jnp.float32)
    for buf in (s_buf0, s_buf1, acc_buf):
        buf[...] = jnp.zeros_like(buf)

    row = lax.broadcasted_iota(jnp.int32, (LANES, tq), 0)
    nsplit = 3
    pick = (jnp.where(row < nsplit, 1.0, 0.0).astype(jnp.bfloat16),
            jnp.where((row >= HEAD_DIM) & (row < HEAD_DIM + nsplit), 1.0, 0.0).astype(jnp.bfloat16))
    ones_rows = jnp.ones((acc_rows - HEAD_DIM, tk), jnp.bfloat16)

    def step(carry, s_cur, s_prv, masked):
        (i_a, j_a), (i_b, j_b, v_b), smax, m = carry

        @pl.when(j_a == 0)
        def _():
            q0 = pl.multiple_of(jnp.minimum(i_a, nq - 1) * tq, tq)
            qt = qt_ref[:, pl.ds(q0, tq)].astype(jnp.float32)
            q_lo = jnp.where(row < HEAD_DIM, qt, 0.0).astype(jnp.bfloat16)
            q_hi = jnp.where(row < HEAD_DIM, 0.0, qt).astype(jnp.bfloat16)
            w_buf[:, 0:tq] = jnp.concatenate([q_lo, pick[0]], axis=0)
            w_buf[:, tq:2 * tq] = jnp.concatenate([q_hi, pick[1]], axis=0)

        k0 = pl.multiple_of(jnp.where(i_a < nq, j_a, 0) * tk, tk)
        ka = jnp.concatenate([k_ref[pl.ds(k0, tk), :], ca_ref[pl.ds(k0, tk), :]], axis=1)
        if masked:
            bias_slot = jnp.where(i_a < nq, j_a - per_q * i_a, all_masked)
        first = j_b == 0
        vt = vt_ref[0, j_b]
        n_chunks = 2 * tq // FOX_CHUNK
        smax_new, m_new = [], []

        def chunk_cols(ch):
            hh, start = divmod(ch * FOX_CHUNK, tq)
            return slice(ch * FOX_CHUNK, (ch + 1) * FOX_CHUNK), hh, slice(start, start + FOX_CHUNK)

        def stage_a(ch):
            cols, _, hcols = chunk_cols(ch)
            s_new = _bdot(ka, w_buf[:, cols])
            if masked:
                s_new = s_new + bias_buf[bias_slot, :, hcols]
            s_cur[:, cols] = s_new
            smax_new.append(jnp.max(s_new, axis=0, keepdims=True))

        def stage_b(ch):
            cols, hh, hcols = chunk_cols(ch)
            m_prev = jnp.where(first, NEG_INF, m[ch])
            mn = jnp.maximum(m_prev, smax[ch])
            p = jnp.exp2(s_prv[:, cols] - mn).astype(jnp.bfloat16)
            lhs = jnp.concatenate([vt[hh * HEAD_DIM:(hh + 1) * HEAD_DIM, :], ones_rows], axis=0)
            acc_buf[hh, :, hcols] = jnp.exp2(m_prev - mn) * acc_buf[hh, :, hcols] + _bdot(lhs, p)
            m_new.append(mn)

        stage_a(0)
        for ch in range(n_chunks):
            if ch + 1 < n_chunks:
                stage_a(ch + 1)
            stage_b(ch)
        smax_new = tuple(smax_new)

        @pl.when(v_b & (j_b == per_q * (i_b + 1) - 1))
        def _():
            q0 = pl.multiple_of(i_b * tq, tq)
            out_t = jnp.concatenate([acc_buf[hh, 0:HEAD_DIM, :] / acc_buf[hh, HEAD_DIM:HEAD_DIM + 1, :]
                                     for hh in range(2)], axis=0)
            y = out_t.T * _silu(g_ref[pl.ds(q0, tq), :])
            o_ref[pl.ds(q0, tq), :] = y.astype(o_ref.dtype)

        last_a = j_a == per_q * (i_a + 1) - 1
        nxt_a = (jnp.where(last_a, i_a + 1, i_a), jnp.where(last_a, 0, j_a + 1))
        return (nxt_a, (i_a, jnp.where(i_a < nq, j_a, 0), i_a < nq), smax_new, tuple(m_new))

    def either(carry, bufs):
        i_a, j_a = carry[0]
        needs_mask = (j_a >= per_q * i_a) | (i_a >= nq)
        return lax.cond(needs_mask, lambda c: step(c, *bufs, True), lambda c: step(c, *bufs, False), carry)

    def body(_, carry):
        carry = either(carry, (s_buf0, s_buf1))
        return either(carry, (s_buf1, s_buf0))

    zi = jnp.int32(0)
    row_vec = lambda v: tuple(jnp.full((1, FOX_CHUNK), v, jnp.float32) for _ in range(2 * tq // FOX_CHUNK))
    init = ((zi, zi), (zi, zi, False), row_vec(0.0), row_vec(0.0))
    lax.fori_loop(0, (n_tiles + 2) // 2, body, init)


def _fox(qbt, kb, caug, vbt, g, batch, seq):
    npair = FOX_HEADS // 2
    nkt = seq // FOX_TK
    gcol0 = SWA_W // LANES
    return pl.pallas_call(
        _fox_kernel,
        grid=(batch, npair),
        in_specs=[pl.BlockSpec((LANES, seq), lambda b, h: (h, b)),
                  pl.BlockSpec((seq, LANES), lambda b, h: (b, h)),
                  pl.BlockSpec((seq, LANES), lambda b, h: (b, h)),
                  pl.BlockSpec((1, nkt, LANES, FOX_TK), lambda b, h: (h, b, 0, 0)),
                  pl.BlockSpec((seq, LANES), lambda b, h: (b, gcol0 + h))],
        out_specs=pl.BlockSpec((seq, LANES), lambda b, h: (b, h)),
        out_shape=jax.ShapeDtypeStruct((batch * seq, FOX_W), jnp.bfloat16),
        scratch_shapes=[pltpu.VMEM((2 * LANES, 2 * FOX_TQ), jnp.bfloat16),
                        pltpu.VMEM((FOX_TK, 2 * FOX_TQ), jnp.float32),
                        pltpu.VMEM((FOX_TK, 2 * FOX_TQ), jnp.float32),
                        pltpu.VMEM((2, HEAD_DIM + 16, FOX_TQ), jnp.float32),
                        pltpu.VMEM((FOX_TQ // FOX_TK + 1, FOX_TK, FOX_TQ), jnp.float32)],
        compiler_params=pltpu.CompilerParams(dimension_semantics=("arbitrary", "arbitrary"),
                                             vmem_limit_bytes=_VMEM_LIMIT),
        name="fox",
    )(qbt, kb, caug, vbt, g)


def _split3(x):
    hi = x.astype(jnp.bfloat16)
    r1 = x - hi.astype(jnp.float32)
    mid = r1.astype(jnp.bfloat16)
    lo = (r1 - mid.astype(jnp.float32)).astype(jnp.bfloat16)
    return hi, mid, lo


def _ssd_kernel(xbc_ref, z_ref, fdt_ref, cw_ref, cb_ref, bias_ref, alog_ref, dskip_ref, nw_ref,
                y_ref, ca_ref, state_ref, tail_ref, ccarry_ref):
    ci = pl.program_id(1)
    L = BLOCK

    @pl.when(ci == 0)
    def _():
        state_ref[...] = jnp.zeros_like(state_ref)
        tail_ref[...] = jnp.zeros_like(tail_ref)
        ccarry_ref[...] = jnp.zeros_like(ccarry_ref)

    u = xbc_ref[...]
    tail_ref[8:8 + L, :] = u
    conv = cb_ref[...] + cw_ref[SSM_CONV - 1:SSM_CONV, :] * u
    for k in range(1, SSM_CONV):
        conv = conv + cw_ref[SSM_CONV - 1 - k:SSM_CONV - k, :] * tail_ref[8 - k:8 - k + L, :]
    tail_ref[0:8, :] = u[L - 8:L]
    act = _silu(conv)
    xs = act[:, :SSM_W]
    bm = act[:, SSM_W:SSM_W + SSM_BC_W].astype(jnp.bfloat16)
    cm = act[:, SSM_W + SSM_BC_W:].astype(jnp.bfloat16)

    lane = lax.broadcasted_iota(jnp.int32, (L, LANES), 1)
    vals = fdt_ref[...] + bias_ref[...]
    is_f = lane < DT_LANE0
    is_dt = (lane >= DT_LANE0) & (lane < DT_LANE0 + SSM_HEADS)
    sp = _softplus(jnp.where(is_f, -vals, vals))
    a_row = jnp.where(is_dt[0:1], -jnp.exp(alog_ref[...]), 0.0)
    dt = jnp.where(is_dt, sp, 0.0)
    scan_in = jnp.where(is_f, -sp, dt * a_row) * LOG2E

    ri = lax.broadcasted_iota(jnp.int32, (L, L), 0)
    cj = lax.broadcasted_iota(jnp.int32, (L, L), 1)
    causal = cj <= ri
    tri = jnp.where(causal, 1.0, 0.0).astype(jnp.bfloat16)
    hi, mid, lo = _split3(scan_in)
    cs = _bdot(tri, hi) + _bdot(tri, mid) + _bdot(tri, lo)
    cs_t = cs.T

    c_full = cs + ccarry_ref[...]
    ccarry_ref[...] = jnp.where(is_f[0:1], c_full[L - 1:L, :], 0.0)
    lane64 = lane % HEAD_DIM
    ca_cols = []
    for hp in range(FOX_HEADS // 2):
        negc = -jnp.where(lane < HEAD_DIM, c_full[:, 2 * hp:2 * hp + 1], c_full[:, 2 * hp + 1:2 * hp + 2])
        hi, mid, lo = (term.astype(jnp.float32) for term in _split3(negc))
        terms = jnp.where(lane64 == 0, hi, jnp.where(lane64 == 1, mid, jnp.where(lane64 == 2, lo, 0.0)))
        ca_cols.append(terms.astype(jnp.bfloat16))
    ca_ref[...] = jnp.concatenate(ca_cols, axis=1)

    rowp = lax.broadcasted_iota(jnp.int32, (L, LANES), 0)
    low_l = lane < HEAD_DIM
    low_r = rowp < HEAD_DIM
    hpg = SSM_HEADS // 2
    npairs = SSM_HEADS // 2
    grp_of = [(2 * pair) // hpg for pair in range(npairs)]
    bm_g = [bm[:, g * SSM_STATE:(g + 1) * SSM_STATE] for g in range(2)]
    cm_g = [cm[:, g * SSM_STATE:(g + 1) * SSM_STATE] for g in range(2)]
    cb = [_bdot_nt(cm_g[g], bm_g[g]) for g in range(2)]
    ys = []
    for pair in range(npairs):
        g = grp_of[pair]
        ha, hb = DT_LANE0 + 2 * pair, DT_LANE0 + 2 * pair + 1
        col = (cs[:, ha:ha + 1], cs[:, hb:hb + 1])
        acs_row = (cs_t[ha:ha + 1, :], cs_t[hb:hb + 1, :])
        xdt = xs[:, pair * LANES:(pair + 1) * LANES] * jnp.where(low_l, dt[:, ha:ha + 1], dt[:, hb:hb + 1])
        xdt_b = xdt.astype(jnp.bfloat16)
        prev = state_ref[pair]
        y_off = _bdot_nt(cm_g[g], prev.astype(jnp.bfloat16))
        last = (col[0][L - 1:L, :], col[1][L - 1:L, :])
        dst = jnp.exp2(jnp.where(low_l, last[0] - col[0], last[1] - col[1]))
        st_new = _bdot((xdt * dst).T.astype(jnp.bfloat16), bm_g[g])
        state_ref[pair] = prev * jnp.exp2(jnp.where(low_r, last[0], last[1])) + st_new
        y_diag = []
        for hh in range(2):
            decay = jnp.exp2(jnp.where(causal, col[hh] - acs_row[hh], NEG_INF))
            gmat = (cb[g] * decay).astype(jnp.bfloat16)
            y_diag.append(_bdot(gmat, xdt_b))
        ys.append(jnp.where(low_l, y_diag[0], y_diag[1]) + y_off * jnp.exp2(jnp.where(low_l, col[0], col[1])))
    y = jnp.concatenate(ys, axis=1) + dskip_ref[...] * xs
    y_ref[...] = _rms(y * _silu(z_ref[...]), nw_ref[...]).astype(y_ref.dtype)


def _ssd(xbc, z, fdt, conv_w, conv_b, b_forget, dt_bias, a_log, d_skip, norm_w, batch, seq):
    nc = seq // BLOCK
    row = lambda b, c: (b * nc + c, 0)
    const = lambda b, c: (0, 0)
    zpad = jnp.zeros((LANES - FOX_HEADS - SSM_HEADS,), jnp.float32)
    bias_row = jnp.concatenate([b_forget, dt_bias, zpad]).reshape(1, LANES)
    alog_row = jnp.concatenate([jnp.zeros((FOX_HEADS,), jnp.float32), a_log, zpad]).reshape(1, LANES)
    dskip_row = jnp.repeat(d_skip, HEAD_DIM).reshape(1, SSM_W)
    return pl.pallas_call(
        _ssd_kernel,
        grid=(batch, nc),
        in_specs=[pl.BlockSpec((BLOCK, SSM_CONV_W), row),
                  pl.BlockSpec((BLOCK, SSM_W), row),
                  pl.BlockSpec((BLOCK, LANES), row),
                  pl.BlockSpec((SSM_CONV, SSM_CONV_W), const),
                  pl.BlockSpec((1, SSM_CONV_W), const),
                  pl.BlockSpec((1, LANES), const),
                  pl.BlockSpec((1, LANES), const),
                  pl.BlockSpec((1, SSM_W), const),
                  pl.BlockSpec((1, SSM_W), const)],
        out_specs=[pl.BlockSpec((BLOCK, SSM_W), row),
                   pl.BlockSpec((BLOCK, FOX_W), row)],
        out_shape=[jax.ShapeDtypeStruct((batch * seq, SSM_W), jnp.bfloat16),
                   jax.ShapeDtypeStruct((batch * seq, FOX_W), jnp.bfloat16)],
        scratch_shapes=[pltpu.VMEM((SSM_HEADS // 2, 2 * HEAD_DIM, SSM_STATE), jnp.float32),
                        pltpu.VMEM((8 + BLOCK, SSM_CONV_W), jnp.float32),
                        pltpu.VMEM((1, LANES), jnp.float32)],
        compiler_params=pltpu.CompilerParams(dimension_semantics=("arbitrary", "arbitrary")),
        name="ssd",
    )(xbc, z, fdt, conv_w, conv_b.reshape(1, SSM_CONV_W), bias_row, alog_row, dskip_row,
      norm_w.reshape(1, SSM_W))


def _memkv_kernel(mem_ref, nw_ref, wk_ref, wv_ref, k_ref, v_ref):
    mn = _rms(mem_ref[...], nw_ref[...]).astype(jnp.bfloat16)
    k_ref[...] = _bdot(mn, wk_ref[...]).astype(k_ref.dtype)
    v_ref[...] = _bdot(mn, wv_ref[...]).astype(v_ref.dtype)


def _memkv(mem2, norm_w, wk, wv, mem_tokens):
    t = mem2.shape[0]
    row = lambda b: (b, 0)
    const = lambda b: (0, 0)
    return pl.pallas_call(
        _memkv_kernel,
        grid=(t // mem_tokens,),
        in_specs=[pl.BlockSpec((mem_tokens, D_MODEL), row),
                  pl.BlockSpec((1, D_MODEL), const),
                  pl.BlockSpec((D_MODEL, D_MODEL), const),
                  pl.BlockSpec((D_MODEL, D_MODEL), const)],
        out_specs=[pl.BlockSpec((mem_tokens, D_MODEL), row)] * 2,
        out_shape=[jax.ShapeDtypeStruct((t, D_MODEL), jnp.bfloat16)] * 2,
        compiler_params=pltpu.CompilerParams(dimension_semantics=("arbitrary",)),
        name="memkv",
    )(mem2, norm_w.reshape(1, D_MODEL), wk.astype(jnp.bfloat16), wv.astype(jnp.bfloat16))


def _out_kernel(x_ref, ya_ref, yb_ref, yc_ref, wo_ref, nq_ref, wq_ref, k_ref, v_ref, wmo_ref, fn_ref,
                o_ref, *, final_norm):
    tm = x_ref.shape[0]
    groups = [slice(r, r + OUT_ROWS) for r in range(0, tm, OUT_ROWS)]
    head_cols = [slice(h * MEM_HEAD_DIM, (h + 1) * MEM_HEAD_DIM) for h in range(MEM_HEADS)]
    x1 = [x_ref[g, :]
          + _bdot(ya_ref[g, :], wo_ref[0:SWA_W, :])
          + _bdot(yb_ref[g, :], wo_ref[SWA_W:SWA_W + FOX_W, :])
          + _bdot(yc_ref[g, :], wo_ref[SWA_W + FOX_W:, :]) for g in groups]
    q = []
    for x1g in x1:
        hq = _rms(x1g, nq_ref[...]).astype(jnp.bfloat16)
        q.append((_bdot(hq, wq_ref[...]) * (MEM_HEAD_DIM ** -0.5 * LOG2E)).astype(jnp.bfloat16))
    scores = [[_bdot_nt(qg[:, sl], k_ref[:, sl]) for sl in head_cols] for qg in q]
    att = []
    for sg in scores:
        heads = []
        for s, sl in zip(sg, head_cols):
            p = jnp.exp2(s - jnp.max(s, axis=-1, keepdims=True))
            probs = (p / jnp.sum(p, axis=-1, keepdims=True)).astype(jnp.bfloat16)
            heads.append(_bdot(probs, v_ref[:, sl]).astype(jnp.bfloat16))
        att.append(jnp.concatenate(heads, axis=1))
    for g, x1g, attg in zip(groups, x1, att):
        x2 = x1g + _bdot(attg, wmo_ref[...])
        if final_norm:
            x2 = _rms(x2, fn_ref[...])
        o_ref[g, :] = x2


def _out_block(x2, ya, yb, yc, w_out, norm_xq_w, w_mq, kmem, vmem, w_mo, final_w, seq, mem_tokens, tm,
               final_norm):
    t = x2.shape[0]
    nblk_s = seq // tm
    row = lambda i: (i, 0)
    const = lambda i: (0, 0)
    memrow = lambda i: (i // nblk_s, 0)
    bf = jnp.bfloat16
    return pl.pallas_call(
        functools.partial(_out_kernel, final_norm=final_norm),
        grid=(t // tm,),
        in_specs=[pl.BlockSpec((tm, D_MODEL), row),
                  pl.BlockSpec((tm, SWA_W), row),
                  pl.BlockSpec((tm, FOX_W), row),
                  pl.BlockSpec((tm, SSM_W), row),
                  pl.BlockSpec((SWA_W + FOX_W + SSM_W, D_MODEL), const, pipeline_mode=pl.Buffered(1)),
                  pl.BlockSpec((1, D_MODEL), const),
                  pl.BlockSpec((D_MODEL, D_MODEL), const, pipeline_mode=pl.Buffered(1)),
                  pl.BlockSpec((mem_tokens, D_MODEL), memrow),
                  pl.BlockSpec((mem_tokens, D_MODEL), memrow),
                  pl.BlockSpec((D_MODEL, D_MODEL), const, pipeline_mode=pl.Buffered(1)),
                  pl.BlockSpec((1, D_MODEL), const)],
        out_specs=pl.BlockSpec((tm, D_MODEL), row),
        out_shape=jax.ShapeDtypeStruct((t, D_MODEL), jnp.float32),
        compiler_params=pltpu.CompilerParams(dimension_semantics=("arbitrary",),
                                             vmem_limit_bytes=_VMEM_LIMIT),
        name="outproj_mem",
    )(x2, ya, yb, yc, w_out.astype(bf), norm_xq_w.reshape(1, D_MODEL), w_mq.astype(bf), kmem, vmem,
      w_mo.astype(bf), final_w.reshape(1, D_MODEL))


def _row_tile(seq):
    return min(512, seq)


def _out_row_tile(seq):
    return min(1024, seq)


def kernel(x, mem, norm_mix_w, w_in, b_forget, swa_sinks, conv_w, conv_b, dt_bias, a_log, d_skip, ssm_norm_w, w_out, norm_xq_w, norm_mem_w, w_mq, w_mk, w_mv, w_mo, final_norm_w):
    batch, seq, _ = x.shape
    mem_tokens = mem.shape[1]
    depth = w_in.shape[0]
    assert seq % BLOCK == 0
    tm = _row_tile(seq)
    assert seq % tm == 0 and seq % FOX_TQ == 0 and seq % SWA_TQ == 0
    rope = _rope_tables(seq)
    x2 = x.reshape(batch * seq, D_MODEL)
    mem2 = mem.reshape(batch * mem_tokens, D_MODEL)
    for l in range(depth):
        w_r, w_t = _arrange_w_in(w_in[l])
        ka, g, kb, z, xbc, fdt, qbt, vbt, qat, vat = _inproj(x2, norm_mix_w[l], w_r, w_t, rope, seq, tm)
        yc, caug = _ssd(xbc, z, fdt, conv_w[l], conv_b[l], b_forget[l], dt_bias[l], a_log[l], d_skip[l],
                       ssm_norm_w[l], batch, seq)
        ya = _swa(qat, ka, vat, g, swa_sinks[l], batch, seq)
        yb = _fox(qbt, kb, caug, vbt, g, batch, seq)
        kmem, vmem = _memkv(mem2, norm_mem_w[l], w_mk[l], w_mv[l], mem_tokens)
        x2 = _out_block(x2, ya, yb, yc, w_out[l], norm_xq_w[l], w_mq[l], kmem, vmem, w_mo[l],
                        final_norm_w, seq, mem_tokens, _out_row_tile(seq), final_norm=(l == depth - 1))
    return x2.reshape(batch, seq, D_MODEL)
```

```python
import functools

import jax
import jax.numpy as jnp
from jax import lax
from jax.experimental import pallas as pl
from jax.experimental.pallas import tpu as pltpu

D_MODEL = 1024
HEAD_DIM = 64
BLOCK = 128
SWA_HEADS = 8
SWA_KV_HEADS = 2
FOX_HEADS = 8
SSM_HEADS = 16
SSM_STATE = 128
SSM_CONV = 4
MEM_HEADS = 4
MEM_HEAD_DIM = 256
ROPE_THETA = 10000.0
EPS = 1e-6
NEG_INF = -1e30

SWA_W = SWA_HEADS * HEAD_DIM
SWA_KV_W = SWA_KV_HEADS * HEAD_DIM
FOX_W = FOX_HEADS * HEAD_DIM
SSM_W = SSM_HEADS * HEAD_DIM
SSM_BC_W = 2 * SSM_STATE
SSM_CONV_W = SSM_W + 2 * SSM_BC_W
LANES = 128
LOG2E = 1.4426950408889634
SWA_TQ = 512
FOX_TQ = 1024
FOX_CHUNK = 256
FOX_SKIP_BELOW = -170.0
FOX_NORM_MARGIN = 1.02
FOX_TK = 512
OUT_ROWS = 256
DT_LANE0 = FOX_HEADS

_O_QA = 0
_O_KA = _O_QA + SWA_W
_O_VA = _O_KA + SWA_KV_W
_O_GA = _O_VA + SWA_KV_W
_O_QB = _O_GA + SWA_W
_O_KB = _O_QB + FOX_W
_O_VB = _O_KB + FOX_W
_O_FB = _O_VB + FOX_W
_O_GB = _O_FB + FOX_HEADS
_O_ZC = _O_GB + FOX_W
_O_XBC = _O_ZC + SSM_W
_O_DT = _O_XBC + SSM_CONV_W
_IN_W = _O_DT + SSM_HEADS

_SEG_W = (2 * SWA_KV_W, 2 * SWA_W, FOX_W, SSM_W, SSM_CONV_W, LANES)
_SEG_OFF = tuple(sum(_SEG_W[:i]) for i in range(len(_SEG_W)))
_PROJ_W = sum(_SEG_W)

_VMEM_LIMIT = 56 * 1024 * 1024


def _bdot(a, b):
    return jnp.dot(a, b, preferred_element_type=jnp.float32)


def _bdot_nt(a, b):
    return lax.dot_general(a, b, (((1,), (1,)), ((), ())), preferred_element_type=jnp.float32)


def _silu(x):
    h = 0.5 * x
    return h + h * jnp.tanh(h)


def _softplus(x):
    return jnp.maximum(x, 0.0) + jnp.log(1.0 + jnp.exp(-jnp.abs(x)))


def _rms(x, w):
    return x * lax.rsqrt(jnp.mean(x * x, axis=-1, keepdims=True) + EPS) * w


def _rope(x, cos, sin_signed):
    width = x.shape[1]
    reps = width // LANES
    lane = lax.broadcasted_iota(jnp.int32, x.shape, 1)
    first_half = (lane % HEAD_DIM) < (HEAD_DIM // 2)
    swapped = jnp.where(first_half,
                        pltpu.roll(x, width - HEAD_DIM // 2, 1),
                        pltpu.roll(x, HEAD_DIM // 2, 1))
    cos_t = jnp.concatenate([cos] * reps, axis=1)
    sin_t = jnp.concatenate([sin_signed] * reps, axis=1)
    return x * cos_t + swapped * sin_t


def _rope_t(x, cos, sin_signed):
    rows = x.shape[0]
    reps = rows // LANES
    r = lax.broadcasted_iota(jnp.int32, x.shape, 0)
    first_half = (r % HEAD_DIM) < (HEAD_DIM // 2)
    swapped = jnp.where(first_half,
                        pltpu.roll(x, rows - HEAD_DIM // 2, 0),
                        pltpu.roll(x, HEAD_DIM // 2, 0))
    cos_t = jnp.concatenate([cos] * reps, axis=0)
    sin_t = jnp.concatenate([sin_signed] * reps, axis=0)
    return x * cos_t + swapped * sin_t


def _inproj_kernel(x_ref, nw_ref, w_ref, wt_ref, cos_ref, sin_ref, cost_ref, sint_ref,
                   ka_ref, g_ref, kb_ref, z_ref, xbc_ref, fdt_ref, qbt_ref, vbt_ref, qat_ref, vat_ref):
    h = _rms(x_ref[...], nw_ref[...]).astype(jnp.bfloat16)

    def seg(i):
        return _bdot(h, w_ref[:, _SEG_OFF[i]:_SEG_OFF[i] + _SEG_W[i]])

    scale = HEAD_DIM ** -0.5 * LOG2E
    ka_ref[...] = _rope(seg(0), cos_ref[...], sin_ref[...]).astype(ka_ref.dtype)
    g_ref[...] = seg(1)
    kb_ref[...] = seg(2).astype(kb_ref.dtype)
    z_ref[...] = seg(3)
    xbc_ref[...] = seg(4)
    fdt_ref[...] = seg(5)
    o_vb, o_qa, o_va = FOX_W, 2 * FOX_W, 2 * FOX_W + SWA_W
    qbt_ref[...] = (_bdot_nt(wt_ref[0:o_vb, :], h) * scale).astype(qbt_ref.dtype)
    qat = _rope_t(_bdot_nt(wt_ref[o_qa:o_va, :], h), cost_ref[...], sint_ref[...])
    qat_ref[...] = (qat * scale).astype(qat_ref.dtype)
    vat_ref[...] = _bdot_nt(wt_ref[o_va:, :], h).astype(vat_ref.dtype)
    vbt = _bdot_nt(wt_ref[o_vb:o_qa, :], h).astype(vbt_ref.dtype)
    for hp in range(FOX_W // LANES):
        for c in range(vbt.shape[1] // FOX_TK):
            vbt_ref[hp, c] = vbt[hp * LANES:(hp + 1) * LANES, c * FOX_TK:(c + 1) * FOX_TK]


def _arrange_w_in(w_in, fox_order):
    def cols(o, n):
        return w_in[:, o:o + n]

    def fox_cols(o):
        return cols(o, FOX_W).reshape(-1, FOX_HEADS, HEAD_DIM)[:, fox_order].reshape(-1, FOX_W)
    k0, k1 = cols(_O_KA, HEAD_DIM), cols(_O_KA + HEAD_DIM, HEAD_DIM)
    pad = jnp.zeros((w_in.shape[0], LANES - FOX_HEADS - SSM_HEADS), w_in.dtype)
    parts = [k0, k1, k1, k0,
             cols(_O_GA, SWA_W), fox_cols(_O_GB),
             fox_cols(_O_KB),
             cols(_O_ZC, SSM_W), cols(_O_XBC, SSM_CONV_W),
             cols(_O_FB, FOX_HEADS)[:, fox_order], cols(_O_DT, SSM_HEADS), pad]
    w_r = jnp.concatenate(parts, axis=1).astype(jnp.bfloat16)
    w_t = jnp.concatenate([fox_cols(_O_QB), fox_cols(_O_VB), cols(_O_QA, SWA_W), cols(_O_VA, SWA_KV_W)],
                          axis=1).T.astype(jnp.bfloat16)
    return w_r, w_t


def _inproj(x2, norm_w, w_r, w_t, rope, seq, tm):
    t = x2.shape[0]
    nblk_s = seq // tm
    row = lambda i: (i, 0)
    const = lambda i: (0, 0)
    pos = lambda i: (i % nblk_s, 0)
    pos_t = lambda i: (0, i % nblk_s)
    col = lambda i: (0, i)
    bf, f32 = jnp.bfloat16, jnp.float32
    outs = [(2 * SWA_KV_W, bf), (2 * SWA_W, f32), (FOX_W, bf), (SSM_W, f32), (SSM_CONV_W, f32), (LANES, f32)]
    npair = FOX_W // LANES
    return pl.pallas_call(
        _inproj_kernel,
        grid=(t // tm,),
        in_specs=[pl.BlockSpec((tm, D_MODEL), row),
                  pl.BlockSpec((1, D_MODEL), const),
                  pl.BlockSpec((D_MODEL, _PROJ_W), const, pipeline_mode=pl.Buffered(1)),
                  pl.BlockSpec(w_t.shape, const, pipeline_mode=pl.Buffered(1)),
                  pl.BlockSpec((tm, LANES), pos),
                  pl.BlockSpec((tm, LANES), pos),
                  pl.BlockSpec((LANES, tm), pos_t),
                  pl.BlockSpec((LANES, tm), pos_t)],
        out_specs=[pl.BlockSpec((tm, w), row) for w, _ in outs]
        + [pl.BlockSpec((FOX_W, tm), col),
           pl.BlockSpec((npair, tm // FOX_TK, LANES, FOX_TK), lambda i: (0, i, 0, 0)),
           pl.BlockSpec((SWA_W, tm), col),
           pl.BlockSpec((SWA_KV_W, tm), col)],
        out_shape=[jax.ShapeDtypeStruct((t, w), d) for w, d in outs]
        + [jax.ShapeDtypeStruct((FOX_W, t), bf),
           jax.ShapeDtypeStruct((npair, t // FOX_TK, LANES, FOX_TK), bf),
           jax.ShapeDtypeStruct((SWA_W, t), bf),
           jax.ShapeDtypeStruct((SWA_KV_W, t), bf)],
        compiler_params=pltpu.CompilerParams(dimension_semantics=("arbitrary",),
                                             vmem_limit_bytes=_VMEM_LIMIT),
        name="inproj",
    )(x2, norm_w.reshape(1, D_MODEL), w_r, w_t, *rope)


def _rope_tables(seq):
    pos = jnp.arange(seq, dtype=jnp.float32)
    inv = 1.0 / (ROPE_THETA ** (jnp.arange(0, HEAD_DIM, 2, dtype=jnp.float32) / HEAD_DIM))
    ang = pos[:, None] * inv[None, :]
    cos, sin = jnp.cos(ang), jnp.sin(ang)
    cos_t = jnp.concatenate([cos, cos, cos, cos], axis=1)
    sin_t = jnp.concatenate([-sin, sin, -sin, sin], axis=1)
    return cos_t, sin_t, cos_t.T, sin_t.T


def _swa_kernel(sink_ref, qt_ref, kc_ref, kp_ref, vc_ref, vp_ref, g_ref, o_ref):
    n = pl.program_id(1)
    nsub = SWA_TQ // BLOCK
    key = lax.broadcasted_iota(jnp.int32, (2 * BLOCK, BLOCK), 0)
    qry = lax.broadcasted_iota(jnp.int32, (2 * BLOCK, BLOCK), 1)
    band = jnp.where((key > qry) & (key <= qry + BLOCK), 0.0, NEG_INF)
    band_first = jnp.where(key < BLOCK, NEG_INF, band)
    row = lax.broadcasted_iota(jnp.int32, (LANES, BLOCK), 0)
    ones_rows = jnp.ones((16, 2 * BLOCK), jnp.bfloat16)

    def scores(u):
        if u == 0:
            kcat = jnp.concatenate([kp_ref[...], kc_ref[0:BLOCK, :]], axis=0)
            vcat = jnp.concatenate([vp_ref[...], vc_ref[:, 0:BLOCK]], axis=1)
            bias = jnp.where(n > 0, band, band_first)
        else:
            kcat = kc_ref[(u - 1) * BLOCK:(u + 1) * BLOCK, :]
            vcat = vc_ref[:, (u - 1) * BLOCK:(u + 1) * BLOCK]
            bias = band
        tiles = []
        for c in range(SWA_HEADS // 2):
            qt = qt_ref[c * LANES:(c + 1) * LANES, u * BLOCK:(u + 1) * BLOCK].astype(jnp.float32)
            for half in range(2):
                kv = (2 * c + half) // (SWA_HEADS // SWA_KV_HEADS)
                sel = 0 if kv == half else 1
                in_half = (row < HEAD_DIM) if half == 0 else (row >= HEAD_DIM)
                w = jnp.where(in_half, qt, 0.0).astype(jnp.bfloat16)
                tiles.append(_bdot(kcat[:, sel * LANES:(sel + 1) * LANES], w) + bias)
        return tiles, vcat

    def finish(u, tiles, vcat):
        probs, sink_terms = [], []
        for head, s in enumerate(tiles):
            sink = sink_ref[0, head] * LOG2E
            m = jnp.maximum(jnp.max(s, axis=0, keepdims=True), sink)
            probs.append(jnp.exp2(s - m).astype(jnp.bfloat16))
            sink_terms.append(jnp.exp2(sink - m))
        outs = []
        for head, p in enumerate(probs):
            kv = head // (SWA_HEADS // SWA_KV_HEADS)
            lhs = jnp.concatenate([vcat[kv * HEAD_DIM:(kv + 1) * HEAD_DIM, :], ones_rows], axis=0)
            pv = _bdot(lhs, p)
            outs.append(pv[0:HEAD_DIM, :] / (pv[HEAD_DIM:HEAD_DIM + 1, :] + sink_terms[head]))
        y = jnp.concatenate(outs, axis=0).T
        rows = slice(u * BLOCK, (u + 1) * BLOCK)
        o_ref[rows, :] = (y * _silu(g_ref[rows, :])).astype(o_ref.dtype)

    pending = scores(0)
    for u in range(nsub):
        nxt = scores(u + 1) if u + 1 < nsub else None
        finish(u, *pending)
        pending = nxt


def _swa(qat, ka, vat, g, sinks, batch, seq):
    n = seq // SWA_TQ
    nsub = SWA_TQ // BLOCK
    cur = lambda b, i: (b * n + i, 0)
    cur_t = lambda b, i: (0, b * n + i)
    prev = lambda b, i: ((b * n + i) * nsub - jnp.minimum(i, 1), 0)
    prev_t = lambda b, i: (0, (b * n + i) * nsub - jnp.minimum(i, 1))
    return pl.pallas_call(
        _swa_kernel,
        grid=(batch, n),
        in_specs=[pl.BlockSpec(memory_space=pltpu.SMEM),
                  pl.BlockSpec((SWA_W, SWA_TQ), cur_t),
                  pl.BlockSpec((SWA_TQ, 2 * SWA_KV_W), cur),
                  pl.BlockSpec((BLOCK, 2 * SWA_KV_W), prev),
                  pl.BlockSpec((SWA_KV_W, SWA_TQ), cur_t),
                  pl.BlockSpec((SWA_KV_W, BLOCK), prev_t),
                  pl.BlockSpec((SWA_TQ, SWA_W), cur)],
        out_specs=pl.BlockSpec((SWA_TQ, SWA_W), cur),
        out_shape=jax.ShapeDtypeStruct((batch * seq, SWA_W), jnp.bfloat16),
        compiler_params=pltpu.CompilerParams(dimension_semantics=("arbitrary", "arbitrary")),
        name="swa",
    )(sinks.reshape(1, SWA_HEADS).astype(jnp.float32), qat, ka, ka, vat, vat, g)


def _fox_kernel(cend_ref, qt_ref, k_ref, ca_ref, vt_ref, g_ref, o_ref, w_buf, s_buf0, s_buf1, acc_buf, bias_buf,
                js_ref):
    tq, tk = FOX_TQ, FOX_TK
    per_q = tq // tk
    nq = qt_ref.shape[1] // tq
    all_masked = per_q
    acc_rows = acc_buf.shape[1]

    krow = lax.broadcasted_iota(jnp.int32, (tk, tq), 0)
    qcol = lax.broadcasted_iota(jnp.int32, (tk, tq), 1)
    for d in range(per_q):
        bias_buf[d] = jnp.where(krow + d * tk <= qcol, 0.0, NEG_INF)
    bias_buf[all_masked] = jnp.full((tk, tq), NEG_INF, jnp.float32)
    for buf in (s_buf0, s_buf1, acc_buf):
        buf[...] = jnp.zeros_like(buf)

    row = lax.broadcasted_iota(jnp.int32, (LANES, tq), 0)
    nsplit = 3
    pick = (jnp.where(row < nsplit, 1.0, 0.0).astype(jnp.bfloat16),
            jnp.where((row >= HEAD_DIM) & (row < HEAD_DIM + nsplit), 1.0, 0.0).astype(jnp.bfloat16))
    ones_rows = jnp.ones((acc_rows - HEAD_DIM, tk), jnp.bfloat16)

    qsq = jnp.square(qt_ref[...].astype(jnp.float32))
    ksq = jnp.square(k_ref[...].astype(jnp.float32))
    lane_sel = lax.broadcasted_iota(jnp.int32, (LANES, LANES), 0) // HEAD_DIM == \
        lax.broadcasted_iota(jnp.int32, (LANES, LANES), 1)
    kn2 = jnp.max(_bdot(ksq.astype(jnp.bfloat16), jnp.where(lane_sel, 1.0, 0.0).astype(jnp.bfloat16)),
                  axis=0, keepdims=True)
    bound = []
    for hh in range(2):
        qn2 = jnp.max(jnp.sum(qsq[hh * HEAD_DIM:(hh + 1) * HEAD_DIM, :], axis=0, keepdims=True),
                      axis=1, keepdims=True)
        bound.append((FOX_NORM_MARGIN * 2.0 * jnp.sqrt(qn2 * kn2[:, hh:hh + 1]))[0, 0])
    head0 = (pl.program_id(0) * (FOX_HEADS // 2) + pl.program_id(1)) * 2
    n_steps = jnp.int32(0)
    for i in range(nq):
        count, prefix = jnp.int32(0), jnp.bool_(True)
        for j in range(per_q * i):
            for hh in range(2):
                drop = cend_ref[head0 + hh, j] - cend_ref[head0 + hh, per_q * i - 1]
                prefix = prefix & (bound[hh] - drop < FOX_SKIP_BELOW)
            count = count + prefix.astype(jnp.int32)
        js_ref[i] = count
        n_steps = n_steps + (per_q * (i + 1) - count)

    def step(carry, s_cur, s_prv, masked):
        (i_a, j_a), (i_b, j_b, v_b, first), smax, m = carry
        first_a = j_a == js_ref[jnp.minimum(i_a, nq - 1)]

        @pl.when(first_a)
        def _():
            q0 = pl.multiple_of(jnp.minimum(i_a, nq - 1) * tq, tq)
            qt = qt_ref[:, pl.ds(q0, tq)].astype(jnp.float32)
            q_lo = jnp.where(row < HEAD_DIM, qt, 0.0).astype(jnp.bfloat16)
            q_hi = jnp.where(row < HEAD_DIM, 0.0, qt).astype(jnp.bfloat16)
            w_buf[:, 0:tq] = jnp.concatenate([q_lo, pick[0]], axis=0)
            w_buf[:, tq:2 * tq] = jnp.concatenate([q_hi, pick[1]], axis=0)

        k0 = pl.multiple_of(jnp.where(i_a < nq, j_a, 0) * tk, tk)
        ka = jnp.concatenate([k_ref[pl.ds(k0, tk), :], ca_ref[pl.ds(k0, tk), :]], axis=1)
        if masked:
            bias_slot = jnp.where(i_a < nq, j_a - per_q * i_a, all_masked)
        vt = vt_ref[0, j_b]
        n_chunks = 2 * tq // FOX_CHUNK
        smax_new, m_new = [], []

        def chunk_cols(ch):
            hh, start = divmod(ch * FOX_CHUNK, tq)
            return slice(ch * FOX_CHUNK, (ch + 1) * FOX_CHUNK), hh, slice(start, start + FOX_CHUNK)

        def stage_a(ch):
            cols, _, hcols = chunk_cols(ch)
            s_new = _bdot(ka, w_buf[:, cols])
            if masked:
                s_new = s_new + bias_buf[bias_slot, :, hcols]
            s_cur[:, cols] = s_new
            smax_new.append(jnp.max(s_new, axis=0, keepdims=True))

        def stage_b(ch):
            cols, hh, hcols = chunk_cols(ch)
            m_prev = jnp.where(first, NEG_INF, m[ch])
            mn = jnp.maximum(m_prev, smax[ch])
            p = jnp.exp2(s_prv[:, cols] - mn).astype(jnp.bfloat16)
            lhs = jnp.concatenate([vt[hh * HEAD_DIM:(hh + 1) * HEAD_DIM, :], ones_rows], axis=0)
            acc_buf[hh, :, hcols] = jnp.exp2(m_prev - mn) * acc_buf[hh, :, hcols] + _bdot(lhs, p)
            m_new.append(mn)

        stage_a(0)
        for ch in range(n_chunks):
            if ch + 1 < n_chunks:
                stage_a(ch + 1)
            stage_b(ch)
        smax_new = tuple(smax_new)

        @pl.when(v_b & (j_b == per_q * (i_b + 1) - 1))
        def _():
            q0 = pl.multiple_of(i_b * tq, tq)
            out_t = jnp.concatenate([acc_buf[hh, 0:HEAD_DIM, :] / acc_buf[hh, HEAD_DIM:HEAD_DIM + 1, :]
                                     for hh in range(2)], axis=0)
            y = out_t.T * _silu(g_ref[pl.ds(q0, tq), :])
            o_ref[pl.ds(q0, tq), :] = y.astype(o_ref.dtype)

        last_a = j_a == per_q * (i_a + 1) - 1
        nxt_a = (jnp.where(last_a, i_a + 1, i_a),
                 jnp.where(last_a, js_ref[jnp.minimum(i_a + 1, nq - 1)], j_a + 1))
        return (nxt_a, (i_a, jnp.where(i_a < nq, j_a, 0), i_a < nq, first_a), smax_new, tuple(m_new))

    def either(carry, bufs):
        i_a, j_a = carry[0]
        needs_mask = (j_a >= per_q * i_a) | (i_a >= nq)
        return lax.cond(needs_mask, lambda c: step(c, *bufs, True), lambda c: step(c, *bufs, False), carry)

    def body(_, carry):
        carry = either(carry, (s_buf0, s_buf1))
        return either(carry, (s_buf1, s_buf0))

    zi = jnp.int32(0)
    row_vec = lambda v: tuple(jnp.full((1, FOX_CHUNK), v, jnp.float32) for _ in range(2 * tq // FOX_CHUNK))
    init = ((zi, zi), (zi, zi, False, False), row_vec(0.0), row_vec(0.0))
    lax.fori_loop(0, (n_steps + 2) // 2, body, init)


def _fox(cend_tiles, qbt, kb, caug, vbt, g, batch, seq):
    npair = FOX_HEADS // 2
    nkt = seq // FOX_TK
    gcol0 = SWA_W // LANES
    return pl.pallas_call(
        _fox_kernel,
        grid=(batch, npair),
        in_specs=[pl.BlockSpec(memory_space=pltpu.SMEM),
                  pl.BlockSpec((LANES, seq), lambda b, h: (h, b)),
                  pl.BlockSpec((seq, LANES), lambda b, h: (b, h)),
                  pl.BlockSpec((seq, LANES), lambda b, h: (b, h)),
                  pl.BlockSpec((1, nkt, LANES, FOX_TK), lambda b, h: (h, b, 0, 0)),
                  pl.BlockSpec((seq, LANES), lambda b, h: (b, gcol0 + h))],
        out_specs=pl.BlockSpec((seq, LANES), lambda b, h: (b, h)),
        out_shape=jax.ShapeDtypeStruct((batch * seq, FOX_W), jnp.bfloat16),
        scratch_shapes=[pltpu.VMEM((2 * LANES, 2 * FOX_TQ), jnp.bfloat16),
                        pltpu.VMEM((FOX_TK, 2 * FOX_TQ), jnp.float32),
                        pltpu.VMEM((FOX_TK, 2 * FOX_TQ), jnp.float32),
                        pltpu.VMEM((2, HEAD_DIM + 16, FOX_TQ), jnp.float32),
                        pltpu.VMEM((FOX_TQ // FOX_TK + 1, FOX_TK, FOX_TQ), jnp.float32),
                        pltpu.SMEM((seq // FOX_TQ,), jnp.int32)],
        compiler_params=pltpu.CompilerParams(dimension_semantics=("arbitrary", "arbitrary"),
                                             vmem_limit_bytes=_VMEM_LIMIT),
        name="fox",
    )(cend_tiles, qbt, kb, caug, vbt, g)


def _split3(x):
    hi = x.astype(jnp.bfloat16)
    r1 = x - hi.astype(jnp.float32)
    mid = r1.astype(jnp.bfloat16)
    lo = (r1 - mid.astype(jnp.float32)).astype(jnp.bfloat16)
    return hi, mid, lo


def _ssd_kernel(xbc_ref, z_ref, fdt_ref, cw_ref, cb_ref, bias_ref, alog_ref, dskip_ref, nw_ref,
                y_ref, ca_ref, cend_ref, state_ref, tail_ref, ccarry_ref):
    ci = pl.program_id(1)
    L = BLOCK

    @pl.when(ci == 0)
    def _():
        state_ref[...] = jnp.zeros_like(state_ref)
        tail_ref[...] = jnp.zeros_like(tail_ref)
        ccarry_ref[...] = jnp.zeros_like(ccarry_ref)

    u = xbc_ref[...]
    tail_ref[8:8 + L, :] = u
    conv = cb_ref[...] + cw_ref[SSM_CONV - 1:SSM_CONV, :] * u
    for k in range(1, SSM_CONV):
        conv = conv + cw_ref[SSM_CONV - 1 - k:SSM_CONV - k, :] * tail_ref[8 - k:8 - k + L, :]
    tail_ref[0:8, :] = u[L - 8:L]
    act = _silu(conv)
    xs = act[:, :SSM_W]
    bm = act[:, SSM_W:SSM_W + SSM_BC_W].astype(jnp.bfloat16)
    cm = act[:, SSM_W + SSM_BC_W:].astype(jnp.bfloat16)

    lane = lax.broadcasted_iota(jnp.int32, (L, LANES), 1)
    vals = fdt_ref[...] + bias_ref[...]
    is_f = lane < DT_LANE0
    is_dt = (lane >= DT_LANE0) & (lane < DT_LANE0 + SSM_HEADS)
    sp = _softplus(jnp.where(is_f, -vals, vals))
    a_row = jnp.where(is_dt[0:1], -jnp.exp(alog_ref[...]), 0.0)
    dt = jnp.where(is_dt, sp, 0.0)
    scan_in = jnp.where(is_f, -sp, dt * a_row) * LOG2E

    ri = lax.broadcasted_iota(jnp.int32, (L, L), 0)
    cj = lax.broadcasted_iota(jnp.int32, (L, L), 1)
    causal = cj <= ri
    tri = jnp.where(causal, 1.0, 0.0).astype(jnp.bfloat16)
    hi, mid, lo = _split3(scan_in)
    cs = _bdot(tri, hi) + _bdot(tri, mid) + _bdot(tri, lo)
    cs_t = cs.T

    c_full = cs + ccarry_ref[...]
    ccarry_ref[...] = jnp.where(is_f[0:1], c_full[L - 1:L, :], 0.0)
    cend_ref[0] = c_full[L - 1:L, :]
    lane64 = lane % HEAD_DIM
    ca_cols = []
    for hp in range(FOX_HEADS // 2):
        negc = -jnp.where(lane < HEAD_DIM, c_full[:, 2 * hp:2 * hp + 1], c_full[:, 2 * hp + 1:2 * hp + 2])
        hi, mid, lo = (term.astype(jnp.float32) for term in _split3(negc))
        terms = jnp.where(lane64 == 0, hi, jnp.where(lane64 == 1, mid, jnp.where(lane64 == 2, lo, 0.0)))
        ca_cols.append(terms.astype(jnp.bfloat16))
    ca_ref[...] = jnp.concatenate(ca_cols, axis=1)

    rowp = lax.broadcasted_iota(jnp.int32, (L, LANES), 0)
    low_l = lane < HEAD_DIM
    low_r = rowp < HEAD_DIM
    hpg = SSM_HEADS // 2
    npairs = SSM_HEADS // 2
    grp_of = [(2 * pair) // hpg for pair in range(npairs)]
    bm_g = [bm[:, g * SSM_STATE:(g + 1) * SSM_STATE] for g in range(2)]
    cm_g = [cm[:, g * SSM_STATE:(g + 1) * SSM_STATE] for g in range(2)]
    cb = [_bdot_nt(cm_g[g], bm_g[g]) for g in range(2)]
    ys = []
    for pair in range(npairs):
        g = grp_of[pair]
        ha, hb = DT_LANE0 + 2 * pair, DT_LANE0 + 2 * pair + 1
        col = (cs[:, ha:ha + 1], cs[:, hb:hb + 1])
        acs_row = (cs_t[ha:ha + 1, :], cs_t[hb:hb + 1, :])
        xdt = xs[:, pair * LANES:(pair + 1) * LANES] * jnp.where(low_l, dt[:, ha:ha + 1], dt[:, hb:hb + 1])
        xdt_b = xdt.astype(jnp.bfloat16)
        prev = state_ref[pair]
        y_off = _bdot_nt(cm_g[g], prev.astype(jnp.bfloat16))
        last = (col[0][L - 1:L, :], col[1][L - 1:L, :])
        dst = jnp.exp2(jnp.where(low_l, last[0] - col[0], last[1] - col[1]))
        st_new = _bdot((xdt * dst).T.astype(jnp.bfloat16), bm_g[g])
        state_ref[pair] = prev * jnp.exp2(jnp.where(low_r, last[0], last[1])) + st_new
        y_diag = []
        for hh in range(2):
            decay = jnp.exp2(jnp.where(causal, col[hh] - acs_row[hh], NEG_INF))
            gmat = (cb[g] * decay).astype(jnp.bfloat16)
            y_diag.append(_bdot(gmat, xdt_b))
        ys.append(jnp.where(low_l, y_diag[0], y_diag[1]) + y_off * jnp.exp2(jnp.where(low_l, col[0], col[1])))
    y = jnp.concatenate(ys, axis=1) + dskip_ref[...] * xs
    y_ref[...] = _rms(y * _silu(z_ref[...]), nw_ref[...]).astype(y_ref.dtype)


def _ssd(xbc, z, fdt, conv_w, conv_b, b_forget, dt_bias, a_log, d_skip, norm_w, batch, seq):
    nc = seq // BLOCK
    row = lambda b, c: (b * nc + c, 0)
    const = lambda b, c: (0, 0)
    zpad = jnp.zeros((LANES - FOX_HEADS - SSM_HEADS,), jnp.float32)
    bias_row = jnp.concatenate([b_forget, dt_bias, zpad]).reshape(1, LANES)
    alog_row = jnp.concatenate([jnp.zeros((FOX_HEADS,), jnp.float32), a_log, zpad]).reshape(1, LANES)
    dskip_row = jnp.repeat(d_skip, HEAD_DIM).reshape(1, SSM_W)
    return pl.pallas_call(
        _ssd_kernel,
        grid=(batch, nc),
        in_specs=[pl.BlockSpec((BLOCK, SSM_CONV_W), row),
                  pl.BlockSpec((BLOCK, SSM_W), row),
                  pl.BlockSpec((BLOCK, LANES), row),
                  pl.BlockSpec((SSM_CONV, SSM_CONV_W), const),
                  pl.BlockSpec((1, SSM_CONV_W), const),
                  pl.BlockSpec((1, LANES), const),
                  pl.BlockSpec((1, LANES), const),
                  pl.BlockSpec((1, SSM_W), const),
                  pl.BlockSpec((1, SSM_W), const)],
        out_specs=[pl.BlockSpec((BLOCK, SSM_W), row),
                   pl.BlockSpec((BLOCK, FOX_W), row),
                   pl.BlockSpec((1, 1, LANES), lambda b, c: (b * nc + c, 0, 0))],
        out_shape=[jax.ShapeDtypeStruct((batch * seq, SSM_W), jnp.bfloat16),
                   jax.ShapeDtypeStruct((batch * seq, FOX_W), jnp.bfloat16),
                   jax.ShapeDtypeStruct((batch * nc, 1, LANES), jnp.float32)],
        scratch_shapes=[pltpu.VMEM((SSM_HEADS // 2, 2 * HEAD_DIM, SSM_STATE), jnp.float32),
                        pltpu.VMEM((8 + BLOCK, SSM_CONV_W), jnp.float32),
                        pltpu.VMEM((1, LANES), jnp.float32)],
        compiler_params=pltpu.CompilerParams(dimension_semantics=("arbitrary", "arbitrary")),
        name="ssd",
    )(xbc, z, fdt, conv_w, conv_b.reshape(1, SSM_CONV_W), bias_row, alog_row, dskip_row,
      norm_w.reshape(1, SSM_W))


def _memkv_kernel(mem_ref, nw_ref, wk_ref, wv_ref, k_ref, v_ref):
    mn = _rms(mem_ref[...], nw_ref[...]).astype(jnp.bfloat16)
    k_ref[...] = _bdot(mn, wk_ref[...]).astype(k_ref.dtype)
    v_ref[...] = _bdot(mn, wv_ref[...]).astype(v_ref.dtype)


def _memkv(mem2, norm_w, wk, wv, mem_tokens):
    t = mem2.shape[0]
    row = lambda b: (b, 0)
    const = lambda b: (0, 0)
    return pl.pallas_call(
        _memkv_kernel,
        grid=(t // mem_tokens,),
        in_specs=[pl.BlockSpec((mem_tokens, D_MODEL), row),
                  pl.BlockSpec((1, D_MODEL), const),
                  pl.BlockSpec((D_MODEL, D_MODEL), const),
                  pl.BlockSpec((D_MODEL, D_MODEL), const)],
        out_specs=[pl.BlockSpec((mem_tokens, D_MODEL), row)] * 2,
        out_shape=[jax.ShapeDtypeStruct((t, D_MODEL), jnp.bfloat16)] * 2,
        compiler_params=pltpu.CompilerParams(dimension_semantics=("arbitrary",)),
        name="memkv",
    )(mem2, norm_w.reshape(1, D_MODEL), wk.astype(jnp.bfloat16), wv.astype(jnp.bfloat16))


def _out_kernel(x_ref, ya_ref, yb_ref, yc_ref, wo_ref, nq_ref, wq_ref, k_ref, v_ref, wmo_ref, fn_ref,
                o_ref, *, final_norm):
    tm = x_ref.shape[0]
    groups = [slice(r, r + OUT_ROWS) for r in range(0, tm, OUT_ROWS)]
    head_cols = [slice(h * MEM_HEAD_DIM, (h + 1) * MEM_HEAD_DIM) for h in range(MEM_HEADS)]
    x1 = [x_ref[g, :]
          + _bdot(ya_ref[g, :], wo_ref[0:SWA_W, :])
          + _bdot(yb_ref[g, :], wo_ref[SWA_W:SWA_W + FOX_W, :])
          + _bdot(yc_ref[g, :], wo_ref[SWA_W + FOX_W:, :]) for g in groups]
    q = []
    for x1g in x1:
        hq = _rms(x1g, nq_ref[...]).astype(jnp.bfloat16)
        q.append((_bdot(hq, wq_ref[...]) * (MEM_HEAD_DIM ** -0.5 * LOG2E)).astype(jnp.bfloat16))
    scores = [[_bdot_nt(qg[:, sl], k_ref[:, sl]) for sl in head_cols] for qg in q]
    att = []
    for sg in scores:
        heads = []
        for s, sl in zip(sg, head_cols):
            p = jnp.exp2(s - jnp.max(s, axis=-1, keepdims=True))
            probs = (p / jnp.sum(p, axis=-1, keepdims=True)).astype(jnp.bfloat16)
            heads.append(_bdot(probs, v_ref[:, sl]).astype(jnp.bfloat16))
        att.append(jnp.concatenate(heads, axis=1))
    for g, x1g, attg in zip(groups, x1, att):
        x2 = x1g + _bdot(attg, wmo_ref[...])
        if final_norm:
            x2 = _rms(x2, fn_ref[...])
        o_ref[g, :] = x2


def _out_block(x2, ya, yb, yc, w_out, norm_xq_w, w_mq, kmem, vmem, w_mo, final_w, seq, mem_tokens, tm,
               final_norm):
    t = x2.shape[0]
    nblk_s = seq // tm
    row = lambda i: (i, 0)
    const = lambda i: (0, 0)
    memrow = lambda i: (i // nblk_s, 0)
    bf = jnp.bfloat16
    return pl.pallas_call(
        functools.partial(_out_kernel, final_norm=final_norm),
        grid=(t // tm,),
        in_specs=[pl.BlockSpec((tm, D_MODEL), row),
                  pl.BlockSpec((tm, SWA_W), row),
                  pl.BlockSpec((tm, FOX_W), row),
                  pl.BlockSpec((tm, SSM_W), row),
                  pl.BlockSpec((SWA_W + FOX_W + SSM_W, D_MODEL), const, pipeline_mode=pl.Buffered(1)),
                  pl.BlockSpec((1, D_MODEL), const),
                  pl.BlockSpec((D_MODEL, D_MODEL), const, pipeline_mode=pl.Buffered(1)),
                  pl.BlockSpec((mem_tokens, D_MODEL), memrow),
                  pl.BlockSpec((mem_tokens, D_MODEL), memrow),
                  pl.BlockSpec((D_MODEL, D_MODEL), const, pipeline_mode=pl.Buffered(1)),
                  pl.BlockSpec((1, D_MODEL), const)],
        out_specs=pl.BlockSpec((tm, D_MODEL), row),
        out_shape=jax.ShapeDtypeStruct((t, D_MODEL), jnp.float32),
        compiler_params=pltpu.CompilerParams(dimension_semantics=("arbitrary",),
                                             vmem_limit_bytes=_VMEM_LIMIT),
        name="outproj_mem",
    )(x2, ya, yb, yc, w_out.astype(bf), norm_xq_w.reshape(1, D_MODEL), w_mq.astype(bf), kmem, vmem,
      w_mo.astype(bf), final_w.reshape(1, D_MODEL))


def _row_tile(seq):
    return min(512, seq)


def _out_row_tile(seq):
    return min(1024, seq)


def kernel(x, mem, norm_mix_w, w_in, b_forget, swa_sinks, conv_w, conv_b, dt_bias, a_log, d_skip, ssm_norm_w, w_out, norm_xq_w, norm_mem_w, w_mq, w_mk, w_mv, w_mo, final_norm_w):
    batch, seq, _ = x.shape
    mem_tokens = mem.shape[1]
    depth = w_in.shape[0]
    assert seq % BLOCK == 0
    tm = _row_tile(seq)
    assert seq % tm == 0 and seq % FOX_TQ == 0 and seq % SWA_TQ == 0
    rope = _rope_tables(seq)
    x2 = x.reshape(batch * seq, D_MODEL)
    mem2 = mem.reshape(batch * mem_tokens, D_MODEL)
    for l in range(depth):
        fox_order = jnp.argsort(b_forget[l])
        w_r, w_t = _arrange_w_in(w_in[l], fox_order)
        ka, g, kb, z, xbc, fdt, qbt, vbt, qat, vat = _inproj(x2, norm_mix_w[l], w_r, w_t, rope, seq, tm)
        yc, caug, cend = _ssd(xbc, z, fdt, conv_w[l], conv_b[l], b_forget[l][fox_order], dt_bias[l], a_log[l],
                             d_skip[l], ssm_norm_w[l], batch, seq)
        per_tile = FOX_TK // BLOCK
        cend_tiles = cend.reshape(batch, seq // BLOCK, LANES)[:, per_tile - 1::per_tile, :FOX_HEADS]
        cend_tiles = cend_tiles.transpose(0, 2, 1).reshape(batch * FOX_HEADS, seq // FOX_TK)
        ya = _swa(qat, ka, vat, g, swa_sinks[l], batch, seq)
        yb = _fox(cend_tiles, qbt, kb, caug, vbt, g, batch, seq)
        kmem, vmem = _memkv(mem2, norm_mem_w[l], w_mk[l], w_mv[l], mem_tokens)
        o_b = SWA_W
        w_o = w_out[l].at[o_b:o_b + FOX_W].set(
            w_out[l][o_b:o_b + FOX_W].reshape(FOX_HEADS, HEAD_DIM, D_MODEL)[fox_order].reshape(FOX_W, D_MODEL))
        x2 = _out_block(x2, ya, yb, yc, w_o, norm_xq_w[l], w_mq[l], kmem, vmem, w_mo[l],
                        final_norm_w, seq, mem_tokens, _out_row_tile(seq), final_norm=(l == depth - 1))
    return x2.reshape(batch, seq, D_MODEL)
```

```python
import functools

import jax
import jax.numpy as jnp
from jax import lax
from jax.experimental import pallas as pl
from jax.experimental.pallas import tpu as pltpu

D_MODEL = 1024
HEAD_DIM = 64
BLOCK = 128
SWA_HEADS = 8
SWA_KV_HEADS = 2
FOX_HEADS = 8
SSM_HEADS = 16
SSM_STATE = 128
SSM_CONV = 4
MEM_HEADS = 4
MEM_HEAD_DIM = 256
ROPE_THETA = 10000.0
EPS = 1e-6
NEG_INF = -1e30

SWA_W = SWA_HEADS * HEAD_DIM
SWA_KV_W = SWA_KV_HEADS * HEAD_DIM
FOX_W = FOX_HEADS * HEAD_DIM
SSM_W = SSM_HEADS * HEAD_DIM
SSM_BC_W = 2 * SSM_STATE
SSM_CONV_W = SSM_W + 2 * SSM_BC_W
LANES = 128
LOG2E = 1.4426950408889634
SWA_TQ = 512
FOX_TQ = 1024
FOX_CHUNK = 256
FOX_SKIP_BELOW = -170.0
FOX_NORM_MARGIN = 1.02
FOX_TK = 512
OUT_ROWS = 256
DT_LANE0 = FOX_HEADS

_O_QA = 0
_O_KA = _O_QA + SWA_W
_O_VA = _O_KA + SWA_KV_W
_O_GA = _O_VA + SWA_KV_W
_O_QB = _O_GA + SWA_W
_O_KB = _O_QB + FOX_W
_O_VB = _O_KB + FOX_W
_O_FB = _O_VB + FOX_W
_O_GB = _O_FB + FOX_HEADS
_O_ZC = _O_GB + FOX_W
_O_XBC = _O_ZC + SSM_W
_O_DT = _O_XBC + SSM_CONV_W
_IN_W = _O_DT + SSM_HEADS

_SEG_W = (2 * SWA_KV_W, 2 * SWA_W, FOX_W, SSM_W, SSM_CONV_W, LANES)
_SEG_OFF = tuple(sum(_SEG_W[:i]) for i in range(len(_SEG_W)))
_PROJ_W = sum(_SEG_W)

_VMEM_LIMIT = 56 * 1024 * 1024


def _bdot(a, b):
    return jnp.dot(a, b, preferred_element_type=jnp.float32)


def _bdot_nt(a, b):
    return lax.dot_general(a, b, (((1,), (1,)), ((), ())), preferred_element_type=jnp.float32)


def _silu(x):
    h = 0.5 * x
    return h + h * jnp.tanh(h)


def _softplus(x):
    return jnp.maximum(x, 0.0) + jnp.log(1.0 + jnp.exp(-jnp.abs(x)))


def _rms(x, w):
    return x * lax.rsqrt(jnp.mean(x * x, axis=-1, keepdims=True) + EPS) * w


def _rope(x, cos, sin_signed):
    width = x.shape[1]
    reps = width // LANES
    lane = lax.broadcasted_iota(jnp.int32, x.shape, 1)
    first_half = (lane % HEAD_DIM) < (HEAD_DIM // 2)
    swapped = jnp.where(first_half,
                        pltpu.roll(x, width - HEAD_DIM // 2, 1),
                        pltpu.roll(x, HEAD_DIM // 2, 1))
    cos_t = jnp.concatenate([cos] * reps, axis=1)
    sin_t = jnp.concatenate([sin_signed] * reps, axis=1)
    return x * cos_t + swapped * sin_t


def _rope_t(x, cos, sin_signed):
    rows = x.shape[0]
    reps = rows // LANES
    r = lax.broadcasted_iota(jnp.int32, x.shape, 0)
    first_half = (r % HEAD_DIM) < (HEAD_DIM // 2)
    swapped = jnp.where(first_half,
                        pltpu.roll(x, rows - HEAD_DIM // 2, 0),
                        pltpu.roll(x, HEAD_DIM // 2, 0))
    cos_t = jnp.concatenate([cos] * reps, axis=0)
    sin_t = jnp.concatenate([sin_signed] * reps, axis=0)
    return x * cos_t + swapped * sin_t


def _inproj_kernel(x_ref, nw_ref, w_ref, wt_ref, cos_ref, sin_ref, cost_ref, sint_ref,
                   ka_ref, g_ref, kb_ref, z_ref, xbc_ref, fdt_ref, qbt_ref, vbt_ref, qat_ref, vat_ref):
    h = _rms(x_ref[...], nw_ref[...]).astype(jnp.bfloat16)

    def seg(i):
        return _bdot(h, w_ref[:, _SEG_OFF[i]:_SEG_OFF[i] + _SEG_W[i]])

    scale = HEAD_DIM ** -0.5 * LOG2E
    ka_ref[...] = _rope(seg(0), cos_ref[...], sin_ref[...]).astype(ka_ref.dtype)
    g_ref[...] = seg(1)
    kb_ref[...] = seg(2).astype(kb_ref.dtype)
    z_ref[...] = seg(3)
    xbc_ref[...] = seg(4)
    fdt_ref[...] = seg(5)
    o_vb, o_qa, o_va = FOX_W, 2 * FOX_W, 2 * FOX_W + SWA_W
    qbt_ref[...] = (_bdot_nt(wt_ref[0:o_vb, :], h) * scale).astype(qbt_ref.dtype)
    qat = _rope_t(_bdot_nt(wt_ref[o_qa:o_va, :], h), cost_ref[...], sint_ref[...])
    qat_ref[...] = (qat * scale).astype(qat_ref.dtype)
    vat_ref[...] = _bdot_nt(wt_ref[o_va:, :], h).astype(vat_ref.dtype)
    vbt = _bdot_nt(wt_ref[o_vb:o_qa, :], h).astype(vbt_ref.dtype)
    for hp in range(FOX_W // LANES):
        for c in range(vbt.shape[1] // FOX_TK):
            vbt_ref[hp, c] = vbt[hp * LANES:(hp + 1) * LANES, c * FOX_TK:(c + 1) * FOX_TK]


def _arrange_w_in(w_in, fox_order):
    def cols(o, n):
        return w_in[:, o:o + n]

    def fox_cols(o):
        return cols(o, FOX_W).reshape(-1, FOX_HEADS, HEAD_DIM)[:, fox_order].reshape(-1, FOX_W)
    k0, k1 = cols(_O_KA, HEAD_DIM), cols(_O_KA + HEAD_DIM, HEAD_DIM)
    pad = jnp.zeros((w_in.shape[0], LANES - FOX_HEADS - SSM_HEADS), w_in.dtype)
    parts = [k0, k1, k1, k0,
             cols(_O_GA, SWA_W), fox_cols(_O_GB),
             fox_cols(_O_KB),
             cols(_O_ZC, SSM_W), cols(_O_XBC, SSM_CONV_W),
             cols(_O_FB, FOX_HEADS)[:, fox_order], cols(_O_DT, SSM_HEADS), pad]
    w_r = jnp.concatenate(parts, axis=1).astype(jnp.bfloat16)
    w_t = jnp.concatenate([fox_cols(_O_QB), fox_cols(_O_VB), cols(_O_QA, SWA_W), cols(_O_VA, SWA_KV_W)],
                          axis=1).T.astype(jnp.bfloat16)
    return w_r, w_t


def _inproj(x2, norm_w, w_r, w_t, rope, seq, tm):
    t = x2.shape[0]
    nblk_s = seq // tm
    row = lambda i: (i, 0)
    const = lambda i: (0, 0)
    pos = lambda i: (i % nblk_s, 0)
    pos_t = lambda i: (0, i % nblk_s)
    col = lambda i: (0, i)
    bf, f32 = jnp.bfloat16, jnp.float32
    outs = [(2 * SWA_KV_W, bf), (2 * SWA_W, f32), (FOX_W, bf), (SSM_W, f32), (SSM_CONV_W, f32), (LANES, f32)]
    npair = FOX_W // LANES
    return pl.pallas_call(
        _inproj_kernel,
        grid=(t // tm,),
        in_specs=[pl.BlockSpec((tm, D_MODEL), row),
                  pl.BlockSpec((1, D_MODEL), const),
                  pl.BlockSpec((D_MODEL, _PROJ_W), const, pipeline_mode=pl.Buffered(1)),
                  pl.BlockSpec(w_t.shape, const, pipeline_mode=pl.Buffered(1)),
                  pl.BlockSpec((tm, LANES), pos),
                  pl.BlockSpec((tm, LANES), pos),
                  pl.BlockSpec((LANES, tm), pos_t),
                  pl.BlockSpec((LANES, tm), pos_t)],
        out_specs=[pl.BlockSpec((tm, w), row) for w, _ in outs]
        + [pl.BlockSpec((FOX_W, tm), col),
           pl.BlockSpec((npair, tm // FOX_TK, LANES, FOX_TK), lambda i: (0, i, 0, 0)),
           pl.BlockSpec((SWA_W, tm), col),
           pl.BlockSpec((SWA_KV_W, tm), col)],
        out_shape=[jax.ShapeDtypeStruct((t, w), d) for w, d in outs]
        + [jax.ShapeDtypeStruct((FOX_W, t), bf),
           jax.ShapeDtypeStruct((npair, t // FOX_TK, LANES, FOX_TK), bf),
           jax.ShapeDtypeStruct((SWA_W, t), bf),
           jax.ShapeDtypeStruct((SWA_KV_W, t), bf)],
        compiler_params=pltpu.CompilerParams(dimension_semantics=("arbitrary",),
                                             vmem_limit_bytes=_VMEM_LIMIT),
        name="inproj",
    )(x2, norm_w.reshape(1, D_MODEL), w_r, w_t, *rope)


def _rope_tables(seq):
    pos = jnp.arange(seq, dtype=jnp.float32)
    inv = 1.0 / (ROPE_THETA ** (jnp.arange(0, HEAD_DIM, 2, dtype=jnp.float32) / HEAD_DIM))
    ang = pos[:, None] * inv[None, :]
    cos, sin = jnp.cos(ang), jnp.sin(ang)
    cos_t = jnp.concatenate([cos, cos, cos, cos], axis=1)
    sin_t = jnp.concatenate([-sin, sin, -sin, sin], axis=1)
    return cos_t, sin_t, cos_t.T, sin_t.T


def _swa_kernel(sink_ref, qt_ref, kc_ref, kp_ref, vc_ref, vp_ref, g_ref, o_ref):
    n = pl.program_id(1)
    nsub = SWA_TQ // BLOCK
    key = lax.broadcasted_iota(jnp.int32, (2 * BLOCK, BLOCK), 0)
    qry = lax.broadcasted_iota(jnp.int32, (2 * BLOCK, BLOCK), 1)
    band = jnp.where((key > qry) & (key <= qry + BLOCK), 0.0, NEG_INF)
    band_first = jnp.where(key < BLOCK, NEG_INF, band)
    row = lax.broadcasted_iota(jnp.int32, (LANES, BLOCK), 0)
    ones_rows = jnp.ones((16, 2 * BLOCK), jnp.bfloat16)

    def scores(u):
        if u == 0:
            kcat = jnp.concatenate([kp_ref[...], kc_ref[0:BLOCK, :]], axis=0)
            vcat = jnp.concatenate([vp_ref[...], vc_ref[:, 0:BLOCK]], axis=1)
            bias = jnp.where(n > 0, band, band_first)
        else:
            kcat = kc_ref[(u - 1) * BLOCK:(u + 1) * BLOCK, :]
            vcat = vc_ref[:, (u - 1) * BLOCK:(u + 1) * BLOCK]
            bias = band
        tiles = []
        for c in range(SWA_HEADS // 2):
            qt = qt_ref[c * LANES:(c + 1) * LANES, u * BLOCK:(u + 1) * BLOCK].astype(jnp.float32)
            for half in range(2):
                kv = (2 * c + half) // (SWA_HEADS // SWA_KV_HEADS)
                sel = 0 if kv == half else 1
                in_half = (row < HEAD_DIM) if half == 0 else (row >= HEAD_DIM)
                w = jnp.where(in_half, qt, 0.0).astype(jnp.bfloat16)
                tiles.append(_bdot(kcat[:, sel * LANES:(sel + 1) * LANES], w) + bias)
        return tiles, vcat

    def finish(u, tiles, vcat):
        probs, sink_terms = [], []
        for head, s in enumerate(tiles):
            sink = sink_ref[0, head] * LOG2E
            m = jnp.maximum(jnp.max(s, axis=0, keepdims=True), sink)
            probs.append(jnp.exp2(s - m).astype(jnp.bfloat16))
            sink_terms.append(jnp.exp2(sink - m))
        outs = []
        for head, p in enumerate(probs):
            kv = head // (SWA_HEADS // SWA_KV_HEADS)
            lhs = jnp.concatenate([vcat[kv * HEAD_DIM:(kv + 1) * HEAD_DIM, :], ones_rows], axis=0)
            pv = _bdot(lhs, p)
            outs.append(pv[0:HEAD_DIM, :] / (pv[HEAD_DIM:HEAD_DIM + 1, :] + sink_terms[head]))
        y = jnp.concatenate(outs, axis=0).T
        rows = slice(u * BLOCK, (u + 1) * BLOCK)
        o_ref[rows, :] = (y * _silu(g_ref[rows, :])).astype(o_ref.dtype)

    pending = scores(0)
    for u in range(nsub):
        nxt = scores(u + 1) if u + 1 < nsub else None
        finish(u, *pending)
        pending = nxt


def _swa(qat, ka, vat, g, sinks, batch, seq):
    n = seq // SWA_TQ
    nsub = SWA_TQ // BLOCK
    cur = lambda b, i: (b * n + i, 0)
    cur_t = lambda b, i: (0, b * n + i)
    prev = lambda b, i: ((b * n + i) * nsub - jnp.minimum(i, 1), 0)
    prev_t = lambda b, i: (0, (b * n + i) * nsub - jnp.minimum(i, 1))
    return pl.pallas_call(
        _swa_kernel,
        grid=(batch, n),
        in_specs=[pl.BlockSpec(memory_space=pltpu.SMEM),
                  pl.BlockSpec((SWA_W, SWA_TQ), cur_t),
                  pl.BlockSpec((SWA_TQ, 2 * SWA_KV_W), cur),
                  pl.BlockSpec((BLOCK, 2 * SWA_KV_W), prev),
                  pl.BlockSpec((SWA_KV_W, SWA_TQ), cur_t),
                  pl.BlockSpec((SWA_KV_W, BLOCK), prev_t),
                  pl.BlockSpec((SWA_TQ, SWA_W), cur)],
        out_specs=pl.BlockSpec((SWA_TQ, SWA_W), cur),
        out_shape=jax.ShapeDtypeStruct((batch * seq, SWA_W), jnp.bfloat16),
        compiler_params=pltpu.CompilerParams(dimension_semantics=("arbitrary", "arbitrary")),
        name="swa",
    )(sinks.reshape(1, SWA_HEADS).astype(jnp.float32), qat, ka, ka, vat, vat, g)


def _fox_kernel(cend_ref, qt_ref, k_ref, ca_ref, vt_ref, g_ref, o_ref, w_buf, s_buf0, s_buf1, acc_buf, bias_buf,
                js_ref):
    tq, tk = FOX_TQ, FOX_TK
    per_q = tq // tk
    nq = qt_ref.shape[1] // tq
    all_masked = per_q
    acc_rows = acc_buf.shape[1]

    krow = lax.broadcasted_iota(jnp.int32, (tk, tq), 0)
    qcol = lax.broadcasted_iota(jnp.int32, (tk, tq), 1)
    for d in range(per_q):
        bias_buf[d] = jnp.where(krow + d * tk <= qcol, 0.0, NEG_INF)
    bias_buf[all_masked] = jnp.full((tk, tq), NEG_INF, jnp.float32)
    for buf in (s_buf0, s_buf1, acc_buf):
        buf[...] = jnp.zeros_like(buf)

    row = lax.broadcasted_iota(jnp.int32, (LANES, tq), 0)
    nsplit = 3
    pick = (jnp.where(row < nsplit, 1.0, 0.0).astype(jnp.bfloat16),
            jnp.where((row >= HEAD_DIM) & (row < HEAD_DIM + nsplit), 1.0, 0.0).astype(jnp.bfloat16))
    ones_rows = jnp.ones((acc_rows - HEAD_DIM, tk), jnp.bfloat16)

    qsq = jnp.square(qt_ref[...].astype(jnp.float32))
    ksq = jnp.square(k_ref[...].astype(jnp.float32))
    lane_sel = lax.broadcasted_iota(jnp.int32, (LANES, LANES), 0) // HEAD_DIM == \
        lax.broadcasted_iota(jnp.int32, (LANES, LANES), 1)
    kn2 = jnp.max(_bdot(ksq.astype(jnp.bfloat16), jnp.where(lane_sel, 1.0, 0.0).astype(jnp.bfloat16)),
                  axis=0, keepdims=True)
    bound = []
    for hh in range(2):
        qn2 = jnp.max(jnp.sum(qsq[hh * HEAD_DIM:(hh + 1) * HEAD_DIM, :], axis=0, keepdims=True),
                      axis=1, keepdims=True)
        bound.append((FOX_NORM_MARGIN * 2.0 * jnp.sqrt(qn2 * kn2[:, hh:hh + 1]))[0, 0])
    head0 = (pl.program_id(0) * (FOX_HEADS // 2) + pl.program_id(1)) * 2
    n_steps = jnp.int32(0)
    for i in range(nq):
        count, prefix = jnp.int32(0), jnp.bool_(True)
        for j in range(per_q * i):
            for hh in range(2):
                drop = cend_ref[head0 + hh, j] - cend_ref[head0 + hh, per_q * i - 1]
                prefix = prefix & (bound[hh] - drop < FOX_SKIP_BELOW)
            count = count + prefix.astype(jnp.int32)
        js_ref[i] = count
        n_steps = n_steps + (per_q * (i + 1) - count)

    def step(carry, s_cur, s_prv, masked):
        (i_a, j_a), (i_b, j_b, v_b, first), smax, m = carry
        first_a = j_a == js_ref[jnp.minimum(i_a, nq - 1)]

        @pl.when(first_a)
        def _():
            q0 = pl.multiple_of(jnp.minimum(i_a, nq - 1) * tq, tq)
            qt = qt_ref[:, pl.ds(q0, tq)].astype(jnp.float32)
            q_lo = jnp.where(row < HEAD_DIM, qt, 0.0).astype(jnp.bfloat16)
            q_hi = jnp.where(row < HEAD_DIM, 0.0, qt).astype(jnp.bfloat16)
            w_buf[:, 0:tq] = jnp.concatenate([q_lo, pick[0]], axis=0)
            w_buf[:, tq:2 * tq] = jnp.concatenate([q_hi, pick[1]], axis=0)

        k0 = pl.multiple_of(jnp.where(i_a < nq, j_a, 0) * tk, tk)
        ka = jnp.concatenate([k_ref[pl.ds(k0, tk), :], ca_ref[pl.ds(k0, tk), :]], axis=1)
        if masked:
            bias_slot = jnp.where(i_a < nq, j_a - per_q * i_a, all_masked)
        vt = vt_ref[0, j_b]
        n_chunks = 2 * tq // FOX_CHUNK
        smax_new, m_new = [], []

        def chunk_cols(ch):
            hh, start = divmod(ch * FOX_CHUNK, tq)
            return slice(ch * FOX_CHUNK, (ch + 1) * FOX_CHUNK), hh, slice(start, start + FOX_CHUNK)

        def stage_a(ch):
            cols, _, hcols = chunk_cols(ch)
            s_new = _bdot(ka, w_buf[:, cols])
            if masked:
                s_new = s_new + bias_buf[bias_slot, :, hcols]
            s_cur[:, cols] = s_new
            smax_new.append(jnp.max(s_new, axis=0, keepdims=True))

        def stage_b(ch):
            cols, hh, hcols = chunk_cols(ch)
            m_prev = jnp.where(first, NEG_INF, m[ch])
            mn = jnp.maximum(m_prev, smax[ch])
            p = jnp.exp2(s_prv[:, cols] - mn).astype(jnp.bfloat16)
            lhs = jnp.concatenate([vt[hh * HEAD_DIM:(hh + 1) * HEAD_DIM, :], ones_rows], axis=0)
            acc_buf[hh, :, hcols] = jnp.exp2(m_prev - mn) * acc_buf[hh, :, hcols] + _bdot(lhs, p)
            m_new.append(mn)

        stage_a(0)
        for ch in range(n_chunks):
            if ch + 1 < n_chunks:
                stage_a(ch + 1)
            stage_b(ch)
        smax_new = tuple(smax_new)

        @pl.when(v_b & (j_b == per_q * (i_b + 1) - 1))
        def _():
            q0 = pl.multiple_of(i_b * tq, tq)
            out_t = jnp.concatenate([acc_buf[hh, 0:HEAD_DIM, :] / acc_buf[hh, HEAD_DIM:HEAD_DIM + 1, :]
                                     for hh in range(2)], axis=0)
            y = out_t.T * _silu(g_ref[pl.ds(q0, tq), :])
            o_ref[pl.ds(q0, tq), :] = y.astype(o_ref.dtype)

        last_a = j_a == per_q * (i_a + 1) - 1
        nxt_a = (jnp.where(last_a, i_a + 1, i_a),
                 jnp.where(last_a, js_ref[jnp.minimum(i_a + 1, nq - 1)], j_a + 1))
        return (nxt_a, (i_a, jnp.where(i_a < nq, j_a, 0), i_a < nq, first_a), smax_new, tuple(m_new))

    def either(carry, bufs):
        i_a, j_a = carry[0]
        needs_mask = (j_a >= per_q * i_a) | (i_a >= nq)
        return lax.cond(needs_mask, lambda c: step(c, *bufs, True), lambda c: step(c, *bufs, False), carry)

    def body(_, carry):
        carry = either(carry, (s_buf0, s_buf1))
        return either(carry, (s_buf1, s_buf0))

    zi = jnp.int32(0)
    row_vec = lambda v: tuple(jnp.full((1, FOX_CHUNK), v, jnp.float32) for _ in range(2 * tq // FOX_CHUNK))
    init = ((zi, zi), (zi, zi, False, False), row_vec(0.0), row_vec(0.0))
    lax.fori_loop(0, (n_steps + 2) // 2, body, init)


def _fox(cend_tiles, qbt, kb, caug, vbt, g, batch, seq):
    npair = FOX_HEADS // 2
    nkt = seq // FOX_TK
    gcol0 = SWA_W // LANES
    return pl.pallas_call(
        _fox_kernel,
        grid=(batch, npair),
        in_specs=[pl.BlockSpec(memory_space=pltpu.SMEM),
                  pl.BlockSpec((LANES, seq), lambda b, h: (h, b)),
                  pl.BlockSpec((seq, LANES), lambda b, h: (b, h)),
                  pl.BlockSpec((seq, LANES), lambda b, h: (b, h)),
                  pl.BlockSpec((1, nkt, LANES, FOX_TK), lambda b, h: (h, b, 0, 0)),
                  pl.BlockSpec((seq, LANES), lambda b, h: (b, gcol0 + h))],
        out_specs=pl.BlockSpec((seq, LANES), lambda b, h: (b, h)),
        out_shape=jax.ShapeDtypeStruct((batch * seq, FOX_W), jnp.bfloat16),
        scratch_shapes=[pltpu.VMEM((2 * LANES, 2 * FOX_TQ), jnp.bfloat16),
                        pltpu.VMEM((FOX_TK, 2 * FOX_TQ), jnp.float32),
                        pltpu.VMEM((FOX_TK, 2 * FOX_TQ), jnp.float32),
                        pltpu.VMEM((2, HEAD_DIM + 16, FOX_TQ), jnp.float32),
                        pltpu.VMEM((FOX_TQ // FOX_TK + 1, FOX_TK, FOX_TQ), jnp.float32),
                        pltpu.SMEM((seq // FOX_TQ,), jnp.int32)],
        compiler_params=pltpu.CompilerParams(dimension_semantics=("arbitrary", "arbitrary"),
                                             vmem_limit_bytes=_VMEM_LIMIT),
        name="fox",
    )(cend_tiles, qbt, kb, caug, vbt, g)


def _split3(x):
    hi = x.astype(jnp.bfloat16)
    r1 = x - hi.astype(jnp.float32)
    mid = r1.astype(jnp.bfloat16)
    lo = (r1 - mid.astype(jnp.float32)).astype(jnp.bfloat16)
    return hi, mid, lo


def _ssd_kernel(xbc_ref, z_ref, fdt_ref, cw_ref, cb_ref, bias_ref, alog_ref, dskip_ref, nw_ref,
                y_ref, ca_ref, cend_ref, state_ref, tail_ref, ccarry_ref):
    ci = pl.program_id(1)
    L = BLOCK

    @pl.when(ci == 0)
    def _():
        state_ref[...] = jnp.zeros_like(state_ref)
        tail_ref[...] = jnp.zeros_like(tail_ref)
        ccarry_ref[...] = jnp.zeros_like(ccarry_ref)

    u = xbc_ref[...]
    tail_ref[8:8 + L, :] = u
    conv = cb_ref[...] + cw_ref[SSM_CONV - 1:SSM_CONV, :] * u
    for k in range(1, SSM_CONV):
        conv = conv + cw_ref[SSM_CONV - 1 - k:SSM_CONV - k, :] * tail_ref[8 - k:8 - k + L, :]
    tail_ref[0:8, :] = u[L - 8:L]
    act = _silu(conv)
    xs = act[:, :SSM_W]
    bm_f = act[:, SSM_W:SSM_W + SSM_BC_W]
    bm = bm_f.astype(jnp.bfloat16)
    cm = act[:, SSM_W + SSM_BC_W:].astype(jnp.bfloat16)

    lane = lax.broadcasted_iota(jnp.int32, (L, LANES), 1)
    vals = fdt_ref[...] + bias_ref[...]
    is_f = lane < DT_LANE0
    is_dt = (lane >= DT_LANE0) & (lane < DT_LANE0 + SSM_HEADS)
    sp = _softplus(jnp.where(is_f, -vals, vals))
    a_row = jnp.where(is_dt[0:1], -jnp.exp(alog_ref[...]), 0.0)
    dt = jnp.where(is_dt, sp, 0.0)
    scan_in = jnp.where(is_f, -sp, dt * a_row) * LOG2E

    ri = lax.broadcasted_iota(jnp.int32, (L, L), 0)
    cj = lax.broadcasted_iota(jnp.int32, (L, L), 1)
    causal = cj <= ri
    tri = jnp.where(causal, 1.0, 0.0).astype(jnp.bfloat16)
    hi, mid, lo = _split3(scan_in)
    cs = _bdot(tri, hi) + _bdot(tri, mid) + _bdot(tri, lo)
    cs_t = cs.T

    c_full = cs + ccarry_ref[...]
    ccarry_ref[...] = jnp.where(is_f[0:1], c_full[L - 1:L, :], 0.0)
    cend_ref[0] = c_full[L - 1:L, :]
    lane64 = lane % HEAD_DIM
    ca_cols = []
    for hp in range(FOX_HEADS // 2):
        negc = -jnp.where(lane < HEAD_DIM, c_full[:, 2 * hp:2 * hp + 1], c_full[:, 2 * hp + 1:2 * hp + 2])
        hi, mid, lo = (term.astype(jnp.float32) for term in _split3(negc))
        terms = jnp.where(lane64 == 0, hi, jnp.where(lane64 == 1, mid, jnp.where(lane64 == 2, lo, 0.0)))
        ca_cols.append(terms.astype(jnp.bfloat16))
    ca_ref[...] = jnp.concatenate(ca_cols, axis=1)

    low_l = lane < HEAD_DIM
    hpg = SSM_HEADS // 2
    npairs = SSM_HEADS // 2
    grp_of = [(2 * pair) // hpg for pair in range(npairs)]
    bm_g = [bm[:, g * SSM_STATE:(g + 1) * SSM_STATE] for g in range(2)]
    cm_g = [cm[:, g * SSM_STATE:(g + 1) * SSM_STATE] for g in range(2)]
    bm_t = [bm_f[:, g * SSM_STATE:(g + 1) * SSM_STATE].T.astype(jnp.bfloat16) for g in range(2)]
    cb = [_bdot_nt(cm_g[g], bm_g[g]) for g in range(2)]
    ys = []
    for pair in range(npairs):
        g = grp_of[pair]
        ha, hb = DT_LANE0 + 2 * pair, DT_LANE0 + 2 * pair + 1
        col = (cs[:, ha:ha + 1], cs[:, hb:hb + 1])
        acs_row = (cs_t[ha:ha + 1, :], cs_t[hb:hb + 1, :])
        xdt = xs[:, pair * LANES:(pair + 1) * LANES] * jnp.where(low_l, dt[:, ha:ha + 1], dt[:, hb:hb + 1])
        xdt_b = xdt.astype(jnp.bfloat16)
        prev = state_ref[pair]
        y_off = _bdot(cm_g[g], prev.astype(jnp.bfloat16))
        last = (col[0][L - 1:L, :], col[1][L - 1:L, :])
        dst = jnp.exp2(jnp.where(low_l, last[0] - col[0], last[1] - col[1]))
        st_new = _bdot(bm_t[g], (xdt * dst).astype(jnp.bfloat16))
        state_ref[pair] = prev * jnp.exp2(jnp.where(low_l, last[0], last[1])) + st_new
        y_diag = []
        for hh in range(2):
            decay = jnp.exp2(jnp.where(causal, col[hh] - acs_row[hh], NEG_INF))
            gmat = (cb[g] * decay).astype(jnp.bfloat16)
            y_diag.append(_bdot(gmat, xdt_b))
        ys.append(jnp.where(low_l, y_diag[0], y_diag[1]) + y_off * jnp.exp2(jnp.where(low_l, col[0], col[1])))
    y = jnp.concatenate(ys, axis=1) + dskip_ref[...] * xs
    y_ref[...] = _rms(y * _silu(z_ref[...]), nw_ref[...]).astype(y_ref.dtype)


def _ssd(xbc, z, fdt, conv_w, conv_b, b_forget, dt_bias, a_log, d_skip, norm_w, batch, seq):
    nc = seq // BLOCK
    row = lambda b, c: (b * nc + c, 0)
    const = lambda b, c: (0, 0)
    zpad = jnp.zeros((LANES - FOX_HEADS - SSM_HEADS,), jnp.float32)
    bias_row = jnp.concatenate([b_forget, dt_bias, zpad]).reshape(1, LANES)
    alog_row = jnp.concatenate([jnp.zeros((FOX_HEADS,), jnp.float32), a_log, zpad]).reshape(1, LANES)
    dskip_row = jnp.repeat(d_skip, HEAD_DIM).reshape(1, SSM_W)
    return pl.pallas_call(
        _ssd_kernel,
        grid=(batch, nc),
        in_specs=[pl.BlockSpec((BLOCK, SSM_CONV_W), row),
                  pl.BlockSpec((BLOCK, SSM_W), row),
                  pl.BlockSpec((BLOCK, LANES), row),
                  pl.BlockSpec((SSM_CONV, SSM_CONV_W), const),
                  pl.BlockSpec((1, SSM_CONV_W), const),
                  pl.BlockSpec((1, LANES), const),
                  pl.BlockSpec((1, LANES), const),
                  pl.BlockSpec((1, SSM_W), const),
                  pl.BlockSpec((1, SSM_W), const)],
        out_specs=[pl.BlockSpec((BLOCK, SSM_W), row),
                   pl.BlockSpec((BLOCK, FOX_W), row),
                   pl.BlockSpec((1, 1, LANES), lambda b, c: (b * nc + c, 0, 0))],
        out_shape=[jax.ShapeDtypeStruct((batch * seq, SSM_W), jnp.bfloat16),
                   jax.ShapeDtypeStruct((batch * seq, FOX_W), jnp.bfloat16),
                   jax.ShapeDtypeStruct((batch * nc, 1, LANES), jnp.float32)],
        scratch_shapes=[pltpu.VMEM((SSM_HEADS // 2, 2 * HEAD_DIM, SSM_STATE), jnp.float32),
                        pltpu.VMEM((8 + BLOCK, SSM_CONV_W), jnp.float32),
                        pltpu.VMEM((1, LANES), jnp.float32)],
        compiler_params=pltpu.CompilerParams(dimension_semantics=("arbitrary", "arbitrary")),
        name="ssd",
    )(xbc, z, fdt, conv_w, conv_b.reshape(1, SSM_CONV_W), bias_row, alog_row, dskip_row,
      norm_w.reshape(1, SSM_W))


def _memkv_kernel(mem_ref, nw_ref, wk_ref, wv_ref, k_ref, v_ref):
    mn = _rms(mem_ref[...], nw_ref[...]).astype(jnp.bfloat16)
    k_ref[...] = _bdot(mn, wk_ref[...]).astype(k_ref.dtype)
    v_ref[...] = _bdot(mn, wv_ref[...]).astype(v_ref.dtype)


def _memkv(mem2, norm_w, wk, wv, mem_tokens):
    t = mem2.shape[0]
    row = lambda b: (b, 0)
    const = lambda b: (0, 0)
    return pl.pallas_call(
        _memkv_kernel,
        grid=(t // mem_tokens,),
        in_specs=[pl.BlockSpec((mem_tokens, D_MODEL), row),
                  pl.BlockSpec((1, D_MODEL), const),
                  pl.BlockSpec((D_MODEL, D_MODEL), const),
                  pl.BlockSpec((D_MODEL, D_MODEL), const)],
        out_specs=[pl.BlockSpec((mem_tokens, D_MODEL), row)] * 2,
        out_shape=[jax.ShapeDtypeStruct((t, D_MODEL), jnp.bfloat16)] * 2,
        compiler_params=pltpu.CompilerParams(dimension_semantics=("arbitrary",)),
        name="memkv",
    )(mem2, norm_w.reshape(1, D_MODEL), wk.astype(jnp.bfloat16), wv.astype(jnp.bfloat16))


def _out_kernel(x_ref, ya_ref, yb_ref, yc_ref, wo_ref, nq_ref, wq_ref, k_ref, v_ref, wmo_ref, fn_ref,
                o_ref, *, final_norm):
    tm = x_ref.shape[0]
    groups = [slice(r, r + OUT_ROWS) for r in range(0, tm, OUT_ROWS)]
    head_cols = [slice(h * MEM_HEAD_DIM, (h + 1) * MEM_HEAD_DIM) for h in range(MEM_HEADS)]
    x1 = [x_ref[g, :]
          + _bdot(ya_ref[g, :], wo_ref[0:SWA_W, :])
          + _bdot(yb_ref[g, :], wo_ref[SWA_W:SWA_W + FOX_W, :])
          + _bdot(yc_ref[g, :], wo_ref[SWA_W + FOX_W:, :]) for g in groups]
    q = []
    for x1g in x1:
        hq = _rms(x1g, nq_ref[...]).astype(jnp.bfloat16)
        q.append((_bdot(hq, wq_ref[...]) * (MEM_HEAD_DIM ** -0.5 * LOG2E)).astype(jnp.bfloat16))
    scores = [[_bdot_nt(qg[:, sl], k_ref[:, sl]) for sl in head_cols] for qg in q]
    att = []
    for sg in scores:
        heads = []
        for s, sl in zip(sg, head_cols):
            p = jnp.exp2(s - jnp.max(s, axis=-1, keepdims=True))
            probs = (p / jnp.sum(p, axis=-1, keepdims=True)).astype(jnp.bfloat16)
            heads.append(_bdot(probs, v_ref[:, sl]).astype(jnp.bfloat16))
        att.append(jnp.concatenate(heads, axis=1))
    for g, x1g, attg in zip(groups, x1, att):
        x2 = x1g + _bdot(attg, wmo_ref[...])
        if final_norm:
            x2 = _rms(x2, fn_ref[...])
        o_ref[g, :] = x2


def _out_block(x2, ya, yb, yc, w_out, norm_xq_w, w_mq, kmem, vmem, w_mo, final_w, seq, mem_tokens, tm,
               final_norm):
    t = x2.shape[0]
    nblk_s = seq // tm
    row = lambda i: (i, 0)
    const = lambda i: (0, 0)
    memrow = lambda i: (i // nblk_s, 0)
    bf = jnp.bfloat16
    return pl.pallas_call(
        functools.partial(_out_kernel, final_norm=final_norm),
        grid=(t // tm,),
        in_specs=[pl.BlockSpec((tm, D_MODEL), row),
                  pl.BlockSpec((tm, SWA_W), row),
                  pl.BlockSpec((tm, FOX_W), row),
                  pl.BlockSpec((tm, SSM_W), row),
                  pl.BlockSpec((SWA_W + FOX_W + SSM_W, D_MODEL), const, pipeline_mode=pl.Buffered(1)),
                  pl.BlockSpec((1, D_MODEL), const),
                  pl.BlockSpec((D_MODEL, D_MODEL), const, pipeline_mode=pl.Buffered(1)),
                  pl.BlockSpec((mem_tokens, D_MODEL), memrow),
                  pl.BlockSpec((mem_tokens, D_MODEL), memrow),
                  pl.BlockSpec((D_MODEL, D_MODEL), const, pipeline_mode=pl.Buffered(1)),
                  pl.BlockSpec((1, D_MODEL), const)],
        out_specs=pl.BlockSpec((tm, D_MODEL), row),
        out_shape=jax.ShapeDtypeStruct((t, D_MODEL), jnp.float32),
        compiler_params=pltpu.CompilerParams(dimension_semantics=("arbitrary",),
                                             vmem_limit_bytes=_VMEM_LIMIT),
        name="outproj_mem",
    )(x2, ya, yb, yc, w_out.astype(bf), norm_xq_w.reshape(1, D_MODEL), w_mq.astype(bf), kmem, vmem,
      w_mo.astype(bf), final_w.reshape(1, D_MODEL))


def _row_tile(seq):
    return min(512, seq)


def _out_row_tile(seq):
    return min(1024, seq)


def kernel(x, mem, norm_mix_w, w_in, b_forget, swa_sinks, conv_w, conv_b, dt_bias, a_log, d_skip, ssm_norm_w, w_out, norm_xq_w, norm_mem_w, w_mq, w_mk, w_mv, w_mo, final_norm_w):
    batch, seq, _ = x.shape
    mem_tokens = mem.shape[1]
    depth = w_in.shape[0]
    assert seq % BLOCK == 0
    tm = _row_tile(seq)
    assert seq % tm == 0 and seq % FOX_TQ == 0 and seq % SWA_TQ == 0
    rope = _rope_tables(seq)
    x2 = x.reshape(batch * seq, D_MODEL)
    mem2 = mem.reshape(batch * mem_tokens, D_MODEL)
    for l in range(depth):
        fox_order = jnp.argsort(b_forget[l])
        w_r, w_t = _arrange_w_in(w_in[l], fox_order)
        ka, g, kb, z, xbc, fdt, qbt, vbt, qat, vat = _inproj(x2, norm_mix_w[l], w_r, w_t, rope, seq, tm)
        yc, caug, cend = _ssd(xbc, z, fdt, conv_w[l], conv_b[l], b_forget[l][fox_order], dt_bias[l], a_log[l],
                             d_skip[l], ssm_norm_w[l], batch, seq)
        per_tile = FOX_TK // BLOCK
        cend_tiles = cend.reshape(batch, seq // BLOCK, LANES)[:, per_tile - 1::per_tile, :FOX_HEADS]
        cend_tiles = cend_tiles.transpose(0, 2, 1).reshape(batch * FOX_HEADS, seq // FOX_TK)
        ya = _swa(qat, ka, vat, g, swa_sinks[l], batch, seq)
        yb = _fox(cend_tiles, qbt, kb, caug, vbt, g, batch, seq)
        kmem, vmem = _memkv(mem2, norm_mem_w[l], w_mk[l], w_mv[l], mem_tokens)
        o_b = SWA_W
        w_o = w_out[l].at[o_b:o_b + FOX_W].set(
            w_out[l][o_b:o_b + FOX_W].reshape(FOX_HEADS, HEAD_DIM, D_MODEL)[fox_order].reshape(FOX_W, D_MODEL))
        x2 = _out_block(x2, ya, yb, yc, w_o, norm_xq_w[l], w_mq[l], kmem, vmem, w_mo[l],
                        final_norm_w, seq, mem_tokens, _out_row_tile(seq), final_norm=(l == depth - 1))
    return x2.reshape(batch, seq, D_MODEL)
```

```python
import functools

import jax
import jax.numpy as jnp
from jax import lax
from jax.experimental import pallas as pl
from jax.experimental.pallas import tpu as pltpu

D_MODEL = 1024
HEAD_DIM = 64
BLOCK = 128
SWA_HEADS = 8
SWA_KV_HEADS = 2
FOX_HEADS = 8
SSM_HEADS = 16
SSM_STATE = 128
SSM_CONV = 4
MEM_HEADS = 4
MEM_HEAD_DIM = 256
ROPE_THETA = 10000.0
EPS = 1e-6
NEG_INF = -1e30

SWA_W = SWA_HEADS * HEAD_DIM
SWA_KV_W = SWA_KV_HEADS * HEAD_DIM
FOX_W = FOX_HEADS * HEAD_DIM
SSM_W = SSM_HEADS * HEAD_DIM
SSM_BC_W = 2 * SSM_STATE
SSM_CONV_W = SSM_W + 2 * SSM_BC_W
LANES = 128
LOG2E = 1.4426950408889634
SWA_TQ = 512
FOX_TQ = 1024
FOX_CHUNK = 256
FOX_SKIP_BELOW = -170.0
FOX_NORM_MARGIN = 1.02
FOX_TK = 512
OUT_ROWS = 256
SSD_ROWS = 512
DT_LANE0 = FOX_HEADS

_O_QA = 0
_O_KA = _O_QA + SWA_W
_O_VA = _O_KA + SWA_KV_W
_O_GA = _O_VA + SWA_KV_W
_O_QB = _O_GA + SWA_W
_O_KB = _O_QB + FOX_W
_O_VB = _O_KB + FOX_W
_O_FB = _O_VB + FOX_W
_O_GB = _O_FB + FOX_HEADS
_O_ZC = _O_GB + FOX_W
_O_XBC = _O_ZC + SSM_W
_O_DT = _O_XBC + SSM_CONV_W
_IN_W = _O_DT + SSM_HEADS

_SEG_W = (2 * SWA_KV_W, 2 * SWA_W, FOX_W, SSM_W, SSM_CONV_W, LANES)
_SEG_OFF = tuple(sum(_SEG_W[:i]) for i in range(len(_SEG_W)))
_PROJ_W = sum(_SEG_W)

_VMEM_LIMIT = 56 * 1024 * 1024


def _bdot(a, b):
    return jnp.dot(a, b, preferred_element_type=jnp.float32)


def _bdot_nt(a, b):
    return lax.dot_general(a, b, (((1,), (1,)), ((), ())), preferred_element_type=jnp.float32)


def _silu(x):
    h = 0.5 * x
    return h + h * jnp.tanh(h)


def _softplus(x):
    return jnp.maximum(x, 0.0) + jnp.log(1.0 + jnp.exp(-jnp.abs(x)))


def _rms(x, w):
    return x * lax.rsqrt(jnp.mean(x * x, axis=-1, keepdims=True) + EPS) * w


def _rope(x, cos, sin_signed):
    width = x.shape[1]
    reps = width // LANES
    lane = lax.broadcasted_iota(jnp.int32, x.shape, 1)
    first_half = (lane % HEAD_DIM) < (HEAD_DIM // 2)
    swapped = jnp.where(first_half,
                        pltpu.roll(x, width - HEAD_DIM // 2, 1),
                        pltpu.roll(x, HEAD_DIM // 2, 1))
    cos_t = jnp.concatenate([cos] * reps, axis=1)
    sin_t = jnp.concatenate([sin_signed] * reps, axis=1)
    return x * cos_t + swapped * sin_t


def _rope_t(x, cos, sin_signed):
    rows = x.shape[0]
    reps = rows // LANES
    r = lax.broadcasted_iota(jnp.int32, x.shape, 0)
    first_half = (r % HEAD_DIM) < (HEAD_DIM // 2)
    swapped = jnp.where(first_half,
                        pltpu.roll(x, rows - HEAD_DIM // 2, 0),
                        pltpu.roll(x, HEAD_DIM // 2, 0))
    cos_t = jnp.concatenate([cos] * reps, axis=0)
    sin_t = jnp.concatenate([sin_signed] * reps, axis=0)
    return x * cos_t + swapped * sin_t


def _inproj_kernel(x_ref, nw_ref, w_ref, wt_ref, cos_ref, sin_ref, cost_ref, sint_ref,
                   ka_ref, g_ref, kb_ref, z_ref, xbc_ref, fdt_ref, qbt_ref, vbt_ref, qat_ref, vat_ref):
    h = _rms(x_ref[...], nw_ref[...]).astype(jnp.bfloat16)

    def seg(i):
        return _bdot(h, w_ref[:, _SEG_OFF[i]:_SEG_OFF[i] + _SEG_W[i]])

    scale = HEAD_DIM ** -0.5 * LOG2E
    ka_ref[...] = _rope(seg(0), cos_ref[...], sin_ref[...]).astype(ka_ref.dtype)
    g_ref[...] = seg(1)
    kb_ref[...] = seg(2).astype(kb_ref.dtype)
    z_ref[...] = seg(3)
    xbc_ref[...] = seg(4)
    fdt_ref[...] = seg(5)
    o_vb, o_qa, o_va = FOX_W, 2 * FOX_W, 2 * FOX_W + SWA_W
    qbt_ref[...] = (_bdot_nt(wt_ref[0:o_vb, :], h) * scale).astype(qbt_ref.dtype)
    qat = _rope_t(_bdot_nt(wt_ref[o_qa:o_va, :], h), cost_ref[...], sint_ref[...])
    qat_ref[...] = (qat * scale).astype(qat_ref.dtype)
    vat_ref[...] = _bdot_nt(wt_ref[o_va:, :], h).astype(vat_ref.dtype)
    vbt = _bdot_nt(wt_ref[o_vb:o_qa, :], h).astype(vbt_ref.dtype)
    for hp in range(FOX_W // LANES):
        for c in range(vbt.shape[1] // FOX_TK):
            vbt_ref[hp, c] = vbt[hp * LANES:(hp + 1) * LANES, c * FOX_TK:(c + 1) * FOX_TK]


def _arrange_w_in(w_in, fox_order):
    def cols(o, n):
        return w_in[:, o:o + n]

    def fox_cols(o):
        return cols(o, FOX_W).reshape(-1, FOX_HEADS, HEAD_DIM)[:, fox_order].reshape(-1, FOX_W)
    k0, k1 = cols(_O_KA, HEAD_DIM), cols(_O_KA + HEAD_DIM, HEAD_DIM)
    pad = jnp.zeros((w_in.shape[0], LANES - FOX_HEADS - SSM_HEADS), w_in.dtype)
    parts = [k0, k1, k1, k0,
             cols(_O_GA, SWA_W), fox_cols(_O_GB),
             fox_cols(_O_KB),
             cols(_O_ZC, SSM_W), cols(_O_XBC, SSM_CONV_W),
             cols(_O_FB, FOX_HEADS)[:, fox_order], cols(_O_DT, SSM_HEADS), pad]
    w_r = jnp.concatenate(parts, axis=1).astype(jnp.bfloat16)
    w_t = jnp.concatenate([fox_cols(_O_QB), fox_cols(_O_VB), cols(_O_QA, SWA_W), cols(_O_VA, SWA_KV_W)],
                          axis=1).T.astype(jnp.bfloat16)
    return w_r, w_t


def _inproj(x2, norm_w, w_r, w_t, rope, seq, tm):
    t = x2.shape[0]
    nblk_s = seq // tm
    row = lambda i: (i, 0)
    const = lambda i: (0, 0)
    pos = lambda i: (i % nblk_s, 0)
    pos_t = lambda i: (0, i % nblk_s)
    col = lambda i: (0, i)
    bf, f32 = jnp.bfloat16, jnp.float32
    outs = [(2 * SWA_KV_W, bf), (2 * SWA_W, f32), (FOX_W, bf), (SSM_W, f32), (SSM_CONV_W, f32), (LANES, f32)]
    npair = FOX_W // LANES
    return pl.pallas_call(
        _inproj_kernel,
        grid=(t // tm,),
        in_specs=[pl.BlockSpec((tm, D_MODEL), row),
                  pl.BlockSpec((1, D_MODEL), const),
                  pl.BlockSpec((D_MODEL, _PROJ_W), const, pipeline_mode=pl.Buffered(1)),
                  pl.BlockSpec(w_t.shape, const, pipeline_mode=pl.Buffered(1)),
                  pl.BlockSpec((tm, LANES), pos),
                  pl.BlockSpec((tm, LANES), pos),
                  pl.BlockSpec((LANES, tm), pos_t),
                  pl.BlockSpec((LANES, tm), pos_t)],
        out_specs=[pl.BlockSpec((tm, w), row) for w, _ in outs]
        + [pl.BlockSpec((FOX_W, tm), col),
           pl.BlockSpec((npair, tm // FOX_TK, LANES, FOX_TK), lambda i: (0, i, 0, 0)),
           pl.BlockSpec((SWA_W, tm), col),
           pl.BlockSpec((SWA_KV_W, tm), col)],
        out_shape=[jax.ShapeDtypeStruct((t, w), d) for w, d in outs]
        + [jax.ShapeDtypeStruct((FOX_W, t), bf),
           jax.ShapeDtypeStruct((npair, t // FOX_TK, LANES, FOX_TK), bf),
           jax.ShapeDtypeStruct((SWA_W, t), bf),
           jax.ShapeDtypeStruct((SWA_KV_W, t), bf)],
        compiler_params=pltpu.CompilerParams(dimension_semantics=("arbitrary",),
                                             vmem_limit_bytes=_VMEM_LIMIT),
        name="inproj",
    )(x2, norm_w.reshape(1, D_MODEL), w_r, w_t, *rope)


def _rope_tables(seq):
    pos = jnp.arange(seq, dtype=jnp.float32)
    inv = 1.0 / (ROPE_THETA ** (jnp.arange(0, HEAD_DIM, 2, dtype=jnp.float32) / HEAD_DIM))
    ang = pos[:, None] * inv[None, :]
    cos, sin = jnp.cos(ang), jnp.sin(ang)
    cos_t = jnp.concatenate([cos, cos, cos, cos], axis=1)
    sin_t = jnp.concatenate([-sin, sin, -sin, sin], axis=1)
    return cos_t, sin_t, cos_t.T, sin_t.T


def _swa_kernel(sink_ref, qt_ref, kc_ref, kp_ref, vc_ref, vp_ref, g_ref, o_ref):
    n = pl.program_id(1)
    nsub = SWA_TQ // BLOCK
    key = lax.broadcasted_iota(jnp.int32, (2 * BLOCK, BLOCK), 0)
    qry = lax.broadcasted_iota(jnp.int32, (2 * BLOCK, BLOCK), 1)
    band = jnp.where((key > qry) & (key <= qry + BLOCK), 0.0, NEG_INF)
    band_first = jnp.where(key < BLOCK, NEG_INF, band)
    row = lax.broadcasted_iota(jnp.int32, (LANES, BLOCK), 0)
    ones_rows = jnp.ones((16, 2 * BLOCK), jnp.bfloat16)

    def scores(u):
        if u == 0:
            kcat = jnp.concatenate([kp_ref[...], kc_ref[0:BLOCK, :]], axis=0)
            vcat = jnp.concatenate([vp_ref[...], vc_ref[:, 0:BLOCK]], axis=1)
            bias = jnp.where(n > 0, band, band_first)
        else:
            kcat = kc_ref[(u - 1) * BLOCK:(u + 1) * BLOCK, :]
            vcat = vc_ref[:, (u - 1) * BLOCK:(u + 1) * BLOCK]
            bias = band
        tiles = []
        for c in range(SWA_HEADS // 2):
            qt = qt_ref[c * LANES:(c + 1) * LANES, u * BLOCK:(u + 1) * BLOCK].astype(jnp.float32)
            for half in range(2):
                kv = (2 * c + half) // (SWA_HEADS // SWA_KV_HEADS)
                sel = 0 if kv == half else 1
                in_half = (row < HEAD_DIM) if half == 0 else (row >= HEAD_DIM)
                w = jnp.where(in_half, qt, 0.0).astype(jnp.bfloat16)
                tiles.append(_bdot(kcat[:, sel * LANES:(sel + 1) * LANES], w) + bias)
        return tiles, vcat

    def finish(u, tiles, vcat):
        probs, sink_terms = [], []
        for head, s in enumerate(tiles):
            sink = sink_ref[0, head] * LOG2E
            m = jnp.maximum(jnp.max(s, axis=0, keepdims=True), sink)
            probs.append(jnp.exp2(s - m).astype(jnp.bfloat16))
            sink_terms.append(jnp.exp2(sink - m))
        outs = []
        for head, p in enumerate(probs):
            kv = head // (SWA_HEADS // SWA_KV_HEADS)
            lhs = jnp.concatenate([vcat[kv * HEAD_DIM:(kv + 1) * HEAD_DIM, :], ones_rows], axis=0)
            pv = _bdot(lhs, p)
            outs.append(pv[0:HEAD_DIM, :] / (pv[HEAD_DIM:HEAD_DIM + 1, :] + sink_terms[head]))
        y = jnp.concatenate(outs, axis=0).T
        rows = slice(u * BLOCK, (u + 1) * BLOCK)
        o_ref[rows, :] = (y * _silu(g_ref[rows, :])).astype(o_ref.dtype)

    pending = scores(0)
    for u in range(nsub):
        nxt = scores(u + 1) if u + 1 < nsub else None
        finish(u, *pending)
        pending = nxt


def _swa(qat, ka, vat, g, sinks, batch, seq):
    n = seq // SWA_TQ
    nsub = SWA_TQ // BLOCK
    cur = lambda b, i: (b * n + i, 0)
    cur_t = lambda b, i: (0, b * n + i)
    prev = lambda b, i: ((b * n + i) * nsub - jnp.minimum(i, 1), 0)
    prev_t = lambda b, i: (0, (b * n + i) * nsub - jnp.minimum(i, 1))
    return pl.pallas_call(
        _swa_kernel,
        grid=(batch, n),
        in_specs=[pl.BlockSpec(memory_space=pltpu.SMEM),
                  pl.BlockSpec((SWA_W, SWA_TQ), cur_t),
                  pl.BlockSpec((SWA_TQ, 2 * SWA_KV_W), cur),
                  pl.BlockSpec((BLOCK, 2 * SWA_KV_W), prev),
                  pl.BlockSpec((SWA_KV_W, SWA_TQ), cur_t),
                  pl.BlockSpec((SWA_KV_W, BLOCK), prev_t),
                  pl.BlockSpec((SWA_TQ, SWA_W), cur)],
        out_specs=pl.BlockSpec((SWA_TQ, SWA_W), cur),
        out_shape=jax.ShapeDtypeStruct((batch * seq, SWA_W), jnp.bfloat16),
        compiler_params=pltpu.CompilerParams(dimension_semantics=("arbitrary", "arbitrary")),
        name="swa",
    )(sinks.reshape(1, SWA_HEADS).astype(jnp.float32), qat, ka, ka, vat, vat, g)


def _fox_kernel(cend_ref, qt_ref, k_ref, ca_ref, vt_ref, g_ref, o_ref, w_buf, s_buf0, s_buf1, acc_buf, bias_buf,
                js_ref):
    tq, tk = FOX_TQ, FOX_TK
    per_q = tq // tk
    nq = qt_ref.shape[1] // tq
    all_masked = per_q
    acc_rows = acc_buf.shape[1]

    krow = lax.broadcasted_iota(jnp.int32, (tk, tq), 0)
    qcol = lax.broadcasted_iota(jnp.int32, (tk, tq), 1)
    for d in range(per_q):
        bias_buf[d] = jnp.where(krow + d * tk <= qcol, 0.0, NEG_INF)
    bias_buf[all_masked] = jnp.full((tk, tq), NEG_INF, jnp.float32)
    for buf in (s_buf0, s_buf1, acc_buf):
        buf[...] = jnp.zeros_like(buf)

    row = lax.broadcasted_iota(jnp.int32, (LANES, tq), 0)
    nsplit = 3
    pick = (jnp.where(row < nsplit, 1.0, 0.0).astype(jnp.bfloat16),
            jnp.where((row >= HEAD_DIM) & (row < HEAD_DIM + nsplit), 1.0, 0.0).astype(jnp.bfloat16))
    ones_rows = jnp.ones((acc_rows - HEAD_DIM, tk), jnp.bfloat16)

    qsq = jnp.square(qt_ref[...].astype(jnp.float32))
    ksq = jnp.square(k_ref[...].astype(jnp.float32))
    lane_sel = lax.broadcasted_iota(jnp.int32, (LANES, LANES), 0) // HEAD_DIM == \
        lax.broadcasted_iota(jnp.int32, (LANES, LANES), 1)
    kn2 = jnp.max(_bdot(ksq.astype(jnp.bfloat16), jnp.where(lane_sel, 1.0, 0.0).astype(jnp.bfloat16)),
                  axis=0, keepdims=True)
    bound = []
    for hh in range(2):
        qn2 = jnp.max(jnp.sum(qsq[hh * HEAD_DIM:(hh + 1) * HEAD_DIM, :], axis=0, keepdims=True),
                      axis=1, keepdims=True)
        bound.append((FOX_NORM_MARGIN * 2.0 * jnp.sqrt(qn2 * kn2[:, hh:hh + 1]))[0, 0])
    head0 = (pl.program_id(0) * (FOX_HEADS // 2) + pl.program_id(1)) * 2
    n_steps = jnp.int32(0)
    for i in range(nq):
        count, prefix = jnp.int32(0), jnp.bool_(True)
        for j in range(per_q * i):
            for hh in range(2):
                drop = cend_ref[head0 + hh, j] - cend_ref[head0 + hh, per_q * i - 1]
                prefix = prefix & (bound[hh] - drop < FOX_SKIP_BELOW)
            count = count + prefix.astype(jnp.int32)
        js_ref[i] = count
        n_steps = n_steps + (per_q * (i + 1) - count)

    def step(carry, s_cur, s_prv, masked):
        (i_a, j_a), (i_b, j_b, v_b, first), smax, m = carry
        first_a = j_a == js_ref[jnp.minimum(i_a, nq - 1)]

        @pl.when(first_a)
        def _():
            q0 = pl.multiple_of(jnp.minimum(i_a, nq - 1) * tq, tq)
            qt = qt_ref[:, pl.ds(q0, tq)].astype(jnp.float32)
            q_lo = jnp.where(row < HEAD_DIM, qt, 0.0).astype(jnp.bfloat16)
            q_hi = jnp.where(row < HEAD_DIM, 0.0, qt).astype(jnp.bfloat16)
            w_buf[:, 0:tq] = jnp.concatenate([q_lo, pick[0]], axis=0)
            w_buf[:, tq:2 * tq] = jnp.concatenate([q_hi, pick[1]], axis=0)

        k0 = pl.multiple_of(jnp.where(i_a < nq, j_a, 0) * tk, tk)
        ka = jnp.concatenate([k_ref[pl.ds(k0, tk), :], ca_ref[pl.ds(k0, tk), :]], axis=1)
        if masked:
            bias_slot = jnp.where(i_a < nq, j_a - per_q * i_a, all_masked)
        vt = vt_ref[0, j_b]
        n_chunks = 2 * tq // FOX_CHUNK
        smax_new, m_new = [], []

        def chunk_cols(ch):
            hh, start = divmod(ch * FOX_CHUNK, tq)
            return slice(ch * FOX_CHUNK, (ch + 1) * FOX_CHUNK), hh, slice(start, start + FOX_CHUNK)

        def stage_a(ch):
            cols, _, hcols = chunk_cols(ch)
            s_new = _bdot(ka, w_buf[:, cols])
            if masked:
                s_new = s_new + bias_buf[bias_slot, :, hcols]
            s_cur[:, cols] = s_new
            smax_new.append(jnp.max(s_new, axis=0, keepdims=True))

        def stage_b(ch):
            cols, hh, hcols = chunk_cols(ch)
            m_prev = jnp.where(first, NEG_INF, m[ch])
            mn = jnp.maximum(m_prev, smax[ch])
            p = jnp.exp2(s_prv[:, cols] - mn).astype(jnp.bfloat16)
            lhs = jnp.concatenate([vt[hh * HEAD_DIM:(hh + 1) * HEAD_DIM, :], ones_rows], axis=0)
            acc_buf[hh, :, hcols] = jnp.exp2(m_prev - mn) * acc_buf[hh, :, hcols] + _bdot(lhs, p)
            m_new.append(mn)

        stage_a(0)
        for ch in range(n_chunks):
            if ch + 1 < n_chunks:
                stage_a(ch + 1)
            stage_b(ch)
        smax_new = tuple(smax_new)

        @pl.when(v_b & (j_b == per_q * (i_b + 1) - 1))
        def _():
            q0 = pl.multiple_of(i_b * tq, tq)
            out_t = jnp.concatenate([acc_buf[hh, 0:HEAD_DIM, :] / acc_buf[hh, HEAD_DIM:HEAD_DIM + 1, :]
                                     for hh in range(2)], axis=0)
            y = out_t.T * _silu(g_ref[pl.ds(q0, tq), :])
            o_ref[pl.ds(q0, tq), :] = y.astype(o_ref.dtype)

        last_a = j_a == per_q * (i_a + 1) - 1
        nxt_a = (jnp.where(last_a, i_a + 1, i_a),
                 jnp.where(last_a, js_ref[jnp.minimum(i_a + 1, nq - 1)], j_a + 1))
        return (nxt_a, (i_a, jnp.where(i_a < nq, j_a, 0), i_a < nq, first_a), smax_new, tuple(m_new))

    def either(carry, bufs):
        i_a, j_a = carry[0]
        needs_mask = (j_a >= per_q * i_a) | (i_a >= nq)
        return lax.cond(needs_mask, lambda c: step(c, *bufs, True), lambda c: step(c, *bufs, False), carry)

    def body(_, carry):
        carry = either(carry, (s_buf0, s_buf1))
        return either(carry, (s_buf1, s_buf0))

    zi = jnp.int32(0)
    row_vec = lambda v: tuple(jnp.full((1, FOX_CHUNK), v, jnp.float32) for _ in range(2 * tq // FOX_CHUNK))
    init = ((zi, zi), (zi, zi, False, False), row_vec(0.0), row_vec(0.0))
    lax.fori_loop(0, (n_steps + 2) // 2, body, init)


def _fox(cend_tiles, qbt, kb, caug, vbt, g, batch, seq):
    npair = FOX_HEADS // 2
    nkt = seq // FOX_TK
    gcol0 = SWA_W // LANES
    return pl.pallas_call(
        _fox_kernel,
        grid=(batch, npair),
        in_specs=[pl.BlockSpec(memory_space=pltpu.SMEM),
                  pl.BlockSpec((LANES, seq), lambda b, h: (h, b)),
                  pl.BlockSpec((seq, LANES), lambda b, h: (b, h)),
                  pl.BlockSpec((seq, LANES), lambda b, h: (b, h)),
                  pl.BlockSpec((1, nkt, LANES, FOX_TK), lambda b, h: (h, b, 0, 0)),
                  pl.BlockSpec((seq, LANES), lambda b, h: (b, gcol0 + h))],
        out_specs=pl.BlockSpec((seq, LANES), lambda b, h: (b, h)),
        out_shape=jax.ShapeDtypeStruct((batch * seq, FOX_W), jnp.bfloat16),
        scratch_shapes=[pltpu.VMEM((2 * LANES, 2 * FOX_TQ), jnp.bfloat16),
                        pltpu.VMEM((FOX_TK, 2 * FOX_TQ), jnp.float32),
                        pltpu.VMEM((FOX_TK, 2 * FOX_TQ), jnp.float32),
                        pltpu.VMEM((2, HEAD_DIM + 16, FOX_TQ), jnp.float32),
                        pltpu.VMEM((FOX_TQ // FOX_TK + 1, FOX_TK, FOX_TQ), jnp.float32),
                        pltpu.SMEM((seq // FOX_TQ,), jnp.int32)],
        compiler_params=pltpu.CompilerParams(dimension_semantics=("arbitrary", "arbitrary"),
                                             vmem_limit_bytes=_VMEM_LIMIT),
        name="fox",
    )(cend_tiles, qbt, kb, caug, vbt, g)


def _split3(x):
    hi = x.astype(jnp.bfloat16)
    r1 = x - hi.astype(jnp.float32)
    mid = r1.astype(jnp.bfloat16)
    lo = (r1 - mid.astype(jnp.float32)).astype(jnp.bfloat16)
    return hi, mid, lo


def _ssd_kernel(xbc_ref, z_ref, fdt_ref, cw_ref, cb_ref, bias_ref, alog_ref, dskip_ref, nw_ref,
                y_ref, ca_ref, cend_ref, state_ref, tail_ref, ccarry_ref):
    @pl.when(pl.program_id(1) == 0)
    def _():
        state_ref[...] = jnp.zeros_like(state_ref)
        tail_ref[...] = jnp.zeros_like(tail_ref)
        ccarry_ref[...] = jnp.zeros_like(ccarry_ref)

    for sub in range(xbc_ref.shape[0] // BLOCK):
        _ssd_chunk(sub, xbc_ref, z_ref, fdt_ref, cw_ref, cb_ref, bias_ref, alog_ref, dskip_ref, nw_ref,
                   y_ref, ca_ref, cend_ref, state_ref, tail_ref, ccarry_ref)


def _ssd_chunk(sub, xbc_ref, z_ref, fdt_ref, cw_ref, cb_ref, bias_ref, alog_ref, dskip_ref, nw_ref,
               y_ref, ca_ref, cend_ref, state_ref, tail_ref, ccarry_ref):
    L = BLOCK
    rows = slice(sub * L, (sub + 1) * L)
    u = xbc_ref[rows, :]
    tail_ref[8:8 + L, :] = u
    conv = cb_ref[...] + cw_ref[SSM_CONV - 1:SSM_CONV, :] * u
    for k in range(1, SSM_CONV):
        conv = conv + cw_ref[SSM_CONV - 1 - k:SSM_CONV - k, :] * tail_ref[8 - k:8 - k + L, :]
    tail_ref[0:8, :] = u[L - 8:L]
    act = _silu(conv)
    xs = act[:, :SSM_W]
    bm_f = act[:, SSM_W:SSM_W + SSM_BC_W]
    bm = bm_f.astype(jnp.bfloat16)
    cm = act[:, SSM_W + SSM_BC_W:].astype(jnp.bfloat16)

    lane = lax.broadcasted_iota(jnp.int32, (L, LANES), 1)
    vals = fdt_ref[rows, :] + bias_ref[...]
    is_f = lane < DT_LANE0
    is_dt = (lane >= DT_LANE0) & (lane < DT_LANE0 + SSM_HEADS)
    sp = _softplus(jnp.where(is_f, -vals, vals))
    a_row = jnp.where(is_dt[0:1], -jnp.exp(alog_ref[...]), 0.0)
    dt = jnp.where(is_dt, sp, 0.0)
    scan_in = jnp.where(is_f, -sp, dt * a_row) * LOG2E

    ri = lax.broadcasted_iota(jnp.int32, (L, L), 0)
    cj = lax.broadcasted_iota(jnp.int32, (L, L), 1)
    causal = cj <= ri
    tri = jnp.where(causal, 1.0, 0.0).astype(jnp.bfloat16)
    hi, mid, lo = _split3(scan_in)
    cs = _bdot(tri, hi) + _bdot(tri, mid) + _bdot(tri, lo)
    cs_t = cs.T

    c_full = cs + ccarry_ref[...]
    ccarry_ref[...] = jnp.where(is_f[0:1], c_full[L - 1:L, :], 0.0)
    cend_ref[sub] = c_full[L - 1:L, :]
    lane64 = lane % HEAD_DIM
    ca_cols = []
    for hp in range(FOX_HEADS // 2):
        negc = -jnp.where(lane < HEAD_DIM, c_full[:, 2 * hp:2 * hp + 1], c_full[:, 2 * hp + 1:2 * hp + 2])
        hi, mid, lo = (term.astype(jnp.float32) for term in _split3(negc))
        terms = jnp.where(lane64 == 0, hi, jnp.where(lane64 == 1, mid, jnp.where(lane64 == 2, lo, 0.0)))
        ca_cols.append(terms.astype(jnp.bfloat16))
    ca_ref[rows, :] = jnp.concatenate(ca_cols, axis=1)

    low_l = lane < HEAD_DIM
    hpg = SSM_HEADS // 2
    npairs = SSM_HEADS // 2
    grp_of = [(2 * pair) // hpg for pair in range(npairs)]
    bm_g = [bm[:, g * SSM_STATE:(g + 1) * SSM_STATE] for g in range(2)]
    cm_g = [cm[:, g * SSM_STATE:(g + 1) * SSM_STATE] for g in range(2)]
    bm_t = [bm_f[:, g * SSM_STATE:(g + 1) * SSM_STATE].T.astype(jnp.bfloat16) for g in range(2)]
    cb = [_bdot_nt(cm_g[g], bm_g[g]) for g in range(2)]
    ys = []
    for pair in range(npairs):
        g = grp_of[pair]
        ha, hb = DT_LANE0 + 2 * pair, DT_LANE0 + 2 * pair + 1
        col = (cs[:, ha:ha + 1], cs[:, hb:hb + 1])
        acs_row = (cs_t[ha:ha + 1, :], cs_t[hb:hb + 1, :])
        xdt = xs[:, pair * LANES:(pair + 1) * LANES] * jnp.where(low_l, dt[:, ha:ha + 1], dt[:, hb:hb + 1])
        xdt_b = xdt.astype(jnp.bfloat16)
        prev = state_ref[pair]
        y_off = _bdot(cm_g[g], prev.astype(jnp.bfloat16))
        last = (col[0][L - 1:L, :], col[1][L - 1:L, :])
        dst = jnp.exp2(jnp.where(low_l, last[0] - col[0], last[1] - col[1]))
        st_new = _bdot(bm_t[g], (xdt * dst).astype(jnp.bfloat16))
        state_ref[pair] = prev * jnp.exp2(jnp.where(low_l, last[0], last[1])) + st_new
        y_diag = []
        for hh in range(2):
            decay = jnp.exp2(jnp.where(causal, col[hh] - acs_row[hh], NEG_INF))
            gmat = (cb[g] * decay).astype(jnp.bfloat16)
            y_diag.append(_bdot(gmat, xdt_b))
        ys.append(jnp.where(low_l, y_diag[0], y_diag[1]) + y_off * jnp.exp2(jnp.where(low_l, col[0], col[1])))
    y = jnp.concatenate(ys, axis=1) + dskip_ref[...] * xs
    y_ref[rows, :] = _rms(y * _silu(z_ref[rows, :]), nw_ref[...]).astype(y_ref.dtype)


def _ssd(xbc, z, fdt, conv_w, conv_b, b_forget, dt_bias, a_log, d_skip, norm_w, batch, seq):
    rows = min(SSD_ROWS, seq)
    per_step = rows // BLOCK
    nc = seq // rows
    row = lambda b, c: (b * nc + c, 0)
    const = lambda b, c: (0, 0)
    zpad = jnp.zeros((LANES - FOX_HEADS - SSM_HEADS,), jnp.float32)
    bias_row = jnp.concatenate([b_forget, dt_bias, zpad]).reshape(1, LANES)
    alog_row = jnp.concatenate([jnp.zeros((FOX_HEADS,), jnp.float32), a_log, zpad]).reshape(1, LANES)
    dskip_row = jnp.repeat(d_skip, HEAD_DIM).reshape(1, SSM_W)
    return pl.pallas_call(
        _ssd_kernel,
        grid=(batch, nc),
        in_specs=[pl.BlockSpec((rows, SSM_CONV_W), row),
                  pl.BlockSpec((rows, SSM_W), row),
                  pl.BlockSpec((rows, LANES), row),
                  pl.BlockSpec((SSM_CONV, SSM_CONV_W), const),
                  pl.BlockSpec((1, SSM_CONV_W), const),
                  pl.BlockSpec((1, LANES), const),
                  pl.BlockSpec((1, LANES), const),
                  pl.BlockSpec((1, SSM_W), const),
                  pl.BlockSpec((1, SSM_W), const)],
        out_specs=[pl.BlockSpec((rows, SSM_W), row),
                   pl.BlockSpec((rows, FOX_W), row),
                   pl.BlockSpec((per_step, 1, LANES), lambda b, c: (b * nc + c, 0, 0))],
        out_shape=[jax.ShapeDtypeStruct((batch * seq, SSM_W), jnp.bfloat16),
                   jax.ShapeDtypeStruct((batch * seq, FOX_W), jnp.bfloat16),
                   jax.ShapeDtypeStruct((batch * nc * per_step, 1, LANES), jnp.float32)],
        scratch_shapes=[pltpu.VMEM((SSM_HEADS // 2, 2 * HEAD_DIM, SSM_STATE), jnp.float32),
                        pltpu.VMEM((8 + BLOCK, SSM_CONV_W), jnp.float32),
                        pltpu.VMEM((1, LANES), jnp.float32)],
        compiler_params=pltpu.CompilerParams(dimension_semantics=("arbitrary", "arbitrary")),
        name="ssd",
    )(xbc, z, fdt, conv_w, conv_b.reshape(1, SSM_CONV_W), bias_row, alog_row, dskip_row,
      norm_w.reshape(1, SSM_W))


def _memkv_kernel(mem_ref, nw_ref, wk_ref, wv_ref, k_ref, v_ref):
    mn = _rms(mem_ref[...], nw_ref[...]).astype(jnp.bfloat16)
    k_ref[...] = _bdot(mn, wk_ref[...]).astype(k_ref.dtype)
    v_ref[...] = _bdot(mn, wv_ref[...]).astype(v_ref.dtype)


def _memkv(mem2, norm_w, wk, wv, mem_tokens):
    t = mem2.shape[0]
    row = lambda b: (b, 0)
    const = lambda b: (0, 0)
    return pl.pallas_call(
        _memkv_kernel,
        grid=(t // mem_tokens,),
        in_specs=[pl.BlockSpec((mem_tokens, D_MODEL), row),
                  pl.BlockSpec((1, D_MODEL), const),
                  pl.BlockSpec((D_MODEL, D_MODEL), const),
                  pl.BlockSpec((D_MODEL, D_MODEL), const)],
        out_specs=[pl.BlockSpec((mem_tokens, D_MODEL), row)] * 2,
        out_shape=[jax.ShapeDtypeStruct((t, D_MODEL), jnp.bfloat16)] * 2,
        compiler_params=pltpu.CompilerParams(dimension_semantics=("arbitrary",)),
        name="memkv",
    )(mem2, norm_w.reshape(1, D_MODEL), wk.astype(jnp.bfloat16), wv.astype(jnp.bfloat16))


def _out_kernel(x_ref, ya_ref, yb_ref, yc_ref, wo_ref, nq_ref, wq_ref, k_ref, v_ref, wmo_ref, fn_ref,
                o_ref, *, final_norm):
    tm = x_ref.shape[0]
    groups = [slice(r, r + OUT_ROWS) for r in range(0, tm, OUT_ROWS)]
    head_cols = [slice(h * MEM_HEAD_DIM, (h + 1) * MEM_HEAD_DIM) for h in range(MEM_HEADS)]
    x1 = [x_ref[g, :]
          + _bdot(ya_ref[g, :], wo_ref[0:SWA_W, :])
          + _bdot(yb_ref[g, :], wo_ref[SWA_W:SWA_W + FOX_W, :])
          + _bdot(yc_ref[g, :], wo_ref[SWA_W + FOX_W:, :]) for g in groups]
    q = []
    for x1g in x1:
        hq = _rms(x1g, nq_ref[...]).astype(jnp.bfloat16)
        q.append((_bdot(hq, wq_ref[...]) * (MEM_HEAD_DIM ** -0.5 * LOG2E)).astype(jnp.bfloat16))
    scores = [[_bdot_nt(qg[:, sl], k_ref[:, sl]) for sl in head_cols] for qg in q]
    att = []
    for sg in scores:
        heads = []
        for s, sl in zip(sg, head_cols):
            p = jnp.exp2(s - jnp.max(s, axis=-1, keepdims=True))
            probs = (p / jnp.sum(p, axis=-1, keepdims=True)).astype(jnp.bfloat16)
            heads.append(_bdot(probs, v_ref[:, sl]).astype(jnp.bfloat16))
        att.append(jnp.concatenate(heads, axis=1))
    for g, x1g, attg in zip(groups, x1, att):
        x2 = x1g + _bdot(attg, wmo_ref[...])
        if final_norm:
            x2 = _rms(x2, fn_ref[...])
        o_ref[g, :] = x2


def _out_block(x2, ya, yb, yc, w_out, norm_xq_w, w_mq, kmem, vmem, w_mo, final_w, seq, mem_tokens, tm,
               final_norm):
    t = x2.shape[0]
    nblk_s = seq // tm
    row = lambda i: (i, 0)
    const = lambda i: (0, 0)
    memrow = lambda i: (i // nblk_s, 0)
    bf = jnp.bfloat16
    return pl.pallas_call(
        functools.partial(_out_kernel, final_norm=final_norm),
        grid=(t // tm,),
        in_specs=[pl.BlockSpec((tm, D_MODEL), row),
                  pl.BlockSpec((tm, SWA_W), row),
                  pl.BlockSpec((tm, FOX_W), row),
                  pl.BlockSpec((tm, SSM_W), row),
                  pl.BlockSpec((SWA_W + FOX_W + SSM_W, D_MODEL), const, pipeline_mode=pl.Buffered(1)),
                  pl.BlockSpec((1, D_MODEL), const),
                  pl.BlockSpec((D_MODEL, D_MODEL), const, pipeline_mode=pl.Buffered(1)),
                  pl.BlockSpec((mem_tokens, D_MODEL), memrow),
                  pl.BlockSpec((mem_tokens, D_MODEL), memrow),
                  pl.BlockSpec((D_MODEL, D_MODEL), const, pipeline_mode=pl.Buffered(1)),
                  pl.BlockSpec((1, D_MODEL), const)],
        out_specs=pl.BlockSpec((tm, D_MODEL), row),
        out_shape=jax.ShapeDtypeStruct((t, D_MODEL), jnp.float32),
        compiler_params=pltpu.CompilerParams(dimension_semantics=("arbitrary",),
                                             vmem_limit_bytes=_VMEM_LIMIT),
        name="outproj_mem",
    )(x2, ya, yb, yc, w_out.astype(bf), norm_xq_w.reshape(1, D_MODEL), w_mq.astype(bf), kmem, vmem,
      w_mo.astype(bf), final_w.reshape(1, D_MODEL))


def _row_tile(seq):
    return min(512, seq)


def _out_row_tile(seq):
    return min(1024, seq)


def kernel(x, mem, norm_mix_w, w_in, b_forget, swa_sinks, conv_w, conv_b, dt_bias, a_log, d_skip, ssm_norm_w, w_out, norm_xq_w, norm_mem_w, w_mq, w_mk, w_mv, w_mo, final_norm_w):
    batch, seq, _ = x.shape
    mem_tokens = mem.shape[1]
    depth = w_in.shape[0]
    assert seq % BLOCK == 0
    tm = _row_tile(seq)
    assert seq % tm == 0 and seq % FOX_TQ == 0 and seq % SWA_TQ == 0
    rope = _rope_tables(seq)
    x2 = x.reshape(batch * seq, D_MODEL)
    mem2 = mem.reshape(batch * mem_tokens, D_MODEL)
    for l in range(depth):
        fox_order = jnp.argsort(b_forget[l])
        w_r, w_t = _arrange_w_in(w_in[l], fox_order)
        ka, g, kb, z, xbc, fdt, qbt, vbt, qat, vat = _inproj(x2, norm_mix_w[l], w_r, w_t, rope, seq, tm)
        yc, caug, cend = _ssd(xbc, z, fdt, conv_w[l], conv_b[l], b_forget[l][fox_order], dt_bias[l], a_log[l],
                             d_skip[l], ssm_norm_w[l], batch, seq)
        per_tile = FOX_TK // BLOCK
        cend_tiles = cend.reshape(batch, seq // BLOCK, LANES)[:, per_tile - 1::per_tile, :FOX_HEADS]
        cend_tiles = cend_tiles.transpose(0, 2, 1).reshape(batch * FOX_HEADS, seq // FOX_TK)
        ya = _swa(qat, ka, vat, g, swa_sinks[l], batch, seq)
        yb = _fox(cend_tiles, qbt, kb, caug, vbt, g, batch, seq)
        kmem, vmem = _memkv(mem2, norm_mem_w[l], w_mk[l], w_mv[l], mem_tokens)
        o_b = SWA_W
        w_o = w_out[l].at[o_b:o_b + FOX_W].set(
            w_out[l][o_b:o_b + FOX_W].reshape(FOX_HEADS, HEAD_DIM, D_MODEL)[fox_order].reshape(FOX_W, D_MODEL))
        x2 = _out_block(x2, ya, yb, yc, w_o, norm_xq_w[l], w_mq[l], kmem, vmem, w_mo[l],
                        final_norm_w, seq, mem_tokens, _out_row_tile(seq), final_norm=(l == depth - 1))
    return x2.reshape(batch, seq, D_MODEL)
```

```python
import functools

import jax
import jax.numpy as jnp
from jax import lax
from jax.experimental import pallas as pl
from jax.experimental.pallas import tpu as pltpu

D_MODEL = 1024
HEAD_DIM = 64
BLOCK = 128
SWA_HEADS = 8
SWA_KV_HEADS = 2
FOX_HEADS = 8
SSM_HEADS = 16
SSM_STATE = 128
SSM_CONV = 4
MEM_HEADS = 4
MEM_HEAD_DIM = 256
ROPE_THETA = 10000.0
EPS = 1e-6
NEG_INF = -1e30

SWA_W = SWA_HEADS * HEAD_DIM
SWA_KV_W = SWA_KV_HEADS * HEAD_DIM
FOX_W = FOX_HEADS * HEAD_DIM
SSM_W = SSM_HEADS * HEAD_DIM
SSM_BC_W = 2 * SSM_STATE
SSM_CONV_W = SSM_W + 2 * SSM_BC_W
LANES = 128
LOG2E = 1.4426950408889634
SWA_TQ = 512
FOX_TQ = 1024
FOX_CHUNK = 256
FOX_SKIP_BELOW = -170.0
FOX_NORM_MARGIN = 1.02
FOX_TK = 512
OUT_ROWS = 256
SSD_ROWS = 512
DT_LANE0 = FOX_HEADS

_O_QA = 0
_O_KA = _O_QA + SWA_W
_O_VA = _O_KA + SWA_KV_W
_O_GA = _O_VA + SWA_KV_W
_O_QB = _O_GA + SWA_W
_O_KB = _O_QB + FOX_W
_O_VB = _O_KB + FOX_W
_O_FB = _O_VB + FOX_W
_O_GB = _O_FB + FOX_HEADS
_O_ZC = _O_GB + FOX_W
_O_XBC = _O_ZC + SSM_W
_O_DT = _O_XBC + SSM_CONV_W
_IN_W = _O_DT + SSM_HEADS

_SEG_W = (SWA_KV_W + LANES, 2 * SWA_W, FOX_W, SSM_W, SSM_CONV_W, FOX_W, FOX_W, SWA_W, SWA_KV_W)
_SEG_OFF = tuple(sum(_SEG_W[:i]) for i in range(len(_SEG_W)))
_PROJ_W = sum(_SEG_W)

_VMEM_LIMIT = 56 * 1024 * 1024


def _bdot(a, b):
    return jnp.dot(a, b, preferred_element_type=jnp.float32)


def _bdot_nt(a, b):
    return lax.dot_general(a, b, (((1,), (1,)), ((), ())), preferred_element_type=jnp.float32)


def _silu(x):
    h = 0.5 * x
    return h + h * jnp.tanh(h)


def _softplus(x):
    return jnp.maximum(x, 0.0) + jnp.log(1.0 + jnp.exp(-jnp.abs(x)))


def _rms(x, w):
    return x * lax.rsqrt(jnp.mean(x * x, axis=-1, keepdims=True) + EPS) * w


def _rope(x, cos, sin_signed):
    width = x.shape[1]
    reps = width // LANES
    lane = lax.broadcasted_iota(jnp.int32, x.shape, 1)
    first_half = (lane % HEAD_DIM) < (HEAD_DIM // 2)
    swapped = jnp.where(first_half,
                        pltpu.roll(x, width - HEAD_DIM // 2, 1),
                        pltpu.roll(x, HEAD_DIM // 2, 1))
    cos_t = jnp.concatenate([cos] * reps, axis=1)
    sin_t = jnp.concatenate([sin_signed] * reps, axis=1)
    return x * cos_t + swapped * sin_t


def _rope_t(x, cos, sin_signed):
    rows = x.shape[0]
    reps = rows // LANES
    r = lax.broadcasted_iota(jnp.int32, x.shape, 0)
    first_half = (r % HEAD_DIM) < (HEAD_DIM // 2)
    swapped = jnp.where(first_half,
                        pltpu.roll(x, rows - HEAD_DIM // 2, 0),
                        pltpu.roll(x, HEAD_DIM // 2, 0))
    cos_t = jnp.concatenate([cos] * reps, axis=0)
    sin_t = jnp.concatenate([sin_signed] * reps, axis=0)
    return x * cos_t + swapped * sin_t


def _inproj_kernel(x_ref, nw_ref, wt_ref, cos_ref, sin_ref, cost_ref, sint_ref,
                   ka_ref, g_ref, kb_ref, z_ref, xbc_ref, fdt_ref, qbt_ref, vbt_ref, qat_ref, vat_ref):
    h = _rms(x_ref[...], nw_ref[...]).astype(jnp.bfloat16)

    def w_rows(i):
        return wt_ref[_SEG_OFF[i]:_SEG_OFF[i] + _SEG_W[i], :]

    scale = HEAD_DIM ** -0.5 * LOG2E
    k_f = _bdot_nt(h, w_rows(0))
    ka_ref[...] = _rope(k_f[:, :SWA_KV_W], cos_ref[...], sin_ref[...]).astype(ka_ref.dtype)
    fdt_ref[...] = k_f[:, SWA_KV_W:]
    g_ref[...] = _bdot_nt(h, w_rows(1))
    kb_ref[...] = _bdot_nt(h, w_rows(2)).astype(kb_ref.dtype)
    z_ref[...] = _bdot_nt(h, w_rows(3))
    xbc_ref[...] = _bdot_nt(h, w_rows(4))
    qbt_ref[...] = (_bdot_nt(w_rows(5), h) * scale).astype(qbt_ref.dtype)
    qat = _rope_t(_bdot_nt(w_rows(7), h), cost_ref[...], sint_ref[...])
    qat_ref[...] = (qat * scale).astype(qat_ref.dtype)
    vat_ref[...] = _bdot_nt(w_rows(8), h).astype(vat_ref.dtype)
    vbt = _bdot_nt(w_rows(6), h).astype(vbt_ref.dtype)
    for hp in range(FOX_W // LANES):
        for c in range(vbt.shape[1] // FOX_TK):
            vbt_ref[hp, c] = vbt[hp * LANES:(hp + 1) * LANES, c * FOX_TK:(c + 1) * FOX_TK]


def _arrange_w_in(w_in, fox_order):
    wt = w_in.T.astype(jnp.bfloat16)

    def rows(o, n):
        return wt[o:o + n]

    def fox_rows(o):
        return rows(o, FOX_W).reshape(FOX_HEADS, HEAD_DIM, -1)[fox_order].reshape(FOX_W, -1)
    pad = jnp.zeros((LANES - FOX_HEADS - SSM_HEADS, wt.shape[1]), wt.dtype)
    parts = [rows(_O_KA, SWA_KV_W), rows(_O_FB, FOX_HEADS)[fox_order], rows(_O_DT, SSM_HEADS), pad,
             rows(_O_GA, SWA_W), fox_rows(_O_GB),
             fox_rows(_O_KB),
             rows(_O_ZC, SSM_W), rows(_O_XBC, SSM_CONV_W),
             fox_rows(_O_QB), fox_rows(_O_VB), rows(_O_QA, SWA_W), rows(_O_VA, SWA_KV_W)]
    return jnp.concatenate(parts, axis=0)


def _inproj(x2, norm_w, w_t, rope, seq, tm):
    t = x2.shape[0]
    nblk_s = seq // tm
    row = lambda i: (i, 0)
    const = lambda i: (0, 0)
    pos = lambda i: (i % nblk_s, 0)
    pos_t = lambda i: (0, i % nblk_s)
    col = lambda i: (0, i)
    bf, f32 = jnp.bfloat16, jnp.float32
    outs = [(SWA_KV_W, bf), (2 * SWA_W, f32), (FOX_W, bf), (SSM_W, f32), (SSM_CONV_W, f32), (LANES, f32)]
    npair = FOX_W // LANES
    return pl.pallas_call(
        _inproj_kernel,
        grid=(t // tm,),
        in_specs=[pl.BlockSpec((tm, D_MODEL), row),
                  pl.BlockSpec((1, D_MODEL), const),
                  pl.BlockSpec((_PROJ_W, D_MODEL), const, pipeline_mode=pl.Buffered(1)),
                  pl.BlockSpec((tm, LANES), pos),
                  pl.BlockSpec((tm, LANES), pos),
                  pl.BlockSpec((LANES, tm), pos_t),
                  pl.BlockSpec((LANES, tm), pos_t)],
        out_specs=[pl.BlockSpec((tm, w), row) for w, _ in outs]
        + [pl.BlockSpec((FOX_W, tm), col),
           pl.BlockSpec((npair, tm // FOX_TK, LANES, FOX_TK), lambda i: (0, i, 0, 0)),
           pl.BlockSpec((SWA_W, tm), col),
           pl.BlockSpec((SWA_KV_W, tm), col)],
        out_shape=[jax.ShapeDtypeStruct((t, w), d) for w, d in outs]
        + [jax.ShapeDtypeStruct((FOX_W, t), bf),
           jax.ShapeDtypeStruct((npair, t // FOX_TK, LANES, FOX_TK), bf),
           jax.ShapeDtypeStruct((SWA_W, t), bf),
           jax.ShapeDtypeStruct((SWA_KV_W, t), bf)],
        compiler_params=pltpu.CompilerParams(dimension_semantics=("arbitrary",),
                                             vmem_limit_bytes=_VMEM_LIMIT),
        name="inproj",
    )(x2, norm_w.reshape(1, D_MODEL), w_t, *rope)


def _rope_tables(seq):
    pos = jnp.arange(seq, dtype=jnp.float32)
    inv = 1.0 / (ROPE_THETA ** (jnp.arange(0, HEAD_DIM, 2, dtype=jnp.float32) / HEAD_DIM))
    ang = pos[:, None] * inv[None, :]
    cos, sin = jnp.cos(ang), jnp.sin(ang)
    cos_t = jnp.concatenate([cos, cos, cos, cos], axis=1)
    sin_t = jnp.concatenate([-sin, sin, -sin, sin], axis=1)
    return cos_t, sin_t, cos_t.T, sin_t.T


def _swa_kernel(sink_ref, qt_ref, kc_ref, kp_ref, vc_ref, vp_ref, g_ref, o_ref):
    n = pl.program_id(1)
    nsub = SWA_TQ // BLOCK
    key = lax.broadcasted_iota(jnp.int32, (2 * BLOCK, BLOCK), 0)
    qry = lax.broadcasted_iota(jnp.int32, (2 * BLOCK, BLOCK), 1)
    band = jnp.where((key > qry) & (key <= qry + BLOCK), 0.0, NEG_INF)
    band_first = jnp.where(key < BLOCK, NEG_INF, band)
    row = lax.broadcasted_iota(jnp.int32, (LANES, BLOCK), 0)
    ones_rows = jnp.ones((16, 2 * BLOCK), jnp.bfloat16)

    def scores(u):
        if u == 0:
            kcat = jnp.concatenate([kp_ref[...], kc_ref[0:BLOCK, :]], axis=0)
            vcat = jnp.concatenate([vp_ref[...], vc_ref[:, 0:BLOCK]], axis=1)
            bias = jnp.where(n > 0, band, band_first)
        else:
            kcat = kc_ref[(u - 1) * BLOCK:(u + 1) * BLOCK, :]
            vcat = vc_ref[:, (u - 1) * BLOCK:(u + 1) * BLOCK]
            bias = band
        tiles = []
        for c in range(SWA_HEADS // 2):
            qt = qt_ref[c * LANES:(c + 1) * LANES, u * BLOCK:(u + 1) * BLOCK].astype(jnp.float32)
            for half in range(2):
                kv = (2 * c + half) // (SWA_HEADS // SWA_KV_HEADS)
                q_rows = qt if kv == half else pltpu.roll(qt, HEAD_DIM, 0)
                in_half = (row < HEAD_DIM) if kv == 0 else (row >= HEAD_DIM)
                w = jnp.where(in_half, q_rows, 0.0).astype(jnp.bfloat16)
                tiles.append(_bdot(kcat, w) + bias)
        return tiles, vcat

    def finish(u, tiles, vcat):
        probs, sink_terms = [], []
        for head, s in enumerate(tiles):
            sink = sink_ref[0, head] * LOG2E
            m = jnp.maximum(jnp.max(s, axis=0, keepdims=True), sink)
            probs.append(jnp.exp2(s - m).astype(jnp.bfloat16))
            sink_terms.append(jnp.exp2(sink - m))
        outs = []
        for head, p in enumerate(probs):
            kv = head // (SWA_HEADS // SWA_KV_HEADS)
            lhs = jnp.concatenate([vcat[kv * HEAD_DIM:(kv + 1) * HEAD_DIM, :], ones_rows], axis=0)
            pv = _bdot(lhs, p)
            outs.append(pv[0:HEAD_DIM, :] / (pv[HEAD_DIM:HEAD_DIM + 1, :] + sink_terms[head]))
        y = jnp.concatenate(outs, axis=0).T
        rows = slice(u * BLOCK, (u + 1) * BLOCK)
        o_ref[rows, :] = (y * _silu(g_ref[rows, :])).astype(o_ref.dtype)

    pending = scores(0)
    for u in range(nsub):
        nxt = scores(u + 1) if u + 1 < nsub else None
        finish(u, *pending)
        pending = nxt


def _swa(qat, ka, vat, g, sinks, batch, seq):
    n = seq // SWA_TQ
    nsub = SWA_TQ // BLOCK
    cur = lambda b, i: (b * n + i, 0)
    cur_t = lambda b, i: (0, b * n + i)
    prev = lambda b, i: ((b * n + i) * nsub - jnp.minimum(i, 1), 0)
    prev_t = lambda b, i: (0, (b * n + i) * nsub - jnp.minimum(i, 1))
    return pl.pallas_call(
        _swa_kernel,
        grid=(batch, n),
        in_specs=[pl.BlockSpec(memory_space=pltpu.SMEM),
                  pl.BlockSpec((SWA_W, SWA_TQ), cur_t),
                  pl.BlockSpec((SWA_TQ, SWA_KV_W), cur),
                  pl.BlockSpec((BLOCK, SWA_KV_W), prev),
                  pl.BlockSpec((SWA_KV_W, SWA_TQ), cur_t),
                  pl.BlockSpec((SWA_KV_W, BLOCK), prev_t),
                  pl.BlockSpec((SWA_TQ, SWA_W), cur)],
        out_specs=pl.BlockSpec((SWA_TQ, SWA_W), cur),
        out_shape=jax.ShapeDtypeStruct((batch * seq, SWA_W), jnp.bfloat16),
        compiler_params=pltpu.CompilerParams(dimension_semantics=("arbitrary", "arbitrary")),
        name="swa",
    )(sinks.reshape(1, SWA_HEADS).astype(jnp.float32), qat, ka, ka, vat, vat, g)


def _fox_kernel(cend_ref, qt_ref, k_ref, ca_ref, vt_ref, g_ref, o_ref, w_buf, s_buf0, s_buf1, acc_buf, bias_buf,
                js_ref):
    tq, tk = FOX_TQ, FOX_TK
    per_q = tq // tk
    nq = qt_ref.shape[1] // tq
    all_masked = per_q
    acc_rows = acc_buf.shape[1]

    krow = lax.broadcasted_iota(jnp.int32, (tk, tq), 0)
    qcol = lax.broadcasted_iota(jnp.int32, (tk, tq), 1)
    for d in range(per_q):
        bias_buf[d] = jnp.where(krow + d * tk <= qcol, 0.0, NEG_INF)
    bias_buf[all_masked] = jnp.full((tk, tq), NEG_INF, jnp.float32)
    for buf in (s_buf0, s_buf1, acc_buf):
        buf[...] = jnp.zeros_like(buf)

    row = lax.broadcasted_iota(jnp.int32, (LANES, tq), 0)
    nsplit = 3
    pick = (jnp.where(row < nsplit, 1.0, 0.0).astype(jnp.bfloat16),
            jnp.where((row >= HEAD_DIM) & (row < HEAD_DIM + nsplit), 1.0, 0.0).astype(jnp.bfloat16))
    ones_rows = jnp.ones((acc_rows - HEAD_DIM, tk), jnp.bfloat16)

    qsq = jnp.square(qt_ref[...].astype(jnp.float32))
    ksq = jnp.square(k_ref[...].astype(jnp.float32))
    lane_sel = lax.broadcasted_iota(jnp.int32, (LANES, LANES), 0) // HEAD_DIM == \
        lax.broadcasted_iota(jnp.int32, (LANES, LANES), 1)
    kn2 = jnp.max(_bdot(ksq.astype(jnp.bfloat16), jnp.where(lane_sel, 1.0, 0.0).astype(jnp.bfloat16)),
                  axis=0, keepdims=True)
    bound = []
    for hh in range(2):
        qn2 = jnp.max(jnp.sum(qsq[hh * HEAD_DIM:(hh + 1) * HEAD_DIM, :], axis=0, keepdims=True),
                      axis=1, keepdims=True)
        bound.append((FOX_NORM_MARGIN * 2.0 * jnp.sqrt(qn2 * kn2[:, hh:hh + 1]))[0, 0])
    head0 = (pl.program_id(0) * (FOX_HEADS // 2) + pl.program_id(1)) * 2
    n_steps = jnp.int32(0)
    for i in range(nq):
        count, prefix = jnp.int32(0), jnp.bool_(True)
        for j in range(per_q * i):
            for hh in range(2):
                drop = cend_ref[head0 + hh, j] - cend_ref[head0 + hh, per_q * i - 1]
                prefix = prefix & (bound[hh] - drop < FOX_SKIP_BELOW)
            count = count + prefix.astype(jnp.int32)
        js_ref[i] = count
        n_steps = n_steps + (per_q * (i + 1) - count)

    def step(carry, s_cur, s_prv, masked):
        (i_a, j_a), (i_b, j_b, v_b, first), smax, m = carry
        first_a = j_a == js_ref[jnp.minimum(i_a, nq - 1)]

        @pl.when(first_a)
        def _():
            q0 = pl.multiple_of(jnp.minimum(i_a, nq - 1) * tq, tq)
            qt = qt_ref[:, pl.ds(q0, tq)].astype(jnp.float32)
            q_lo = jnp.where(row < HEAD_DIM, qt, 0.0).astype(jnp.bfloat16)
            q_hi = jnp.where(row < HEAD_DIM, 0.0, qt).astype(jnp.bfloat16)
            w_buf[:, 0:tq] = jnp.concatenate([q_lo, pick[0]], axis=0)
            w_buf[:, tq:2 * tq] = jnp.concatenate([q_hi, pick[1]], axis=0)

        k0 = pl.multiple_of(jnp.where(i_a < nq, j_a, 0) * tk, tk)
        ka = jnp.concatenate([k_ref[pl.ds(k0, tk), :], ca_ref[pl.ds(k0, tk), :]], axis=1)
        if masked:
            bias_slot = jnp.where(i_a < nq, j_a - per_q * i_a, all_masked)
        vt = vt_ref[0, j_b]
        n_chunks = 2 * tq // FOX_CHUNK
        smax_new, m_new = [], []

        def chunk_cols(ch):
            hh, start = divmod(ch * FOX_CHUNK, tq)
            return slice(ch * FOX_CHUNK, (ch + 1) * FOX_CHUNK), hh, slice(start, start + FOX_CHUNK)

        def stage_a(ch):
            cols, _, hcols = chunk_cols(ch)
            s_new = _bdot(ka, w_buf[:, cols])
            if masked:
                s_new = s_new + bias_buf[bias_slot, :, hcols]
            s_cur[:, cols] = s_new
            smax_new.append(jnp.max(s_new, axis=0, keepdims=True))

        def stage_b(ch):
            cols, hh, hcols = chunk_cols(ch)
            m_prev = jnp.where(first, NEG_INF, m[ch])
            mn = jnp.maximum(m_prev, smax[ch])
            p = jnp.exp2(s_prv[:, cols] - mn).astype(jnp.bfloat16)
            lhs = jnp.concatenate([vt[hh * HEAD_DIM:(hh + 1) * HEAD_DIM, :], ones_rows], axis=0)
            acc_buf[hh, :, hcols] = jnp.exp2(m_prev - mn) * acc_buf[hh, :, hcols] + _bdot(lhs, p)
            m_new.append(mn)

        stage_a(0)
        for ch in range(n_chunks):
            if ch + 1 < n_chunks:
                stage_a(ch + 1)
            stage_b(ch)
        smax_new = tuple(smax_new)

        @pl.when(v_b & (j_b == per_q * (i_b + 1) - 1))
        def _():
            q0 = pl.multiple_of(i_b * tq, tq)
            out_t = jnp.concatenate([acc_buf[hh, 0:HEAD_DIM, :] / acc_buf[hh, HEAD_DIM:HEAD_DIM + 1, :]
                                     for hh in range(2)], axis=0)
            y = out_t.T * _silu(g_ref[pl.ds(q0, tq), :])
            o_ref[pl.ds(q0, tq), :] = y.astype(o_ref.dtype)

        last_a = j_a == per_q * (i_a + 1) - 1
        nxt_a = (jnp.where(last_a, i_a + 1, i_a),
                 jnp.where(last_a, js_ref[jnp.minimum(i_a + 1, nq - 1)], j_a + 1))
        return (nxt_a, (i_a, jnp.where(i_a < nq, j_a, 0), i_a < nq, first_a), smax_new, tuple(m_new))

    def either(carry, bufs):
        i_a, j_a = carry[0]
        needs_mask = (j_a >= per_q * i_a) | (i_a >= nq)
        return lax.cond(needs_mask, lambda c: step(c, *bufs, True), lambda c: step(c, *bufs, False), carry)

    def body(_, carry):
        carry = either(carry, (s_buf0, s_buf1))
        return either(carry, (s_buf1, s_buf0))

    zi = jnp.int32(0)
    row_vec = lambda v: tuple(jnp.full((1, FOX_CHUNK), v, jnp.float32) for _ in range(2 * tq // FOX_CHUNK))
    init = ((zi, zi), (zi, zi, False, False), row_vec(0.0), row_vec(0.0))
    lax.fori_loop(0, (n_steps + 2) // 2, body, init)


def _fox(cend_tiles, qbt, kb, caug, vbt, g, batch, seq):
    npair = FOX_HEADS // 2
    nkt = seq // FOX_TK
    gcol0 = SWA_W // LANES
    return pl.pallas_call(
        _fox_kernel,
        grid=(batch, npair),
        in_specs=[pl.BlockSpec(memory_space=pltpu.SMEM),
                  pl.BlockSpec((LANES, seq), lambda b, h: (h, b)),
                  pl.BlockSpec((seq, LANES), lambda b, h: (b, h)),
                  pl.BlockSpec((seq, LANES), lambda b, h: (b, h)),
                  pl.BlockSpec((1, nkt, LANES, FOX_TK), lambda b, h: (h, b, 0, 0)),
                  pl.BlockSpec((seq, LANES), lambda b, h: (b, gcol0 + h))],
        out_specs=pl.BlockSpec((seq, LANES), lambda b, h: (b, h)),
        out_shape=jax.ShapeDtypeStruct((batch * seq, FOX_W), jnp.bfloat16),
        scratch_shapes=[pltpu.VMEM((2 * LANES, 2 * FOX_TQ), jnp.bfloat16),
                        pltpu.VMEM((FOX_TK, 2 * FOX_TQ), jnp.float32),
                        pltpu.VMEM((FOX_TK, 2 * FOX_TQ), jnp.float32),
                        pltpu.VMEM((2, HEAD_DIM + 16, FOX_TQ), jnp.float32),
                        pltpu.VMEM((FOX_TQ // FOX_TK + 1, FOX_TK, FOX_TQ), jnp.float32),
                        pltpu.SMEM((seq // FOX_TQ,), jnp.int32)],
        compiler_params=pltpu.CompilerParams(dimension_semantics=("arbitrary", "arbitrary"),
                                             vmem_limit_bytes=_VMEM_LIMIT),
        name="fox",
    )(cend_tiles, qbt, kb, caug, vbt, g)


def _split3(x):
    hi = x.astype(jnp.bfloat16)
    r1 = x - hi.astype(jnp.float32)
    mid = r1.astype(jnp.bfloat16)
    lo = (r1 - mid.astype(jnp.float32)).astype(jnp.bfloat16)
    return hi, mid, lo


def _ssd_kernel(xbc_ref, z_ref, fdt_ref, cw_ref, cb_ref, bias_ref, alog_ref, dskip_ref, nw_ref,
                y_ref, ca_ref, cend_ref, state_ref, tail_ref, ccarry_ref):
    @pl.when(pl.program_id(1) == 0)
    def _():
        state_ref[...] = jnp.zeros_like(state_ref)
        tail_ref[...] = jnp.zeros_like(tail_ref)
        ccarry_ref[...] = jnp.zeros_like(ccarry_ref)

    for sub in range(xbc_ref.shape[0] // BLOCK):
        _ssd_chunk(sub, xbc_ref, z_ref, fdt_ref, cw_ref, cb_ref, bias_ref, alog_ref, dskip_ref, nw_ref,
                   y_ref, ca_ref, cend_ref, state_ref, tail_ref, ccarry_ref)


def _ssd_chunk(sub, xbc_ref, z_ref, fdt_ref, cw_ref, cb_ref, bias_ref, alog_ref, dskip_ref, nw_ref,
               y_ref, ca_ref, cend_ref, state_ref, tail_ref, ccarry_ref):
    L = BLOCK
    rows = slice(sub * L, (sub + 1) * L)
    u = xbc_ref[rows, :]
    tail_ref[8:8 + L, :] = u
    conv = cb_ref[...] + cw_ref[SSM_CONV - 1:SSM_CONV, :] * u
    for k in range(1, SSM_CONV):
        conv = conv + cw_ref[SSM_CONV - 1 - k:SSM_CONV - k, :] * tail_ref[8 - k:8 - k + L, :]
    tail_ref[0:8, :] = u[L - 8:L]
    act = _silu(conv)
    xs = act[:, :SSM_W]
    bm_f = act[:, SSM_W:SSM_W + SSM_BC_W]
    bm = bm_f.astype(jnp.bfloat16)
    cm = act[:, SSM_W + SSM_BC_W:].astype(jnp.bfloat16)

    lane = lax.broadcasted_iota(jnp.int32, (L, LANES), 1)
    vals = fdt_ref[rows, :] + bias_ref[...]
    is_f = lane < DT_LANE0
    is_dt = (lane >= DT_LANE0) & (lane < DT_LANE0 + SSM_HEADS)
    sp = _softplus(jnp.where(is_f, -vals, vals))
    a_row = jnp.where(is_dt[0:1], -jnp.exp(alog_ref[...]), 0.0)
    dt = jnp.where(is_dt, sp, 0.0)
    scan_in = jnp.where(is_f, -sp, dt * a_row) * LOG2E

    ri = lax.broadcasted_iota(jnp.int32, (L, L), 0)
    cj = lax.broadcasted_iota(jnp.int32, (L, L), 1)
    causal = cj <= ri
    tri = jnp.where(causal, 1.0, 0.0).astype(jnp.bfloat16)
    hi, mid, lo = _split3(scan_in)
    cs = _bdot(tri, hi) + _bdot(tri, mid) + _bdot(tri, lo)
    cs_t = cs.T

    c_full = cs + ccarry_ref[...]
    ccarry_ref[...] = jnp.where(is_f[0:1], c_full[L - 1:L, :], 0.0)
    cend_ref[sub] = c_full[L - 1:L, :]
    lane64 = lane % HEAD_DIM
    ca_cols = []
    for hp in range(FOX_HEADS // 2):
        negc = -jnp.where(lane < HEAD_DIM, c_full[:, 2 * hp:2 * hp + 1], c_full[:, 2 * hp + 1:2 * hp + 2])
        hi, mid, lo = (term.astype(jnp.float32) for term in _split3(negc))
        terms = jnp.where(lane64 == 0, hi, jnp.where(lane64 == 1, mid, jnp.where(lane64 == 2, lo, 0.0)))
        ca_cols.append(terms.astype(jnp.bfloat16))
    ca_ref[rows, :] = jnp.concatenate(ca_cols, axis=1)

    low_l = lane < HEAD_DIM
    hpg = SSM_HEADS // 2
    npairs = SSM_HEADS // 2
    grp_of = [(2 * pair) // hpg for pair in range(npairs)]
    bm_g = [bm[:, g * SSM_STATE:(g + 1) * SSM_STATE] for g in range(2)]
    cm_g = [cm[:, g * SSM_STATE:(g + 1) * SSM_STATE] for g in range(2)]
    bm_t = [bm_f[:, g * SSM_STATE:(g + 1) * SSM_STATE].T.astype(jnp.bfloat16) for g in range(2)]
    cb = [_bdot_nt(cm_g[g], bm_g[g]) for g in range(2)]
    ys = []
    for pair in range(npairs):
        g = grp_of[pair]
        ha, hb = DT_LANE0 + 2 * pair, DT_LANE0 + 2 * pair + 1
        col = (cs[:, ha:ha + 1], cs[:, hb:hb + 1])
        acs_row = (cs_t[ha:ha + 1, :], cs_t[hb:hb + 1, :])
        xdt = xs[:, pair * LANES:(pair + 1) * LANES] * jnp.where(low_l, dt[:, ha:ha + 1], dt[:, hb:hb + 1])
        xdt_b = xdt.astype(jnp.bfloat16)
        prev = state_ref[pair]
        y_off = _bdot(cm_g[g], prev.astype(jnp.bfloat16))
        last = (col[0][L - 1:L, :], col[1][L - 1:L, :])
        dst = jnp.exp2(jnp.where(low_l, last[0] - col[0], last[1] - col[1]))
        st_new = _bdot(bm_t[g], (xdt * dst).astype(jnp.bfloat16))
        state_ref[pair] = prev * jnp.exp2(jnp.where(low_l, last[0], last[1])) + st_new
        y_diag = []
        for hh in range(2):
            decay = jnp.exp2(jnp.where(causal, col[hh] - acs_row[hh], NEG_INF))
            gmat = (cb[g] * decay).astype(jnp.bfloat16)
            y_diag.append(_bdot(gmat, xdt_b))
        ys.append(jnp.where(low_l, y_diag[0], y_diag[1]) + y_off * jnp.exp2(jnp.where(low_l, col[0], col[1])))
    y = jnp.concatenate(ys, axis=1) + dskip_ref[...] * xs
    y_ref[rows, :] = _rms(y * _silu(z_ref[rows, :]), nw_ref[...]).astype(y_ref.dtype)


def _ssd(xbc, z, fdt, conv_w, conv_b, b_forget, dt_bias, a_log, d_skip, norm_w, batch, seq):
    rows = min(SSD_ROWS, seq)
    per_step = rows // BLOCK
    nc = seq // rows
    row = lambda b, c: (b * nc + c, 0)
    const = lambda b, c: (0, 0)
    zpad = jnp.zeros((LANES - FOX_HEADS - SSM_HEADS,), jnp.float32)
    bias_row = jnp.concatenate([b_forget, dt_bias, zpad]).reshape(1, LANES)
    alog_row = jnp.concatenate([jnp.zeros((FOX_HEADS,), jnp.float32), a_log, zpad]).reshape(1, LANES)
    dskip_row = jnp.repeat(d_skip, HEAD_DIM).reshape(1, SSM_W)
    return pl.pallas_call(
        _ssd_kernel,
        grid=(batch, nc),
        in_specs=[pl.BlockSpec((rows, SSM_CONV_W), row),
                  pl.BlockSpec((rows, SSM_W), row),
                  pl.BlockSpec((rows, LANES), row),
                  pl.BlockSpec((SSM_CONV, SSM_CONV_W), const),
                  pl.BlockSpec((1, SSM_CONV_W), const),
                  pl.BlockSpec((1, LANES), const),
                  pl.BlockSpec((1, LANES), const),
                  pl.BlockSpec((1, SSM_W), const),
                  pl.BlockSpec((1, SSM_W), const)],
        out_specs=[pl.BlockSpec((rows, SSM_W), row),
                   pl.BlockSpec((rows, FOX_W), row),
                   pl.BlockSpec((per_step, 1, LANES), lambda b, c: (b * nc + c, 0, 0))],
        out_shape=[jax.ShapeDtypeStruct((batch * seq, SSM_W), jnp.bfloat16),
                   jax.ShapeDtypeStruct((batch * seq, FOX_W), jnp.bfloat16),
                   jax.ShapeDtypeStruct((batch * nc * per_step, 1, LANES), jnp.float32)],
        scratch_shapes=[pltpu.VMEM((SSM_HEADS // 2, 2 * HEAD_DIM, SSM_STATE), jnp.float32),
                        pltpu.VMEM((8 + BLOCK, SSM_CONV_W), jnp.float32),
                        pltpu.VMEM((1, LANES), jnp.float32)],
        compiler_params=pltpu.CompilerParams(dimension_semantics=("arbitrary", "arbitrary")),
        name="ssd",
    )(xbc, z, fdt, conv_w, conv_b.reshape(1, SSM_CONV_W), bias_row, alog_row, dskip_row,
      norm_w.reshape(1, SSM_W))


def _memkv_kernel(mem_ref, nw_ref, wk_ref, wv_ref, k_ref, v_ref):
    mn = _rms(mem_ref[...], nw_ref[...]).astype(jnp.bfloat16)
    k_ref[...] = _bdot(mn, wk_ref[...]).astype(k_ref.dtype)
    v_ref[...] = _bdot(mn, wv_ref[...]).astype(v_ref.dtype)


def _memkv(mem2, norm_w, wk, wv, mem_tokens):
    t = mem2.shape[0]
    row = lambda b: (b, 0)
    const = lambda b: (0, 0)
    return pl.pallas_call(
        _memkv_kernel,
        grid=(t // mem_tokens,),
        in_specs=[pl.BlockSpec((mem_tokens, D_MODEL), row),
                  pl.BlockSpec((1, D_MODEL), const),
                  pl.BlockSpec((D_MODEL, D_MODEL), const),
                  pl.BlockSpec((D_MODEL, D_MODEL), const)],
        out_specs=[pl.BlockSpec((mem_tokens, D_MODEL), row)] * 2,
        out_shape=[jax.ShapeDtypeStruct((t, D_MODEL), jnp.bfloat16)] * 2,
        compiler_params=pltpu.CompilerParams(dimension_semantics=("arbitrary",)),
        name="memkv",
    )(mem2, norm_w.reshape(1, D_MODEL), wk.astype(jnp.bfloat16), wv.astype(jnp.bfloat16))


def _out_kernel(x_ref, ya_ref, yb_ref, yc_ref, wo_ref, nq_ref, wq_ref, k_ref, v_ref, wmo_ref, fn_ref,
                o_ref, *, final_norm):
    tm = x_ref.shape[0]
    groups = [slice(r, r + OUT_ROWS) for r in range(0, tm, OUT_ROWS)]
    head_cols = [slice(h * MEM_HEAD_DIM, (h + 1) * MEM_HEAD_DIM) for h in range(MEM_HEADS)]
    x1 = [x_ref[g, :]
          + _bdot(ya_ref[g, :], wo_ref[0:SWA_W, :])
          + _bdot(yb_ref[g, :], wo_ref[SWA_W:SWA_W + FOX_W, :])
          + _bdot(yc_ref[g, :], wo_ref[SWA_W + FOX_W:, :]) for g in groups]
    q = []
    for x1g in x1:
        hq = _rms(x1g, nq_ref[...]).astype(jnp.bfloat16)
        q.append((_bdot(hq, wq_ref[...]) * (MEM_HEAD_DIM ** -0.5 * LOG2E)).astype(jnp.bfloat16))
    scores = [[_bdot_nt(qg[:, sl], k_ref[:, sl]) for sl in head_cols] for qg in q]
    att = []
    for sg in scores:
        heads = []
        for s, sl in zip(sg, head_cols):
            p = jnp.exp2(s - jnp.max(s, axis=-1, keepdims=True))
            probs = (p / jnp.sum(p, axis=-1, keepdims=True)).astype(jnp.bfloat16)
            heads.append(_bdot(probs, v_ref[:, sl]).astype(jnp.bfloat16))
        att.append(jnp.concatenate(heads, axis=1))
    for g, x1g, attg in zip(groups, x1, att):
        x2 = x1g + _bdot(attg, wmo_ref[...])
        if final_norm:
            x2 = _rms(x2, fn_ref[...])
        o_ref[g, :] = x2


def _out_block(x2, ya, yb, yc, w_out, norm_xq_w, w_mq, kmem, vmem, w_mo, final_w, seq, mem_tokens, tm,
               final_norm):
    t = x2.shape[0]
    nblk_s = seq // tm
    row = lambda i: (i, 0)
    const = lambda i: (0, 0)
    memrow = lambda i: (i // nblk_s, 0)
    bf = jnp.bfloat16
    return pl.pallas_call(
        functools.partial(_out_kernel, final_norm=final_norm),
        grid=(t // tm,),
        in_specs=[pl.BlockSpec((tm, D_MODEL), row),
                  pl.BlockSpec((tm, SWA_W), row),
                  pl.BlockSpec((tm, FOX_W), row),
                  pl.BlockSpec((tm, SSM_W), row),
                  pl.BlockSpec((SWA_W + FOX_W + SSM_W, D_MODEL), const, pipeline_mode=pl.Buffered(1)),
                  pl.BlockSpec((1, D_MODEL), const),
                  pl.BlockSpec((D_MODEL, D_MODEL), const, pipeline_mode=pl.Buffered(1)),
                  pl.BlockSpec((mem_tokens, D_MODEL), memrow),
                  pl.BlockSpec((mem_tokens, D_MODEL), memrow),
                  pl.BlockSpec((D_MODEL, D_MODEL), const, pipeline_mode=pl.Buffered(1)),
                  pl.BlockSpec((1, D_MODEL), const)],
        out_specs=pl.BlockSpec((tm, D_MODEL), row),
        out_shape=jax.ShapeDtypeStruct((t, D_MODEL), jnp.float32),
        compiler_params=pltpu.CompilerParams(dimension_semantics=("arbitrary",),
                                             vmem_limit_bytes=_VMEM_LIMIT),
        name="outproj_mem",
    )(x2, ya, yb, yc, w_out.astype(bf), norm_xq_w.reshape(1, D_MODEL), w_mq.astype(bf), kmem, vmem,
      w_mo.astype(bf), final_w.reshape(1, D_MODEL))


def _row_tile(seq):
    return min(512, seq)


def _out_row_tile(seq):
    return min(1024, seq)


def kernel(x, mem, norm_mix_w, w_in, b_forget, swa_sinks, conv_w, conv_b, dt_bias, a_log, d_skip, ssm_norm_w, w_out, norm_xq_w, norm_mem_w, w_mq, w_mk, w_mv, w_mo, final_norm_w):
    batch, seq, _ = x.shape
    mem_tokens = mem.shape[1]
    depth = w_in.shape[0]
    assert seq % BLOCK == 0
    tm = _row_tile(seq)
    assert seq % tm == 0 and seq % FOX_TQ == 0 and seq % SWA_TQ == 0
    rope = _rope_tables(seq)
    x2 = x.reshape(batch * seq, D_MODEL)
    mem2 = mem.reshape(batch * mem_tokens, D_MODEL)
    for l in range(depth):
        fox_order = jnp.argsort(b_forget[l])
        w_t = _arrange_w_in(w_in[l], fox_order)
        ka, g, kb, z, xbc, fdt, qbt, vbt, qat, vat = _inproj(x2, norm_mix_w[l], w_t, rope, seq, tm)
        yc, caug, cend = _ssd(xbc, z, fdt, conv_w[l], conv_b[l], b_forget[l][fox_order], dt_bias[l], a_log[l],
                             d_skip[l], ssm_norm_w[l], batch, seq)
        per_tile = FOX_TK // BLOCK
        cend_tiles = cend.reshape(batch, seq // BLOCK, LANES)[:, per_tile - 1::per_tile, :FOX_HEADS]
        cend_tiles = cend_tiles.transpose(0, 2, 1).reshape(batch * FOX_HEADS, seq // FOX_TK)
        ya = _swa(qat, ka, vat, g, swa_sinks[l], batch, seq)
        yb = _fox(cend_tiles, qbt, kb, caug, vbt, g, batch, seq)
        kmem, vmem = _memkv(mem2, norm_mem_w[l], w_mk[l], w_mv[l], mem_tokens)
        o_b = SWA_W
        w_o = w_out[l].at[o_b:o_b + FOX_W].set(
            w_out[l][o_b:o_b + FOX_W].reshape(FOX_HEADS, HEAD_DIM, D_MODEL)[fox_order].reshape(FOX_W, D_MODEL))
        x2 = _out_block(x2, ya, yb, yc, w_o, norm_xq_w[l], w_mq[l], kmem, vmem, w_mo[l],
                        final_norm_w, seq, mem_tokens, _out_row_tile(seq), final_norm=(l == depth - 1))
    return x2.reshape(batch, seq, D_MODEL)
```

```python
import functools

import jax
import jax.numpy as jnp
from jax import lax
from jax.experimental import pallas as pl
from jax.experimental.pallas import tpu as pltpu

D_MODEL = 1024
HEAD_DIM = 64
BLOCK = 128
SWA_HEADS = 8
SWA_KV_HEADS = 2
FOX_HEADS = 8
SSM_HEADS = 16
SSM_STATE = 128
SSM_CONV = 4
MEM_HEADS = 4
MEM_HEAD_DIM = 256
ROPE_THETA = 10000.0
EPS = 1e-6
NEG_INF = -1e30

SWA_W = SWA_HEADS * HEAD_DIM
SWA_KV_W = SWA_KV_HEADS * HEAD_DIM
FOX_W = FOX_HEADS * HEAD_DIM
SSM_W = SSM_HEADS * HEAD_DIM
SSM_BC_W = 2 * SSM_STATE
SSM_CONV_W = SSM_W + 2 * SSM_BC_W
LANES = 128
LOG2E = 1.4426950408889634
SWA_TQ = 512
FOX_TQ = 1024
FOX_CHUNK = 256
FOX_SKIP_BELOW = -170.0
FOX_NORM_MARGIN = 1.02
FOX_TK = 512
OUT_ROWS = 256
SSD_ROWS = 512
DT_LANE0 = FOX_HEADS

_O_QA = 0
_O_KA = _O_QA + SWA_W
_O_VA = _O_KA + SWA_KV_W
_O_GA = _O_VA + SWA_KV_W
_O_QB = _O_GA + SWA_W
_O_KB = _O_QB + FOX_W
_O_VB = _O_KB + FOX_W
_O_FB = _O_VB + FOX_W
_O_GB = _O_FB + FOX_HEADS
_O_ZC = _O_GB + FOX_W
_O_XBC = _O_ZC + SSM_W
_O_DT = _O_XBC + SSM_CONV_W
_IN_W = _O_DT + SSM_HEADS

_SEG_W = (SWA_KV_W + LANES, 2 * SWA_W, FOX_W, SSM_W, SSM_CONV_W, FOX_W, FOX_W, SWA_W, SWA_KV_W)
_SEG_OFF = tuple(sum(_SEG_W[:i]) for i in range(len(_SEG_W)))
_PROJ_W = sum(_SEG_W)

_VMEM_LIMIT = 56 * 1024 * 1024


def _bdot(a, b):
    return jnp.dot(a, b, preferred_element_type=jnp.float32)


def _bdot_nt(a, b):
    return lax.dot_general(a, b, (((1,), (1,)), ((), ())), preferred_element_type=jnp.float32)


def _silu(x):
    h = 0.5 * x
    return h + h * jnp.tanh(h)


def _softplus(x):
    return jnp.maximum(x, 0.0) + jnp.log(1.0 + jnp.exp(-jnp.abs(x)))


def _rms(x, w):
    return x * lax.rsqrt(jnp.mean(x * x, axis=-1, keepdims=True) + EPS) * w


def _rope(x, cos, sin_signed):
    width = x.shape[1]
    reps = width // LANES
    lane = lax.broadcasted_iota(jnp.int32, x.shape, 1)
    first_half = (lane % HEAD_DIM) < (HEAD_DIM // 2)
    swapped = jnp.where(first_half,
                        pltpu.roll(x, width - HEAD_DIM // 2, 1),
                        pltpu.roll(x, HEAD_DIM // 2, 1))
    cos_t = jnp.concatenate([cos] * reps, axis=1)
    sin_t = jnp.concatenate([sin_signed] * reps, axis=1)
    return x * cos_t + swapped * sin_t


def _rope_t(x, cos, sin_signed):
    rows = x.shape[0]
    reps = rows // LANES
    r = lax.broadcasted_iota(jnp.int32, x.shape, 0)
    first_half = (r % HEAD_DIM) < (HEAD_DIM // 2)
    swapped = jnp.where(first_half,
                        pltpu.roll(x, rows - HEAD_DIM // 2, 0),
                        pltpu.roll(x, HEAD_DIM // 2, 0))
    cos_t = jnp.concatenate([cos] * reps, axis=0)
    sin_t = jnp.concatenate([sin_signed] * reps, axis=0)
    return x * cos_t + swapped * sin_t


def _inproj_kernel(x_ref, nw_ref, wt_ref, cost_ref, sint_ref,
                   ka_ref, g_ref, kb_ref, z_ref, xbc_ref, fdt_ref, qbt_ref, vbt_ref, qat_ref, vat_ref):
    h = _rms(x_ref[...], nw_ref[...]).astype(jnp.bfloat16)

    def w_rows(i):
        return wt_ref[_SEG_OFF[i]:_SEG_OFF[i] + _SEG_W[i], :]

    scale = HEAD_DIM ** -0.5 * LOG2E
    k_f = _bdot_nt(h, w_rows(0))
    ka_ref[...] = _rope(k_f[:, :SWA_KV_W], cost_ref[...].T, sint_ref[...].T).astype(ka_ref.dtype)
    fdt_ref[...] = k_f[:, SWA_KV_W:]
    g_ref[...] = _bdot_nt(h, w_rows(1))
    kb_ref[...] = _bdot_nt(h, w_rows(2)).astype(kb_ref.dtype)
    z_ref[...] = _bdot_nt(h, w_rows(3))
    xbc_ref[...] = _bdot_nt(h, w_rows(4))
    qbt_ref[...] = (_bdot_nt(w_rows(5), h) * scale).astype(qbt_ref.dtype)
    qat = _rope_t(_bdot_nt(w_rows(7), h), cost_ref[...], sint_ref[...])
    qat_ref[...] = (qat * scale).astype(qat_ref.dtype)
    vat_ref[...] = _bdot_nt(w_rows(8), h).astype(vat_ref.dtype)
    vbt = _bdot_nt(w_rows(6), h).astype(vbt_ref.dtype)
    for hp in range(FOX_W // LANES):
        for c in range(vbt.shape[1] // FOX_TK):
            vbt_ref[hp, c] = vbt[hp * LANES:(hp + 1) * LANES, c * FOX_TK:(c + 1) * FOX_TK]


def _arrange_w_in(w_in, fox_order):
    wt = w_in.T.astype(jnp.bfloat16)

    def rows(o, n):
        return wt[o:o + n]

    def fox_rows(o):
        return rows(o, FOX_W).reshape(FOX_HEADS, HEAD_DIM, -1)[fox_order].reshape(FOX_W, -1)
    pad = jnp.zeros((LANES - FOX_HEADS - SSM_HEADS, wt.shape[1]), wt.dtype)
    parts = [rows(_O_KA, SWA_KV_W), rows(_O_FB, FOX_HEADS)[fox_order], rows(_O_DT, SSM_HEADS), pad,
             rows(_O_GA, SWA_W), fox_rows(_O_GB),
             fox_rows(_O_KB),
             rows(_O_ZC, SSM_W), rows(_O_XBC, SSM_CONV_W),
             fox_rows(_O_QB), fox_rows(_O_VB), rows(_O_QA, SWA_W), rows(_O_VA, SWA_KV_W)]
    return jnp.concatenate(parts, axis=0)


def _inproj(x2, norm_w, w_t, rope, seq, tm):
    t = x2.shape[0]
    nblk_s = seq // tm
    row = lambda i: (i, 0)
    const = lambda i: (0, 0)
    pos_t = lambda i: (0, i % nblk_s)
    col = lambda i: (0, i)
    bf, f32 = jnp.bfloat16, jnp.float32
    outs = [(SWA_KV_W, bf), (2 * SWA_W, f32), (FOX_W, bf), (SSM_W, f32), (SSM_CONV_W, f32), (LANES, f32)]
    npair = FOX_W // LANES
    return pl.pallas_call(
        _inproj_kernel,
        grid=(t // tm,),
        in_specs=[pl.BlockSpec((tm, D_MODEL), row),
                  pl.BlockSpec((1, D_MODEL), const),
                  pl.BlockSpec((_PROJ_W, D_MODEL), const, pipeline_mode=pl.Buffered(1)),
                  pl.BlockSpec((LANES, tm), pos_t),
                  pl.BlockSpec((LANES, tm), pos_t)],
        out_specs=[pl.BlockSpec((tm, w), row) for w, _ in outs]
        + [pl.BlockSpec((FOX_W, tm), col),
           pl.BlockSpec((npair, tm // FOX_TK, LANES, FOX_TK), lambda i: (0, i, 0, 0)),
           pl.BlockSpec((SWA_W, tm), col),
           pl.BlockSpec((SWA_KV_W, tm), col)],
        out_shape=[jax.ShapeDtypeStruct((t, w), d) for w, d in outs]
        + [jax.ShapeDtypeStruct((FOX_W, t), bf),
           jax.ShapeDtypeStruct((npair, t // FOX_TK, LANES, FOX_TK), bf),
           jax.ShapeDtypeStruct((SWA_W, t), bf),
           jax.ShapeDtypeStruct((SWA_KV_W, t), bf)],
        compiler_params=pltpu.CompilerParams(dimension_semantics=("arbitrary",),
                                             vmem_limit_bytes=_VMEM_LIMIT),
        name="inproj",
    )(x2, norm_w.reshape(1, D_MODEL), w_t, *rope)


def _rope_tables(seq):
    pos = jnp.arange(seq, dtype=jnp.float32)
    inv = 1.0 / (ROPE_THETA ** (jnp.arange(0, HEAD_DIM, 2, dtype=jnp.float32) / HEAD_DIM))
    ang = inv[:, None] * pos[None, :]
    cos, sin = jnp.cos(ang), jnp.sin(ang)
    cos_t = jnp.concatenate([cos, cos, cos, cos], axis=0)
    sin_t = jnp.concatenate([-sin, sin, -sin, sin], axis=0)
    return cos_t, sin_t


def _swa_kernel(sink_ref, qt_ref, kc_ref, kp_ref, vc_ref, vp_ref, g_ref, o_ref):
    n = pl.program_id(1)
    nsub = SWA_TQ // BLOCK
    key = lax.broadcasted_iota(jnp.int32, (2 * BLOCK, BLOCK), 0)
    qry = lax.broadcasted_iota(jnp.int32, (2 * BLOCK, BLOCK), 1)
    band = jnp.where((key > qry) & (key <= qry + BLOCK), 0.0, NEG_INF)
    band_first = jnp.where(key < BLOCK, NEG_INF, band)
    row = lax.broadcasted_iota(jnp.int32, (LANES, BLOCK), 0)
    ones_rows = jnp.ones((16, 2 * BLOCK), jnp.bfloat16)

    def scores(u):
        if u == 0:
            kcat = jnp.concatenate([kp_ref[...], kc_ref[0:BLOCK, :]], axis=0)
            vcat = jnp.concatenate([vp_ref[...], vc_ref[:, 0:BLOCK]], axis=1)
            bias = jnp.where(n > 0, band, band_first)
        else:
            kcat = kc_ref[(u - 1) * BLOCK:(u + 1) * BLOCK, :]
            vcat = vc_ref[:, (u - 1) * BLOCK:(u + 1) * BLOCK]
            bias = band
        tiles = []
        for c in range(SWA_HEADS // 2):
            qt = qt_ref[c * LANES:(c + 1) * LANES, u * BLOCK:(u + 1) * BLOCK].astype(jnp.float32)
            for half in range(2):
                kv = (2 * c + half) // (SWA_HEADS // SWA_KV_HEADS)
                q_rows = qt if kv == half else pltpu.roll(qt, HEAD_DIM, 0)
                in_half = (row < HEAD_DIM) if kv == 0 else (row >= HEAD_DIM)
                w = jnp.where(in_half, q_rows, 0.0).astype(jnp.bfloat16)
                tiles.append(_bdot(kcat, w) + bias)
        return tiles, vcat

    def finish(u, tiles, vcat):
        probs, sink_terms = [], []
        for head, s in enumerate(tiles):
            sink = sink_ref[0, head] * LOG2E
            m = jnp.maximum(jnp.max(s, axis=0, keepdims=True), sink)
            probs.append(jnp.exp2(s - m).astype(jnp.bfloat16))
            sink_terms.append(jnp.exp2(sink - m))
        outs = []
        for head, p in enumerate(probs):
            kv = head // (SWA_HEADS // SWA_KV_HEADS)
            lhs = jnp.concatenate([vcat[kv * HEAD_DIM:(kv + 1) * HEAD_DIM, :], ones_rows], axis=0)
            pv = _bdot(lhs, p)
            outs.append(pv[0:HEAD_DIM, :] / (pv[HEAD_DIM:HEAD_DIM + 1, :] + sink_terms[head]))
        y = jnp.concatenate(outs, axis=0).T
        rows = slice(u * BLOCK, (u + 1) * BLOCK)
        o_ref[rows, :] = (y * _silu(g_ref[rows, :])).astype(o_ref.dtype)

    pending = scores(0)
    for u in range(nsub):
        nxt = scores(u + 1) if u + 1 < nsub else None
        finish(u, *pending)
        pending = nxt


def _swa(qat, ka, vat, g, sinks, batch, seq):
    n = seq // SWA_TQ
    nsub = SWA_TQ // BLOCK
    cur = lambda b, i: (b * n + i, 0)
    cur_t = lambda b, i: (0, b * n + i)
    prev = lambda b, i: ((b * n + i) * nsub - jnp.minimum(i, 1), 0)
    prev_t = lambda b, i: (0, (b * n + i) * nsub - jnp.minimum(i, 1))
    return pl.pallas_call(
        _swa_kernel,
        grid=(batch, n),
        in_specs=[pl.BlockSpec(memory_space=pltpu.SMEM),
                  pl.BlockSpec((SWA_W, SWA_TQ), cur_t),
                  pl.BlockSpec((SWA_TQ, SWA_KV_W), cur),
                  pl.BlockSpec((BLOCK, SWA_KV_W), prev),
                  pl.BlockSpec((SWA_KV_W, SWA_TQ), cur_t),
                  pl.BlockSpec((SWA_KV_W, BLOCK), prev_t),
                  pl.BlockSpec((SWA_TQ, SWA_W), cur)],
        out_specs=pl.BlockSpec((SWA_TQ, SWA_W), cur),
        out_shape=jax.ShapeDtypeStruct((batch * seq, SWA_W), jnp.bfloat16),
        compiler_params=pltpu.CompilerParams(dimension_semantics=("arbitrary", "arbitrary")),
        name="swa",
    )(sinks.reshape(1, SWA_HEADS).astype(jnp.float32), qat, ka, ka, vat, vat, g)


def _fox_kernel(cend_ref, qt_ref, k_ref, ca_ref, vt_ref, g_ref, o_ref, w_buf, s_buf0, s_buf1, acc_buf, bias_buf,
                js_ref):
    tq, tk = FOX_TQ, FOX_TK
    per_q = tq // tk
    nq = qt_ref.shape[1] // tq
    all_masked = per_q
    acc_rows = acc_buf.shape[1]

    krow = lax.broadcasted_iota(jnp.int32, (tk, tq), 0)
    qcol = lax.broadcasted_iota(jnp.int32, (tk, tq), 1)
    for d in range(per_q):
        bias_buf[d] = jnp.where(krow + d * tk <= qcol, 0.0, NEG_INF)
    bias_buf[all_masked] = jnp.full((tk, tq), NEG_INF, jnp.float32)
    for buf in (s_buf0, s_buf1, acc_buf):
        buf[...] = jnp.zeros_like(buf)

    row = lax.broadcasted_iota(jnp.int32, (LANES, tq), 0)
    nsplit = 3
    pick = (jnp.where(row < nsplit, 1.0, 0.0).astype(jnp.bfloat16),
            jnp.where((row >= HEAD_DIM) & (row < HEAD_DIM + nsplit), 1.0, 0.0).astype(jnp.bfloat16))
    ones_rows = jnp.ones((acc_rows - HEAD_DIM, tk), jnp.bfloat16)

    qsq = jnp.square(qt_ref[...].astype(jnp.float32))
    ksq = jnp.square(k_ref[...].astype(jnp.float32))
    lane_sel = lax.broadcasted_iota(jnp.int32, (LANES, LANES), 0) // HEAD_DIM == \
        lax.broadcasted_iota(jnp.int32, (LANES, LANES), 1)
    kn2 = jnp.max(_bdot(ksq.astype(jnp.bfloat16), jnp.where(lane_sel, 1.0, 0.0).astype(jnp.bfloat16)),
                  axis=0, keepdims=True)
    bound = []
    for hh in range(2):
        qn2 = jnp.max(jnp.sum(qsq[hh * HEAD_DIM:(hh + 1) * HEAD_DIM, :], axis=0, keepdims=True),
                      axis=1, keepdims=True)
        bound.append((FOX_NORM_MARGIN * 2.0 * jnp.sqrt(qn2 * kn2[:, hh:hh + 1]))[0, 0])
    head0 = (pl.program_id(0) * (FOX_HEADS // 2) + pl.program_id(1)) * 2
    n_steps = jnp.int32(0)
    for i in range(nq):
        count, prefix = jnp.int32(0), jnp.bool_(True)
        for j in range(per_q * i):
            for hh in range(2):
                drop = cend_ref[head0 + hh, j] - cend_ref[head0 + hh, per_q * i - 1]
                prefix = prefix & (bound[hh] - drop < FOX_SKIP_BELOW)
            count = count + prefix.astype(jnp.int32)
        js_ref[i] = count
        n_steps = n_steps + (per_q * (i + 1) - count)

    def step(carry, s_cur, s_prv, masked):
        (i_a, j_a), (i_b, j_b, v_b, first), smax, m = carry
        first_a = j_a == js_ref[jnp.minimum(i_a, nq - 1)]

        @pl.when(first_a)
        def _():
            q0 = pl.multiple_of(jnp.minimum(i_a, nq - 1) * tq, tq)
            qt = qt_ref[:, pl.ds(q0, tq)].astype(jnp.float32)
            q_lo = jnp.where(row < HEAD_DIM, qt, 0.0).astype(jnp.bfloat16)
            q_hi = jnp.where(row < HEAD_DIM, 0.0, qt).astype(jnp.bfloat16)
            w_buf[:, 0:tq] = jnp.concatenate([q_lo, pick[0]], axis=0)
            w_buf[:, tq:2 * tq] = jnp.concatenate([q_hi, pick[1]], axis=0)

        k0 = pl.multiple_of(jnp.where(i_a < nq, j_a, 0) * tk, tk)
        ka = jnp.concatenate([k_ref[pl.ds(k0, tk), :], ca_ref[pl.ds(k0, tk), :]], axis=1)
        if masked:
            bias_slot = jnp.where(i_a < nq, j_a - per_q * i_a, all_masked)
        vt = vt_ref[0, j_b]
        n_chunks = 2 * tq // FOX_CHUNK
        smax_new, m_new = [], []

        def chunk_cols(ch):
            hh, start = divmod(ch * FOX_CHUNK, tq)
            return slice(ch * FOX_CHUNK, (ch + 1) * FOX_CHUNK), hh, slice(start, start + FOX_CHUNK)

        def stage_a(ch):
            cols, _, hcols = chunk_cols(ch)
            s_new = _bdot(ka, w_buf[:, cols])
            if masked:
                s_new = s_new + bias_buf[bias_slot, :, hcols]
            s_cur[:, cols] = s_new
            smax_new.append(jnp.max(s_new, axis=0, keepdims=True))

        def stage_b(ch):
            cols, hh, hcols = chunk_cols(ch)
            m_prev = jnp.where(first, NEG_INF, m[ch])
            mn = jnp.maximum(m_prev, smax[ch])
            p = jnp.exp2(s_prv[:, cols] - mn).astype(jnp.bfloat16)
            lhs = jnp.concatenate([vt[hh * HEAD_DIM:(hh + 1) * HEAD_DIM, :], ones_rows], axis=0)
            acc_buf[hh, :, hcols] = jnp.exp2(m_prev - mn) * acc_buf[hh, :, hcols] + _bdot(lhs, p)
            m_new.append(mn)

        stage_a(0)
        for ch in range(n_chunks):
            if ch + 1 < n_chunks:
                stage_a(ch + 1)
            stage_b(ch)
        smax_new = tuple(smax_new)

        @pl.when(v_b & (j_b == per_q * (i_b + 1) - 1))
        def _():
            q0 = pl.multiple_of(i_b * tq, tq)
            out_t = jnp.concatenate([acc_buf[hh, 0:HEAD_DIM, :] / acc_buf[hh, HEAD_DIM:HEAD_DIM + 1, :]
                                     for hh in range(2)], axis=0)
            y = out_t.T * _silu(g_ref[pl.ds(q0, tq), :])
            o_ref[pl.ds(q0, tq), :] = y.astype(o_ref.dtype)

        last_a = j_a == per_q * (i_a + 1) - 1
        nxt_a = (jnp.where(last_a, i_a + 1, i_a),
                 jnp.where(last_a, js_ref[jnp.minimum(i_a + 1, nq - 1)], j_a + 1))
        return (nxt_a, (i_a, jnp.where(i_a < nq, j_a, 0), i_a < nq, first_a), smax_new, tuple(m_new))

    def either(carry, bufs):
        i_a, j_a = carry[0]
        needs_mask = (j_a >= per_q * i_a) | (i_a >= nq)
        return lax.cond(needs_mask, lambda c: step(c, *bufs, True), lambda c: step(c, *bufs, False), carry)

    def body(_, carry):
        carry = either(carry, (s_buf0, s_buf1))
        return either(carry, (s_buf1, s_buf0))

    zi = jnp.int32(0)
    row_vec = lambda v: tuple(jnp.full((1, FOX_CHUNK), v, jnp.float32) for _ in range(2 * tq // FOX_CHUNK))
    init = ((zi, zi), (zi, zi, False, False), row_vec(0.0), row_vec(0.0))
    lax.fori_loop(0, (n_steps + 2) // 2, body, init)


def _fox(cend_tiles, qbt, kb, caug, vbt, g, batch, seq):
    npair = FOX_HEADS // 2
    nkt = seq // FOX_TK
    gcol0 = SWA_W // LANES
    return pl.pallas_call(
        _fox_kernel,
        grid=(batch, npair),
        in_specs=[pl.BlockSpec(memory_space=pltpu.SMEM),
                  pl.BlockSpec((LANES, seq), lambda b, h: (h, b)),
                  pl.BlockSpec((seq, LANES), lambda b, h: (b, h)),
                  pl.BlockSpec((seq, LANES), lambda b, h: (b, h)),
                  pl.BlockSpec((1, nkt, LANES, FOX_TK), lambda b, h: (h, b, 0, 0)),
                  pl.BlockSpec((seq, LANES), lambda b, h: (b, gcol0 + h))],
        out_specs=pl.BlockSpec((seq, LANES), lambda b, h: (b, h)),
        out_shape=jax.ShapeDtypeStruct((batch * seq, FOX_W), jnp.bfloat16),
        scratch_shapes=[pltpu.VMEM((2 * LANES, 2 * FOX_TQ), jnp.bfloat16),
                        pltpu.VMEM((FOX_TK, 2 * FOX_TQ), jnp.float32),
                        pltpu.VMEM((FOX_TK, 2 * FOX_TQ), jnp.float32),
                        pltpu.VMEM((2, HEAD_DIM + 16, FOX_TQ), jnp.float32),
                        pltpu.VMEM((FOX_TQ // FOX_TK + 1, FOX_TK, FOX_TQ), jnp.float32),
                        pltpu.SMEM((seq // FOX_TQ,), jnp.int32)],
        compiler_params=pltpu.CompilerParams(dimension_semantics=("arbitrary", "arbitrary"),
                                             vmem_limit_bytes=_VMEM_LIMIT),
        name="fox",
    )(cend_tiles, qbt, kb, caug, vbt, g)


def _split3(x):
    hi = x.astype(jnp.bfloat16)
    r1 = x - hi.astype(jnp.float32)
    mid = r1.astype(jnp.bfloat16)
    lo = (r1 - mid.astype(jnp.float32)).astype(jnp.bfloat16)
    return hi, mid, lo


def _ssd_kernel(xbc_ref, z_ref, fdt_ref, cw_ref, cb_ref, bias_ref, alog_ref, dskip_ref, nw_ref,
                y_ref, ca_ref, cend_ref, state_ref, tail_ref, ccarry_ref):
    @pl.when(pl.program_id(1) == 0)
    def _():
        state_ref[...] = jnp.zeros_like(state_ref)
        tail_ref[...] = jnp.zeros_like(tail_ref)
        ccarry_ref[...] = jnp.zeros_like(ccarry_ref)

    for sub in range(xbc_ref.shape[0] // BLOCK):
        _ssd_chunk(sub, xbc_ref, z_ref, fdt_ref, cw_ref, cb_ref, bias_ref, alog_ref, dskip_ref, nw_ref,
                   y_ref, ca_ref, cend_ref, state_ref, tail_ref, ccarry_ref)


def _ssd_chunk(sub, xbc_ref, z_ref, fdt_ref, cw_ref, cb_ref, bias_ref, alog_ref, dskip_ref, nw_ref,
               y_ref, ca_ref, cend_ref, state_ref, tail_ref, ccarry_ref):
    L = BLOCK
    rows = slice(sub * L, (sub + 1) * L)
    u = xbc_ref[rows, :]
    tail_ref[8:8 + L, :] = u
    conv = cb_ref[...] + cw_ref[SSM_CONV - 1:SSM_CONV, :] * u
    for k in range(1, SSM_CONV):
        conv = conv + cw_ref[SSM_CONV - 1 - k:SSM_CONV - k, :] * tail_ref[8 - k:8 - k + L, :]
    tail_ref[0:8, :] = u[L - 8:L]
    act = _silu(conv)
    xs = act[:, :SSM_W]
    bm_f = act[:, SSM_W:SSM_W + SSM_BC_W]
    bm = bm_f.astype(jnp.bfloat16)
    cm = act[:, SSM_W + SSM_BC_W:].astype(jnp.bfloat16)

    lane = lax.broadcasted_iota(jnp.int32, (L, LANES), 1)
    vals = fdt_ref[rows, :] + bias_ref[...]
    is_f = lane < DT_LANE0
    is_dt = (lane >= DT_LANE0) & (lane < DT_LANE0 + SSM_HEADS)
    sp = _softplus(jnp.where(is_f, -vals, vals))
    a_row = jnp.where(is_dt[0:1], -jnp.exp(alog_ref[...]), 0.0)
    dt = jnp.where(is_dt, sp, 0.0)
    scan_in = jnp.where(is_f, -sp, dt * a_row) * LOG2E

    ri = lax.broadcasted_iota(jnp.int32, (L, L), 0)
    cj = lax.broadcasted_iota(jnp.int32, (L, L), 1)
    causal = cj <= ri
    tri = jnp.where(causal, 1.0, 0.0).astype(jnp.bfloat16)
    hi, mid, lo = _split3(scan_in)
    cs = _bdot(tri, hi) + _bdot(tri, mid) + _bdot(tri, lo)
    cs_t = cs.T

    c_full = cs + ccarry_ref[...]
    ccarry_ref[...] = jnp.where(is_f[0:1], c_full[L - 1:L, :], 0.0)
    cend_ref[sub] = c_full[L - 1:L, :]
    lane64 = lane % HEAD_DIM
    ca_cols = []
    for hp in range(FOX_HEADS // 2):
        negc = -jnp.where(lane < HEAD_DIM, c_full[:, 2 * hp:2 * hp + 1], c_full[:, 2 * hp + 1:2 * hp + 2])
        hi, mid, lo = (term.astype(jnp.float32) for term in _split3(negc))
        terms = jnp.where(lane64 == 0, hi, jnp.where(lane64 == 1, mid, jnp.where(lane64 == 2, lo, 0.0)))
        ca_cols.append(terms.astype(jnp.bfloat16))
    ca_ref[rows, :] = jnp.concatenate(ca_cols, axis=1)

    low_l = lane < HEAD_DIM
    hpg = SSM_HEADS // 2
    npairs = SSM_HEADS // 2
    grp_of = [(2 * pair) // hpg for pair in range(npairs)]
    bm_g = [bm[:, g * SSM_STATE:(g + 1) * SSM_STATE] for g in range(2)]
    cm_g = [cm[:, g * SSM_STATE:(g + 1) * SSM_STATE] for g in range(2)]
    bm_t = [bm_f[:, g * SSM_STATE:(g + 1) * SSM_STATE].T.astype(jnp.bfloat16) for g in range(2)]
    cb = [_bdot_nt(cm_g[g], bm_g[g]) for g in range(2)]
    ys = []
    for pair in range(npairs):
        g = grp_of[pair]
        ha, hb = DT_LANE0 + 2 * pair, DT_LANE0 + 2 * pair + 1
        col = (cs[:, ha:ha + 1], cs[:, hb:hb + 1])
        acs_row = (cs_t[ha:ha + 1, :], cs_t[hb:hb + 1, :])
        xdt = xs[:, pair * LANES:(pair + 1) * LANES] * jnp.where(low_l, dt[:, ha:ha + 1], dt[:, hb:hb + 1])
        xdt_b = xdt.astype(jnp.bfloat16)
        prev = state_ref[pair]
        y_off = _bdot(cm_g[g], prev.astype(jnp.bfloat16))
        last = (col[0][L - 1:L, :], col[1][L - 1:L, :])
        dst = jnp.exp2(jnp.where(low_l, last[0] - col[0], last[1] - col[1]))
        st_new = _bdot(bm_t[g], (xdt * dst).astype(jnp.bfloat16))
        state_ref[pair] = prev * jnp.exp2(jnp.where(low_l, last[0], last[1])) + st_new
        y_diag = []
        for hh in range(2):
            decay = jnp.exp2(jnp.where(causal, col[hh] - acs_row[hh], NEG_INF))
            gmat = (cb[g] * decay).astype(jnp.bfloat16)
            y_diag.append(_bdot(gmat, xdt_b))
        ys.append(jnp.where(low_l, y_diag[0], y_diag[1]) + y_off * jnp.exp2(jnp.where(low_l, col[0], col[1])))
    y = jnp.concatenate(ys, axis=1) + dskip_ref[...] * xs
    y_ref[rows, :] = _rms(y * _silu(z_ref[rows, :]), nw_ref[...]).astype(y_ref.dtype)


def _ssd(xbc, z, fdt, conv_w, conv_b, b_forget, dt_bias, a_log, d_skip, norm_w, batch, seq):
    rows = min(SSD_ROWS, seq)
    per_step = rows // BLOCK
    nc = seq // rows
    row = lambda b, c: (b * nc + c, 0)
    const = lambda b, c: (0, 0)
    zpad = jnp.zeros((LANES - FOX_HEADS - SSM_HEADS,), jnp.float32)
    bias_row = jnp.concatenate([b_forget, dt_bias, zpad]).reshape(1, LANES)
    alog_row = jnp.concatenate([jnp.zeros((FOX_HEADS,), jnp.float32), a_log, zpad]).reshape(1, LANES)
    dskip_row = jnp.repeat(d_skip, HEAD_DIM).reshape(1, SSM_W)
    return pl.pallas_call(
        _ssd_kernel,
        grid=(batch, nc),
        in_specs=[pl.BlockSpec((rows, SSM_CONV_W), row),
                  pl.BlockSpec((rows, SSM_W), row),
                  pl.BlockSpec((rows, LANES), row),
                  pl.BlockSpec((SSM_CONV, SSM_CONV_W), const),
                  pl.BlockSpec((1, SSM_CONV_W), const),
                  pl.BlockSpec((1, LANES), const),
                  pl.BlockSpec((1, LANES), const),
                  pl.BlockSpec((1, SSM_W), const),
                  pl.BlockSpec((1, SSM_W), const)],
        out_specs=[pl.BlockSpec((rows, SSM_W), row),
                   pl.BlockSpec((rows, FOX_W), row),
                   pl.BlockSpec((per_step, 1, LANES), lambda b, c: (b * nc + c, 0, 0))],
        out_shape=[jax.ShapeDtypeStruct((batch * seq, SSM_W), jnp.bfloat16),
                   jax.ShapeDtypeStruct((batch * seq, FOX_W), jnp.bfloat16),
                   jax.ShapeDtypeStruct((batch * nc * per_step, 1, LANES), jnp.float32)],
        scratch_shapes=[pltpu.VMEM((SSM_HEADS // 2, 2 * HEAD_DIM, SSM_STATE), jnp.float32),
                        pltpu.VMEM((8 + BLOCK, SSM_CONV_W), jnp.float32),
                        pltpu.VMEM((1, LANES), jnp.float32)],
        compiler_params=pltpu.CompilerParams(dimension_semantics=("arbitrary", "arbitrary")),
        name="ssd",
    )(xbc, z, fdt, conv_w, conv_b.reshape(1, SSM_CONV_W), bias_row, alog_row, dskip_row,
      norm_w.reshape(1, SSM_W))


def _memkv_kernel(mem_ref, nw_ref, wk_ref, wv_ref, k_ref, v_ref):
    mn = _rms(mem_ref[...], nw_ref[...]).astype(jnp.bfloat16)
    k_ref[...] = _bdot(mn, wk_ref[...]).astype(k_ref.dtype)
    v_ref[...] = _bdot(mn, wv_ref[...]).astype(v_ref.dtype)


def _memkv(mem2, norm_w, wk, wv, mem_tokens):
    t = mem2.shape[0]
    row = lambda b: (b, 0)
    const = lambda b: (0, 0)
    return pl.pallas_call(
        _memkv_kernel,
        grid=(t // mem_tokens,),
        in_specs=[pl.BlockSpec((mem_tokens, D_MODEL), row),
                  pl.BlockSpec((1, D_MODEL), const),
                  pl.BlockSpec((D_MODEL, D_MODEL), const),
                  pl.BlockSpec((D_MODEL, D_MODEL), const)],
        out_specs=[pl.BlockSpec((mem_tokens, D_MODEL), row)] * 2,
        out_shape=[jax.ShapeDtypeStruct((t, D_MODEL), jnp.bfloat16)] * 2,
        compiler_params=pltpu.CompilerParams(dimension_semantics=("arbitrary",)),
        name="memkv",
    )(mem2, norm_w.reshape(1, D_MODEL), wk.astype(jnp.bfloat16), wv.astype(jnp.bfloat16))


def _out_kernel(x_ref, ya_ref, yb_ref, yc_ref, wo_ref, nq_ref, wq_ref, k_ref, v_ref, wmo_ref, fn_ref,
                o_ref, *, final_norm):
    tm = x_ref.shape[0]
    groups = [slice(r, r + OUT_ROWS) for r in range(0, tm, OUT_ROWS)]
    head_cols = [slice(h * MEM_HEAD_DIM, (h + 1) * MEM_HEAD_DIM) for h in range(MEM_HEADS)]
    x1 = [x_ref[g, :]
          + _bdot(ya_ref[g, :], wo_ref[0:SWA_W, :])
          + _bdot(yb_ref[g, :], wo_ref[SWA_W:SWA_W + FOX_W, :])
          + _bdot(yc_ref[g, :], wo_ref[SWA_W + FOX_W:, :]) for g in groups]
    q = []
    for x1g in x1:
        hq = _rms(x1g, nq_ref[...]).astype(jnp.bfloat16)
        q.append((_bdot(hq, wq_ref[...]) * (MEM_HEAD_DIM ** -0.5 * LOG2E)).astype(jnp.bfloat16))
    scores = [[_bdot_nt(qg[:, sl], k_ref[:, sl]) for sl in head_cols] for qg in q]
    att = []
    for sg in scores:
        heads = []
        for s, sl in zip(sg, head_cols):
            p = jnp.exp2(s - jnp.max(s, axis=-1, keepdims=True))
            probs = (p / jnp.sum(p, axis=-1, keepdims=True)).astype(jnp.bfloat16)
            heads.append(_bdot(probs, v_ref[:, sl]).astype(jnp.bfloat16))
        att.append(jnp.concatenate(heads, axis=1))
    for g, x1g, attg in zip(groups, x1, att):
        x2 = x1g + _bdot(attg, wmo_ref[...])
        if final_norm:
            x2 = _rms(x2, fn_ref[...])
        o_ref[g, :] = x2


def _out_block(x2, ya, yb, yc, w_out, norm_xq_w, w_mq, kmem, vmem, w_mo, final_w, seq, mem_tokens, tm,
               final_norm):
    t = x2.shape[0]
    nblk_s = seq // tm
    row = lambda i: (i, 0)
    const = lambda i: (0, 0)
    memrow = lambda i: (i // nblk_s, 0)
    bf = jnp.bfloat16
    return pl.pallas_call(
        functools.partial(_out_kernel, final_norm=final_norm),
        grid=(t // tm,),
        in_specs=[pl.BlockSpec((tm, D_MODEL), row),
                  pl.BlockSpec((tm, SWA_W), row),
                  pl.BlockSpec((tm, FOX_W), row),
                  pl.BlockSpec((tm, SSM_W), row),
                  pl.BlockSpec((SWA_W + FOX_W + SSM_W, D_MODEL), const, pipeline_mode=pl.Buffered(1)),
                  pl.BlockSpec((1, D_MODEL), const),
                  pl.BlockSpec((D_MODEL, D_MODEL), const, pipeline_mode=pl.Buffered(1)),
                  pl.BlockSpec((mem_tokens, D_MODEL), memrow),
                  pl.BlockSpec((mem_tokens, D_MODEL), memrow),
                  pl.BlockSpec((D_MODEL, D_MODEL), const, pipeline_mode=pl.Buffered(1)),
                  pl.BlockSpec((1, D_MODEL), const)],
        out_specs=pl.BlockSpec((tm, D_MODEL), row),
        out_shape=jax.ShapeDtypeStruct((t, D_MODEL), jnp.float32),
        compiler_params=pltpu.CompilerParams(dimension_semantics=("arbitrary",),
                                             vmem_limit_bytes=_VMEM_LIMIT),
        name="outproj_mem",
    )(x2, ya, yb, yc, w_out.astype(bf), norm_xq_w.reshape(1, D_MODEL), w_mq.astype(bf), kmem, vmem,
      w_mo.astype(bf), final_w.reshape(1, D_MODEL))


def _row_tile(seq):
    return min(512, seq)


def _out_row_tile(seq):
    return min(1024, seq)


def kernel(x, mem, norm_mix_w, w_in, b_forget, swa_sinks, conv_w, conv_b, dt_bias, a_log, d_skip, ssm_norm_w, w_out, norm_xq_w, norm_mem_w, w_mq, w_mk, w_mv, w_mo, final_norm_w):
    batch, seq, _ = x.shape
    mem_tokens = mem.shape[1]
    depth = w_in.shape[0]
    assert seq % BLOCK == 0
    tm = _row_tile(seq)
    assert seq % tm == 0 and seq % FOX_TQ == 0 and seq % SWA_TQ == 0
    rope = _rope_tables(seq)
    x2 = x.reshape(batch * seq, D_MODEL)
    mem2 = mem.reshape(batch * mem_tokens, D_MODEL)
    for l in range(depth):
        fox_order = jnp.argsort(b_forget[l])
        w_t = _arrange_w_in(w_in[l], fox_order)
        ka, g, kb, z, xbc, fdt, qbt, vbt, qat, vat = _inproj(x2, norm_mix_w[l], w_t, rope, seq, tm)
        yc, caug, cend = _ssd(xbc, z, fdt, conv_w[l], conv_b[l], b_forget[l][fox_order], dt_bias[l], a_log[l],
                             d_skip[l], ssm_norm_w[l], batch, seq)
        per_tile = FOX_TK // BLOCK
        cend_tiles = cend.reshape(batch, seq // BLOCK, LANES)[:, per_tile - 1::per_tile, :FOX_HEADS]
        cend_tiles = cend_tiles.transpose(0, 2, 1).reshape(batch * FOX_HEADS, seq // FOX_TK)
        ya = _swa(qat, ka, vat, g, swa_sinks[l], batch, seq)
        yb = _fox(cend_tiles, qbt, kb, caug, vbt, g, batch, seq)
        kmem, vmem = _memkv(mem2, norm_mem_w[l], w_mk[l], w_mv[l], mem_tokens)
        o_b = SWA_W
        w_o = w_out[l].at[o_b:o_b + FOX_W].set(
            w_out[l][o_b:o_b + FOX_W].reshape(FOX_HEADS, HEAD_DIM, D_MODEL)[fox_order].reshape(FOX_W, D_MODEL))
        x2 = _out_block(x2, ya, yb, yc, w_o, norm_xq_w[l], w_mq[l], kmem, vmem, w_mo[l],
                        final_norm_w, seq, mem_tokens, _out_row_tile(seq), final_norm=(l == depth - 1))
    return x2.reshape(batch, seq, D_MODEL)
```

```python
import functools

import jax
import jax.numpy as jnp
from jax import lax
from jax.experimental import pallas as pl
from jax.experimental.pallas import tpu as pltpu

D_MODEL = 1024
HEAD_DIM = 64
BLOCK = 128
SWA_HEADS = 8
SWA_KV_HEADS = 2
FOX_HEADS = 8
SSM_HEADS = 16
SSM_STATE = 128
SSM_CONV = 4
MEM_HEADS = 4
MEM_HEAD_DIM = 256
ROPE_THETA = 10000.0
EPS = 1e-6
NEG_INF = -1e30

SWA_W = SWA_HEADS * HEAD_DIM
SWA_KV_W = SWA_KV_HEADS * HEAD_DIM
FOX_W = FOX_HEADS * HEAD_DIM
SSM_W = SSM_HEADS * HEAD_DIM
SSM_BC_W = 2 * SSM_STATE
SSM_CONV_W = SSM_W + 2 * SSM_BC_W
LANES = 128
LOG2E = 1.4426950408889634
SWA_TQ = 512
FOX_TQ = 1024
FOX_CHUNK = 256
FOX_SKIP_BELOW = -170.0
FOX_NORM_MARGIN = 1.02
FOX_TK = 512
OUT_ROWS = 256
SSD_ROWS = 512
DT_LANE0 = FOX_HEADS

_O_QA = 0
_O_KA = _O_QA + SWA_W
_O_VA = _O_KA + SWA_KV_W
_O_GA = _O_VA + SWA_KV_W
_O_QB = _O_GA + SWA_W
_O_KB = _O_QB + FOX_W
_O_VB = _O_KB + FOX_W
_O_FB = _O_VB + FOX_W
_O_GB = _O_FB + FOX_HEADS
_O_ZC = _O_GB + FOX_W
_O_XBC = _O_ZC + SSM_W
_O_DT = _O_XBC + SSM_CONV_W
_IN_W = _O_DT + SSM_HEADS

_SEG_W = (SWA_KV_W + LANES, 2 * SWA_W, FOX_W, SSM_W, SSM_CONV_W, FOX_W, FOX_W, SWA_W, SWA_KV_W)
_SEG_OFF = tuple(sum(_SEG_W[:i]) for i in range(len(_SEG_W)))
_PROJ_W = sum(_SEG_W)

_VMEM_LIMIT = 56 * 1024 * 1024


def _bdot(a, b):
    return jnp.dot(a, b, preferred_element_type=jnp.float32)


def _bdot_nt(a, b):
    return lax.dot_general(a, b, (((1,), (1,)), ((), ())), preferred_element_type=jnp.float32)


def _silu(x):
    h = 0.5 * x
    return h + h * jnp.tanh(h)


def _softplus(x):
    return jnp.maximum(x, 0.0) + jnp.log(1.0 + jnp.exp(-jnp.abs(x)))


def _rms(x, w):
    return x * lax.rsqrt(jnp.mean(x * x, axis=-1, keepdims=True) + EPS) * w


def _rope(x, cos, sin_signed):
    width = x.shape[1]
    reps = width // LANES
    lane = lax.broadcasted_iota(jnp.int32, x.shape, 1)
    first_half = (lane % HEAD_DIM) < (HEAD_DIM // 2)
    swapped = jnp.where(first_half,
                        pltpu.roll(x, width - HEAD_DIM // 2, 1),
                        pltpu.roll(x, HEAD_DIM // 2, 1))
    cos_t = jnp.concatenate([cos] * reps, axis=1)
    sin_t = jnp.concatenate([sin_signed] * reps, axis=1)
    return x * cos_t + swapped * sin_t


def _rope_t(x, cos, sin_signed):
    rows = x.shape[0]
    reps = rows // LANES
    r = lax.broadcasted_iota(jnp.int32, x.shape, 0)
    first_half = (r % HEAD_DIM) < (HEAD_DIM // 2)
    swapped = jnp.where(first_half,
                        pltpu.roll(x, rows - HEAD_DIM // 2, 0),
                        pltpu.roll(x, HEAD_DIM // 2, 0))
    cos_t = jnp.concatenate([cos] * reps, axis=0)
    sin_t = jnp.concatenate([sin_signed] * reps, axis=0)
    return x * cos_t + swapped * sin_t


def _inproj_kernel(x_ref, nw_ref, wt_ref, cost_ref, sint_ref,
                   ka_ref, g_ref, kb_ref, z_ref, xbc_ref, fdt_ref, qbt_ref, vbt_ref, qat_ref, vat_ref):
    h = _rms(x_ref[...], nw_ref[...]).astype(jnp.bfloat16)

    def w_rows(i):
        return wt_ref[_SEG_OFF[i]:_SEG_OFF[i] + _SEG_W[i], :]

    scale = HEAD_DIM ** -0.5 * LOG2E
    k_f = _bdot_nt(h, w_rows(0))
    ka_ref[...] = _rope(k_f[:, :SWA_KV_W], cost_ref[...].T, sint_ref[...].T).astype(ka_ref.dtype)
    fdt_ref[...] = k_f[:, SWA_KV_W:]
    g_ref[...] = _bdot_nt(h, w_rows(1))
    kb_ref[...] = _bdot_nt(h, w_rows(2)).astype(kb_ref.dtype)
    z_ref[...] = _bdot_nt(h, w_rows(3))
    xbc_ref[...] = _bdot_nt(h, w_rows(4))
    qbt_ref[...] = (_bdot_nt(w_rows(5), h) * scale).astype(qbt_ref.dtype)
    qat = _rope_t(_bdot_nt(w_rows(7), h), cost_ref[...], sint_ref[...])
    qat_ref[...] = (qat * scale).astype(qat_ref.dtype)
    vat_ref[...] = _bdot_nt(w_rows(8), h).astype(vat_ref.dtype)
    vbt = _bdot_nt(w_rows(6), h).astype(vbt_ref.dtype)
    for hp in range(FOX_W // LANES):
        for c in range(vbt.shape[1] // FOX_TK):
            vbt_ref[hp, c] = vbt[hp * LANES:(hp + 1) * LANES, c * FOX_TK:(c + 1) * FOX_TK]


def _arrange_w_in(w_in, fox_order):
    wt = w_in.T.astype(jnp.bfloat16)

    def rows(o, n):
        return wt[o:o + n]

    def fox_rows(o):
        return rows(o, FOX_W).reshape(FOX_HEADS, HEAD_DIM, -1)[fox_order].reshape(FOX_W, -1)
    pad = jnp.zeros((LANES - FOX_HEADS - SSM_HEADS, wt.shape[1]), wt.dtype)
    parts = [rows(_O_KA, SWA_KV_W), rows(_O_FB, FOX_HEADS)[fox_order], rows(_O_DT, SSM_HEADS), pad,
             rows(_O_GA, SWA_W), fox_rows(_O_GB),
             fox_rows(_O_KB),
             rows(_O_ZC, SSM_W), rows(_O_XBC, SSM_CONV_W),
             fox_rows(_O_QB), fox_rows(_O_VB), rows(_O_QA, SWA_W), rows(_O_VA, SWA_KV_W)]
    return jnp.concatenate(parts, axis=0)


def _inproj(x2, norm_w, w_t, rope, seq, tm):
    t = x2.shape[0]
    nblk_s = seq // tm
    row = lambda i: (i, 0)
    const = lambda i: (0, 0)
    pos_t = lambda i: (0, i % nblk_s)
    col = lambda i: (0, i)
    bf, f32 = jnp.bfloat16, jnp.float32
    outs = [(SWA_KV_W, bf), (2 * SWA_W, f32), (FOX_W, bf), (SSM_W, f32), (SSM_CONV_W, f32), (LANES, f32)]
    npair = FOX_W // LANES
    return pl.pallas_call(
        _inproj_kernel,
        grid=(t // tm,),
        in_specs=[pl.BlockSpec((tm, D_MODEL), row),
                  pl.BlockSpec((1, D_MODEL), const),
                  pl.BlockSpec((_PROJ_W, D_MODEL), const, pipeline_mode=pl.Buffered(1)),
                  pl.BlockSpec((LANES, tm), pos_t),
                  pl.BlockSpec((LANES, tm), pos_t)],
        out_specs=[pl.BlockSpec((tm, w), row) for w, _ in outs]
        + [pl.BlockSpec((FOX_W, tm), col),
           pl.BlockSpec((npair, tm // FOX_TK, LANES, FOX_TK), lambda i: (0, i, 0, 0)),
           pl.BlockSpec((SWA_W, tm), col),
           pl.BlockSpec((SWA_KV_W, tm), col)],
        out_shape=[jax.ShapeDtypeStruct((t, w), d) for w, d in outs]
        + [jax.ShapeDtypeStruct((FOX_W, t), bf),
           jax.ShapeDtypeStruct((npair, t // FOX_TK, LANES, FOX_TK), bf),
           jax.ShapeDtypeStruct((SWA_W, t), bf),
           jax.ShapeDtypeStruct((SWA_KV_W, t), bf)],
        compiler_params=pltpu.CompilerParams(dimension_semantics=("arbitrary",),
                                             vmem_limit_bytes=_VMEM_LIMIT),
        name="inproj",
    )(x2, norm_w.reshape(1, D_MODEL), w_t, *rope)


def _rope_tables(seq):
    pos = jnp.arange(seq, dtype=jnp.float32)
    inv = 1.0 / (ROPE_THETA ** (jnp.arange(0, HEAD_DIM, 2, dtype=jnp.float32) / HEAD_DIM))
    ang = inv[:, None] * pos[None, :]
    cos, sin = jnp.cos(ang), jnp.sin(ang)
    cos_t = jnp.concatenate([cos, cos, cos, cos], axis=0)
    sin_t = jnp.concatenate([-sin, sin, -sin, sin], axis=0)
    return cos_t, sin_t


def _swa_kernel(sink_ref, qt_ref, kc_ref, kp_ref, vc_ref, vp_ref, g_ref, o_ref):
    n = pl.program_id(1)
    nsub = SWA_TQ // BLOCK
    key = lax.broadcasted_iota(jnp.int32, (2 * BLOCK, BLOCK), 0)
    qry = lax.broadcasted_iota(jnp.int32, (2 * BLOCK, BLOCK), 1)
    band = jnp.where((key > qry) & (key <= qry + BLOCK), 0.0, NEG_INF)
    band_first = jnp.where(key < BLOCK, NEG_INF, band)
    row = lax.broadcasted_iota(jnp.int32, (LANES, BLOCK), 0)
    ones_rows = jnp.ones((16, 2 * BLOCK), jnp.bfloat16)

    def scores(u):
        if u == 0:
            kcat = jnp.concatenate([kp_ref[...], kc_ref[0:BLOCK, :]], axis=0)
            vcat = jnp.concatenate([vp_ref[...], vc_ref[:, 0:BLOCK]], axis=1)
            bias = jnp.where(n > 0, band, band_first)
        else:
            kcat = kc_ref[(u - 1) * BLOCK:(u + 1) * BLOCK, :]
            vcat = vc_ref[:, (u - 1) * BLOCK:(u + 1) * BLOCK]
            bias = band
        tiles = []
        for c in range(SWA_HEADS // 2):
            qt = qt_ref[c * LANES:(c + 1) * LANES, u * BLOCK:(u + 1) * BLOCK].astype(jnp.float32)
            for half in range(2):
                kv = (2 * c + half) // (SWA_HEADS // SWA_KV_HEADS)
                q_rows = qt if kv == half else pltpu.roll(qt, HEAD_DIM, 0)
                in_half = (row < HEAD_DIM) if kv == 0 else (row >= HEAD_DIM)
                w = jnp.where(in_half, q_rows, 0.0).astype(jnp.bfloat16)
                tiles.append(_bdot(kcat, w) + bias)
        return tiles, vcat

    def finish(u, tiles, vcat):
        probs, sink_terms = [], []
        for head, s in enumerate(tiles):
            sink = sink_ref[0, head] * LOG2E
            m = jnp.maximum(jnp.max(s, axis=0, keepdims=True), sink)
            probs.append(jnp.exp2(s - m).astype(jnp.bfloat16))
            sink_terms.append(jnp.exp2(sink - m))
        outs = []
        for head, p in enumerate(probs):
            kv = head // (SWA_HEADS // SWA_KV_HEADS)
            lhs = jnp.concatenate([vcat[kv * HEAD_DIM:(kv + 1) * HEAD_DIM, :], ones_rows], axis=0)
            pv = _bdot(lhs, p)
            outs.append(pv[0:HEAD_DIM, :] / (pv[HEAD_DIM:HEAD_DIM + 1, :] + sink_terms[head]))
        y = jnp.concatenate(outs, axis=0).T
        rows = slice(u * BLOCK, (u + 1) * BLOCK)
        o_ref[rows, :] = (y * _silu(g_ref[rows, :])).astype(o_ref.dtype)

    pending = scores(0)
    for u in range(nsub):
        nxt = scores(u + 1) if u + 1 < nsub else None
        finish(u, *pending)
        pending = nxt


def _swa(qat, ka, vat, g, sinks, batch, seq):
    n = seq // SWA_TQ
    nsub = SWA_TQ // BLOCK
    cur = lambda b, i: (b * n + i, 0)
    cur_t = lambda b, i: (0, b * n + i)
    prev = lambda b, i: ((b * n + i) * nsub - jnp.minimum(i, 1), 0)
    prev_t = lambda b, i: (0, (b * n + i) * nsub - jnp.minimum(i, 1))
    return pl.pallas_call(
        _swa_kernel,
        grid=(batch, n),
        in_specs=[pl.BlockSpec(memory_space=pltpu.SMEM),
                  pl.BlockSpec((SWA_W, SWA_TQ), cur_t),
                  pl.BlockSpec((SWA_TQ, SWA_KV_W), cur),
                  pl.BlockSpec((BLOCK, SWA_KV_W), prev),
                  pl.BlockSpec((SWA_KV_W, SWA_TQ), cur_t),
                  pl.BlockSpec((SWA_KV_W, BLOCK), prev_t),
                  pl.BlockSpec((SWA_TQ, SWA_W), cur)],
        out_specs=pl.BlockSpec((SWA_TQ, SWA_W), cur),
        out_shape=jax.ShapeDtypeStruct((batch * seq, SWA_W), jnp.bfloat16),
        compiler_params=pltpu.CompilerParams(dimension_semantics=("arbitrary", "arbitrary")),
        name="swa",
    )(sinks.reshape(1, SWA_HEADS).astype(jnp.float32), qat, ka, ka, vat, vat, g)


def _fox_kernel(cend_ref, qt_ref, k_ref, ca_ref, vt_ref, g_ref, o_ref, w_buf, s_buf0, s_buf1, acc_buf, bias_buf,
                js_ref):
    tq, tk = FOX_TQ, FOX_TK
    per_q = tq // tk
    nq = qt_ref.shape[1] // tq
    acc_rows = acc_buf.shape[1]

    krow = lax.broadcasted_iota(jnp.int32, (tk, tq), 0)
    qcol = lax.broadcasted_iota(jnp.int32, (tk, tq), 1)
    for d in range(per_q):
        bias_buf[d] = jnp.where(krow + d * tk <= qcol, 0.0, NEG_INF)
    for buf in (s_buf0, s_buf1, acc_buf):
        buf[...] = jnp.zeros_like(buf)

    row = lax.broadcasted_iota(jnp.int32, (LANES, tq), 0)
    nsplit = 3
    pick = (jnp.where(row < nsplit, 1.0, 0.0).astype(jnp.bfloat16),
            jnp.where((row >= HEAD_DIM) & (row < HEAD_DIM + nsplit), 1.0, 0.0).astype(jnp.bfloat16))
    ones_rows = jnp.ones((acc_rows - HEAD_DIM, tk), jnp.bfloat16)

    qsq = jnp.square(qt_ref[...].astype(jnp.float32))
    ksq = jnp.square(k_ref[...].astype(jnp.float32))
    lane_sel = lax.broadcasted_iota(jnp.int32, (LANES, LANES), 0) // HEAD_DIM == \
        lax.broadcasted_iota(jnp.int32, (LANES, LANES), 1)
    kn2 = jnp.max(_bdot(ksq.astype(jnp.bfloat16), jnp.where(lane_sel, 1.0, 0.0).astype(jnp.bfloat16)),
                  axis=0, keepdims=True)
    bound = []
    for hh in range(2):
        qn2 = jnp.max(jnp.sum(qsq[hh * HEAD_DIM:(hh + 1) * HEAD_DIM, :], axis=0, keepdims=True),
                      axis=1, keepdims=True)
        bound.append((FOX_NORM_MARGIN * 2.0 * jnp.sqrt(qn2 * kn2[:, hh:hh + 1]))[0, 0])
    head0 = (pl.program_id(0) * (FOX_HEADS // 2) + pl.program_id(1)) * 2
    n_steps = jnp.int32(0)
    for i in range(nq):
        count, prefix = jnp.int32(0), jnp.bool_(True)
        for j in range(per_q * i):
            for hh in range(2):
                drop = cend_ref[head0 + hh, j] - cend_ref[head0 + hh, per_q * i - 1]
                prefix = prefix & (bound[hh] - drop < FOX_SKIP_BELOW)
            count = count + prefix.astype(jnp.int32)
        js_ref[i] = count
        n_steps = n_steps + (per_q * (i + 1) - count)

    def step(carry, s_cur, s_prv, diag):
        (i_a, j_a), (i_b, j_b, v_b, first), smax, m = carry
        first_a = j_a == js_ref[jnp.minimum(i_a, nq - 1)]

        @pl.when(first_a)
        def _():
            q0 = pl.multiple_of(jnp.minimum(i_a, nq - 1) * tq, tq)
            qt = qt_ref[:, pl.ds(q0, tq)].astype(jnp.float32)
            q_lo = jnp.where(row < HEAD_DIM, qt, 0.0).astype(jnp.bfloat16)
            q_hi = jnp.where(row < HEAD_DIM, 0.0, qt).astype(jnp.bfloat16)
            w_buf[:, 0:tq] = jnp.concatenate([q_lo, pick[0]], axis=0)
            w_buf[:, tq:2 * tq] = jnp.concatenate([q_hi, pick[1]], axis=0)

        k0 = pl.multiple_of(jnp.where(i_a < nq, j_a, 0) * tk, tk)
        ka = jnp.concatenate([k_ref[pl.ds(k0, tk), :], ca_ref[pl.ds(k0, tk), :]], axis=1)
        vt = vt_ref[0, j_b]
        n_chunks = 2 * tq // FOX_CHUNK
        smax_new, m_new = [], []

        def chunk_cols(ch):
            hh, start = divmod(ch * FOX_CHUNK, tq)
            return slice(ch * FOX_CHUNK, (ch + 1) * FOX_CHUNK), hh, slice(start, start + FOX_CHUNK)

        def stage_a(ch):
            cols, _, hcols = chunk_cols(ch)
            if diag == 'bubble' or (diag is not None and hcols.stop <= diag * tk):
                s_cur[:, cols] = jnp.full((tk, FOX_CHUNK), NEG_INF, jnp.float32)
                smax_new.append(jnp.full((1, FOX_CHUNK), NEG_INF, jnp.float32))
                return
            s_new = _bdot(ka, w_buf[:, cols])
            if diag is not None and hcols.start < (diag + 1) * tk:
                s_new = s_new + bias_buf[diag, :, hcols]
            s_cur[:, cols] = s_new
            smax_new.append(jnp.max(s_new, axis=0, keepdims=True))

        def stage_b(ch):
            cols, hh, hcols = chunk_cols(ch)
            m_prev = jnp.where(first, NEG_INF, m[ch])
            mn = jnp.maximum(m_prev, smax[ch])
            p = jnp.exp2(s_prv[:, cols] - mn).astype(jnp.bfloat16)
            lhs = jnp.concatenate([vt[hh * HEAD_DIM:(hh + 1) * HEAD_DIM, :], ones_rows], axis=0)
            acc_buf[hh, :, hcols] = jnp.exp2(m_prev - mn) * acc_buf[hh, :, hcols] + _bdot(lhs, p)
            m_new.append(mn)

        stage_a(0)
        for ch in range(n_chunks):
            if ch + 1 < n_chunks:
                stage_a(ch + 1)
            stage_b(ch)
        smax_new = tuple(smax_new)

        @pl.when(v_b & (j_b == per_q * (i_b + 1) - 1))
        def _():
            q0 = pl.multiple_of(i_b * tq, tq)
            out_t = jnp.concatenate([acc_buf[hh, 0:HEAD_DIM, :] / acc_buf[hh, HEAD_DIM:HEAD_DIM + 1, :]
                                     for hh in range(2)], axis=0)
            y = out_t.T * _silu(g_ref[pl.ds(q0, tq), :])
            o_ref[pl.ds(q0, tq), :] = y.astype(o_ref.dtype)

        last_a = j_a == per_q * (i_a + 1) - 1
        nxt_a = (jnp.where(last_a, i_a + 1, i_a),
                 jnp.where(last_a, js_ref[jnp.minimum(i_a + 1, nq - 1)], j_a + 1))
        return (nxt_a, (i_a, jnp.where(i_a < nq, j_a, 0), i_a < nq, first_a), smax_new, tuple(m_new))

    def either(carry, bufs):
        i_a, j_a = carry[0]
        diag = j_a - per_q * i_a

        def run(kind):
            return lambda c: step(c, *bufs, kind)
        on_diag = run(per_q - 1)
        for d in range(per_q - 2, -1, -1):
            on_diag = (lambda d, other: lambda c: lax.cond(diag == d, run(d), other, c))(d, on_diag)
        special = lambda c: lax.cond(i_a >= nq, run('bubble'), on_diag, c)
        return lax.cond((diag >= 0) | (i_a >= nq), special, run(None), carry)

    def body(_, carry):
        carry = either(carry, (s_buf0, s_buf1))
        return either(carry, (s_buf1, s_buf0))

    zi = jnp.int32(0)
    row_vec = lambda v: tuple(jnp.full((1, FOX_CHUNK), v, jnp.float32) for _ in range(2 * tq // FOX_CHUNK))
    init = ((zi, zi), (zi, zi, False, False), row_vec(0.0), row_vec(0.0))
    lax.fori_loop(0, (n_steps + 2) // 2, body, init)


def _fox(cend_tiles, qbt, kb, caug, vbt, g, batch, seq):
    npair = FOX_HEADS // 2
    nkt = seq // FOX_TK
    gcol0 = SWA_W // LANES
    return pl.pallas_call(
        _fox_kernel,
        grid=(batch, npair),
        in_specs=[pl.BlockSpec(memory_space=pltpu.SMEM),
                  pl.BlockSpec((LANES, seq), lambda b, h: (h, b)),
                  pl.BlockSpec((seq, LANES), lambda b, h: (b, h)),
                  pl.BlockSpec((seq, LANES), lambda b, h: (b, h)),
                  pl.BlockSpec((1, nkt, LANES, FOX_TK), lambda b, h: (h, b, 0, 0)),
                  pl.BlockSpec((seq, LANES), lambda b, h: (b, gcol0 + h))],
        out_specs=pl.BlockSpec((seq, LANES), lambda b, h: (b, h)),
        out_shape=jax.ShapeDtypeStruct((batch * seq, FOX_W), jnp.bfloat16),
        scratch_shapes=[pltpu.VMEM((2 * LANES, 2 * FOX_TQ), jnp.bfloat16),
                        pltpu.VMEM((FOX_TK, 2 * FOX_TQ), jnp.float32),
                        pltpu.VMEM((FOX_TK, 2 * FOX_TQ), jnp.float32),
                        pltpu.VMEM((2, HEAD_DIM + 16, FOX_TQ), jnp.float32),
                        pltpu.VMEM((FOX_TQ // FOX_TK, FOX_TK, FOX_TQ), jnp.float32),
                        pltpu.SMEM((seq // FOX_TQ,), jnp.int32)],
        compiler_params=pltpu.CompilerParams(dimension_semantics=("arbitrary", "arbitrary"),
                                             vmem_limit_bytes=_VMEM_LIMIT),
        name="fox",
    )(cend_tiles, qbt, kb, caug, vbt, g)


def _split3(x):
    hi = x.astype(jnp.bfloat16)
    r1 = x - hi.astype(jnp.float32)
    mid = r1.astype(jnp.bfloat16)
    lo = (r1 - mid.astype(jnp.float32)).astype(jnp.bfloat16)
    return hi, mid, lo


def _ssd_kernel(xbc_ref, z_ref, fdt_ref, cw_ref, cb_ref, bias_ref, alog_ref, dskip_ref, nw_ref,
                y_ref, ca_ref, cend_ref, state_ref, tail_ref, ccarry_ref):
    @pl.when(pl.program_id(1) == 0)
    def _():
        state_ref[...] = jnp.zeros_like(state_ref)
        tail_ref[...] = jnp.zeros_like(tail_ref)
        ccarry_ref[...] = jnp.zeros_like(ccarry_ref)

    for sub in range(xbc_ref.shape[0] // BLOCK):
        _ssd_chunk(sub, xbc_ref, z_ref, fdt_ref, cw_ref, cb_ref, bias_ref, alog_ref, dskip_ref, nw_ref,
                   y_ref, ca_ref, cend_ref, state_ref, tail_ref, ccarry_ref)


def _ssd_chunk(sub, xbc_ref, z_ref, fdt_ref, cw_ref, cb_ref, bias_ref, alog_ref, dskip_ref, nw_ref,
               y_ref, ca_ref, cend_ref, state_ref, tail_ref, ccarry_ref):
    L = BLOCK
    rows = slice(sub * L, (sub + 1) * L)
    u = xbc_ref[rows, :]
    tail_ref[8:8 + L, :] = u
    conv = cb_ref[...] + cw_ref[SSM_CONV - 1:SSM_CONV, :] * u
    for k in range(1, SSM_CONV):
        conv = conv + cw_ref[SSM_CONV - 1 - k:SSM_CONV - k, :] * tail_ref[8 - k:8 - k + L, :]
    tail_ref[0:8, :] = u[L - 8:L]
    act = _silu(conv)
    xs = act[:, :SSM_W]
    bm_f = act[:, SSM_W:SSM_W + SSM_BC_W]
    bm = bm_f.astype(jnp.bfloat16)
    cm = act[:, SSM_W + SSM_BC_W:].astype(jnp.bfloat16)

    lane = lax.broadcasted_iota(jnp.int32, (L, LANES), 1)
    vals = fdt_ref[rows, :] + bias_ref[...]
    is_f = lane < DT_LANE0
    is_dt = (lane >= DT_LANE0) & (lane < DT_LANE0 + SSM_HEADS)
    sp = _softplus(jnp.where(is_f, -vals, vals))
    a_row = jnp.where(is_dt[0:1], -jnp.exp(alog_ref[...]), 0.0)
    dt = jnp.where(is_dt, sp, 0.0)
    scan_in = jnp.where(is_f, -sp, dt * a_row) * LOG2E

    ri = lax.broadcasted_iota(jnp.int32, (L, L), 0)
    cj = lax.broadcasted_iota(jnp.int32, (L, L), 1)
    causal = cj <= ri
    tri = jnp.where(causal, 1.0, 0.0).astype(jnp.bfloat16)
    hi, mid, lo = _split3(scan_in)
    cs = _bdot(tri, hi) + _bdot(tri, mid) + _bdot(tri, lo)
    cs_t = cs.T

    c_full = cs + ccarry_ref[...]
    ccarry_ref[...] = jnp.where(is_f[0:1], c_full[L - 1:L, :], 0.0)
    cend_ref[sub] = c_full[L - 1:L, :]
    lane64 = lane % HEAD_DIM
    ca_cols = []
    for hp in range(FOX_HEADS // 2):
        negc = -jnp.where(lane < HEAD_DIM, c_full[:, 2 * hp:2 * hp + 1], c_full[:, 2 * hp + 1:2 * hp + 2])
        hi, mid, lo = (term.astype(jnp.float32) for term in _split3(negc))
        terms = jnp.where(lane64 == 0, hi, jnp.where(lane64 == 1, mid, jnp.where(lane64 == 2, lo, 0.0)))
        ca_cols.append(terms.astype(jnp.bfloat16))
    ca_ref[rows, :] = jnp.concatenate(ca_cols, axis=1)

    low_l = lane < HEAD_DIM
    hpg = SSM_HEADS // 2
    npairs = SSM_HEADS // 2
    grp_of = [(2 * pair) // hpg for pair in range(npairs)]
    bm_g = [bm[:, g * SSM_STATE:(g + 1) * SSM_STATE] for g in range(2)]
    cm_g = [cm[:, g * SSM_STATE:(g + 1) * SSM_STATE] for g in range(2)]
    bm_t = [bm_f[:, g * SSM_STATE:(g + 1) * SSM_STATE].T.astype(jnp.bfloat16) for g in range(2)]
    cb = [_bdot_nt(cm_g[g], bm_g[g]) for g in range(2)]
    ys = []
    for pair in range(npairs):
        g = grp_of[pair]
        ha, hb = DT_LANE0 + 2 * pair, DT_LANE0 + 2 * pair + 1
        col = (cs[:, ha:ha + 1], cs[:, hb:hb + 1])
        acs_row = (cs_t[ha:ha + 1, :], cs_t[hb:hb + 1, :])
        xdt = xs[:, pair * LANES:(pair + 1) * LANES] * jnp.where(low_l, dt[:, ha:ha + 1], dt[:, hb:hb + 1])
        xdt_b = xdt.astype(jnp.bfloat16)
        prev = state_ref[pair]
        y_off = _bdot(cm_g[g], prev.astype(jnp.bfloat16))
        last = (col[0][L - 1:L, :], col[1][L - 1:L, :])
        dst = jnp.exp2(jnp.where(low_l, last[0] - col[0], last[1] - col[1]))
        st_new = _bdot(bm_t[g], (xdt * dst).astype(jnp.bfloat16))
        state_ref[pair] = prev * jnp.exp2(jnp.where(low_l, last[0], last[1])) + st_new
        y_diag = []
        for hh in range(2):
            decay = jnp.exp2(jnp.where(causal, col[hh] - acs_row[hh], NEG_INF))
            gmat = (cb[g] * decay).astype(jnp.bfloat16)
            y_diag.append(_bdot(gmat, xdt_b))
        ys.append(jnp.where(low_l, y_diag[0], y_diag[1]) + y_off * jnp.exp2(jnp.where(low_l, col[0], col[1])))
    y = jnp.concatenate(ys, axis=1) + dskip_ref[...] * xs
    y_ref[rows, :] = _rms(y * _silu(z_ref[rows, :]), nw_ref[...]).astype(y_ref.dtype)


def _ssd(xbc, z, fdt, conv_w, conv_b, b_forget, dt_bias, a_log, d_skip, norm_w, batch, seq):
    rows = min(SSD_ROWS, seq)
    per_step = rows // BLOCK
    nc = seq // rows
    row = lambda b, c: (b * nc + c, 0)
    const = lambda b, c: (0, 0)
    zpad = jnp.zeros((LANES - FOX_HEADS - SSM_HEADS,), jnp.float32)
    bias_row = jnp.concatenate([b_forget, dt_bias, zpad]).reshape(1, LANES)
    alog_row = jnp.concatenate([jnp.zeros((FOX_HEADS,), jnp.float32), a_log, zpad]).reshape(1, LANES)
    dskip_row = jnp.repeat(d_skip, HEAD_DIM).reshape(1, SSM_W)
    return pl.pallas_call(
        _ssd_kernel,
        grid=(batch, nc),
        in_specs=[pl.BlockSpec((rows, SSM_CONV_W), row),
                  pl.BlockSpec((rows, SSM_W), row),
                  pl.BlockSpec((rows, LANES), row),
                  pl.BlockSpec((SSM_CONV, SSM_CONV_W), const),
                  pl.BlockSpec((1, SSM_CONV_W), const),
                  pl.BlockSpec((1, LANES), const),
                  pl.BlockSpec((1, LANES), const),
                  pl.BlockSpec((1, SSM_W), const),
                  pl.BlockSpec((1, SSM_W), const)],
        out_specs=[pl.BlockSpec((rows, SSM_W), row),
                   pl.BlockSpec((rows, FOX_W), row),
                   pl.BlockSpec((per_step, 1, LANES), lambda b, c: (b * nc + c, 0, 0))],
        out_shape=[jax.ShapeDtypeStruct((batch * seq, SSM_W), jnp.bfloat16),
                   jax.ShapeDtypeStruct((batch * seq, FOX_W), jnp.bfloat16),
                   jax.ShapeDtypeStruct((batch * nc * per_step, 1, LANES), jnp.float32)],
        scratch_shapes=[pltpu.VMEM((SSM_HEADS // 2, 2 * HEAD_DIM, SSM_STATE), jnp.float32),
                        pltpu.VMEM((8 + BLOCK, SSM_CONV_W), jnp.float32),
                        pltpu.VMEM((1, LANES), jnp.float32)],
        compiler_params=pltpu.CompilerParams(dimension_semantics=("arbitrary", "arbitrary")),
        name="ssd",
    )(xbc, z, fdt, conv_w, conv_b.reshape(1, SSM_CONV_W), bias_row, alog_row, dskip_row,
      norm_w.reshape(1, SSM_W))


def _memkv_kernel(mem_ref, nw_ref, wk_ref, wv_ref, k_ref, v_ref):
    mn = _rms(mem_ref[...], nw_ref[...]).astype(jnp.bfloat16)
    k_ref[...] = _bdot(mn, wk_ref[...]).astype(k_ref.dtype)
    v_ref[...] = _bdot(mn, wv_ref[...]).astype(v_ref.dtype)


def _memkv(mem2, norm_w, wk, wv, mem_tokens):
    t = mem2.shape[0]
    row = lambda b: (b, 0)
    const = lambda b: (0, 0)
    return pl.pallas_call(
        _memkv_kernel,
        grid=(t // mem_tokens,),
        in_specs=[pl.BlockSpec((mem_tokens, D_MODEL), row),
                  pl.BlockSpec((1, D_MODEL), const),
                  pl.BlockSpec((D_MODEL, D_MODEL), const),
                  pl.BlockSpec((D_MODEL, D_MODEL), const)],
        out_specs=[pl.BlockSpec((mem_tokens, D_MODEL), row)] * 2,
        out_shape=[jax.ShapeDtypeStruct((t, D_MODEL), jnp.bfloat16)] * 2,
        compiler_params=pltpu.CompilerParams(dimension_semantics=("arbitrary",)),
        name="memkv",
    )(mem2, norm_w.reshape(1, D_MODEL), wk.astype(jnp.bfloat16), wv.astype(jnp.bfloat16))


def _out_kernel(x_ref, ya_ref, yb_ref, yc_ref, wo_ref, nq_ref, wq_ref, k_ref, v_ref, wmo_ref, fn_ref,
                o_ref, *, final_norm):
    tm = x_ref.shape[0]
    groups = [slice(r, r + OUT_ROWS) for r in range(0, tm, OUT_ROWS)]
    head_cols = [slice(h * MEM_HEAD_DIM, (h + 1) * MEM_HEAD_DIM) for h in range(MEM_HEADS)]
    x1 = [x_ref[g, :]
          + _bdot(ya_ref[g, :], wo_ref[0:SWA_W, :])
          + _bdot(yb_ref[g, :], wo_ref[SWA_W:SWA_W + FOX_W, :])
          + _bdot(yc_ref[g, :], wo_ref[SWA_W + FOX_W:, :]) for g in groups]
    q = []
    for x1g in x1:
        hq = _rms(x1g, nq_ref[...]).astype(jnp.bfloat16)
        q.append((_bdot(hq, wq_ref[...]) * (MEM_HEAD_DIM ** -0.5 * LOG2E)).astype(jnp.bfloat16))
    scores = [[_bdot_nt(qg[:, sl], k_ref[:, sl]) for sl in head_cols] for qg in q]
    att = []
    for sg in scores:
        heads = []
        for s, sl in zip(sg, head_cols):
            p = jnp.exp2(s - jnp.max(s, axis=-1, keepdims=True))
            probs = (p / jnp.sum(p, axis=-1, keepdims=True)).astype(jnp.bfloat16)
            heads.append(_bdot(probs, v_ref[:, sl]).astype(jnp.bfloat16))
        att.append(jnp.concatenate(heads, axis=1))
    for g, x1g, attg in zip(groups, x1, att):
        x2 = x1g + _bdot(attg, wmo_ref[...])
        if final_norm:
            x2 = _rms(x2, fn_ref[...])
        o_ref[g, :] = x2


def _out_block(x2, ya, yb, yc, w_out, norm_xq_w, w_mq, kmem, vmem, w_mo, final_w, seq, mem_tokens, tm,
               final_norm):
    t = x2.shape[0]
    nblk_s = seq // tm
    row = lambda i: (i, 0)
    const = lambda i: (0, 0)
    memrow = lambda i: (i // nblk_s, 0)
    bf = jnp.bfloat16
    return pl.pallas_call(
        functools.partial(_out_kernel, final_norm=final_norm),
        grid=(t // tm,),
        in_specs=[pl.BlockSpec((tm, D_MODEL), row),
                  pl.BlockSpec((tm, SWA_W), row),
                  pl.BlockSpec((tm, FOX_W), row),
                  pl.BlockSpec((tm, SSM_W), row),
                  pl.BlockSpec((SWA_W + FOX_W + SSM_W, D_MODEL), const, pipeline_mode=pl.Buffered(1)),
                  pl.BlockSpec((1, D_MODEL), const),
                  pl.BlockSpec((D_MODEL, D_MODEL), const, pipeline_mode=pl.Buffered(1)),
                  pl.BlockSpec((mem_tokens, D_MODEL), memrow),
                  pl.BlockSpec((mem_tokens, D_MODEL), memrow),
                  pl.BlockSpec((D_MODEL, D_MODEL), const, pipeline_mode=pl.Buffered(1)),
                  pl.BlockSpec((1, D_MODEL), const)],
        out_specs=pl.BlockSpec((tm, D_MODEL), row),
        out_shape=jax.ShapeDtypeStruct((t, D_MODEL), jnp.float32),
        compiler_params=pltpu.CompilerParams(dimension_semantics=("arbitrary",),
                                             vmem_limit_bytes=_VMEM_LIMIT),
        name="outproj_mem",
    )(x2, ya, yb, yc, w_out.astype(bf), norm_xq_w.reshape(1, D_MODEL), w_mq.astype(bf), kmem, vmem,
      w_mo.astype(bf), final_w.reshape(1, D_MODEL))


def _row_tile(seq):
    return min(512, seq)


def _out_row_tile(seq):
    return min(1024, seq)


def kernel(x, mem, norm_mix_w, w_in, b_forget, swa_sinks, conv_w, conv_b, dt_bias, a_log, d_skip, ssm_norm_w, w_out, norm_xq_w, norm_mem_w, w_mq, w_mk, w_mv, w_mo, final_norm_w):
    batch, seq, _ = x.shape
    mem_tokens = mem.shape[1]
    depth = w_in.shape[0]
    assert seq % BLOCK == 0
    tm = _row_tile(seq)
    assert seq % tm == 0 and seq % FOX_TQ == 0 and seq % SWA_TQ == 0
    rope = _rope_tables(seq)
    x2 = x.reshape(batch * seq, D_MODEL)
    mem2 = mem.reshape(batch * mem_tokens, D_MODEL)
    for l in range(depth):
        fox_order = jnp.argsort(b_forget[l])
        w_t = _arrange_w_in(w_in[l], fox_order)
        ka, g, kb, z, xbc, fdt, qbt, vbt, qat, vat = _inproj(x2, norm_mix_w[l], w_t, rope, seq, tm)
        yc, caug, cend = _ssd(xbc, z, fdt, conv_w[l], conv_b[l], b_forget[l][fox_order], dt_bias[l], a_log[l],
                             d_skip[l], ssm_norm_w[l], batch, seq)
        per_tile = FOX_TK // BLOCK
        cend_tiles = cend.reshape(batch, seq // BLOCK, LANES)[:, per_tile - 1::per_tile, :FOX_HEADS]
        cend_tiles = cend_tiles.transpose(0, 2, 1).reshape(batch * FOX_HEADS, seq // FOX_TK)
        ya = _swa(qat, ka, vat, g, swa_sinks[l], batch, seq)
        yb = _fox(cend_tiles, qbt, kb, caug, vbt, g, batch, seq)
        kmem, vmem = _memkv(mem2, norm_mem_w[l], w_mk[l], w_mv[l], mem_tokens)
        o_b = SWA_W
        w_o = w_out[l].at[o_b:o_b + FOX_W].set(
            w_out[l][o_b:o_b + FOX_W].reshape(FOX_HEADS, HEAD_DIM, D_MODEL)[fox_order].reshape(FOX_W, D_MODEL))
        x2 = _out_block(x2, ya, yb, yc, w_o, norm_xq_w[l], w_mq[l], kmem, vmem, w_mo[l],
                        final_norm_w, seq, mem_tokens, _out_row_tile(seq), final_norm=(l == depth - 1))
    return x2.reshape(batch, seq, D_MODEL)
```

```python
import functools

import jax
import jax.numpy as jnp
from jax import lax
from jax.experimental import pallas as pl
from jax.experimental.pallas import tpu as pltpu

D_MODEL = 1024
HEAD_DIM = 64
BLOCK = 128
SWA_HEADS = 8
SWA_KV_HEADS = 2
FOX_HEADS = 8
SSM_HEADS = 16
SSM_STATE = 128
SSM_CONV = 4
MEM_HEADS = 4
MEM_HEAD_DIM = 256
ROPE_THETA = 10000.0
EPS = 1e-6
NEG_INF = -1e30

SWA_W = SWA_HEADS * HEAD_DIM
SWA_KV_W = SWA_KV_HEADS * HEAD_DIM
FOX_W = FOX_HEADS * HEAD_DIM
SSM_W = SSM_HEADS * HEAD_DIM
SSM_BC_W = 2 * SSM_STATE
SSM_CONV_W = SSM_W + 2 * SSM_BC_W
LANES = 128
LOG2E = 1.4426950408889634
SWA_TQ = 1024
FOX_TQ = 1024
FOX_CHUNK = 256
FOX_SKIP_BELOW = -170.0
FOX_NORM_MARGIN = 1.02
FOX_TK = 512
OUT_ROWS = 256
SSD_ROWS = 512
DT_LANE0 = FOX_HEADS

_O_QA = 0
_O_KA = _O_QA + SWA_W
_O_VA = _O_KA + SWA_KV_W
_O_GA = _O_VA + SWA_KV_W
_O_QB = _O_GA + SWA_W
_O_KB = _O_QB + FOX_W
_O_VB = _O_KB + FOX_W
_O_FB = _O_VB + FOX_W
_O_GB = _O_FB + FOX_HEADS
_O_ZC = _O_GB + FOX_W
_O_XBC = _O_ZC + SSM_W
_O_DT = _O_XBC + SSM_CONV_W
_IN_W = _O_DT + SSM_HEADS

_SEG_W = (SWA_KV_W + LANES, 2 * SWA_W, FOX_W, SSM_W, SSM_CONV_W, FOX_W, FOX_W, SWA_W, SWA_KV_W)
_SEG_OFF = tuple(sum(_SEG_W[:i]) for i in range(len(_SEG_W)))
_PROJ_W = sum(_SEG_W)

_VMEM_LIMIT = 56 * 1024 * 1024


def _bdot(a, b):
    return jnp.dot(a, b, preferred_element_type=jnp.float32)


def _bdot_nt(a, b):
    return lax.dot_general(a, b, (((1,), (1,)), ((), ())), preferred_element_type=jnp.float32)


def _silu(x):
    h = 0.5 * x
    return h + h * jnp.tanh(h)


def _softplus(x):
    return jnp.maximum(x, 0.0) + jnp.log(1.0 + jnp.exp(-jnp.abs(x)))


def _rms(x, w):
    return x * lax.rsqrt(jnp.mean(x * x, axis=-1, keepdims=True) + EPS) * w


def _rope(x, cos, sin_signed):
    width = x.shape[1]
    reps = width // LANES
    lane = lax.broadcasted_iota(jnp.int32, x.shape, 1)
    first_half = (lane % HEAD_DIM) < (HEAD_DIM // 2)
    swapped = jnp.where(first_half,
                        pltpu.roll(x, width - HEAD_DIM // 2, 1),
                        pltpu.roll(x, HEAD_DIM // 2, 1))
    cos_t = jnp.concatenate([cos] * reps, axis=1)
    sin_t = jnp.concatenate([sin_signed] * reps, axis=1)
    return x * cos_t + swapped * sin_t


def _rope_t(x, cos, sin_signed):
    rows = x.shape[0]
    reps = rows // LANES
    r = lax.broadcasted_iota(jnp.int32, x.shape, 0)
    first_half = (r % HEAD_DIM) < (HEAD_DIM // 2)
    swapped = jnp.where(first_half,
                        pltpu.roll(x, rows - HEAD_DIM // 2, 0),
                        pltpu.roll(x, HEAD_DIM // 2, 0))
    cos_t = jnp.concatenate([cos] * reps, axis=0)
    sin_t = jnp.concatenate([sin_signed] * reps, axis=0)
    return x * cos_t + swapped * sin_t


def _inproj_kernel(x_ref, nw_ref, wt_ref, cost_ref, sint_ref,
                   ka_ref, g_ref, kb_ref, z_ref, xbc_ref, fdt_ref, qbt_ref, vbt_ref, qat_ref, vat_ref):
    h = _rms(x_ref[...], nw_ref[...]).astype(jnp.bfloat16)

    def w_rows(i):
        return wt_ref[_SEG_OFF[i]:_SEG_OFF[i] + _SEG_W[i], :]

    scale = HEAD_DIM ** -0.5 * LOG2E
    k_f = _bdot_nt(h, w_rows(0))
    ka_ref[...] = _rope(k_f[:, :SWA_KV_W], cost_ref[...].T, sint_ref[...].T).astype(ka_ref.dtype)
    fdt_ref[...] = k_f[:, SWA_KV_W:]
    g_ref[...] = _bdot_nt(h, w_rows(1))
    kb_ref[...] = _bdot_nt(h, w_rows(2)).astype(kb_ref.dtype)
    z_ref[...] = _bdot_nt(h, w_rows(3))
    xbc_ref[...] = _bdot_nt(h, w_rows(4))
    qbt_ref[...] = (_bdot_nt(w_rows(5), h) * scale).astype(qbt_ref.dtype)
    qat = _rope_t(_bdot_nt(w_rows(7), h), cost_ref[...], sint_ref[...])
    qat_ref[...] = (qat * scale).astype(qat_ref.dtype)
    vat_ref[...] = _bdot_nt(w_rows(8), h).astype(vat_ref.dtype)
    vbt = _bdot_nt(w_rows(6), h).astype(vbt_ref.dtype)
    for hp in range(FOX_W // LANES):
        for c in range(vbt.shape[1] // FOX_TK):
            vbt_ref[hp, c] = vbt[hp * LANES:(hp + 1) * LANES, c * FOX_TK:(c + 1) * FOX_TK]


def _arrange_w_in(w_in, fox_order):
    wt = w_in.T.astype(jnp.bfloat16)

    def rows(o, n):
        return wt[o:o + n]

    def fox_rows(o):
        return rows(o, FOX_W).reshape(FOX_HEADS, HEAD_DIM, -1)[fox_order].reshape(FOX_W, -1)
    pad = jnp.zeros((LANES - FOX_HEADS - SSM_HEADS, wt.shape[1]), wt.dtype)
    parts = [rows(_O_KA, SWA_KV_W), rows(_O_FB, FOX_HEADS)[fox_order], rows(_O_DT, SSM_HEADS), pad,
             rows(_O_GA, SWA_W), fox_rows(_O_GB),
             fox_rows(_O_KB),
             rows(_O_ZC, SSM_W), rows(_O_XBC, SSM_CONV_W),
             fox_rows(_O_QB), fox_rows(_O_VB), rows(_O_QA, SWA_W), rows(_O_VA, SWA_KV_W)]
    return jnp.concatenate(parts, axis=0)


def _inproj(x2, norm_w, w_t, rope, seq, tm):
    t = x2.shape[0]
    nblk_s = seq // tm
    row = lambda i: (i, 0)
    const = lambda i: (0, 0)
    pos_t = lambda i: (0, i % nblk_s)
    col = lambda i: (0, i)
    bf, f32 = jnp.bfloat16, jnp.float32
    outs = [(SWA_KV_W, bf), (2 * SWA_W, f32), (FOX_W, bf), (SSM_W, f32), (SSM_CONV_W, f32), (LANES, f32)]
    npair = FOX_W // LANES
    return pl.pallas_call(
        _inproj_kernel,
        grid=(t // tm,),
        in_specs=[pl.BlockSpec((tm, D_MODEL), row),
                  pl.BlockSpec((1, D_MODEL), const),
                  pl.BlockSpec((_PROJ_W, D_MODEL), const, pipeline_mode=pl.Buffered(1)),
                  pl.BlockSpec((LANES, tm), pos_t),
                  pl.BlockSpec((LANES, tm), pos_t)],
        out_specs=[pl.BlockSpec((tm, w), row) for w, _ in outs]
        + [pl.BlockSpec((FOX_W, tm), col),
           pl.BlockSpec((npair, tm // FOX_TK, LANES, FOX_TK), lambda i: (0, i, 0, 0)),
           pl.BlockSpec((SWA_W, tm), col),
           pl.BlockSpec((SWA_KV_W, tm), col)],
        out_shape=[jax.ShapeDtypeStruct((t, w), d) for w, d in outs]
        + [jax.ShapeDtypeStruct((FOX_W, t), bf),
           jax.ShapeDtypeStruct((npair, t // FOX_TK, LANES, FOX_TK), bf),
           jax.ShapeDtypeStruct((SWA_W, t), bf),
           jax.ShapeDtypeStruct((SWA_KV_W, t), bf)],
        compiler_params=pltpu.CompilerParams(dimension_semantics=("arbitrary",),
                                             vmem_limit_bytes=_VMEM_LIMIT),
        name="inproj",
    )(x2, norm_w.reshape(1, D_MODEL), w_t, *rope)


def _rope_tables(seq):
    pos = jnp.arange(seq, dtype=jnp.float32)
    inv = 1.0 / (ROPE_THETA ** (jnp.arange(0, HEAD_DIM, 2, dtype=jnp.float32) / HEAD_DIM))
    ang = inv[:, None] * pos[None, :]
    cos, sin = jnp.cos(ang), jnp.sin(ang)
    cos_t = jnp.concatenate([cos, cos, cos, cos], axis=0)
    sin_t = jnp.concatenate([-sin, sin, -sin, sin], axis=0)
    return cos_t, sin_t


def _swa_kernel(sink_ref, qt_ref, kc_ref, kp_ref, vc_ref, vp_ref, g_ref, o_ref):
    n = pl.program_id(1)
    nsub = SWA_TQ // BLOCK
    key = lax.broadcasted_iota(jnp.int32, (2 * BLOCK, BLOCK), 0)
    qry = lax.broadcasted_iota(jnp.int32, (2 * BLOCK, BLOCK), 1)
    band = jnp.where((key > qry) & (key <= qry + BLOCK), 0.0, NEG_INF)
    band_first = jnp.where(key < BLOCK, NEG_INF, band)
    row = lax.broadcasted_iota(jnp.int32, (LANES, BLOCK), 0)
    ones_rows = jnp.ones((16, 2 * BLOCK), jnp.bfloat16)

    def scores(u):
        if u == 0:
            kcat = jnp.concatenate([kp_ref[...], kc_ref[0:BLOCK, :]], axis=0)
            vcat = jnp.concatenate([vp_ref[...], vc_ref[:, 0:BLOCK]], axis=1)
            bias = jnp.where(n > 0, band, band_first)
        else:
            kcat = kc_ref[(u - 1) * BLOCK:(u + 1) * BLOCK, :]
            vcat = vc_ref[:, (u - 1) * BLOCK:(u + 1) * BLOCK]
            bias = band
        tiles = []
        for c in range(SWA_HEADS // 2):
            qt = qt_ref[c * LANES:(c + 1) * LANES, u * BLOCK:(u + 1) * BLOCK].astype(jnp.float32)
            for half in range(2):
                kv = (2 * c + half) // (SWA_HEADS // SWA_KV_HEADS)
                q_rows = qt if kv == half else pltpu.roll(qt, HEAD_DIM, 0)
                in_half = (row < HEAD_DIM) if kv == 0 else (row >= HEAD_DIM)
                w = jnp.where(in_half, q_rows, 0.0).astype(jnp.bfloat16)
                tiles.append(_bdot(kcat, w) + bias)
        return tiles, vcat

    def finish(u, tiles, vcat):
        probs, sink_terms = [], []
        for head, s in enumerate(tiles):
            sink = sink_ref[0, head] * LOG2E
            m = jnp.maximum(jnp.max(s, axis=0, keepdims=True), sink)
            probs.append(jnp.exp2(s - m).astype(jnp.bfloat16))
            sink_terms.append(jnp.exp2(sink - m))
        outs = []
        for head, p in enumerate(probs):
            kv = head // (SWA_HEADS // SWA_KV_HEADS)
            lhs = jnp.concatenate([vcat[kv * HEAD_DIM:(kv + 1) * HEAD_DIM, :], ones_rows], axis=0)
            pv = _bdot(lhs, p)
            outs.append(pv[0:HEAD_DIM, :] / (pv[HEAD_DIM:HEAD_DIM + 1, :] + sink_terms[head]))
        y = jnp.concatenate(outs, axis=0).T
        rows = slice(u * BLOCK, (u + 1) * BLOCK)
        o_ref[rows, :] = (y * _silu(g_ref[rows, :])).astype(o_ref.dtype)

    pending = scores(0)
    for u in range(nsub):
        nxt = scores(u + 1) if u + 1 < nsub else None
        finish(u, *pending)
        pending = nxt


def _swa(qat, ka, vat, g, sinks, batch, seq):
    n = seq // SWA_TQ
    nsub = SWA_TQ // BLOCK
    cur = lambda b, i: (b * n + i, 0)
    cur_t = lambda b, i: (0, b * n + i)
    prev = lambda b, i: ((b * n + i) * nsub - jnp.minimum(i, 1), 0)
    prev_t = lambda b, i: (0, (b * n + i) * nsub - jnp.minimum(i, 1))
    return pl.pallas_call(
        _swa_kernel,
        grid=(batch, n),
        in_specs=[pl.BlockSpec(memory_space=pltpu.SMEM),
                  pl.BlockSpec((SWA_W, SWA_TQ), cur_t),
                  pl.BlockSpec((SWA_TQ, SWA_KV_W), cur),
                  pl.BlockSpec((BLOCK, SWA_KV_W), prev),
                  pl.BlockSpec((SWA_KV_W, SWA_TQ), cur_t),
                  pl.BlockSpec((SWA_KV_W, BLOCK), prev_t),
                  pl.BlockSpec((SWA_TQ, SWA_W), cur)],
        out_specs=pl.BlockSpec((SWA_TQ, SWA_W), cur),
        out_shape=jax.ShapeDtypeStruct((batch * seq, SWA_W), jnp.bfloat16),
        compiler_params=pltpu.CompilerParams(dimension_semantics=("arbitrary", "arbitrary")),
        name="swa",
    )(sinks.reshape(1, SWA_HEADS).astype(jnp.float32), qat, ka, ka, vat, vat, g)


def _fox_kernel(cend_ref, qt_ref, k_ref, ca_ref, vt_ref, g_ref, o_ref, w_buf, s_buf0, s_buf1, acc_buf, bias_buf,
                js_ref):
    tq, tk = FOX_TQ, FOX_TK
    per_q = tq // tk
    nq = qt_ref.shape[1] // tq
    acc_rows = acc_buf.shape[1]

    krow = lax.broadcasted_iota(jnp.int32, (tk, tq), 0)
    qcol = lax.broadcasted_iota(jnp.int32, (tk, tq), 1)
    for d in range(per_q):
        bias_buf[d] = jnp.where(krow + d * tk <= qcol, 0.0, NEG_INF)
    for buf in (s_buf0, s_buf1, acc_buf):
        buf[...] = jnp.zeros_like(buf)

    row = lax.broadcasted_iota(jnp.int32, (LANES, tq), 0)
    nsplit = 3
    pick = (jnp.where(row < nsplit, 1.0, 0.0).astype(jnp.bfloat16),
            jnp.where((row >= HEAD_DIM) & (row < HEAD_DIM + nsplit), 1.0, 0.0).astype(jnp.bfloat16))
    ones_rows = jnp.ones((acc_rows - HEAD_DIM, tk), jnp.bfloat16)

    qsq = jnp.square(qt_ref[...].astype(jnp.float32))
    ksq = jnp.square(k_ref[...].astype(jnp.float32))
    lane_sel = lax.broadcasted_iota(jnp.int32, (LANES, LANES), 0) // HEAD_DIM == \
        lax.broadcasted_iota(jnp.int32, (LANES, LANES), 1)
    kn2 = jnp.max(_bdot(ksq.astype(jnp.bfloat16), jnp.where(lane_sel, 1.0, 0.0).astype(jnp.bfloat16)),
                  axis=0, keepdims=True)
    bound = []
    for hh in range(2):
        qn2 = jnp.max(jnp.sum(qsq[hh * HEAD_DIM:(hh + 1) * HEAD_DIM, :], axis=0, keepdims=True),
                      axis=1, keepdims=True)
        bound.append((FOX_NORM_MARGIN * 2.0 * jnp.sqrt(qn2 * kn2[:, hh:hh + 1]))[0, 0])
    head0 = (pl.program_id(0) * (FOX_HEADS // 2) + pl.program_id(1)) * 2
    n_steps = jnp.int32(0)
    for i in range(nq):
        count, prefix = jnp.int32(0), jnp.bool_(True)
        for j in range(per_q * i):
            for hh in range(2):
                drop = cend_ref[head0 + hh, j] - cend_ref[head0 + hh, per_q * i - 1]
                prefix = prefix & (bound[hh] - drop < FOX_SKIP_BELOW)
            count = count + prefix.astype(jnp.int32)
        js_ref[i] = count
        n_steps = n_steps + (per_q * (i + 1) - count)

    def step(carry, s_cur, s_prv, diag):
        (i_a, j_a), (i_b, j_b, v_b, first), smax, m = carry
        first_a = j_a == js_ref[jnp.minimum(i_a, nq - 1)]

        @pl.when(first_a)
        def _():
            q0 = pl.multiple_of(jnp.minimum(i_a, nq - 1) * tq, tq)
            qt = qt_ref[:, pl.ds(q0, tq)].astype(jnp.float32)
            q_lo = jnp.where(row < HEAD_DIM, qt, 0.0).astype(jnp.bfloat16)
            q_hi = jnp.where(row < HEAD_DIM, 0.0, qt).astype(jnp.bfloat16)
            w_buf[:, 0:tq] = jnp.concatenate([q_lo, pick[0]], axis=0)
            w_buf[:, tq:2 * tq] = jnp.concatenate([q_hi, pick[1]], axis=0)

        k0 = pl.multiple_of(jnp.where(i_a < nq, j_a, 0) * tk, tk)
        ka = jnp.concatenate([k_ref[pl.ds(k0, tk), :], ca_ref[pl.ds(k0, tk), :]], axis=1)
        vt = vt_ref[0, j_b]
        n_chunks = 2 * tq // FOX_CHUNK
        smax_new, m_new = [], []

        def chunk_cols(ch):
            hh, start = divmod(ch * FOX_CHUNK, tq)
            return slice(ch * FOX_CHUNK, (ch + 1) * FOX_CHUNK), hh, slice(start, start + FOX_CHUNK)

        def stage_a(ch):
            cols, _, hcols = chunk_cols(ch)
            if diag == 'bubble' or (diag is not None and hcols.stop <= diag * tk):
                s_cur[:, cols] = jnp.full((tk, FOX_CHUNK), NEG_INF, jnp.float32)
                smax_new.append(jnp.full((1, FOX_CHUNK), NEG_INF, jnp.float32))
                return
            s_new = _bdot(ka, w_buf[:, cols])
            if diag is not None and hcols.start < (diag + 1) * tk:
                s_new = s_new + bias_buf[diag, :, hcols]
            s_cur[:, cols] = s_new
            smax_new.append(jnp.max(s_new, axis=0, keepdims=True))

        def stage_b(ch):
            cols, hh, hcols = chunk_cols(ch)
            m_prev = jnp.where(first, NEG_INF, m[ch])
            mn = jnp.maximum(m_prev, smax[ch])
            p = jnp.exp2(s_prv[:, cols] - mn).astype(jnp.bfloat16)
            lhs = jnp.concatenate([vt[hh * HEAD_DIM:(hh + 1) * HEAD_DIM, :], ones_rows], axis=0)
            acc_buf[hh, :, hcols] = jnp.exp2(m_prev - mn) * acc_buf[hh, :, hcols] + _bdot(lhs, p)
            m_new.append(mn)

        stage_a(0)
        for ch in range(n_chunks):
            if ch + 1 < n_chunks:
                stage_a(ch + 1)
            stage_b(ch)
        smax_new = tuple(smax_new)

        @pl.when(v_b & (j_b == per_q * (i_b + 1) - 1))
        def _():
            q0 = pl.multiple_of(i_b * tq, tq)
            out_t = jnp.concatenate([acc_buf[hh, 0:HEAD_DIM, :] / acc_buf[hh, HEAD_DIM:HEAD_DIM + 1, :]
                                     for hh in range(2)], axis=0)
            y = out_t.T * _silu(g_ref[pl.ds(q0, tq), :])
            o_ref[pl.ds(q0, tq), :] = y.astype(o_ref.dtype)

        last_a = j_a == per_q * (i_a + 1) - 1
        nxt_a = (jnp.where(last_a, i_a + 1, i_a),
                 jnp.where(last_a, js_ref[jnp.minimum(i_a + 1, nq - 1)], j_a + 1))
        return (nxt_a, (i_a, jnp.where(i_a < nq, j_a, 0), i_a < nq, first_a), smax_new, tuple(m_new))

    def either(carry, bufs):
        i_a, j_a = carry[0]
        diag = j_a - per_q * i_a

        def run(kind):
            return lambda c: step(c, *bufs, kind)
        on_diag = run(per_q - 1)
        for d in range(per_q - 2, -1, -1):
            on_diag = (lambda d, other: lambda c: lax.cond(diag == d, run(d), other, c))(d, on_diag)
        special = lambda c: lax.cond(i_a >= nq, run('bubble'), on_diag, c)
        return lax.cond((diag >= 0) | (i_a >= nq), special, run(None), carry)

    def body(_, carry):
        carry = either(carry, (s_buf0, s_buf1))
        return either(carry, (s_buf1, s_buf0))

    zi = jnp.int32(0)
    row_vec = lambda v: tuple(jnp.full((1, FOX_CHUNK), v, jnp.float32) for _ in range(2 * tq // FOX_CHUNK))
    init = ((zi, zi), (zi, zi, False, False), row_vec(0.0), row_vec(0.0))
    lax.fori_loop(0, (n_steps + 2) // 2, body, init)


def _fox(cend_tiles, qbt, kb, caug, vbt, g, batch, seq):
    npair = FOX_HEADS // 2
    nkt = seq // FOX_TK
    gcol0 = SWA_W // LANES
    return pl.pallas_call(
        _fox_kernel,
        grid=(batch, npair),
        in_specs=[pl.BlockSpec(memory_space=pltpu.SMEM),
                  pl.BlockSpec((LANES, seq), lambda b, h: (h, b)),
                  pl.BlockSpec((seq, LANES), lambda b, h: (b, h)),
                  pl.BlockSpec((seq, LANES), lambda b, h: (b, h)),
                  pl.BlockSpec((1, nkt, LANES, FOX_TK), lambda b, h: (h, b, 0, 0)),
                  pl.BlockSpec((seq, LANES), lambda b, h: (b, gcol0 + h))],
        out_specs=pl.BlockSpec((seq, LANES), lambda b, h: (b, h)),
        out_shape=jax.ShapeDtypeStruct((batch * seq, FOX_W), jnp.bfloat16),
        scratch_shapes=[pltpu.VMEM((2 * LANES, 2 * FOX_TQ), jnp.bfloat16),
                        pltpu.VMEM((FOX_TK, 2 * FOX_TQ), jnp.float32),
                        pltpu.VMEM((FOX_TK, 2 * FOX_TQ), jnp.float32),
                        pltpu.VMEM((2, HEAD_DIM + 16, FOX_TQ), jnp.float32),
                        pltpu.VMEM((FOX_TQ // FOX_TK, FOX_TK, FOX_TQ), jnp.float32),
                        pltpu.SMEM((seq // FOX_TQ,), jnp.int32)],
        compiler_params=pltpu.CompilerParams(dimension_semantics=("arbitrary", "arbitrary"),
                                             vmem_limit_bytes=_VMEM_LIMIT),
        name="fox",
    )(cend_tiles, qbt, kb, caug, vbt, g)


def _split3(x):
    hi = x.astype(jnp.bfloat16)
    r1 = x - hi.astype(jnp.float32)
    mid = r1.astype(jnp.bfloat16)
    lo = (r1 - mid.astype(jnp.float32)).astype(jnp.bfloat16)
    return hi, mid, lo


def _ssd_kernel(xbc_ref, z_ref, fdt_ref, cw_ref, cb_ref, bias_ref, alog_ref, dskip_ref, nw_ref,
                y_ref, ca_ref, cend_ref, state_ref, tail_ref, ccarry_ref):
    @pl.when(pl.program_id(1) == 0)
    def _():
        state_ref[...] = jnp.zeros_like(state_ref)
        tail_ref[...] = jnp.zeros_like(tail_ref)
        ccarry_ref[...] = jnp.zeros_like(ccarry_ref)

    for sub in range(xbc_ref.shape[0] // BLOCK):
        _ssd_chunk(sub, xbc_ref, z_ref, fdt_ref, cw_ref, cb_ref, bias_ref, alog_ref, dskip_ref, nw_ref,
                   y_ref, ca_ref, cend_ref, state_ref, tail_ref, ccarry_ref)


def _ssd_chunk(sub, xbc_ref, z_ref, fdt_ref, cw_ref, cb_ref, bias_ref, alog_ref, dskip_ref, nw_ref,
               y_ref, ca_ref, cend_ref, state_ref, tail_ref, ccarry_ref):
    L = BLOCK
    rows = slice(sub * L, (sub + 1) * L)
    u = xbc_ref[rows, :]
    tail_ref[8:8 + L, :] = u
    conv = cb_ref[...] + cw_ref[SSM_CONV - 1:SSM_CONV, :] * u
    for k in range(1, SSM_CONV):
        conv = conv + cw_ref[SSM_CONV - 1 - k:SSM_CONV - k, :] * tail_ref[8 - k:8 - k + L, :]
    tail_ref[0:8, :] = u[L - 8:L]
    act = _silu(conv)
    xs = act[:, :SSM_W]
    bm_f = act[:, SSM_W:SSM_W + SSM_BC_W]
    bm = bm_f.astype(jnp.bfloat16)
    cm = act[:, SSM_W + SSM_BC_W:].astype(jnp.bfloat16)

    lane = lax.broadcasted_iota(jnp.int32, (L, LANES), 1)
    vals = fdt_ref[rows, :] + bias_ref[...]
    is_f = lane < DT_LANE0
    is_dt = (lane >= DT_LANE0) & (lane < DT_LANE0 + SSM_HEADS)
    sp = _softplus(jnp.where(is_f, -vals, vals))
    a_row = jnp.where(is_dt[0:1], -jnp.exp(alog_ref[...]), 0.0)
    dt = jnp.where(is_dt, sp, 0.0)
    scan_in = jnp.where(is_f, -sp, dt * a_row) * LOG2E

    ri = lax.broadcasted_iota(jnp.int32, (L, L), 0)
    cj = lax.broadcasted_iota(jnp.int32, (L, L), 1)
    causal = cj <= ri
    tri = jnp.where(causal, 1.0, 0.0).astype(jnp.bfloat16)
    hi, mid, lo = _split3(scan_in)
    cs = _bdot(tri, hi) + _bdot(tri, mid) + _bdot(tri, lo)
    cs_t = cs.T

    c_full = cs + ccarry_ref[...]
    ccarry_ref[...] = jnp.where(is_f[0:1], c_full[L - 1:L, :], 0.0)
    cend_ref[sub] = c_full[L - 1:L, :]
    lane64 = lane % HEAD_DIM
    ca_cols = []
    for hp in range(FOX_HEADS // 2):
        negc = -jnp.where(lane < HEAD_DIM, c_full[:, 2 * hp:2 * hp + 1], c_full[:, 2 * hp + 1:2 * hp + 2])
        hi, mid, lo = (term.astype(jnp.float32) for term in _split3(negc))
        terms = jnp.where(lane64 == 0, hi, jnp.where(lane64 == 1, mid, jnp.where(lane64 == 2, lo, 0.0)))
        ca_cols.append(terms.astype(jnp.bfloat16))
    ca_ref[rows, :] = jnp.concatenate(ca_cols, axis=1)

    low_l = lane < HEAD_DIM
    hpg = SSM_HEADS // 2
    npairs = SSM_HEADS // 2
    grp_of = [(2 * pair) // hpg for pair in range(npairs)]
    bm_g = [bm[:, g * SSM_STATE:(g + 1) * SSM_STATE] for g in range(2)]
    cm_g = [cm[:, g * SSM_STATE:(g + 1) * SSM_STATE] for g in range(2)]
    bm_t = [bm_f[:, g * SSM_STATE:(g + 1) * SSM_STATE].T.astype(jnp.bfloat16) for g in range(2)]
    cb = [_bdot_nt(cm_g[g], bm_g[g]) for g in range(2)]
    ys = []
    for pair in range(npairs):
        g = grp_of[pair]
        ha, hb = DT_LANE0 + 2 * pair, DT_LANE0 + 2 * pair + 1
        col = (cs[:, ha:ha + 1], cs[:, hb:hb + 1])
        acs_row = (cs_t[ha:ha + 1, :], cs_t[hb:hb + 1, :])
        xdt = xs[:, pair * LANES:(pair + 1) * LANES] * jnp.where(low_l, dt[:, ha:ha + 1], dt[:, hb:hb + 1])
        xdt_b = xdt.astype(jnp.bfloat16)
        prev = state_ref[pair]
        y_off = _bdot(cm_g[g], prev.astype(jnp.bfloat16))
        last = (col[0][L - 1:L, :], col[1][L - 1:L, :])
        dst = jnp.exp2(jnp.where(low_l, last[0] - col[0], last[1] - col[1]))
        st_new = _bdot(bm_t[g], (xdt * dst).astype(jnp.bfloat16))
        state_ref[pair] = prev * jnp.exp2(jnp.where(low_l, last[0], last[1])) + st_new
        y_diag = []
        for hh in range(2):
            decay = jnp.exp2(jnp.where(causal, col[hh] - acs_row[hh], NEG_INF))
            gmat = (cb[g] * decay).astype(jnp.bfloat16)
            y_diag.append(_bdot(gmat, xdt_b))
        ys.append(jnp.where(low_l, y_diag[0], y_diag[1]) + y_off * jnp.exp2(jnp.where(low_l, col[0], col[1])))
    y = jnp.concatenate(ys, axis=1) + dskip_ref[...] * xs
    y_ref[rows, :] = _rms(y * _silu(z_ref[rows, :]), nw_ref[...]).astype(y_ref.dtype)


def _ssd(xbc, z, fdt, conv_w, conv_b, b_forget, dt_bias, a_log, d_skip, norm_w, batch, seq):
    rows = min(SSD_ROWS, seq)
    per_step = rows // BLOCK
    nc = seq // rows
    row = lambda b, c: (b * nc + c, 0)
    const = lambda b, c: (0, 0)
    zpad = jnp.zeros((LANES - FOX_HEADS - SSM_HEADS,), jnp.float32)
    bias_row = jnp.concatenate([b_forget, dt_bias, zpad]).reshape(1, LANES)
    alog_row = jnp.concatenate([jnp.zeros((FOX_HEADS,), jnp.float32), a_log, zpad]).reshape(1, LANES)
    dskip_row = jnp.repeat(d_skip, HEAD_DIM).reshape(1, SSM_W)
    return pl.pallas_call(
        _ssd_kernel,
        grid=(batch, nc),
        in_specs=[pl.BlockSpec((rows, SSM_CONV_W), row),
                  pl.BlockSpec((rows, SSM_W), row),
                  pl.BlockSpec((rows, LANES), row),
                  pl.BlockSpec((SSM_CONV, SSM_CONV_W), const),
                  pl.BlockSpec((1, SSM_CONV_W), const),
                  pl.BlockSpec((1, LANES), const),
                  pl.BlockSpec((1, LANES), const),
                  pl.BlockSpec((1, SSM_W), const),
                  pl.BlockSpec((1, SSM_W), const)],
        out_specs=[pl.BlockSpec((rows, SSM_W), row),
                   pl.BlockSpec((rows, FOX_W), row),
                   pl.BlockSpec((per_step, 1, LANES), lambda b, c: (b * nc + c, 0, 0))],
        out_shape=[jax.ShapeDtypeStruct((batch * seq, SSM_W), jnp.bfloat16),
                   jax.ShapeDtypeStruct((batch * seq, FOX_W), jnp.bfloat16),
                   jax.ShapeDtypeStruct((batch * nc * per_step, 1, LANES), jnp.float32)],
        scratch_shapes=[pltpu.VMEM((SSM_HEADS // 2, 2 * HEAD_DIM, SSM_STATE), jnp.float32),
                        pltpu.VMEM((8 + BLOCK, SSM_CONV_W), jnp.float32),
                        pltpu.VMEM((1, LANES), jnp.float32)],
        compiler_params=pltpu.CompilerParams(dimension_semantics=("arbitrary", "arbitrary")),
        name="ssd",
    )(xbc, z, fdt, conv_w, conv_b.reshape(1, SSM_CONV_W), bias_row, alog_row, dskip_row,
      norm_w.reshape(1, SSM_W))


def _memkv_kernel(mem_ref, nw_ref, wk_ref, wv_ref, k_ref, v_ref):
    mn = _rms(mem_ref[...], nw_ref[...]).astype(jnp.bfloat16)
    k_ref[...] = _bdot(mn, wk_ref[...].astype(jnp.bfloat16)).astype(k_ref.dtype)
    v_ref[...] = _bdot(mn, wv_ref[...].astype(jnp.bfloat16)).astype(v_ref.dtype)


def _memkv(mem2, norm_w, wk, wv, mem_tokens):
    t = mem2.shape[0]
    row = lambda b: (b, 0)
    const = lambda b: (0, 0)
    return pl.pallas_call(
        _memkv_kernel,
        grid=(t // mem_tokens,),
        in_specs=[pl.BlockSpec((mem_tokens, D_MODEL), row),
                  pl.BlockSpec((1, D_MODEL), const),
                  pl.BlockSpec((D_MODEL, D_MODEL), const),
                  pl.BlockSpec((D_MODEL, D_MODEL), const)],
        out_specs=[pl.BlockSpec((mem_tokens, D_MODEL), row)] * 2,
        out_shape=[jax.ShapeDtypeStruct((t, D_MODEL), jnp.bfloat16)] * 2,
        compiler_params=pltpu.CompilerParams(dimension_semantics=("arbitrary",)),
        name="memkv",
    )(mem2, norm_w.reshape(1, D_MODEL), wk, wv)


def _out_kernel(x_ref, ya_ref, yb_ref, yc_ref, wo_ref, nq_ref, wq32_ref, k_ref, v_ref, wmo32_ref, fn_ref,
                o_ref, wq_ref, wmo_ref, *, final_norm):
    @pl.when(pl.program_id(0) == 0)
    def _():
        wq_ref[...] = wq32_ref[...].astype(jnp.bfloat16)
        wmo_ref[...] = wmo32_ref[...].astype(jnp.bfloat16)

    tm = x_ref.shape[0]
    groups = [slice(r, r + OUT_ROWS) for r in range(0, tm, OUT_ROWS)]
    head_cols = [slice(h * MEM_HEAD_DIM, (h + 1) * MEM_HEAD_DIM) for h in range(MEM_HEADS)]
    x1 = [x_ref[g, :]
          + _bdot(ya_ref[g, :], wo_ref[0:SWA_W, :])
          + _bdot(yb_ref[g, :], wo_ref[SWA_W:SWA_W + FOX_W, :])
          + _bdot(yc_ref[g, :], wo_ref[SWA_W + FOX_W:, :]) for g in groups]
    q = []
    for x1g in x1:
        hq = _rms(x1g, nq_ref[...]).astype(jnp.bfloat16)
        q.append((_bdot(hq, wq_ref[...]) * (MEM_HEAD_DIM ** -0.5 * LOG2E)).astype(jnp.bfloat16))
    scores = [[_bdot_nt(qg[:, sl], k_ref[:, sl]) for sl in head_cols] for qg in q]
    att = []
    for sg in scores:
        heads = []
        for s, sl in zip(sg, head_cols):
            p = jnp.exp2(s - jnp.max(s, axis=-1, keepdims=True))
            probs = (p / jnp.sum(p, axis=-1, keepdims=True)).astype(jnp.bfloat16)
            heads.append(_bdot(probs, v_ref[:, sl]).astype(jnp.bfloat16))
        att.append(jnp.concatenate(heads, axis=1))
    for g, x1g, attg in zip(groups, x1, att):
        x2 = x1g + _bdot(attg, wmo_ref[...])
        if final_norm:
            x2 = _rms(x2, fn_ref[...])
        o_ref[g, :] = x2


def _out_block(x2, ya, yb, yc, w_out, norm_xq_w, w_mq, kmem, vmem, w_mo, final_w, seq, mem_tokens, tm,
               final_norm):
    t = x2.shape[0]
    nblk_s = seq // tm
    row = lambda i: (i, 0)
    const = lambda i: (0, 0)
    memrow = lambda i: (i // nblk_s, 0)
    bf = jnp.bfloat16
    return pl.pallas_call(
        functools.partial(_out_kernel, final_norm=final_norm),
        grid=(t // tm,),
        in_specs=[pl.BlockSpec((tm, D_MODEL), row),
                  pl.BlockSpec((tm, SWA_W), row),
                  pl.BlockSpec((tm, FOX_W), row),
                  pl.BlockSpec((tm, SSM_W), row),
                  pl.BlockSpec((SWA_W + FOX_W + SSM_W, D_MODEL), const, pipeline_mode=pl.Buffered(1)),
                  pl.BlockSpec((1, D_MODEL), const),
                  pl.BlockSpec((D_MODEL, D_MODEL), const, pipeline_mode=pl.Buffered(1)),
                  pl.BlockSpec((mem_tokens, D_MODEL), memrow),
                  pl.BlockSpec((mem_tokens, D_MODEL), memrow),
                  pl.BlockSpec((D_MODEL, D_MODEL), const, pipeline_mode=pl.Buffered(1)),
                  pl.BlockSpec((1, D_MODEL), const)],
        out_specs=pl.BlockSpec((tm, D_MODEL), row),
        out_shape=jax.ShapeDtypeStruct((t, D_MODEL), jnp.float32),
        scratch_shapes=[pltpu.VMEM(w_mq.shape, bf), pltpu.VMEM(w_mo.shape, bf)],
        compiler_params=pltpu.CompilerParams(dimension_semantics=("arbitrary",),
                                             vmem_limit_bytes=_VMEM_LIMIT),
        name="outproj_mem",
    )(x2, ya, yb, yc, w_out.astype(bf), norm_xq_w.reshape(1, D_MODEL), w_mq, kmem, vmem, w_mo,
      final_w.reshape(1, D_MODEL))


def _row_tile(seq):
    return min(512, seq)


def _out_row_tile(seq):
    return min(1024, seq)


def kernel(x, mem, norm_mix_w, w_in, b_forget, swa_sinks, conv_w, conv_b, dt_bias, a_log, d_skip, ssm_norm_w, w_out, norm_xq_w, norm_mem_w, w_mq, w_mk, w_mv, w_mo, final_norm_w):
    batch, seq, _ = x.shape
    mem_tokens = mem.shape[1]
    depth = w_in.shape[0]
    assert seq % BLOCK == 0
    tm = _row_tile(seq)
    assert seq % tm == 0 and seq % FOX_TQ == 0 and seq % SWA_TQ == 0
    rope = _rope_tables(seq)
    x2 = x.reshape(batch * seq, D_MODEL)
    mem2 = mem.reshape(batch * mem_tokens, D_MODEL)
    for l in range(depth):
        fox_order = jnp.argsort(b_forget[l])
        w_t = _arrange_w_in(w_in[l], fox_order)
        ka, g, kb, z, xbc, fdt, qbt, vbt, qat, vat = _inproj(x2, norm_mix_w[l], w_t, rope, seq, tm)
        yc, caug, cend = _ssd(xbc, z, fdt, conv_w[l], conv_b[l], b_forget[l][fox_order], dt_bias[l], a_log[l],
                             d_skip[l], ssm_norm_w[l], batch, seq)
        per_tile = FOX_TK // BLOCK
        cend_tiles = cend.reshape(batch, seq // BLOCK, LANES)[:, per_tile - 1::per_tile, :FOX_HEADS]
        cend_tiles = cend_tiles.transpose(0, 2, 1).reshape(batch * FOX_HEADS, seq // FOX_TK)
        ya = _swa(qat, ka, vat, g, swa_sinks[l], batch, seq)
        yb = _fox(cend_tiles, qbt, kb, caug, vbt, g, batch, seq)
        kmem, vmem = _memkv(mem2, norm_mem_w[l], w_mk[l], w_mv[l], mem_tokens)
        o_b = SWA_W
        w_o = w_out[l].at[o_b:o_b + FOX_W].set(
            w_out[l][o_b:o_b + FOX_W].reshape(FOX_HEADS, HEAD_DIM, D_MODEL)[fox_order].reshape(FOX_W, D_MODEL))
        x2 = _out_block(x2, ya, yb, yc, w_o, norm_xq_w[l], w_mq[l], kmem, vmem, w_mo[l],
                        final_norm_w, seq, mem_tokens, _out_row_tile(seq), final_norm=(l == depth - 1))
    return x2.reshape(batch, seq, D_MODEL)
```

```python
import functools

import jax
import jax.numpy as jnp
from jax import lax
from jax.experimental import pallas as pl
from jax.experimental.pallas import tpu as pltpu

D_MODEL = 1024
HEAD_DIM = 64
BLOCK = 128
SWA_HEADS = 8
SWA_KV_HEADS = 2
FOX_HEADS = 8
SSM_HEADS = 16
SSM_STATE = 128
SSM_CONV = 4
MEM_HEADS = 4
MEM_HEAD_DIM = 256
ROPE_THETA = 10000.0
EPS = 1e-6
NEG_INF = -1e30

SWA_W = SWA_HEADS * HEAD_DIM
SWA_KV_W = SWA_KV_HEADS * HEAD_DIM
FOX_W = FOX_HEADS * HEAD_DIM
SSM_W = SSM_HEADS * HEAD_DIM
SSM_BC_W = 2 * SSM_STATE
SSM_CONV_W = SSM_W + 2 * SSM_BC_W
LANES = 128
LOG2E = 1.4426950408889634
SWA_TQ = 1024
FOX_TQ = 1024
FOX_CHUNK = 256
FOX_SKIP_BELOW = -156.0
FOX_NORM_MARGIN = 1.02
FOX_TK = 512
OUT_ROWS = 256
SSD_ROWS = 512
DT_LANE0 = FOX_HEADS

_O_QA = 0
_O_KA = _O_QA + SWA_W
_O_VA = _O_KA + SWA_KV_W
_O_GA = _O_VA + SWA_KV_W
_O_QB = _O_GA + SWA_W
_O_KB = _O_QB + FOX_W
_O_VB = _O_KB + FOX_W
_O_FB = _O_VB + FOX_W
_O_GB = _O_FB + FOX_HEADS
_O_ZC = _O_GB + FOX_W
_O_XBC = _O_ZC + SSM_W
_O_DT = _O_XBC + SSM_CONV_W
_IN_W = _O_DT + SSM_HEADS

_SEG_W = (SWA_KV_W + LANES, 2 * SWA_W, FOX_W, SSM_W, SSM_CONV_W, FOX_W, FOX_W, SWA_W, SWA_KV_W)
_SEG_OFF = tuple(sum(_SEG_W[:i]) for i in range(len(_SEG_W)))
_PROJ_W = sum(_SEG_W)

_VMEM_LIMIT = 56 * 1024 * 1024


def _bdot(a, b):
    return jnp.dot(a, b, preferred_element_type=jnp.float32)


def _bdot_nt(a, b):
    return lax.dot_general(a, b, (((1,), (1,)), ((), ())), preferred_element_type=jnp.float32)


def _silu(x):
    h = 0.5 * x
    return h + h * jnp.tanh(h)


def _softplus(x):
    return jnp.maximum(x, 0.0) + jnp.log(1.0 + jnp.exp(-jnp.abs(x)))


def _rms(x, w):
    return x * lax.rsqrt(jnp.mean(x * x, axis=-1, keepdims=True) + EPS) * w


def _rope(x, cos, sin_signed):
    width = x.shape[1]
    reps = width // LANES
    lane = lax.broadcasted_iota(jnp.int32, x.shape, 1)
    first_half = (lane % HEAD_DIM) < (HEAD_DIM // 2)
    swapped = jnp.where(first_half,
                        pltpu.roll(x, width - HEAD_DIM // 2, 1),
                        pltpu.roll(x, HEAD_DIM // 2, 1))
    cos_t = jnp.concatenate([cos] * reps, axis=1)
    sin_t = jnp.concatenate([sin_signed] * reps, axis=1)
    return x * cos_t + swapped * sin_t


def _rope_t(x, cos, sin_signed):
    rows = x.shape[0]
    reps = rows // LANES
    r = lax.broadcasted_iota(jnp.int32, x.shape, 0)
    first_half = (r % HEAD_DIM) < (HEAD_DIM // 2)
    swapped = jnp.where(first_half,
                        pltpu.roll(x, rows - HEAD_DIM // 2, 0),
                        pltpu.roll(x, HEAD_DIM // 2, 0))
    cos_t = jnp.concatenate([cos] * reps, axis=0)
    sin_t = jnp.concatenate([sin_signed] * reps, axis=0)
    return x * cos_t + swapped * sin_t


def _inproj_kernel(x_ref, nw_ref, wt_ref, cost_ref, sint_ref,
                   ka_ref, g_ref, kb_ref, z_ref, xbc_ref, fdt_ref, qbt_ref, vbt_ref, qat_ref, vat_ref):
    h = _rms(x_ref[...], nw_ref[...]).astype(jnp.bfloat16)

    def w_rows(i):
        return wt_ref[_SEG_OFF[i]:_SEG_OFF[i] + _SEG_W[i], :]

    scale = HEAD_DIM ** -0.5 * LOG2E
    k_f = _bdot_nt(h, w_rows(0))
    ka_ref[...] = _rope(k_f[:, :SWA_KV_W], cost_ref[...].T, sint_ref[...].T).astype(ka_ref.dtype)
    fdt_ref[...] = k_f[:, SWA_KV_W:]
    g_ref[...] = _bdot_nt(h, w_rows(1))
    kb_ref[...] = _bdot_nt(h, w_rows(2)).astype(kb_ref.dtype)
    z_ref[...] = _bdot_nt(h, w_rows(3))
    xbc_ref[...] = _bdot_nt(h, w_rows(4))
    qbt_ref[...] = (_bdot_nt(w_rows(5), h) * scale).astype(qbt_ref.dtype)
    qat = _rope_t(_bdot_nt(w_rows(7), h), cost_ref[...], sint_ref[...])
    qat_ref[...] = (qat * scale).astype(qat_ref.dtype)
    vat_ref[...] = _bdot_nt(w_rows(8), h).astype(vat_ref.dtype)
    vbt = _bdot_nt(w_rows(6), h).astype(vbt_ref.dtype)
    for hp in range(FOX_W // LANES):
        for c in range(vbt.shape[1] // FOX_TK):
            vbt_ref[hp, c] = vbt[hp * LANES:(hp + 1) * LANES, c * FOX_TK:(c + 1) * FOX_TK]


def _arrange_w_in(w_in, fox_order):
    wt = w_in.T.astype(jnp.bfloat16)

    def rows(o, n):
        return wt[o:o + n]

    def fox_rows(o):
        return rows(o, FOX_W).reshape(FOX_HEADS, HEAD_DIM, -1)[fox_order].reshape(FOX_W, -1)
    pad = jnp.zeros((LANES - FOX_HEADS - SSM_HEADS, wt.shape[1]), wt.dtype)
    parts = [rows(_O_KA, SWA_KV_W), rows(_O_FB, FOX_HEADS)[fox_order], rows(_O_DT, SSM_HEADS), pad,
             rows(_O_GA, SWA_W), fox_rows(_O_GB),
             fox_rows(_O_KB),
             rows(_O_ZC, SSM_W), rows(_O_XBC, SSM_CONV_W),
             fox_rows(_O_QB), fox_rows(_O_VB), rows(_O_QA, SWA_W), rows(_O_VA, SWA_KV_W)]
    return jnp.concatenate(parts, axis=0)


def _inproj(x2, norm_w, w_t, rope, seq, tm):
    t = x2.shape[0]
    nblk_s = seq // tm
    row = lambda i: (i, 0)
    const = lambda i: (0, 0)
    pos_t = lambda i: (0, i % nblk_s)
    col = lambda i: (0, i)
    bf, f32 = jnp.bfloat16, jnp.float32
    outs = [(SWA_KV_W, bf), (2 * SWA_W, f32), (FOX_W, bf), (SSM_W, f32), (SSM_CONV_W, f32), (LANES, f32)]
    npair = FOX_W // LANES
    return pl.pallas_call(
        _inproj_kernel,
        grid=(t // tm,),
        in_specs=[pl.BlockSpec((tm, D_MODEL), row),
                  pl.BlockSpec((1, D_MODEL), const),
                  pl.BlockSpec((_PROJ_W, D_MODEL), const, pipeline_mode=pl.Buffered(1)),
                  pl.BlockSpec((LANES, tm), pos_t),
                  pl.BlockSpec((LANES, tm), pos_t)],
        out_specs=[pl.BlockSpec((tm, w), row) for w, _ in outs]
        + [pl.BlockSpec((FOX_W, tm), col),
           pl.BlockSpec((npair, tm // FOX_TK, LANES, FOX_TK), lambda i: (0, i, 0, 0)),
           pl.BlockSpec((SWA_W, tm), col),
           pl.BlockSpec((SWA_KV_W, tm), col)],
        out_shape=[jax.ShapeDtypeStruct((t, w), d) for w, d in outs]
        + [jax.ShapeDtypeStruct((FOX_W, t), bf),
           jax.ShapeDtypeStruct((npair, t // FOX_TK, LANES, FOX_TK), bf),
           jax.ShapeDtypeStruct((SWA_W, t), bf),
           jax.ShapeDtypeStruct((SWA_KV_W, t), bf)],
        compiler_params=pltpu.CompilerParams(dimension_semantics=("arbitrary",),
                                             vmem_limit_bytes=_VMEM_LIMIT),
        name="inproj",
    )(x2, norm_w.reshape(1, D_MODEL), w_t, *rope)


def _rope_tables(seq):
    pos = jnp.arange(seq, dtype=jnp.float32)
    inv = 1.0 / (ROPE_THETA ** (jnp.arange(0, HEAD_DIM, 2, dtype=jnp.float32) / HEAD_DIM))
    ang = inv[:, None] * pos[None, :]
    cos, sin = jnp.cos(ang), jnp.sin(ang)
    cos_t = jnp.concatenate([cos, cos, cos, cos], axis=0)
    sin_t = jnp.concatenate([-sin, sin, -sin, sin], axis=0)
    return cos_t, sin_t


def _swa_kernel(sink_ref, qt_ref, kc_ref, kp_ref, vc_ref, vp_ref, g_ref, o_ref):
    n = pl.program_id(1)
    nsub = SWA_TQ // BLOCK
    key = lax.broadcasted_iota(jnp.int32, (2 * BLOCK, BLOCK), 0)
    qry = lax.broadcasted_iota(jnp.int32, (2 * BLOCK, BLOCK), 1)
    band = jnp.where((key > qry) & (key <= qry + BLOCK), 0.0, NEG_INF)
    band_first = jnp.where(key < BLOCK, NEG_INF, band)
    row = lax.broadcasted_iota(jnp.int32, (LANES, BLOCK), 0)
    ones_rows = jnp.ones((16, 2 * BLOCK), jnp.bfloat16)

    def scores(u):
        if u == 0:
            kcat = jnp.concatenate([kp_ref[...], kc_ref[0:BLOCK, :]], axis=0)
            vcat = jnp.concatenate([vp_ref[...], vc_ref[:, 0:BLOCK]], axis=1)
            bias = jnp.where(n > 0, band, band_first)
        else:
            kcat = kc_ref[(u - 1) * BLOCK:(u + 1) * BLOCK, :]
            vcat = vc_ref[:, (u - 1) * BLOCK:(u + 1) * BLOCK]
            bias = band
        tiles = []
        for c in range(SWA_HEADS // 2):
            qt = qt_ref[c * LANES:(c + 1) * LANES, u * BLOCK:(u + 1) * BLOCK].astype(jnp.float32)
            for half in range(2):
                kv = (2 * c + half) // (SWA_HEADS // SWA_KV_HEADS)
                q_rows = qt if kv == half else pltpu.roll(qt, HEAD_DIM, 0)
                in_half = (row < HEAD_DIM) if kv == 0 else (row >= HEAD_DIM)
                w = jnp.where(in_half, q_rows, 0.0).astype(jnp.bfloat16)
                tiles.append(_bdot(kcat, w) + bias)
        return tiles, vcat

    def finish(u, tiles, vcat):
        probs, sink_terms = [], []
        for head, s in enumerate(tiles):
            sink = sink_ref[0, head] * LOG2E
            m = jnp.maximum(jnp.max(s, axis=0, keepdims=True), sink)
            probs.append(jnp.exp2(s - m).astype(jnp.bfloat16))
            sink_terms.append(jnp.exp2(sink - m))
        outs = []
        for head, p in enumerate(probs):
            kv = head // (SWA_HEADS // SWA_KV_HEADS)
            lhs = jnp.concatenate([vcat[kv * HEAD_DIM:(kv + 1) * HEAD_DIM, :], ones_rows], axis=0)
            pv = _bdot(lhs, p)
            outs.append(pv[0:HEAD_DIM, :] / (pv[HEAD_DIM:HEAD_DIM + 1, :] + sink_terms[head]))
        y = jnp.concatenate(outs, axis=0).T
        rows = slice(u * BLOCK, (u + 1) * BLOCK)
        o_ref[rows, :] = (y * _silu(g_ref[rows, :])).astype(o_ref.dtype)

    pending = scores(0)
    for u in range(nsub):
        nxt = scores(u + 1) if u + 1 < nsub else None
        finish(u, *pending)
        pending = nxt


def _swa(qat, ka, vat, g, sinks, batch, seq):
    n = seq // SWA_TQ
    nsub = SWA_TQ // BLOCK
    cur = lambda b, i: (b * n + i, 0)
    cur_t = lambda b, i: (0, b * n + i)
    prev = lambda b, i: ((b * n + i) * nsub - jnp.minimum(i, 1), 0)
    prev_t = lambda b, i: (0, (b * n + i) * nsub - jnp.minimum(i, 1))
    return pl.pallas_call(
        _swa_kernel,
        grid=(batch, n),
        in_specs=[pl.BlockSpec(memory_space=pltpu.SMEM),
                  pl.BlockSpec((SWA_W, SWA_TQ), cur_t),
                  pl.BlockSpec((SWA_TQ, SWA_KV_W), cur),
                  pl.BlockSpec((BLOCK, SWA_KV_W), prev),
                  pl.BlockSpec((SWA_KV_W, SWA_TQ), cur_t),
                  pl.BlockSpec((SWA_KV_W, BLOCK), prev_t),
                  pl.BlockSpec((SWA_TQ, SWA_W), cur)],
        out_specs=pl.BlockSpec((SWA_TQ, SWA_W), cur),
        out_shape=jax.ShapeDtypeStruct((batch * seq, SWA_W), jnp.bfloat16),
        compiler_params=pltpu.CompilerParams(dimension_semantics=("arbitrary", "arbitrary")),
        name="swa",
    )(sinks.reshape(1, SWA_HEADS).astype(jnp.float32), qat, ka, ka, vat, vat, g)


def _fox_kernel(cend_ref, qt_ref, k_ref, ca_ref, vt_ref, g_ref, o_ref, w_buf, s_buf0, s_buf1, acc_buf, bias_buf,
                js_ref):
    tq, tk = FOX_TQ, FOX_TK
    per_q = tq // tk
    nq = qt_ref.shape[1] // tq
    acc_rows = acc_buf.shape[1]

    krow = lax.broadcasted_iota(jnp.int32, (tk, tq), 0)
    qcol = lax.broadcasted_iota(jnp.int32, (tk, tq), 1)
    for d in range(per_q):
        bias_buf[d] = jnp.where(krow + d * tk <= qcol, 0.0, NEG_INF)
    for buf in (s_buf0, s_buf1, acc_buf):
        buf[...] = jnp.zeros_like(buf)

    row = lax.broadcasted_iota(jnp.int32, (LANES, tq), 0)
    nsplit = 3
    pick = (jnp.where(row < nsplit, 1.0, 0.0).astype(jnp.bfloat16),
            jnp.where((row >= HEAD_DIM) & (row < HEAD_DIM + nsplit), 1.0, 0.0).astype(jnp.bfloat16))
    ones_rows = jnp.ones((acc_rows - HEAD_DIM, tk), jnp.bfloat16)

    qsq = jnp.square(qt_ref[...].astype(jnp.float32))
    ksq = jnp.square(k_ref[...].astype(jnp.float32))
    lane_sel = lax.broadcasted_iota(jnp.int32, (LANES, LANES), 0) // HEAD_DIM == \
        lax.broadcasted_iota(jnp.int32, (LANES, LANES), 1)
    kn2 = jnp.max(_bdot(ksq.astype(jnp.bfloat16), jnp.where(lane_sel, 1.0, 0.0).astype(jnp.bfloat16)),
                  axis=0, keepdims=True)
    bound = []
    for hh in range(2):
        qn2 = jnp.max(jnp.sum(qsq[hh * HEAD_DIM:(hh + 1) * HEAD_DIM, :], axis=0, keepdims=True),
                      axis=1, keepdims=True)
        bound.append((FOX_NORM_MARGIN * 2.0 * jnp.sqrt(qn2 * kn2[:, hh:hh + 1]))[0, 0])
    head0 = (pl.program_id(0) * (FOX_HEADS // 2) + pl.program_id(1)) * 2
    n_steps = jnp.int32(0)
    for i in range(nq):
        count, prefix = jnp.int32(0), jnp.bool_(True)
        for j in range(per_q * i):
            for hh in range(2):
                drop = cend_ref[head0 + hh, j] - cend_ref[head0 + hh, per_q * i - 1]
                prefix = prefix & (bound[hh] - drop < FOX_SKIP_BELOW)
            count = count + prefix.astype(jnp.int32)
        js_ref[i] = count
        n_steps = n_steps + (per_q * (i + 1) - count)

    def step(carry, s_cur, s_prv, diag):
        (i_a, j_a), (i_b, j_b, v_b, first), smax, m = carry
        first_a = j_a == js_ref[jnp.minimum(i_a, nq - 1)]

        @pl.when(first_a)
        def _():
            q0 = pl.multiple_of(jnp.minimum(i_a, nq - 1) * tq, tq)
            qt = qt_ref[:, pl.ds(q0, tq)].astype(jnp.float32)
            q_lo = jnp.where(row < HEAD_DIM, qt, 0.0).astype(jnp.bfloat16)
            q_hi = jnp.where(row < HEAD_DIM, 0.0, qt).astype(jnp.bfloat16)
            w_buf[:, 0:tq] = jnp.concatenate([q_lo, pick[0]], axis=0)
            w_buf[:, tq:2 * tq] = jnp.concatenate([q_hi, pick[1]], axis=0)

        k0 = pl.multiple_of(jnp.where(i_a < nq, j_a, 0) * tk, tk)
        ka = jnp.concatenate([k_ref[pl.ds(k0, tk), :], ca_ref[pl.ds(k0, tk), :]], axis=1)
        vt = vt_ref[0, j_b]
        n_chunks = 2 * tq // FOX_CHUNK
        smax_new, m_new = [], []

        def chunk_cols(ch):
            hh, start = divmod(ch * FOX_CHUNK, tq)
            return slice(ch * FOX_CHUNK, (ch + 1) * FOX_CHUNK), hh, slice(start, start + FOX_CHUNK)

        def stage_a(ch):
            cols, _, hcols = chunk_cols(ch)
            if diag == 'bubble' or (diag is not None and hcols.stop <= diag * tk):
                s_cur[:, cols] = jnp.full((tk, FOX_CHUNK), NEG_INF, jnp.float32)
                smax_new.append(jnp.full((1, FOX_CHUNK), NEG_INF, jnp.float32))
                return
            s_new = _bdot(ka, w_buf[:, cols])
            if diag is not None and hcols.start < (diag + 1) * tk:
                s_new = s_new + bias_buf[diag, :, hcols]
            s_cur[:, cols] = s_new
            smax_new.append(jnp.max(s_new, axis=0, keepdims=True))

        def stage_b(ch):
            cols, hh, hcols = chunk_cols(ch)
            m_prev = jnp.where(first, NEG_INF, m[ch])
            mn = jnp.maximum(m_prev, smax[ch])
            p = jnp.exp2(s_prv[:, cols] - mn).astype(jnp.bfloat16)
            lhs = jnp.concatenate([vt[hh * HEAD_DIM:(hh + 1) * HEAD_DIM, :], ones_rows], axis=0)
            acc_buf[hh, :, hcols] = jnp.exp2(m_prev - mn) * acc_buf[hh, :, hcols] + _bdot(lhs, p)
            m_new.append(mn)

        stage_a(0)
        for ch in range(n_chunks):
            if ch + 1 < n_chunks:
                stage_a(ch + 1)
            stage_b(ch)
        smax_new = tuple(smax_new)

        @pl.when(v_b & (j_b == per_q * (i_b + 1) - 1))
        def _():
            q0 = pl.multiple_of(i_b * tq, tq)
            out_t = jnp.concatenate([acc_buf[hh, 0:HEAD_DIM, :] / acc_buf[hh, HEAD_DIM:HEAD_DIM + 1, :]
                                     for hh in range(2)], axis=0)
            y = out_t.T * _silu(g_ref[pl.ds(q0, tq), :])
            o_ref[pl.ds(q0, tq), :] = y.astype(o_ref.dtype)

        last_a = j_a == per_q * (i_a + 1) - 1
        nxt_a = (jnp.where(last_a, i_a + 1, i_a),
                 jnp.where(last_a, js_ref[jnp.minimum(i_a + 1, nq - 1)], j_a + 1))
        return (nxt_a, (i_a, jnp.where(i_a < nq, j_a, 0), i_a < nq, first_a), smax_new, tuple(m_new))

    def either(carry, bufs):
        i_a, j_a = carry[0]
        diag = j_a - per_q * i_a

        def run(kind):
            return lambda c: step(c, *bufs, kind)
        on_diag = run(per_q - 1)
        for d in range(per_q - 2, -1, -1):
            on_diag = (lambda d, other: lambda c: lax.cond(diag == d, run(d), other, c))(d, on_diag)
        special = lambda c: lax.cond(i_a >= nq, run('bubble'), on_diag, c)
        return lax.cond((diag >= 0) | (i_a >= nq), special, run(None), carry)

    def body(_, carry):
        carry = either(carry, (s_buf0, s_buf1))
        return either(carry, (s_buf1, s_buf0))

    zi = jnp.int32(0)
    row_vec = lambda v: tuple(jnp.full((1, FOX_CHUNK), v, jnp.float32) for _ in range(2 * tq // FOX_CHUNK))
    init = ((zi, zi), (zi, zi, False, False), row_vec(0.0), row_vec(0.0))
    lax.fori_loop(0, (n_steps + 2) // 2, body, init)


def _fox(cend_tiles, qbt, kb, caug, vbt, g, batch, seq):
    npair = FOX_HEADS // 2
    nkt = seq // FOX_TK
    gcol0 = SWA_W // LANES
    return pl.pallas_call(
        _fox_kernel,
        grid=(batch, npair),
        in_specs=[pl.BlockSpec(memory_space=pltpu.SMEM),
                  pl.BlockSpec((LANES, seq), lambda b, h: (h, b)),
                  pl.BlockSpec((seq, LANES), lambda b, h: (b, h)),
                  pl.BlockSpec((seq, LANES), lambda b, h: (b, h)),
                  pl.BlockSpec((1, nkt, LANES, FOX_TK), lambda b, h: (h, b, 0, 0)),
                  pl.BlockSpec((seq, LANES), lambda b, h: (b, gcol0 + h))],
        out_specs=pl.BlockSpec((seq, LANES), lambda b, h: (b, h)),
        out_shape=jax.ShapeDtypeStruct((batch * seq, FOX_W), jnp.bfloat16),
        scratch_shapes=[pltpu.VMEM((2 * LANES, 2 * FOX_TQ), jnp.bfloat16),
                        pltpu.VMEM((FOX_TK, 2 * FOX_TQ), jnp.float32),
                        pltpu.VMEM((FOX_TK, 2 * FOX_TQ), jnp.float32),
                        pltpu.VMEM((2, HEAD_DIM + 16, FOX_TQ), jnp.float32),
                        pltpu.VMEM((FOX_TQ // FOX_TK, FOX_TK, FOX_TQ), jnp.float32),
                        pltpu.SMEM((seq // FOX_TQ,), jnp.int32)],
        compiler_params=pltpu.CompilerParams(dimension_semantics=("arbitrary", "arbitrary"),
                                             vmem_limit_bytes=_VMEM_LIMIT),
        name="fox",
    )(cend_tiles, qbt, kb, caug, vbt, g)


def _split3(x):
    hi = x.astype(jnp.bfloat16)
    r1 = x - hi.astype(jnp.float32)
    mid = r1.astype(jnp.bfloat16)
    lo = (r1 - mid.astype(jnp.float32)).astype(jnp.bfloat16)
    return hi, mid, lo


def _ssd_kernel(xbc_ref, z_ref, fdt_ref, cw_ref, cb_ref, bias_ref, alog_ref, dskip_ref, nw_ref,
                y_ref, ca_ref, cend_ref, state_ref, tail_ref, ccarry_ref):
    @pl.when(pl.program_id(1) == 0)
    def _():
        state_ref[...] = jnp.zeros_like(state_ref)
        tail_ref[...] = jnp.zeros_like(tail_ref)
        ccarry_ref[...] = jnp.zeros_like(ccarry_ref)

    for sub in range(xbc_ref.shape[0] // BLOCK):
        _ssd_chunk(sub, xbc_ref, z_ref, fdt_ref, cw_ref, cb_ref, bias_ref, alog_ref, dskip_ref, nw_ref,
                   y_ref, ca_ref, cend_ref, state_ref, tail_ref, ccarry_ref)


def _ssd_chunk(sub, xbc_ref, z_ref, fdt_ref, cw_ref, cb_ref, bias_ref, alog_ref, dskip_ref, nw_ref,
               y_ref, ca_ref, cend_ref, state_ref, tail_ref, ccarry_ref):
    L = BLOCK
    rows = slice(sub * L, (sub + 1) * L)
    u = xbc_ref[rows, :]
    tail_ref[8:8 + L, :] = u
    conv = cb_ref[...] + cw_ref[SSM_CONV - 1:SSM_CONV, :] * u
    for k in range(1, SSM_CONV):
        conv = conv + cw_ref[SSM_CONV - 1 - k:SSM_CONV - k, :] * tail_ref[8 - k:8 - k + L, :]
    tail_ref[0:8, :] = u[L - 8:L]
    act = _silu(conv)
    xs = act[:, :SSM_W]
    bm_f = act[:, SSM_W:SSM_W + SSM_BC_W]
    bm = bm_f.astype(jnp.bfloat16)
    cm = act[:, SSM_W + SSM_BC_W:].astype(jnp.bfloat16)

    lane = lax.broadcasted_iota(jnp.int32, (L, LANES), 1)
    vals = fdt_ref[rows, :] + bias_ref[...]
    is_f = lane < DT_LANE0
    is_dt = (lane >= DT_LANE0) & (lane < DT_LANE0 + SSM_HEADS)
    sp = _softplus(jnp.where(is_f, -vals, vals))
    a_row = jnp.where(is_dt[0:1], -jnp.exp(alog_ref[...]), 0.0)
    dt = jnp.where(is_dt, sp, 0.0)
    scan_in = jnp.where(is_f, -sp, dt * a_row) * LOG2E

    ri = lax.broadcasted_iota(jnp.int32, (L, L), 0)
    cj = lax.broadcasted_iota(jnp.int32, (L, L), 1)
    causal = cj <= ri
    tri = jnp.where(causal, 1.0, 0.0).astype(jnp.bfloat16)
    hi, mid, lo = _split3(scan_in)
    cs = _bdot(tri, hi) + _bdot(tri, mid) + _bdot(tri, lo)
    cs_t = cs.T

    c_full = cs + ccarry_ref[...]
    ccarry_ref[...] = jnp.where(is_f[0:1], c_full[L - 1:L, :], 0.0)
    cend_ref[sub] = c_full[L - 1:L, :]
    lane64 = lane % HEAD_DIM
    ca_cols = []
    for hp in range(FOX_HEADS // 2):
        negc = -jnp.where(lane < HEAD_DIM, c_full[:, 2 * hp:2 * hp + 1], c_full[:, 2 * hp + 1:2 * hp + 2])
        hi, mid, lo = (term.astype(jnp.float32) for term in _split3(negc))
        terms = jnp.where(lane64 == 0, hi, jnp.where(lane64 == 1, mid, jnp.where(lane64 == 2, lo, 0.0)))
        ca_cols.append(terms.astype(jnp.bfloat16))
    ca_ref[rows, :] = jnp.concatenate(ca_cols, axis=1)

    low_l = lane < HEAD_DIM
    hpg = SSM_HEADS // 2
    npairs = SSM_HEADS // 2
    grp_of = [(2 * pair) // hpg for pair in range(npairs)]
    bm_g = [bm[:, g * SSM_STATE:(g + 1) * SSM_STATE] for g in range(2)]
    cm_g = [cm[:, g * SSM_STATE:(g + 1) * SSM_STATE] for g in range(2)]
    bm_t = [bm_f[:, g * SSM_STATE:(g + 1) * SSM_STATE].T.astype(jnp.bfloat16) for g in range(2)]
    cb = [_bdot_nt(cm_g[g], bm_g[g]) for g in range(2)]
    ys = []
    for pair in range(npairs):
        g = grp_of[pair]
        ha, hb = DT_LANE0 + 2 * pair, DT_LANE0 + 2 * pair + 1
        col = (cs[:, ha:ha + 1], cs[:, hb:hb + 1])
        acs_row = (cs_t[ha:ha + 1, :], cs_t[hb:hb + 1, :])
        xdt = xs[:, pair * LANES:(pair + 1) * LANES] * jnp.where(low_l, dt[:, ha:ha + 1], dt[:, hb:hb + 1])
        xdt_b = xdt.astype(jnp.bfloat16)
        prev = state_ref[pair]
        y_off = _bdot(cm_g[g], prev.astype(jnp.bfloat16))
        last = (col[0][L - 1:L, :], col[1][L - 1:L, :])
        dst = jnp.exp2(jnp.where(low_l, last[0] - col[0], last[1] - col[1]))
        st_new = _bdot(bm_t[g], (xdt * dst).astype(jnp.bfloat16))
        state_ref[pair] = prev * jnp.exp2(jnp.where(low_l, last[0], last[1])) + st_new
        y_diag = []
        for hh in range(2):
            decay = jnp.exp2(jnp.where(causal, col[hh] - acs_row[hh], NEG_INF))
            gmat = (cb[g] * decay).astype(jnp.bfloat16)
            y_diag.append(_bdot(gmat, xdt_b))
        ys.append(jnp.where(low_l, y_diag[0], y_diag[1]) + y_off * jnp.exp2(jnp.where(low_l, col[0], col[1])))
    y = jnp.concatenate(ys, axis=1) + dskip_ref[...] * xs
    y_ref[rows, :] = _rms(y * _silu(z_ref[rows, :]), nw_ref[...]).astype(y_ref.dtype)


def _ssd(xbc, z, fdt, conv_w, conv_b, b_forget, dt_bias, a_log, d_skip, norm_w, batch, seq):
    rows = min(SSD_ROWS, seq)
    per_step = rows // BLOCK
    nc = seq // rows
    row = lambda b, c: (b * nc + c, 0)
    const = lambda b, c: (0, 0)
    zpad = jnp.zeros((LANES - FOX_HEADS - SSM_HEADS,), jnp.float32)
    bias_row = jnp.concatenate([b_forget, dt_bias, zpad]).reshape(1, LANES)
    alog_row = jnp.concatenate([jnp.zeros((FOX_HEADS,), jnp.float32), a_log, zpad]).reshape(1, LANES)
    dskip_row = jnp.repeat(d_skip, HEAD_DIM).reshape(1, SSM_W)
    return pl.pallas_call(
        _ssd_kernel,
        grid=(batch, nc),
        in_specs=[pl.BlockSpec((rows, SSM_CONV_W), row),
                  pl.BlockSpec((rows, SSM_W), row),
                  pl.BlockSpec((rows, LANES), row),
                  pl.BlockSpec((SSM_CONV, SSM_CONV_W), const),
                  pl.BlockSpec((1, SSM_CONV_W), const),
                  pl.BlockSpec((1, LANES), const),
                  pl.BlockSpec((1, LANES), const),
                  pl.BlockSpec((1, SSM_W), const),
                  pl.BlockSpec((1, SSM_W), const)],
        out_specs=[pl.BlockSpec((rows, SSM_W), row),
                   pl.BlockSpec((rows, FOX_W), row),
                   pl.BlockSpec((per_step, 1, LANES), lambda b, c: (b * nc + c, 0, 0))],
        out_shape=[jax.ShapeDtypeStruct((batch * seq, SSM_W), jnp.bfloat16),
                   jax.ShapeDtypeStruct((batch * seq, FOX_W), jnp.bfloat16),
                   jax.ShapeDtypeStruct((batch * nc * per_step, 1, LANES), jnp.float32)],
        scratch_shapes=[pltpu.VMEM((SSM_HEADS // 2, 2 * HEAD_DIM, SSM_STATE), jnp.float32),
                        pltpu.VMEM((8 + BLOCK, SSM_CONV_W), jnp.float32),
                        pltpu.VMEM((1, LANES), jnp.float32)],
        compiler_params=pltpu.CompilerParams(dimension_semantics=("arbitrary", "arbitrary")),
        name="ssd",
    )(xbc, z, fdt, conv_w, conv_b.reshape(1, SSM_CONV_W), bias_row, alog_row, dskip_row,
      norm_w.reshape(1, SSM_W))


def _memkv_kernel(mem_ref, nw_ref, wk_ref, wv_ref, k_ref, v_ref):
    mn = _rms(mem_ref[...], nw_ref[...]).astype(jnp.bfloat16)
    k_ref[...] = _bdot(mn, wk_ref[...]).astype(k_ref.dtype)
    v_ref[...] = _bdot(mn, wv_ref[...]).astype(v_ref.dtype)


def _memkv(mem2, norm_w, wk, wv, mem_tokens):
    t = mem2.shape[0]
    row = lambda b: (b, 0)
    const = lambda b: (0, 0)
    return pl.pallas_call(
        _memkv_kernel,
        grid=(t // mem_tokens,),
        in_specs=[pl.BlockSpec((mem_tokens, D_MODEL), row),
                  pl.BlockSpec((1, D_MODEL), const),
                  pl.BlockSpec((D_MODEL, D_MODEL), const),
                  pl.BlockSpec((D_MODEL, D_MODEL), const)],
        out_specs=[pl.BlockSpec((mem_tokens, D_MODEL), row)] * 2,
        out_shape=[jax.ShapeDtypeStruct((t, D_MODEL), jnp.bfloat16)] * 2,
        compiler_params=pltpu.CompilerParams(dimension_semantics=("arbitrary",)),
        name="memkv",
    )(mem2, norm_w.reshape(1, D_MODEL), wk.astype(jnp.bfloat16), wv.astype(jnp.bfloat16))


def _out_kernel(x_ref, ya_ref, yb_ref, yc_ref, wo_ref, nq_ref, wq_ref, k_ref, v_ref, wmo_ref, fn_ref,
                o_ref, *, final_norm):
    tm = x_ref.shape[0]
    groups = [slice(r, r + OUT_ROWS) for r in range(0, tm, OUT_ROWS)]
    head_cols = [slice(h * MEM_HEAD_DIM, (h + 1) * MEM_HEAD_DIM) for h in range(MEM_HEADS)]
    x1 = [x_ref[g, :]
          + _bdot(ya_ref[g, :], wo_ref[0:SWA_W, :])
          + _bdot(yb_ref[g, :], wo_ref[SWA_W:SWA_W + FOX_W, :])
          + _bdot(yc_ref[g, :], wo_ref[SWA_W + FOX_W:, :]) for g in groups]
    q = []
    for x1g in x1:
        hq = _rms(x1g, nq_ref[...]).astype(jnp.bfloat16)
        q.append((_bdot(hq, wq_ref[...]) * (MEM_HEAD_DIM ** -0.5 * LOG2E)).astype(jnp.bfloat16))
    scores = [[_bdot_nt(qg[:, sl], k_ref[:, sl]) for sl in head_cols] for qg in q]
    att = []
    for sg in scores:
        heads = []
        for s, sl in zip(sg, head_cols):
            p = jnp.exp2(s - jnp.max(s, axis=-1, keepdims=True))
            probs = (p / jnp.sum(p, axis=-1, keepdims=True)).astype(jnp.bfloat16)
            heads.append(_bdot(probs, v_ref[:, sl]).astype(jnp.bfloat16))
        att.append(jnp.concatenate(heads, axis=1))
    for g, x1g, attg in zip(groups, x1, att):
        x2 = x1g + _bdot(attg, wmo_ref[...])
        if final_norm:
            x2 = _rms(x2, fn_ref[...])
        o_ref[g, :] = x2


def _out_block(x2, ya, yb, yc, w_out, norm_xq_w, w_mq, kmem, vmem, w_mo, final_w, seq, mem_tokens, tm,
               final_norm):
    t = x2.shape[0]
    nblk_s = seq // tm
    row = lambda i: (i, 0)
    const = lambda i: (0, 0)
    memrow = lambda i: (i // nblk_s, 0)
    bf = jnp.bfloat16
    return pl.pallas_call(
        functools.partial(_out_kernel, final_norm=final_norm),
        grid=(t // tm,),
        in_specs=[pl.BlockSpec((tm, D_MODEL), row),
                  pl.BlockSpec((tm, SWA_W), row),
                  pl.BlockSpec((tm, FOX_W), row),
                  pl.BlockSpec((tm, SSM_W), row),
                  pl.BlockSpec((SWA_W + FOX_W + SSM_W, D_MODEL), const, pipeline_mode=pl.Buffered(1)),
                  pl.BlockSpec((1, D_MODEL), const),
                  pl.BlockSpec((D_MODEL, D_MODEL), const, pipeline_mode=pl.Buffered(1)),
                  pl.BlockSpec((mem_tokens, D_MODEL), memrow),
                  pl.BlockSpec((mem_tokens, D_MODEL), memrow),
                  pl.BlockSpec((D_MODEL, D_MODEL), const, pipeline_mode=pl.Buffered(1)),
                  pl.BlockSpec((1, D_MODEL), const)],
        out_specs=pl.BlockSpec((tm, D_MODEL), row),
        out_shape=jax.ShapeDtypeStruct((t, D_MODEL), jnp.float32),
        compiler_params=pltpu.CompilerParams(dimension_semantics=("arbitrary",),
                                             vmem_limit_bytes=_VMEM_LIMIT),
        name="outproj_mem",
    )(x2, ya, yb, yc, w_out.astype(bf), norm_xq_w.reshape(1, D_MODEL), w_mq.astype(bf), kmem, vmem,
      w_mo.astype(bf), final_w.reshape(1, D_MODEL))


def _row_tile(seq):
    return min(512, seq)


def _out_row_tile(seq):
    return min(1024, seq)


def kernel(x, mem, norm_mix_w, w_in, b_forget, swa_sinks, conv_w, conv_b, dt_bias, a_log, d_skip, ssm_norm_w, w_out, norm_xq_w, norm_mem_w, w_mq, w_mk, w_mv, w_mo, final_norm_w):
    batch, seq, _ = x.shape
    mem_tokens = mem.shape[1]
    depth = w_in.shape[0]
    assert seq % BLOCK == 0
    tm = _row_tile(seq)
    assert seq % tm == 0 and seq % FOX_TQ == 0 and seq % SWA_TQ == 0
    rope = _rope_tables(seq)
    x2 = x.reshape(batch * seq, D_MODEL)
    mem2 = mem.reshape(batch * mem_tokens, D_MODEL)
    for l in range(depth):
        fox_order = jnp.argsort(b_forget[l])
        w_t = _arrange_w_in(w_in[l], fox_order)
        ka, g, kb, z, xbc, fdt, qbt, vbt, qat, vat = _inproj(x2, norm_mix_w[l], w_t, rope, seq, tm)
        yc, caug, cend = _ssd(xbc, z, fdt, conv_w[l], conv_b[l], b_forget[l][fox_order], dt_bias[l], a_log[l],
                             d_skip[l], ssm_norm_w[l], batch, seq)
        per_tile = FOX_TK // BLOCK
        cend_tiles = cend.reshape(batch, seq // BLOCK, LANES)[:, per_tile - 1::per_tile, :FOX_HEADS]
        cend_tiles = cend_tiles.transpose(0, 2, 1).reshape(batch * FOX_HEADS, seq // FOX_TK)
        ya = _swa(qat, ka, vat, g, swa_sinks[l], batch, seq)
        yb = _fox(cend_tiles, qbt, kb, caug, vbt, g, batch, seq)
        kmem, vmem = _memkv(mem2, norm_mem_w[l], w_mk[l], w_mv[l], mem_tokens)
        o_b = SWA_W
        w_o = w_out[l].at[o_b:o_b + FOX_W].set(
            w_out[l][o_b:o_b + FOX_W].reshape(FOX_HEADS, HEAD_DIM, D_MODEL)[fox_order].reshape(FOX_W, D_MODEL))
        x2 = _out_block(x2, ya, yb, yc, w_o, norm_xq_w[l], w_mq[l], kmem, vmem, w_mo[l],
                        final_norm_w, seq, mem_tokens, _out_row_tile(seq), final_norm=(l == depth - 1))
    return x2.reshape(batch, seq, D_MODEL)
```
